```python
import math
import jax, jax.numpy as jnp
from jax import lax
import numpy as np

D_MODEL = 1024
BATCH = 8
SEQ = 2048
DEPTH = 2

N_A_LAYERS = max(1, DEPTH // 2)
N_B_LAYERS = DEPTH - N_A_LAYERS
CONV_WIDTH = 3
N_HEADS = 8
HEAD_DIM = D_MODEL // (2 * N_HEADS)
N_GROUPS = 4
EXPERTS_PER_GROUP = 8
N_EXPERTS = N_GROUPS * EXPERTS_PER_GROUP
TOP_K = 2
D_EXPERT = D_MODEL // 2
Q_BLOCK = 128
ROW_BLOCK = 128
ALPHA = (2.0 * DEPTH) ** 0.25
BETA = (8.0 * DEPTH) ** -0.25
LN_EPS = 1e-5
RMS_EPS = 1e-5

kernel_name = "yoco_shortconv_diffattn_hmoe"


def layer_norm(x, g, b):
    xf = x.astype(jnp.float32)
    mu = jnp.mean(xf, axis=-1, keepdims=True)
    var = jnp.mean(jnp.square(xf - mu), axis=-1, keepdims=True)
    return ((xf - mu) * lax.rsqrt(var + LN_EPS) * g.astype(jnp.float32) + b.astype(jnp.float32)).astype(x.dtype)


def rms_norm(x, g):
    xf = x.astype(jnp.float32)
    return xf * lax.rsqrt(jnp.mean(jnp.square(xf), axis=-1, keepdims=True) + RMS_EPS) * g.astype(jnp.float32)


def short_conv_mixer(x, w_in, conv_w, w_out):
    h = jnp.einsum('bsd,de->bse', x, w_in)
    gate_b, gate_c, v = jnp.split(h, 3, axis=-1)
    u = gate_c * v
    u_conv = lax.conv_general_dilated(
        u, conv_w[:, None, :], window_strides=(1,), padding=[(CONV_WIDTH - 1, 0)],
        dimension_numbers=('NWC', 'WIO', 'NWC'), feature_group_count=D_MODEL)
    return jnp.einsum('bsd,de->bse', gate_b * u_conv, w_out)


def shared_kv(x, kv_w):
    b, s, _ = x.shape
    kv = jnp.einsum('bsd,de->bse', x, kv_w)
    k, v = jnp.split(kv, 2, axis=-1)
    return k.reshape(b, s, N_HEADS, 2, HEAD_DIM), v.reshape(b, s, N_HEADS, 2 * HEAD_DIM)


def diff_attention(x, w_q, lam, subln_g, w_o, k, v, layer_idx):
    b, s, _ = x.shape
    q = jnp.einsum('bsd,de->bse', x, w_q).reshape(b, s, N_HEADS, 2, HEAD_DIM)
    lam_init = 0.8 - 0.6 * math.exp(-0.3 * layer_idx)
    lamf = lam.astype(jnp.float32)
    lam_full = jnp.exp(jnp.sum(lamf[0] * lamf[1])) - jnp.exp(jnp.sum(lamf[2] * lamf[3])) + lam_init
    scale = HEAD_DIM ** -0.5
    outs = []
    for i in range(s // Q_BLOCK):
        kv_len = (i + 1) * Q_BLOCK
        q_blk = q[:, i * Q_BLOCK:kv_len]
        k_pre = k[:, :kv_len]
        v_pre = v[:, :kv_len]
        sc = jnp.einsum('bqhcd,bkhcd->bhcqk', q_blk, k_pre,
                        preferred_element_type=jnp.float32) * scale
        q_pos = i * Q_BLOCK + jnp.arange(Q_BLOCK)
        k_pos = jnp.arange(kv_len)
        sc = jnp.where(k_pos[None, :] <= q_pos[:, None], sc, -jnp.inf)
        p = jax.nn.softmax(sc, axis=-1)
        attn = p[:, :, 0] - lam_full * p[:, :, 1]
        outs.append(jnp.einsum('bhqk,bkhe->bqhe', attn, v_pre.astype(jnp.float32)))
    o = jnp.concatenate(outs, axis=1)
    o = rms_norm(o, subln_g) * (1.0 - lam_init)
    o = o.reshape(b, s, N_HEADS * 2 * HEAD_DIM).astype(x.dtype)
    return jnp.einsum('bse,ed->bsd', o, w_o)


def routed_experts(xf, expert_idx, gates, w_in, w_down):
    t, d = xf.shape
    a = t * TOP_K
    flat_e = expert_idx.reshape(a)
    order = jnp.argsort(flat_e)
    sorted_e = flat_e[order]
    counts = jnp.zeros((N_EXPERTS,), jnp.int32).at[flat_e].add(1)
    padded = (counts + ROW_BLOCK - 1) // ROW_BLOCK * ROW_BLOCK
    pad_end = jnp.cumsum(padded)
    pad_start = pad_end - padded
    start = jnp.cumsum(counts) - counts
    dest = pad_start[sorted_e] + jnp.arange(a, dtype=jnp.int32) - start[sorted_e]
    n_rows = -(-a // ROW_BLOCK) * ROW_BLOCK + N_EXPERTS * ROW_BLOCK
    n_blocks = n_rows // ROW_BLOCK
    tok = (order // TOP_K).astype(jnp.int32)
    row_token = jnp.full((n_rows,), t, jnp.int32).at[dest].set(tok)
    x_pad = jnp.concatenate([xf, jnp.zeros((1, d), xf.dtype)], axis=0)
    rows = x_pad[row_token].reshape(n_blocks, ROW_BLOCK, d)
    block_expert = jnp.minimum(
        jnp.searchsorted(pad_end, jnp.arange(n_blocks) * ROW_BLOCK, side='right'), N_EXPERTS - 1)

    def expert_block(args):
        xb, e = args
        g, u = jnp.split(xb @ w_in[e], 2, axis=-1)
        return (jax.nn.silu(g) * u) @ w_down[e]

    y_rows = lax.map(expert_block, (rows, block_expert)).reshape(n_rows, d)
    contrib = y_rows[dest] * gates.reshape(a)[order][:, None].astype(y_rows.dtype)
    return jax.ops.segment_sum(contrib, tok, num_segments=t)


def hierarchical_moe(x, rg_w, rg_b, re_w, re_b, w_in, w_down):
    b, s, d = x.shape
    xf = x.reshape(b * s, d)
    t = xf.shape[0]
    g_prob = jax.nn.softmax((xf @ rg_w).astype(jnp.float32) + rg_b.astype(jnp.float32), axis=-1)
    g_top, g_idx = lax.top_k(g_prob, 1)
    e_logits = ((xf @ re_w).astype(jnp.float32) + re_b.astype(jnp.float32)).reshape(t, N_GROUPS, EXPERTS_PER_GROUP)
    e_sel = jnp.take_along_axis(e_logits, g_idx[:, :, None], axis=1)[:, 0]
    w_top, local_idx = lax.top_k(jax.nn.softmax(e_sel, axis=-1), TOP_K)
    gates = g_top * (w_top / jnp.sum(w_top, axis=-1, keepdims=True))
    expert_idx = g_idx * EXPERTS_PER_GROUP + local_idx
    return routed_experts(xf, expert_idx, gates, w_in, w_down).reshape(b, s, d)


def setup_inputs(seed: int = 0) -> dict:
    key = jax.random.key(seed)
    ks = jax.random.split(key, 24)
    n = jax.random.normal
    f32 = jnp.float32
    s_d = D_MODEL ** -0.5
    kv_scale = jnp.concatenate([jnp.full((D_MODEL,), s_d, f32), jnp.full((D_MODEL,), s_d * BETA, f32)])
    return {
        "x": n(ks[0], (BATCH, SEQ, D_MODEL), f32),
        "a_w_in": n(ks[1], (N_A_LAYERS, D_MODEL, 3 * D_MODEL), f32) * s_d,
        "a_conv_w": n(ks[2], (N_A_LAYERS, CONV_WIDTH, D_MODEL), f32) * CONV_WIDTH ** -0.5,
        "a_w_out": n(ks[3], (N_A_LAYERS, D_MODEL, D_MODEL), f32) * s_d * BETA,
        "kv_w": n(ks[4], (D_MODEL, 2 * D_MODEL), f32) * kv_scale,
        "b_w_q": n(ks[5], (N_B_LAYERS, D_MODEL, 2 * N_HEADS * HEAD_DIM), f32) * s_d,
        "b_lambda": n(ks[6], (N_B_LAYERS, 4, HEAD_DIM), f32) * 0.1,
        "b_subln_g": 1.0 + 0.01 * n(ks[7], (N_B_LAYERS, 2 * HEAD_DIM), f32),
        "b_w_o": n(ks[8], (N_B_LAYERS, 2 * N_HEADS * HEAD_DIM, D_MODEL), f32) * s_d * BETA,
        "ln1_g": 1.0 + 0.01 * n(ks[9], (DEPTH, D_MODEL), f32),
        "ln1_b": 0.01 * n(ks[10], (DEPTH, D_MODEL), f32),
        "ln2_g": 1.0 + 0.01 * n(ks[11], (DEPTH, D_MODEL), f32),
        "ln2_b": 0.01 * n(ks[12], (DEPTH, D_MODEL), f32),
        "rg_w": n(ks[13], (DEPTH, D_MODEL, N_GROUPS), f32) * s_d,
        "rg_b": 0.01 * n(ks[14], (DEPTH, N_GROUPS), f32),
        "re_w": n(ks[15], (DEPTH, D_MODEL, N_EXPERTS), f32) * s_d,
        "re_b": 0.01 * n(ks[16], (DEPTH, N_EXPERTS), f32),
        "e_w_in": n(ks[17], (DEPTH, N_EXPERTS, D_MODEL, 2 * D_EXPERT), f32) * s_d,
        "e_w_down": n(ks[18], (DEPTH, N_EXPERTS, D_EXPERT, D_MODEL), f32) * D_EXPERT ** -0.5 * BETA,
    }


def reference(x, a_w_in, a_conv_w, a_w_out, kv_w, b_w_q, b_lambda, b_subln_g, b_w_o,
              ln1_g, ln1_b, ln2_g, ln2_b, rg_w, rg_b, re_w, re_b, e_w_in, e_w_down):
    k_shared = None
    v_shared = None
    for layer in range(DEPTH):
        if layer < N_A_LAYERS:
            mix = short_conv_mixer(x, a_w_in[layer], a_conv_w[layer], a_w_out[layer])
        else:
            j = layer - N_A_LAYERS
            mix = diff_attention(x, b_w_q[j], b_lambda[j], b_subln_g[j], b_w_o[j],
                                 k_shared, v_shared, layer)
        x = layer_norm(ALPHA * x + mix, ln1_g[layer], ln1_b[layer])
        ffn = hierarchical_moe(x, rg_w[layer], rg_b[layer], re_w[layer], re_b[layer],
                               e_w_in[layer], e_w_down[layer])
        x = layer_norm(ALPHA * x + ffn, ln2_g[layer], ln2_b[layer])
        if layer == N_A_LAYERS - 1:
            k_shared, v_shared = shared_kv(x, kv_w)
    return x
```

```python
import functools
import math

import jax
import jax.numpy as jnp
from jax import lax
from jax.experimental import pallas as pl
from jax.experimental.pallas import tpu as pltpu

N_HEADS = 8
N_GROUPS = 4
LN_EPS = 1e-5
RMS_EPS = 1e-5

LANES = 128
SUBLANES = 8
VMEM_LIMIT_BYTES = 56 * 1024 * 1024

ROW_TILE = 512
EXPERT_BLOCK = 256
ATTN_BLOCK = 256
ISSUE_UNROLL = 8

_E1, _E2, _R1, _R2, _G1, _G2 = range(6)

_F32 = jnp.float32
_BF16 = jnp.bfloat16
_NT = (((1,), (1,)), ((), ()))


def _dot(a, b):
    return jnp.dot(a, b, preferred_element_type=_F32)


def _layer_norm(z, g, b):
    mu = jnp.mean(z, axis=-1, keepdims=True)
    d = z - mu
    var = jnp.mean(d * d, axis=-1, keepdims=True)
    return d * lax.rsqrt(var + LN_EPS) * g + b


def _route_tail(x1, wrh_ref, wrl_ref, rb_ref, route_ref, counts_ref, *, epg):
    tm = x1.shape[0]
    xh = x1.astype(_BF16)
    xl = (x1 - xh.astype(_F32)).astype(_BF16)
    wh = wrh_ref[...]
    logits = _dot(xh, wh) + _dot(xl, wh) + _dot(xh, wrl_ref[...]) + rb_ref[...]

    lane = lax.broadcasted_iota(jnp.int32, (tm, LANES), 1)
    lanef = lane.astype(_F32)
    neg = -jnp.inf
    gl = jnp.where(lane < N_GROUPS, logits, neg)
    gmax = jnp.max(gl, axis=-1, keepdims=True)
    gidx = jnp.min(jnp.where(gl == gmax, lanef, float(LANES)), axis=-1, keepdims=True)
    gtop = 1.0 / jnp.sum(jnp.exp(gl - gmax), axis=-1, keepdims=True)

    lo = float(N_GROUPS) + gidx * float(epg)
    el = jnp.where((lanef >= lo) & (lanef < lo + float(epg)), logits, neg)
    m1 = jnp.max(el, axis=-1, keepdims=True)
    i1 = jnp.min(jnp.where(el == m1, lanef, float(LANES)), axis=-1, keepdims=True)
    el2 = jnp.where(lanef == i1, neg, el)
    m2 = jnp.max(el2, axis=-1, keepdims=True)
    i2 = jnp.min(jnp.where(el2 == m2, lanef, float(LANES)), axis=-1, keepdims=True)
    w2 = jnp.exp(m2 - m1)
    inv = 1.0 / (1.0 + w2)
    g1 = gtop * inv
    g2 = gtop * w2 * inv
    e1 = i1 - float(N_GROUPS)
    e2 = i2 - float(N_GROUPS)

    onehot = jnp.where((lanef == e1) | (lanef == e2), 1.0, 0.0)
    r = lax.broadcasted_iota(jnp.int32, (tm, tm), 0)
    c = lax.broadcasted_iota(jnp.int32, (tm, tm), 1)
    ltri = jnp.where(c < r, 1.0, 0.0).astype(_BF16)
    prefix = _dot(ltri, onehot.astype(_BF16)) + counts_ref[...]
    r1 = jnp.sum(jnp.where(lanef == e1, prefix, 0.0), axis=-1, keepdims=True)
    r2 = jnp.sum(jnp.where(lanef == e2, prefix, 0.0), axis=-1, keepdims=True)
    counts_ref[...] += jnp.sum(onehot, axis=0, keepdims=True)

    rec = jnp.zeros((tm, LANES), _F32)
    for slot, val in ((_E1, e1), (_E2, e2), (_R1, r1), (_R2, r2), (_G1, g1), (_G2, g2)):
        rec = jnp.where(lane == slot, val, rec)
    route_ref[...] = rec


def _conv_mixer_kernel(x_ref, win_ref, cw_ref, wout_ref, g_ref, b_ref, wrh_ref, wrl_ref, rb_ref,
                       x1_ref, route_ref, counts_ref, ubuf_ref, *, tiles_per_seq, alpha, epg):
    i = pl.program_id(0)
    tm, d = x_ref.shape

    @pl.when(i == 0)
    def _():
        counts_ref[...] = jnp.zeros_like(counts_ref)

    @pl.when(i % tiles_per_seq == 0)
    def _():
        ubuf_ref[0:SUBLANES, :] = jnp.zeros((SUBLANES, d), _F32)

    x = x_ref[...]
    h = _dot(x.astype(_BF16), win_ref[...])
    u = h[:, d:2 * d] * h[:, 2 * d:]
    ubuf_ref[SUBLANES:SUBLANES + tm, :] = u
    cw = cw_ref[...]
    uc = (cw[0:1, :] * ubuf_ref[SUBLANES - 2:SUBLANES - 2 + tm, :]
          + cw[1:2, :] * ubuf_ref[SUBLANES - 1:SUBLANES - 1 + tm, :]
          + cw[2:3, :] * u)
    ubuf_ref[0:SUBLANES, :] = ubuf_ref[tm:tm + SUBLANES, :]
    y = _dot((h[:, :d] * uc).astype(_BF16), wout_ref[...])
    x1 = _layer_norm(alpha * x + y, g_ref[...], b_ref[...])
    x1_ref[...] = x1
    _route_tail(x1, wrh_ref, wrl_ref, rb_ref, route_ref, counts_ref, epg=epg)


def _attn_out_kernel(x_ref, o_ref, wo_ref, g_ref, b_ref, wrh_ref, wrl_ref, rb_ref,
                     x1_ref, route_ref, counts_ref, *, alpha, epg):
    @pl.when(pl.program_id(0) == 0)
    def _():
        counts_ref[...] = jnp.zeros_like(counts_ref)

    y = _dot(o_ref[...], wo_ref[...])
    x1 = _layer_norm(alpha * x_ref[...] + y, g_ref[...], b_ref[...])
    x1_ref[...] = x1
    _route_tail(x1, wrh_ref, wrl_ref, rb_ref, route_ref, counts_ref, epg=epg)


def _row_copy(src_hbm, row, dst_vmem, dst_row, sem):
    return pltpu.make_async_copy(src_hbm.at[pl.ds(row, 1), :], dst_vmem.at[pl.ds(dst_row, 1), :], sem)


def _expert_ffn_kernel(bexp_ref, nused_ref, rowtok_ref, x_hbm, win_ref, wdn_ref, y_ref,
                       xbuf, sem, winb, wdnb):
    i = pl.program_id(0)
    n_used = nused_ref[0]
    blk = xbuf.shape[1]
    de = wdn_ref.shape[1]
    slot = i % 2

    def issue(b, s):
        def body(r, carry):
            _row_copy(x_hbm, rowtok_ref[b * blk + r], xbuf.at[s], r, sem.at[s]).start()
            return carry
        lax.fori_loop(0, blk, body, 0, unroll=ISSUE_UNROLL)

    @pl.when(i == 0)
    def _():
        issue(0, 0)

    @pl.when(i + 1 < n_used)
    def _():
        issue(i + 1, 1 - slot)

    @pl.when(i < n_used)
    def _():
        def wait(r, carry):
            _row_copy(x_hbm, 0, xbuf.at[slot], r, sem.at[slot]).wait()
            return carry
        lax.fori_loop(0, blk, wait, 0, unroll=ISSUE_UNROLL)

        prev = bexp_ref[jnp.maximum(i - 1, 0)]

        @pl.when((i == 0) | (bexp_ref[i] != prev))
        def _():
            winb[...] = win_ref[0].astype(_BF16)
            wdnb[...] = wdn_ref[0].astype(_BF16)

        h = _dot(xbuf[slot].astype(_BF16), winb[...])
        g = h[:, :de]
        a = g * jax.nn.sigmoid(g) * h[:, de:]
        y_ref[...] = _dot(a.astype(_BF16), wdnb[...])

    @pl.when(i >= n_used)
    def _():
        y_ref[...] = jnp.zeros_like(y_ref)


def _combine_body(dest_ref, x1_ref, route_ref, y_hbm, g_ref, b_ref, ybuf, sem, *, alpha):
    i = pl.program_id(0)
    n = pl.num_programs(0)
    tm = x1_ref.shape[0]
    slot = i % 2

    def issue(tile, s):
        def body(r, carry):
            t = tile * tm + r
            _row_copy(y_hbm, dest_ref[2 * t], ybuf.at[s, 0], r, sem.at[s]).start()
            _row_copy(y_hbm, dest_ref[2 * t + 1], ybuf.at[s, 1], r, sem.at[s]).start()
            return carry
        lax.fori_loop(0, tm, body, 0, unroll=ISSUE_UNROLL)

    @pl.when(i == 0)
    def _():
        issue(0, 0)

    @pl.when(i + 1 < n)
    def _():
        issue(i + 1, 1 - slot)

    def wait(r, carry):
        _row_copy(y_hbm, 0, ybuf.at[slot, 0], r, sem.at[slot]).wait()
        _row_copy(y_hbm, 0, ybuf.at[slot, 1], r, sem.at[slot]).wait()
        return carry
    lax.fori_loop(0, tm, wait, 0, unroll=ISSUE_UNROLL)

    rec = route_ref[...]
    ffn = rec[:, _G1:_G1 + 1] * ybuf[slot, 0] + rec[:, _G2:_G2 + 1] * ybuf[slot, 1]
    return _layer_norm(alpha * x1_ref[...] + ffn, g_ref[...], b_ref[...])


def _combine_kernel(dest_ref, x1_ref, route_ref, y_hbm, g_ref, b_ref, x2_ref, ybuf, sem, *, alpha):
    x2_ref[...] = _combine_body(dest_ref, x1_ref, route_ref, y_hbm, g_ref, b_ref, ybuf, sem, alpha=alpha)


def _combine_qkv_kernel(dest_ref, x1_ref, route_ref, y_hbm, g_ref, b_ref, wk_ref, wvt_ref, wq_ref,
                        x2_ref, k_ref, vt_ref, q_ref, ybuf, sem, *, alpha, q_scale):
    x2 = _combine_body(dest_ref, x1_ref, route_ref, y_hbm, g_ref, b_ref, ybuf, sem, alpha=alpha)
    x2_ref[...] = x2
    xb = x2.astype(_BF16)
    k_ref[...] = _dot(xb, wk_ref[...]).astype(_BF16)
    q_ref[...] = (_dot(xb, wq_ref[...]) * q_scale).astype(_BF16)
    vt = lax.dot_general(wvt_ref[...], xb, _NT, preferred_element_type=_F32).astype(_BF16)
    tk = vt_ref.shape[2]
    for s in range(vt_ref.shape[0]):
        vt_ref[s] = vt[:, s * tk:(s + 1) * tk]


def _diff_attn_kernel(lam_ref, g_ref, q_ref, k_ref, vt_ref, o_ref, lamfull_ref, acc_ref, *, lam_init):
    bi, hi, i = pl.program_id(0), pl.program_id(1), pl.program_id(2)
    tq = q_ref.shape[0]
    tk = k_ref.shape[1]
    hd = lam_ref.shape[1]

    @pl.when((bi == 0) & (hi == 0) & (i == 0))
    def _():
        lam = lam_ref[...]
        a = jnp.sum(lam[0:1, :] * lam[1:2, :], axis=-1, keepdims=True)
        b = jnp.sum(lam[2:3, :] * lam[3:4, :], axis=-1, keepdims=True)
        lamfull_ref[...] = jnp.broadcast_to(jnp.exp(a) - jnp.exp(b) + lam_init, lamfull_ref.shape)

    q = q_ref[...]
    lane = lax.broadcasted_iota(jnp.int32, q.shape, 1)
    zero = jnp.zeros_like(q)
    qs = (jnp.where(lane < hd, q, zero), jnp.where(lane >= hd, q, zero))
    acc_ref[...] = jnp.zeros_like(acc_ref)

    def kv_step(j, carry, masked):
        kb = k_ref[j]
        vb = vt_ref[j]
        out = []
        for c in range(2):
            m, l = carry[2 * c], carry[2 * c + 1]
            s = lax.dot_general(kb, qs[c], _NT, preferred_element_type=_F32)
            if masked:
                kpos = j * tk + lax.broadcasted_iota(jnp.int32, s.shape, 0)
                qpos = i * tq + lax.broadcasted_iota(jnp.int32, s.shape, 1)
                s = jnp.where(kpos <= qpos, s, -jnp.inf)
            mn = jnp.maximum(m, jnp.max(s, axis=0, keepdims=True))
            p = jnp.exp(s - mn)
            corr = jnp.exp(m - mn)
            l = l * corr + jnp.sum(p, axis=0, keepdims=True)
            acc_ref[c] = acc_ref[c] * corr + _dot(vb, p.astype(_BF16))
            out += [mn, l]
        return tuple(out)

    minf = jnp.full((1, tq), -jnp.inf, _F32)
    zl = jnp.zeros((1, tq), _F32)
    carry = lax.fori_loop(0, i, functools.partial(kv_step, masked=False), (minf, zl, minf, zl))
    _, l1, _, l2 = kv_step(i, carry, masked=True)

    lam_full = lamfull_ref[0:1, 0:1]
    ot = acc_ref[0] * (1.0 / l1) - lam_full * (acc_ref[1] * (1.0 / l2))
    ot = ot * lax.rsqrt(jnp.mean(ot * ot, axis=0, keepdims=True) + RMS_EPS)
    o_ref[...] = (ot.T * g_ref[...] * (1.0 - lam_init)).astype(o_ref.dtype)


def _params(*sem):
    return pltpu.CompilerParams(dimension_semantics=sem, vmem_limit_bytes=VMEM_LIMIT_BYTES)


def _full(shape):
    return pl.BlockSpec(shape, lambda *_: (0,) * len(shape))


def _router_weights(rg_w, rg_b, re_w, re_b):
    d = rg_w.shape[0]
    n = rg_w.shape[1] + re_w.shape[1]
    w = jnp.concatenate([rg_w, re_w, jnp.zeros((d, LANES - n), _F32)], axis=1)
    b = jnp.concatenate([rg_b, re_b, jnp.zeros((LANES - n,), _F32)])[None, :]
    wh = w.astype(_BF16)
    wl = (w - wh.astype(_F32)).astype(_BF16)
    return wh, wl, b


def _route_outs(t, d, tm):
    shapes = (jax.ShapeDtypeStruct((t, d), _F32),
              jax.ShapeDtypeStruct((t, LANES), _F32),
              jax.ShapeDtypeStruct((1, LANES), _F32))
    specs = (pl.BlockSpec((tm, d), lambda i: (i, 0)),
             pl.BlockSpec((tm, LANES), lambda i: (i, 0)),
             pl.BlockSpec((1, LANES), lambda i: (0, 0)))
    return shapes, specs


def _conv_mixer(x, w_in, conv_w, w_out, ln_g, ln_b, router, *, seq, alpha, epg):
    t, d = x.shape
    tm = ROW_TILE
    wrh, wrl, rb = router
    shapes, specs = _route_outs(t, d, tm)
    return pl.pallas_call(
        functools.partial(_conv_mixer_kernel, tiles_per_seq=seq // tm, alpha=alpha, epg=epg),
        grid=(t // tm,),
        in_specs=[pl.BlockSpec((tm, d), lambda i: (i, 0)),
                  _full(w_in.shape), _full(conv_w.shape), _full(w_out.shape),
                  _full((1, d)), _full((1, d)),
                  _full(wrh.shape), _full(wrl.shape), _full(rb.shape)],
        out_specs=specs, out_shape=shapes,
        scratch_shapes=[pltpu.VMEM((tm + SUBLANES, d), _F32)],
        compiler_params=_params("arbitrary"),
        name="conv_mixer_ln_router",
    )(x, w_in, conv_w, w_out, ln_g[None, :], ln_b[None, :], wrh, wrl, rb)


def _attn_out(x, o, w_o, ln_g, ln_b, router, *, alpha, epg):
    t, d = x.shape
    tm = ROW_TILE
    wrh, wrl, rb = router
    shapes, specs = _route_outs(t, d, tm)
    return pl.pallas_call(
        functools.partial(_attn_out_kernel, alpha=alpha, epg=epg),
        grid=(t // tm,),
        in_specs=[pl.BlockSpec((tm, d), lambda i: (i, 0)),
                  pl.BlockSpec((tm, d), lambda i: (i, 0)),
                  _full(w_o.shape), _full((1, d)), _full((1, d)),
                  _full(wrh.shape), _full(wrl.shape), _full(rb.shape)],
        out_specs=specs, out_shape=shapes,
        compiler_params=_params("arbitrary"),
        name="attn_out_ln_router",
    )(x, o, w_o, ln_g[None, :], ln_b[None, :], wrh, wrl, rb)


def _dispatch_plan(route, counts, n_experts):
    t = route.shape[0]
    blk = EXPERT_BLOCK
    e = route[:, _E1:_E2 + 1].astype(jnp.int32)
    rank = route[:, _R1:_R2 + 1].astype(jnp.int32)
    cnt = counts[0, :n_experts].astype(jnp.int32)
    padded = (cnt + blk - 1) // blk * blk
    pad_end = jnp.cumsum(padded)
    pad_start = pad_end - padded
    dest = (pad_start[e] + rank).reshape(-1)
    n_rows = 2 * t + n_experts * blk
    n_blocks = n_rows // blk
    row_token = jnp.zeros((n_rows,), jnp.int32).at[dest].set(jnp.arange(2 * t, dtype=jnp.int32) // 2)
    block_expert = jnp.minimum(
        jnp.searchsorted(pad_end, jnp.arange(n_blocks, dtype=jnp.int32) * blk, side="right"),
        n_experts - 1).astype(jnp.int32)
    n_used = (pad_end[-1:] // blk).astype(jnp.int32)
    return dest, row_token, block_expert, n_used


def _expert_ffn(x1, w_in, w_down, row_token, block_expert, n_used):
    t, d = x1.shape
    n_exp, _, de2 = w_in.shape
    de = w_down.shape[1]
    blk = EXPERT_BLOCK
    n_rows = row_token.shape[0]
    return pl.pallas_call(
        _expert_ffn_kernel,
        grid_spec=pltpu.PrefetchScalarGridSpec(
            num_scalar_prefetch=3,
            grid=(n_rows // blk,),
            in_specs=[pl.BlockSpec(memory_space=pl.ANY),
                      pl.BlockSpec((1, d, de2), lambda i, be, nu, rt: (be[i], 0, 0)),
                      pl.BlockSpec((1, de, d), lambda i, be, nu, rt: (be[i], 0, 0))],
            out_specs=pl.BlockSpec((blk, d), lambda i, be, nu, rt: (i, 0)),
            scratch_shapes=[pltpu.VMEM((2, blk, d), _F32),
                            pltpu.SemaphoreType.DMA((2,)),
                            pltpu.VMEM((d, de2), _BF16),
                            pltpu.VMEM((de, d), _BF16)]),
        out_shape=jax.ShapeDtypeStruct((n_rows, d), _F32),
        compiler_params=_params("arbitrary"),
        name="expert_ffn",
    )(block_expert, n_used, row_token, x1, w_in, w_down)


def _combine_scratch(tm, d):
    return [pltpu.VMEM((2, 2, tm, d), _F32), pltpu.SemaphoreType.DMA((2,))]


def _combine(dest, x1, route, y_rows, ln_g, ln_b, *, alpha):
    t, d = x1.shape
    tm = ROW_TILE
    return pl.pallas_call(
        functools.partial(_combine_kernel, alpha=alpha),
        grid_spec=pltpu.PrefetchScalarGridSpec(
            num_scalar_prefetch=1,
            grid=(t // tm,),
            in_specs=[pl.BlockSpec((tm, d), lambda i, ds: (i, 0)),
                      pl.BlockSpec((tm, LANES), lambda i, ds: (i, 0)),
                      pl.BlockSpec(memory_space=pl.ANY),
                      pl.BlockSpec((1, d), lambda i, ds: (0, 0)),
                      pl.BlockSpec((1, d), lambda i, ds: (0, 0))],
            out_specs=pl.BlockSpec((tm, d), lambda i, ds: (i, 0)),
            scratch_shapes=_combine_scratch(tm, d)),
        out_shape=jax.ShapeDtypeStruct((t, d), _F32),
        compiler_params=_params("arbitrary"),
        name="combine_ln",
    )(dest, x1, route, y_rows, ln_g[None, :], ln_b[None, :])


def _combine_qkv(dest, x1, route, y_rows, ln_g, ln_b, w_k, w_vt, w_q, *, alpha, q_scale):
    t, d = x1.shape
    tm = ROW_TILE
    tk = ATTN_BLOCK
    per = tm // tk
    wspec = pl.BlockSpec((d, d), lambda i, ds: (0, 0))
    row = pl.BlockSpec((tm, d), lambda i, ds: (i, 0))
    return pl.pallas_call(
        functools.partial(_combine_qkv_kernel, alpha=alpha, q_scale=q_scale),
        grid_spec=pltpu.PrefetchScalarGridSpec(
            num_scalar_prefetch=1,
            grid=(t // tm,),
            in_specs=[row,
                      pl.BlockSpec((tm, LANES), lambda i, ds: (i, 0)),
                      pl.BlockSpec(memory_space=pl.ANY),
                      pl.BlockSpec((1, d), lambda i, ds: (0, 0)),
                      pl.BlockSpec((1, d), lambda i, ds: (0, 0)),
                      wspec, wspec, wspec],
            out_specs=(row, row,
                       pl.BlockSpec((per, d, tk), lambda i, ds: (i, 0, 0)),
                       row),
            scratch_shapes=_combine_scratch(tm, d)),
        out_shape=(jax.ShapeDtypeStruct((t, d), _F32),
                   jax.ShapeDtypeStruct((t, d), _BF16),
                   jax.ShapeDtypeStruct((t // tk, d, tk), _BF16),
                   jax.ShapeDtypeStruct((t, d), _BF16)),
        compiler_params=_params("arbitrary"),
        name="combine_ln_qkv",
    )(dest, x1, route, y_rows, ln_g[None, :], ln_b[None, :], w_k, w_vt, w_q)


def _diff_attention(q, k, vt, lam, subln_g, *, batch, seq, lam_init):
    t, d = q.shape
    tq = tk = ATTN_BLOCK
    nq = seq // tq
    nk = seq // tk
    hw = d // N_HEADS
    k3 = k.reshape(t // tk, tk, d)
    return pl.pallas_call(
        functools.partial(_diff_attn_kernel, lam_init=lam_init),
        grid=(batch, N_HEADS, nq),
        in_specs=[_full(lam.shape), _full((1, hw)),
                  pl.BlockSpec((tq, hw), lambda b, h, i: (b * nq + i, h)),
                  pl.BlockSpec((nk, tk, hw), lambda b, h, i: (b, 0, h)),
                  pl.BlockSpec((nk, hw, tk), lambda b, h, i: (b, h, 0))],
        out_specs=pl.BlockSpec((tq, hw), lambda b, h, i: (b * nq + i, h)),
        out_shape=jax.ShapeDtypeStruct((t, d), _BF16),
        scratch_shapes=[pltpu.VMEM((1, LANES), _F32), pltpu.VMEM((2, hw, tq), _F32)],
        compiler_params=_params("arbitrary", "arbitrary", "arbitrary"),
        name="diff_attention",
    )(lam, subln_g[None, :], q, k3, vt)


def _moe(x1, route, counts, w_in, w_down):
    n_exp = w_in.shape[0]
    dest, row_token, block_expert, n_used = _dispatch_plan(route, counts, n_exp)
    y_rows = _expert_ffn(x1, w_in, w_down, row_token, block_expert, n_used)
    return dest, y_rows


def kernel(x, a_w_in, a_conv_w, a_w_out, kv_w, b_w_q, b_lambda, b_subln_g, b_w_o,
           ln1_g, ln1_b, ln2_g, ln2_b, rg_w, rg_b, re_w, re_b, e_w_in, e_w_down):
    batch, seq, d = x.shape
    depth = ln1_g.shape[0]
    assert depth == 2 and a_w_in.shape[0] == 1 and b_w_q.shape[0] == 1
    assert seq % ROW_TILE == 0 and seq % ATTN_BLOCK == 0 and ROW_TILE % ATTN_BLOCK == 0
    t = batch * seq
    alpha = (2.0 * depth) ** 0.25
    epg = re_w.shape[2] // N_GROUPS
    head_dim = d // (2 * N_HEADS)
    xf = x.reshape(t, d)

    router0 = _router_weights(rg_w[0], rg_b[0], re_w[0], re_b[0])
    x1, route, counts = _conv_mixer(xf, a_w_in[0].astype(_BF16), a_conv_w[0], a_w_out[0].astype(_BF16),
                                    ln1_g[0], ln1_b[0], router0, seq=seq, alpha=alpha, epg=epg)
    dest, y_rows = _moe(x1, route, counts, e_w_in[0], e_w_down[0])
    x2, k, vt, q = _combine_qkv(dest, x1, route, y_rows, ln2_g[0], ln2_b[0],
                                kv_w[:, :d].astype(_BF16), kv_w[:, d:].T.astype(_BF16),
                                b_w_q[0].astype(_BF16), alpha=alpha, q_scale=head_dim ** -0.5)

    lam_init = 0.8 - 0.6 * math.exp(-0.3 * 1)
    o = _diff_attention(q, k, vt, b_lambda[0], b_subln_g[0], batch=batch, seq=seq, lam_init=lam_init)
    router1 = _router_weights(rg_w[1], rg_b[1], re_w[1], re_b[1])
    x1, route, counts = _attn_out(x2, o, b_w_o[0].astype(_BF16), ln1_g[1], ln1_b[1], router1,
                                  alpha=alpha, epg=epg)
    dest, y_rows = _moe(x1, route, counts, e_w_in[1], e_w_down[1])
    out = _combine(dest, x1, route, y_rows, ln2_g[1], ln2_b[1], alpha=alpha)
    return out.reshape(batch, seq, d)
```

```python
import functools
import math

import jax
import jax.numpy as jnp
from jax import lax
from jax.experimental import pallas as pl
from jax.experimental.pallas import tpu as pltpu

N_HEADS = 8
N_GROUPS = 4
LN_EPS = 1e-5
RMS_EPS = 1e-5

LANES = 128
SUBLANES = 8
VMEM_LIMIT_BYTES = 56 * 1024 * 1024

ROW_TILE = 512
EXPERT_BLOCK = 256
ATTN_BLOCK = 256
ISSUE_UNROLL = 8

_E1, _E2, _R1, _R2, _G1, _G2 = range(6)

_F32 = jnp.float32
_BF16 = jnp.bfloat16
_NT = (((1,), (1,)), ((), ()))


def _dot(a, b):
    return jnp.dot(a, b, preferred_element_type=_F32)


def _layer_norm(z, g, b):
    mu = jnp.mean(z, axis=-1, keepdims=True)
    d = z - mu
    var = jnp.mean(d * d, axis=-1, keepdims=True)
    return d * lax.rsqrt(var + LN_EPS) * g + b


def _route_tail(x1, wrh_ref, wrl_ref, rb_ref, route_ref, counts_ref, *, epg):
    tm = x1.shape[0]
    xh = x1.astype(_BF16)
    xl = (x1 - xh.astype(_F32)).astype(_BF16)
    wh = wrh_ref[...]
    logits = _dot(xh, wh) + _dot(xl, wh) + _dot(xh, wrl_ref[...]) + rb_ref[...]

    lane = lax.broadcasted_iota(jnp.int32, (tm, LANES), 1)
    lanef = lane.astype(_F32)
    neg = -jnp.inf
    gl = jnp.where(lane < N_GROUPS, logits, neg)
    gmax = jnp.max(gl, axis=-1, keepdims=True)
    gidx = jnp.min(jnp.where(gl == gmax, lanef, float(LANES)), axis=-1, keepdims=True)
    gtop = 1.0 / jnp.sum(jnp.exp(gl - gmax), axis=-1, keepdims=True)

    lo = float(N_GROUPS) + gidx * float(epg)
    el = jnp.where((lanef >= lo) & (lanef < lo + float(epg)), logits, neg)
    m1 = jnp.max(el, axis=-1, keepdims=True)
    i1 = jnp.min(jnp.where(el == m1, lanef, float(LANES)), axis=-1, keepdims=True)
    el2 = jnp.where(lanef == i1, neg, el)
    m2 = jnp.max(el2, axis=-1, keepdims=True)
    i2 = jnp.min(jnp.where(el2 == m2, lanef, float(LANES)), axis=-1, keepdims=True)
    w2 = jnp.exp(m2 - m1)
    inv = 1.0 / (1.0 + w2)
    g1 = gtop * inv
    g2 = gtop * w2 * inv
    e1 = i1 - float(N_GROUPS)
    e2 = i2 - float(N_GROUPS)

    onehot = jnp.where((lanef == e1) | (lanef == e2), 1.0, 0.0)
    r = lax.broadcasted_iota(jnp.int32, (tm, tm), 0)
    c = lax.broadcasted_iota(jnp.int32, (tm, tm), 1)
    ltri = jnp.where(c < r, 1.0, 0.0).astype(_BF16)
    prefix = _dot(ltri, onehot.astype(_BF16)) + counts_ref[...]
    r1 = jnp.sum(jnp.where(lanef == e1, prefix, 0.0), axis=-1, keepdims=True)
    r2 = jnp.sum(jnp.where(lanef == e2, prefix, 0.0), axis=-1, keepdims=True)
    counts_ref[...] += jnp.sum(onehot, axis=0, keepdims=True)

    rec = jnp.zeros((tm, LANES), _F32)
    for slot, val in ((_E1, e1), (_E2, e2), (_R1, r1), (_R2, r2), (_G1, g1), (_G2, g2)):
        rec = jnp.where(lane == slot, val, rec)
    route_ref[...] = rec


def _conv_mixer_kernel(x_ref, win_ref, cw_ref, wout_ref, g_ref, b_ref, wrh_ref, wrl_ref, rb_ref,
                       x1_ref, route_ref, counts_ref, ubuf_ref, *, tiles_per_seq, alpha, epg):
    i = pl.program_id(0)
    tm, d = x_ref.shape

    @pl.when(i == 0)
    def _():
        counts_ref[...] = jnp.zeros_like(counts_ref)

    @pl.when(i % tiles_per_seq == 0)
    def _():
        ubuf_ref[0:SUBLANES, :] = jnp.zeros((SUBLANES, d), _F32)

    x = x_ref[...]
    h = _dot(x.astype(_BF16), win_ref[...])
    u = h[:, d:2 * d] * h[:, 2 * d:]
    ubuf_ref[SUBLANES:SUBLANES + tm, :] = u
    cw = cw_ref[...]
    uc = (cw[0:1, :] * ubuf_ref[SUBLANES - 2:SUBLANES - 2 + tm, :]
          + cw[1:2, :] * ubuf_ref[SUBLANES - 1:SUBLANES - 1 + tm, :]
          + cw[2:3, :] * u)
    ubuf_ref[0:SUBLANES, :] = ubuf_ref[tm:tm + SUBLANES, :]
    y = _dot((h[:, :d] * uc).astype(_BF16), wout_ref[...])
    x1 = _layer_norm(alpha * x + y, g_ref[...], b_ref[...])
    x1_ref[...] = x1
    _route_tail(x1, wrh_ref, wrl_ref, rb_ref, route_ref, counts_ref, epg=epg)


def _attn_out_kernel(x_ref, o_ref, wo_ref, g_ref, b_ref, wrh_ref, wrl_ref, rb_ref,
                     x1_ref, route_ref, counts_ref, *, alpha, epg):
    @pl.when(pl.program_id(0) == 0)
    def _():
        counts_ref[...] = jnp.zeros_like(counts_ref)

    y = _dot(o_ref[...], wo_ref[...])
    x1 = _layer_norm(alpha * x_ref[...] + y, g_ref[...], b_ref[...])
    x1_ref[...] = x1
    _route_tail(x1, wrh_ref, wrl_ref, rb_ref, route_ref, counts_ref, epg=epg)


def _row_copy(src_hbm, row, dst_vmem, dst_row, sem):
    return pltpu.make_async_copy(src_hbm.at[pl.ds(row, 1), :], dst_vmem.at[pl.ds(dst_row, 1), :], sem)


def _expert_ffn_kernel(bexp_ref, nused_ref, rowtok_ref, x_hbm, win_ref, wdn_ref, y_ref,
                       xbuf, sem, winb, wdnb):
    i = pl.program_id(0)
    n_used = nused_ref[0]
    blk = xbuf.shape[1]
    de = wdn_ref.shape[1]
    slot = i % 2

    def issue(b, s):
        def body(r, carry):
            _row_copy(x_hbm, rowtok_ref[b * blk + r], xbuf.at[s], r, sem.at[s]).start()
            return carry
        lax.fori_loop(0, blk, body, 0, unroll=ISSUE_UNROLL)

    @pl.when(i == 0)
    def _():
        issue(0, 0)

    @pl.when(i + 1 < n_used)
    def _():
        issue(i + 1, 1 - slot)

    @pl.when(i < n_used)
    def _():
        def wait(r, carry):
            _row_copy(x_hbm, 0, xbuf.at[slot], r, sem.at[slot]).wait()
            return carry
        lax.fori_loop(0, blk, wait, 0, unroll=ISSUE_UNROLL)

        prev = bexp_ref[jnp.maximum(i - 1, 0)]

        @pl.when((i == 0) | (bexp_ref[i] != prev))
        def _():
            winb[...] = win_ref[0].astype(_BF16)
            wdnb[...] = wdn_ref[0].astype(_BF16)

        h = _dot(xbuf[slot].astype(_BF16), winb[...])
        g = h[:, :de]
        a = g * jax.nn.sigmoid(g) * h[:, de:]
        y_ref[...] = _dot(a.astype(_BF16), wdnb[...])

    @pl.when(i >= n_used)
    def _():
        y_ref[...] = jnp.zeros_like(y_ref)


def _combine_body(dest_ref, x1_ref, route_ref, y_hbm, g_ref, b_ref, ybuf, sem, *, alpha):
    i = pl.program_id(0)
    n = pl.num_programs(0)
    tm = x1_ref.shape[0]
    slot = i % 2

    def issue(tile, s):
        def body(r, carry):
            t = tile * tm + r
            _row_copy(y_hbm, dest_ref[t], ybuf.at[s, 0], r, sem.at[s]).start()
            _row_copy(y_hbm, dest_ref[n * tm + t], ybuf.at[s, 1], r, sem.at[s]).start()
            return carry
        lax.fori_loop(0, tm, body, 0, unroll=ISSUE_UNROLL)

    @pl.when(i == 0)
    def _():
        issue(0, 0)

    @pl.when(i + 1 < n)
    def _():
        issue(i + 1, 1 - slot)

    def wait(r, carry):
        _row_copy(y_hbm, 0, ybuf.at[slot, 0], r, sem.at[slot]).wait()
        _row_copy(y_hbm, 0, ybuf.at[slot, 1], r, sem.at[slot]).wait()
        return carry
    lax.fori_loop(0, tm, wait, 0, unroll=ISSUE_UNROLL)

    rec = route_ref[...]
    ffn = rec[:, _G1:_G1 + 1] * ybuf[slot, 0] + rec[:, _G2:_G2 + 1] * ybuf[slot, 1]
    return _layer_norm(alpha * x1_ref[...] + ffn, g_ref[...], b_ref[...])


def _combine_kernel(dest_ref, x1_ref, route_ref, y_hbm, g_ref, b_ref, x2_ref, ybuf, sem, *, alpha):
    x2_ref[...] = _combine_body(dest_ref, x1_ref, route_ref, y_hbm, g_ref, b_ref, ybuf, sem, alpha=alpha)


def _combine_qkv_kernel(dest_ref, x1_ref, route_ref, y_hbm, g_ref, b_ref, wk_ref, wvt_ref, wq_ref,
                        x2_ref, k_ref, vt_ref, q_ref, ybuf, sem, *, alpha, q_scale):
    x2 = _combine_body(dest_ref, x1_ref, route_ref, y_hbm, g_ref, b_ref, ybuf, sem, alpha=alpha)
    x2_ref[...] = x2
    xb = x2.astype(_BF16)
    k_ref[...] = _dot(xb, wk_ref[...]).astype(_BF16)
    q_ref[...] = (_dot(xb, wq_ref[...]) * q_scale).astype(_BF16)
    vt_ref[0] = lax.dot_general(wvt_ref[...], xb, _NT, preferred_element_type=_F32).astype(_BF16)


def _diff_attn_kernel(lam_ref, g_ref, q_ref, k_ref, vt_ref, o_ref, lamfull_ref, s_ref, p_ref, *, lam_init):
    bi, hi = pl.program_id(0), pl.program_id(1)
    seq, hw = k_ref.shape
    tq = tk = ATTN_BLOCK
    nk = seq // tk
    hd = lam_ref.shape[1]

    @pl.when((bi == 0) & (hi == 0))
    def _():
        lam = lam_ref[...]
        a = jnp.sum(lam[0:1, :] * lam[1:2, :], axis=-1, keepdims=True)
        b = jnp.sum(lam[2:3, :] * lam[3:4, :], axis=-1, keepdims=True)
        lamfull_ref[...] = jnp.broadcast_to(jnp.exp(a) - jnp.exp(b) + lam_init, lamfull_ref.shape)

    lam_full = lamfull_ref[0:1, 0:1]
    lane = lax.broadcasted_iota(jnp.int32, (tq, hw), 1)
    causal = (lax.broadcasted_iota(jnp.int32, (tk, tq), 0) <= lax.broadcasted_iota(jnp.int32, (tk, tq), 1))

    for i in range(nk):
        par = i % 2
        q = q_ref[i * tq:(i + 1) * tq, :]
        zero = jnp.zeros_like(q)
        qs = (jnp.where(lane < hd, q, zero), jnp.where(lane >= hd, q, zero))
        outs = []
        for c in range(2):
            colmax = None
            for j in range(i + 1):
                s = lax.dot_general(k_ref[j * tk:(j + 1) * tk, :], qs[c], _NT,
                                    preferred_element_type=_F32)
                if j == i:
                    s = jnp.where(causal, s, -jnp.inf)
                s_ref[par, c, j] = s
                cm = jnp.max(s, axis=0, keepdims=True)
                colmax = cm if colmax is None else jnp.maximum(colmax, cm)
            l = jnp.zeros((1, tq), _F32)
            for j in range(i + 1):
                p = jnp.exp2(s_ref[par, c, j] - colmax)
                l = l + jnp.sum(p, axis=0, keepdims=True)
                p_ref[par, c, j * tk:(j + 1) * tk, :] = p.astype(_BF16)
            kv = (i + 1) * tk
            outs.append(_dot(vt_ref[0, :, 0:kv], p_ref[par, c, 0:kv, :]) * (1.0 / l))
        ot = outs[0] - lam_full * outs[1]
        ot = ot * lax.rsqrt(jnp.mean(ot * ot, axis=0, keepdims=True) + RMS_EPS)
        o_ref[i * tq:(i + 1) * tq, :] = (ot.T * g_ref[...] * (1.0 - lam_init)).astype(o_ref.dtype)


def _params(*sem):
    return pltpu.CompilerParams(dimension_semantics=sem, vmem_limit_bytes=VMEM_LIMIT_BYTES)


def _full(shape):
    return pl.BlockSpec(shape, lambda *_: (0,) * len(shape))


def _router_weights(rg_w, rg_b, re_w, re_b):
    d = rg_w.shape[0]
    n = rg_w.shape[1] + re_w.shape[1]
    w = jnp.concatenate([rg_w, re_w, jnp.zeros((d, LANES - n), _F32)], axis=1)
    b = jnp.concatenate([rg_b, re_b, jnp.zeros((LANES - n,), _F32)])[None, :]
    wh = w.astype(_BF16)
    wl = (w - wh.astype(_F32)).astype(_BF16)
    return wh, wl, b


def _route_outs(t, d, tm):
    shapes = (jax.ShapeDtypeStruct((t, d), _F32),
              jax.ShapeDtypeStruct((t, LANES), _F32),
              jax.ShapeDtypeStruct((1, LANES), _F32))
    specs = (pl.BlockSpec((tm, d), lambda i: (i, 0)),
             pl.BlockSpec((tm, LANES), lambda i: (i, 0)),
             pl.BlockSpec((1, LANES), lambda i: (0, 0)))
    return shapes, specs


def _conv_mixer(x, w_in, conv_w, w_out, ln_g, ln_b, router, *, seq, alpha, epg):
    t, d = x.shape
    tm = ROW_TILE
    wrh, wrl, rb = router
    shapes, specs = _route_outs(t, d, tm)
    return pl.pallas_call(
        functools.partial(_conv_mixer_kernel, tiles_per_seq=seq // tm, alpha=alpha, epg=epg),
        grid=(t // tm,),
        in_specs=[pl.BlockSpec((tm, d), lambda i: (i, 0)),
                  _full(w_in.shape), _full(conv_w.shape), _full(w_out.shape),
                  _full((1, d)), _full((1, d)),
                  _full(wrh.shape), _full(wrl.shape), _full(rb.shape)],
        out_specs=specs, out_shape=shapes,
        scratch_shapes=[pltpu.VMEM((tm + SUBLANES, d), _F32)],
        compiler_params=_params("arbitrary"),
        name="conv_mixer_ln_router",
    )(x, w_in, conv_w, w_out, ln_g[None, :], ln_b[None, :], wrh, wrl, rb)


def _attn_out(x, o, w_o, ln_g, ln_b, router, *, alpha, epg):
    t, d = x.shape
    tm = ROW_TILE
    wrh, wrl, rb = router
    shapes, specs = _route_outs(t, d, tm)
    return pl.pallas_call(
        functools.partial(_attn_out_kernel, alpha=alpha, epg=epg),
        grid=(t // tm,),
        in_specs=[pl.BlockSpec((tm, d), lambda i: (i, 0)),
                  pl.BlockSpec((tm, d), lambda i: (i, 0)),
                  _full(w_o.shape), _full((1, d)), _full((1, d)),
                  _full(wrh.shape), _full(wrl.shape), _full(rb.shape)],
        out_specs=specs, out_shape=shapes,
        compiler_params=_params("arbitrary"),
        name="attn_out_ln_router",
    )(x, o, w_o, ln_g[None, :], ln_b[None, :], wrh, wrl, rb)


def _dispatch_plan(route, counts, n_experts):
    t = route.shape[0]
    blk = EXPERT_BLOCK
    cnt = counts[0, :n_experts].astype(jnp.int32)
    padded = (cnt + blk - 1) // blk * blk
    pad_end = jnp.cumsum(padded)
    pad_start = pad_end - padded
    ids = jnp.arange(n_experts, dtype=jnp.int32)

    def sorted_row(e_slot, r_slot):
        e = route[:, e_slot].astype(jnp.int32)
        start = jnp.sum(jnp.where(e[:, None] == ids[None, :], pad_start[None, :], 0), axis=1)
        return start + route[:, r_slot].astype(jnp.int32)

    dest = jnp.concatenate([sorted_row(_E1, _R1), sorted_row(_E2, _R2)])
    n_rows = 2 * t + n_experts * blk
    n_blocks = n_rows // blk
    tok = jnp.arange(t, dtype=jnp.int32)
    row_token = jnp.zeros((n_rows,), jnp.int32).at[dest].set(jnp.concatenate([tok, tok]))
    block_start = jnp.arange(n_blocks, dtype=jnp.int32) * blk
    block_expert = jnp.minimum(jnp.sum(pad_end[None, :] <= block_start[:, None], axis=1),
                               n_experts - 1).astype(jnp.int32)
    n_used = (pad_end[-1:] // blk).astype(jnp.int32)
    return dest, row_token, block_expert, n_used


def _expert_ffn(x1, w_in, w_down, layer, row_token, block_expert, n_used):
    t, d = x1.shape
    _, n_exp, _, de2 = w_in.shape
    de = w_down.shape[2]
    blk = EXPERT_BLOCK
    n_rows = row_token.shape[0]
    return pl.pallas_call(
        _expert_ffn_kernel,
        grid_spec=pltpu.PrefetchScalarGridSpec(
            num_scalar_prefetch=3,
            grid=(n_rows // blk,),
            in_specs=[pl.BlockSpec(memory_space=pl.ANY),
                      pl.BlockSpec((None, 1, d, de2), lambda i, be, nu, rt: (layer, be[i], 0, 0)),
                      pl.BlockSpec((None, 1, de, d), lambda i, be, nu, rt: (layer, be[i], 0, 0))],
            out_specs=pl.BlockSpec((blk, d), lambda i, be, nu, rt: (i, 0)),
            scratch_shapes=[pltpu.VMEM((2, blk, d), _F32),
                            pltpu.SemaphoreType.DMA((2,)),
                            pltpu.VMEM((d, de2), _BF16),
                            pltpu.VMEM((de, d), _BF16)]),
        out_shape=jax.ShapeDtypeStruct((n_rows, d), _F32),
        compiler_params=_params("arbitrary"),
        name="expert_ffn",
    )(block_expert, n_used, row_token, x1, w_in, w_down)


def _combine_scratch(tm, d):
    return [pltpu.VMEM((2, 2, tm, d), _F32), pltpu.SemaphoreType.DMA((2,))]


def _combine(dest, x1, route, y_rows, ln_g, ln_b, *, alpha):
    t, d = x1.shape
    tm = ROW_TILE
    return pl.pallas_call(
        functools.partial(_combine_kernel, alpha=alpha),
        grid_spec=pltpu.PrefetchScalarGridSpec(
            num_scalar_prefetch=1,
            grid=(t // tm,),
            in_specs=[pl.BlockSpec((tm, d), lambda i, ds: (i, 0)),
                      pl.BlockSpec((tm, LANES), lambda i, ds: (i, 0)),
                      pl.BlockSpec(memory_space=pl.ANY),
                      pl.BlockSpec((1, d), lambda i, ds: (0, 0)),
                      pl.BlockSpec((1, d), lambda i, ds: (0, 0))],
            out_specs=pl.BlockSpec((tm, d), lambda i, ds: (i, 0)),
            scratch_shapes=_combine_scratch(tm, d)),
        out_shape=jax.ShapeDtypeStruct((t, d), _F32),
        compiler_params=_params("arbitrary"),
        name="combine_ln",
    )(dest, x1, route, y_rows, ln_g[None, :], ln_b[None, :])


def _combine_qkv(dest, x1, route, y_rows, ln_g, ln_b, w_k, w_vt, w_q, *, seq, alpha, q_scale):
    t, d = x1.shape
    tm = ROW_TILE
    per_seq = seq // tm
    wspec = pl.BlockSpec((d, d), lambda i, ds: (0, 0))
    row = pl.BlockSpec((tm, d), lambda i, ds: (i, 0))
    return pl.pallas_call(
        functools.partial(_combine_qkv_kernel, alpha=alpha, q_scale=q_scale),
        grid_spec=pltpu.PrefetchScalarGridSpec(
            num_scalar_prefetch=1,
            grid=(t // tm,),
            in_specs=[row,
                      pl.BlockSpec((tm, LANES), lambda i, ds: (i, 0)),
                      pl.BlockSpec(memory_space=pl.ANY),
                      pl.BlockSpec((1, d), lambda i, ds: (0, 0)),
                      pl.BlockSpec((1, d), lambda i, ds: (0, 0)),
                      wspec, wspec, wspec],
            out_specs=(row, row,
                       pl.BlockSpec((1, d, tm), lambda i, ds: (i // per_seq, 0, i % per_seq)),
                       row),
            scratch_shapes=_combine_scratch(tm, d)),
        out_shape=(jax.ShapeDtypeStruct((t, d), _F32),
                   jax.ShapeDtypeStruct((t, d), _BF16),
                   jax.ShapeDtypeStruct((t // seq, d, seq), _BF16),
                   jax.ShapeDtypeStruct((t, d), _BF16)),
        compiler_params=_params("arbitrary"),
        name="combine_ln_qkv",
    )(dest, x1, route, y_rows, ln_g[None, :], ln_b[None, :], w_k, w_vt, w_q)


def _diff_attention(q, k, vt, lam, subln_g, *, batch, seq, lam_init):
    t, d = q.shape
    tq = ATTN_BLOCK
    hw = d // N_HEADS
    head = pl.BlockSpec((seq, hw), lambda b, h: (b, h))
    return pl.pallas_call(
        functools.partial(_diff_attn_kernel, lam_init=lam_init),
        grid=(batch, N_HEADS),
        in_specs=[_full(lam.shape), _full((1, hw)), head, head,
                  pl.BlockSpec((1, hw, seq), lambda b, h: (b, h, 0))],
        out_specs=head,
        out_shape=jax.ShapeDtypeStruct((t, d), _BF16),
        scratch_shapes=[pltpu.VMEM((1, LANES), _F32),
                        pltpu.VMEM((2, 2, seq // tq, tq, tq), _F32),
                        pltpu.VMEM((2, 2, seq, tq), _BF16)],
        compiler_params=_params("arbitrary", "arbitrary"),
        name="diff_attention",
    )(lam, subln_g[None, :], q, k, vt)


def _moe(x1, route, counts, w_in, w_down, layer):
    n_exp = w_in.shape[1]
    dest, row_token, block_expert, n_used = _dispatch_plan(route, counts, n_exp)
    y_rows = _expert_ffn(x1, w_in, w_down, layer, row_token, block_expert, n_used)
    return dest, y_rows


def kernel(x, a_w_in, a_conv_w, a_w_out, kv_w, b_w_q, b_lambda, b_subln_g, b_w_o,
           ln1_g, ln1_b, ln2_g, ln2_b, rg_w, rg_b, re_w, re_b, e_w_in, e_w_down):
    batch, seq, d = x.shape
    depth = ln1_g.shape[0]
    assert depth == 2 and a_w_in.shape[0] == 1 and b_w_q.shape[0] == 1
    assert seq % ROW_TILE == 0 and seq % ATTN_BLOCK == 0 and ROW_TILE % ATTN_BLOCK == 0
    t = batch * seq
    alpha = (2.0 * depth) ** 0.25
    epg = re_w.shape[2] // N_GROUPS
    head_dim = d // (2 * N_HEADS)
    xf = x.reshape(t, d)

    router0 = _router_weights(rg_w[0], rg_b[0], re_w[0], re_b[0])
    x1, route, counts = _conv_mixer(xf, a_w_in[0].astype(_BF16), a_conv_w[0], a_w_out[0].astype(_BF16),
                                    ln1_g[0], ln1_b[0], router0, seq=seq, alpha=alpha, epg=epg)
    dest, y_rows = _moe(x1, route, counts, e_w_in, e_w_down, 0)
    q_scale = head_dim ** -0.5 * math.log2(math.e)
    x2, k, vt, q = _combine_qkv(dest, x1, route, y_rows, ln2_g[0], ln2_b[0],
                                kv_w[:, :d].astype(_BF16), kv_w[:, d:].T.astype(_BF16),
                                b_w_q[0].astype(_BF16), seq=seq, alpha=alpha, q_scale=q_scale)

    lam_init = 0.8 - 0.6 * math.exp(-0.3 * 1)
    o = _diff_attention(q, k, vt, b_lambda[0], b_subln_g[0], batch=batch, seq=seq, lam_init=lam_init)
    router1 = _router_weights(rg_w[1], rg_b[1], re_w[1], re_b[1])
    x1, route, counts = _attn_out(x2, o, b_w_o[0].astype(_BF16), ln1_g[1], ln1_b[1], router1,
                                  alpha=alpha, epg=epg)
    dest, y_rows = _moe(x1, route, counts, e_w_in, e_w_down, 1)
    out = _combine(dest, x1, route, y_rows, ln2_g[1], ln2_b[1], alpha=alpha)
    return out.reshape(batch, seq, d)
```

```python
import functools
import math

import jax
import jax.numpy as jnp
from jax import lax
from jax.experimental import pallas as pl
from jax.experimental.pallas import tpu as pltpu

N_HEADS = 8
N_GROUPS = 4
LN_EPS = 1e-5
RMS_EPS = 1e-5

LANES = 128
SUBLANES = 8
VMEM_LIMIT_BYTES = 56 * 1024 * 1024

ROW_TILE = 512
EXPERT_BLOCK = 256
ATTN_BLOCK = 256
ISSUE_UNROLL = 8

_E1, _E2, _R1, _R2, _G1, _G2 = range(6)

_F32 = jnp.float32
_BF16 = jnp.bfloat16
_NT = (((1,), (1,)), ((), ()))


def _dot(a, b):
    return jnp.dot(a, b, preferred_element_type=_F32)


def _layer_norm(z, g, b):
    mu = jnp.mean(z, axis=-1, keepdims=True)
    d = z - mu
    var = jnp.mean(d * d, axis=-1, keepdims=True)
    return d * lax.rsqrt(var + LN_EPS) * g + b


def _route_tail(x1, wrh_ref, wrl_ref, rb_ref, route_ref, counts_ref, *, epg):
    tm = x1.shape[0]
    xh = x1.astype(_BF16)
    xl = (x1 - xh.astype(_F32)).astype(_BF16)
    wh = wrh_ref[...]
    logits = _dot(xh, wh) + _dot(xl, wh) + _dot(xh, wrl_ref[...]) + rb_ref[...]

    lane = lax.broadcasted_iota(jnp.int32, (tm, LANES), 1)
    lanef = lane.astype(_F32)
    neg = -jnp.inf
    gl = jnp.where(lane < N_GROUPS, logits, neg)
    gmax = jnp.max(gl, axis=-1, keepdims=True)
    gidx = jnp.min(jnp.where(gl == gmax, lanef, float(LANES)), axis=-1, keepdims=True)
    gtop = 1.0 / jnp.sum(jnp.exp(gl - gmax), axis=-1, keepdims=True)

    lo = float(N_GROUPS) + gidx * float(epg)
    el = jnp.where((lanef >= lo) & (lanef < lo + float(epg)), logits, neg)
    m1 = jnp.max(el, axis=-1, keepdims=True)
    i1 = jnp.min(jnp.where(el == m1, lanef, float(LANES)), axis=-1, keepdims=True)
    el2 = jnp.where(lanef == i1, neg, el)
    m2 = jnp.max(el2, axis=-1, keepdims=True)
    i2 = jnp.min(jnp.where(el2 == m2, lanef, float(LANES)), axis=-1, keepdims=True)
    w2 = jnp.exp(m2 - m1)
    inv = 1.0 / (1.0 + w2)
    g1 = gtop * inv
    g2 = gtop * w2 * inv
    e1 = i1 - float(N_GROUPS)
    e2 = i2 - float(N_GROUPS)

    onehot = jnp.where((lanef == e1) | (lanef == e2), 1.0, 0.0)
    r = lax.broadcasted_iota(jnp.int32, (tm, tm), 0)
    c = lax.broadcasted_iota(jnp.int32, (tm, tm), 1)
    ltri = jnp.where(c < r, 1.0, 0.0).astype(_BF16)
    prefix = _dot(ltri, onehot.astype(_BF16)) + counts_ref[...]
    r1 = jnp.sum(jnp.where(lanef == e1, prefix, 0.0), axis=-1, keepdims=True)
    r2 = jnp.sum(jnp.where(lanef == e2, prefix, 0.0), axis=-1, keepdims=True)
    counts_ref[...] += jnp.sum(onehot, axis=0, keepdims=True)

    rec = jnp.zeros((tm, LANES), _F32)
    for slot, val in ((_E1, e1), (_E2, e2), (_R1, r1), (_R2, r2), (_G1, g1), (_G2, g2)):
        rec = jnp.where(lane == slot, val, rec)
    route_ref[...] = rec


def _store_token_tiles(ref, x):
    n, d = x.shape
    per = d // LANES
    for s in range(per):
        ref[pl.ds(s, n, stride=per), :] = x[:, s * LANES:(s + 1) * LANES]


def _load_token_tiles(ref, first_token, n, d):
    per = d // LANES
    start = first_token * per
    return jnp.concatenate([ref[pl.ds(start + s, n, stride=per), :] for s in range(per)], axis=1)


def _tile_copy(src_hbm, token, dst_vmem, dst_token, sem):
    rows = pl.ds(pl.multiple_of(dst_token * SUBLANES, SUBLANES), SUBLANES)
    return pltpu.make_async_copy(src_hbm.at[token], dst_vmem.at[rows, :], sem)


def _conv_mixer_kernel(x_ref, win_ref, cw_ref, wout_ref, g_ref, b_ref, wrh_ref, wrl_ref, rb_ref,
                       x1_ref, x1t_ref, route_ref, counts_ref, ubuf_ref, *, tiles_per_seq, alpha, epg):
    i = pl.program_id(0)
    tm, d = x_ref.shape

    @pl.when(i == 0)
    def _():
        counts_ref[...] = jnp.zeros_like(counts_ref)

    @pl.when(i % tiles_per_seq == 0)
    def _():
        ubuf_ref[0:SUBLANES, :] = jnp.zeros((SUBLANES, d), _F32)

    x = x_ref[...]
    h = _dot(x.astype(_BF16), win_ref[...])
    u = h[:, d:2 * d] * h[:, 2 * d:]
    ubuf_ref[SUBLANES:SUBLANES + tm, :] = u
    cw = cw_ref[...]
    uc = (cw[0:1, :] * ubuf_ref[SUBLANES - 2:SUBLANES - 2 + tm, :]
          + cw[1:2, :] * ubuf_ref[SUBLANES - 1:SUBLANES - 1 + tm, :]
          + cw[2:3, :] * u)
    ubuf_ref[0:SUBLANES, :] = ubuf_ref[tm:tm + SUBLANES, :]
    y = _dot((h[:, :d] * uc).astype(_BF16), wout_ref[...])
    x1 = _layer_norm(alpha * x + y, g_ref[...], b_ref[...])
    x1_ref[...] = x1
    _store_token_tiles(x1t_ref, x1)
    _route_tail(x1, wrh_ref, wrl_ref, rb_ref, route_ref, counts_ref, epg=epg)


def _attn_out_kernel(x_ref, o_ref, wo_ref, g_ref, b_ref, wrh_ref, wrl_ref, rb_ref,
                     x1_ref, x1t_ref, route_ref, counts_ref, *, alpha, epg):
    @pl.when(pl.program_id(0) == 0)
    def _():
        counts_ref[...] = jnp.zeros_like(counts_ref)

    y = _dot(o_ref[...], wo_ref[...])
    x1 = _layer_norm(alpha * x_ref[...] + y, g_ref[...], b_ref[...])
    x1_ref[...] = x1
    _store_token_tiles(x1t_ref, x1)
    _route_tail(x1, wrh_ref, wrl_ref, rb_ref, route_ref, counts_ref, epg=epg)


def _expert_ffn_kernel(bexp_ref, nused_ref, dest_ref, x_hbm, win_ref, wdn_ref, y_ref,
                       rowtok, xbuf, sem, winb, wdnb):
    i = pl.program_id(0)
    n_used = nused_ref[0]
    blk = y_ref.shape[0] // SUBLANES
    d, de2 = win_ref.shape[1:]
    de = de2 // 2
    n_tokens = dest_ref.shape[0] // 2
    slot = i % 2

    def issue(b, s):
        def body(r, carry):
            _tile_copy(x_hbm, rowtok[b * blk + r], xbuf, s * blk + r, sem.at[s]).start()
            return carry
        lax.fori_loop(0, blk, body, 0, unroll=ISSUE_UNROLL)

    @pl.when(i == 0)
    def _():
        def fill(j, carry):
            rowtok[j] = 0
            return carry
        lax.fori_loop(0, rowtok.shape[0], fill, 0, unroll=ISSUE_UNROLL)

        def invert(t, carry):
            rowtok[dest_ref[t]] = t
            rowtok[dest_ref[n_tokens + t]] = t
            return carry
        lax.fori_loop(0, n_tokens, invert, 0, unroll=ISSUE_UNROLL)
        issue(0, 0)

    @pl.when(i + 1 < n_used)
    def _():
        issue(i + 1, 1 - slot)

    @pl.when(i < n_used)
    def _():
        def wait(r, carry):
            _tile_copy(x_hbm, 0, xbuf, slot * blk + r, sem.at[slot]).wait()
            return carry
        lax.fori_loop(0, blk, wait, 0, unroll=ISSUE_UNROLL)

        prev = bexp_ref[jnp.maximum(i - 1, 0)]

        @pl.when((i == 0) | (bexp_ref[i] != prev))
        def _():
            winb[...] = win_ref[0].astype(_BF16)
            wdnb[...] = wdn_ref[0].astype(_BF16)

        x = _load_token_tiles(xbuf, slot * blk, blk, d)
        h = _dot(x.astype(_BF16), winb[...])
        g = h[:, :de]
        a = g * jax.nn.sigmoid(g) * h[:, de:]
        _store_token_tiles(y_ref, _dot(a.astype(_BF16), wdnb[...]))

    @pl.when(i >= n_used)
    def _():
        y_ref[...] = jnp.zeros_like(y_ref)


def _combine_body(dest_ref, x1_ref, route_ref, y_hbm, g_ref, b_ref, ybuf, sem, *, alpha):
    i = pl.program_id(0)
    n = pl.num_programs(0)
    tm, d = x1_ref.shape
    slot = i % 2

    def issue(tile, s):
        def body(r, carry):
            t = tile * tm + r
            _tile_copy(y_hbm, dest_ref[t], ybuf, (2 * s) * tm + r, sem.at[s]).start()
            _tile_copy(y_hbm, dest_ref[n * tm + t], ybuf, (2 * s + 1) * tm + r, sem.at[s]).start()
            return carry
        lax.fori_loop(0, tm, body, 0, unroll=ISSUE_UNROLL)

    @pl.when(i == 0)
    def _():
        issue(0, 0)

    @pl.when(i + 1 < n)
    def _():
        issue(i + 1, 1 - slot)

    def wait(r, carry):
        _tile_copy(y_hbm, 0, ybuf, (2 * slot) * tm + r, sem.at[slot]).wait()
        _tile_copy(y_hbm, 0, ybuf, (2 * slot + 1) * tm + r, sem.at[slot]).wait()
        return carry
    lax.fori_loop(0, tm, wait, 0, unroll=ISSUE_UNROLL)

    rec = route_ref[...]
    ffn = (rec[:, _G1:_G1 + 1] * _load_token_tiles(ybuf, (2 * slot) * tm, tm, d)
           + rec[:, _G2:_G2 + 1] * _load_token_tiles(ybuf, (2 * slot + 1) * tm, tm, d))
    return _layer_norm(alpha * x1_ref[...] + ffn, g_ref[...], b_ref[...])


def _combine_kernel(dest_ref, x1_ref, route_ref, y_hbm, g_ref, b_ref, x2_ref, ybuf, sem, *, alpha):
    x2_ref[...] = _combine_body(dest_ref, x1_ref, route_ref, y_hbm, g_ref, b_ref, ybuf, sem, alpha=alpha)


def _combine_qkv_kernel(dest_ref, x1_ref, route_ref, y_hbm, g_ref, b_ref, wk_ref, wvt_ref, wq_ref,
                        x2_ref, k_ref, vt_ref, q_ref, ybuf, sem, *, alpha, q_scale):
    x2 = _combine_body(dest_ref, x1_ref, route_ref, y_hbm, g_ref, b_ref, ybuf, sem, alpha=alpha)
    x2_ref[...] = x2
    xb = x2.astype(_BF16)
    k_ref[...] = _dot(xb, wk_ref[...]).astype(_BF16)
    q_ref[...] = (_dot(xb, wq_ref[...]) * q_scale).astype(_BF16)
    vt_ref[0] = lax.dot_general(wvt_ref[...], xb, _NT, preferred_element_type=_F32).astype(_BF16)


def _diff_attn_kernel(lam_ref, g_ref, q_ref, k_ref, vt_ref, o_ref, lamfull_ref, s_ref, p_ref, *, lam_init):
    bi, hi = pl.program_id(0), pl.program_id(1)
    seq, hw = k_ref.shape
    tq = tk = ATTN_BLOCK
    nk = seq // tk
    hd = lam_ref.shape[1]

    @pl.when((bi == 0) & (hi == 0))
    def _():
        lam = lam_ref[...]
        a = jnp.sum(lam[0:1, :] * lam[1:2, :], axis=-1, keepdims=True)
        b = jnp.sum(lam[2:3, :] * lam[3:4, :], axis=-1, keepdims=True)
        lamfull_ref[...] = jnp.broadcast_to(jnp.exp(a) - jnp.exp(b) + lam_init, lamfull_ref.shape)

    lam_full = lamfull_ref[0:1, 0:1]
    lane = lax.broadcasted_iota(jnp.int32, (tq, hw), 1)
    causal = (lax.broadcasted_iota(jnp.int32, (tk, tq), 0) <= lax.broadcasted_iota(jnp.int32, (tk, tq), 1))

    for i in range(nk):
        par = i % 2
        q = q_ref[i * tq:(i + 1) * tq, :]
        zero = jnp.zeros_like(q)
        qs = (jnp.where(lane < hd, q, zero), jnp.where(lane >= hd, q, zero))
        outs = []
        for c in range(2):
            colmax = None
            for j in range(i + 1):
                s = lax.dot_general(k_ref[j * tk:(j + 1) * tk, :], qs[c], _NT,
                                    preferred_element_type=_F32)
                if j == i:
                    s = jnp.where(causal, s, -jnp.inf)
                s_ref[par, c, j] = s
                cm = jnp.max(s, axis=0, keepdims=True)
                colmax = cm if colmax is None else jnp.maximum(colmax, cm)
            l = jnp.zeros((1, tq), _F32)
            for j in range(i + 1):
                p = jnp.exp2(s_ref[par, c, j] - colmax)
                l = l + jnp.sum(p, axis=0, keepdims=True)
                p_ref[par, c, j * tk:(j + 1) * tk, :] = p.astype(_BF16)
            kv = (i + 1) * tk
            outs.append(_dot(vt_ref[0, :, 0:kv], p_ref[par, c, 0:kv, :]) * (1.0 / l))
        ot = outs[0] - lam_full * outs[1]
        ot = ot * lax.rsqrt(jnp.mean(ot * ot, axis=0, keepdims=True) + RMS_EPS)
        o_ref[i * tq:(i + 1) * tq, :] = (ot.T * g_ref[...] * (1.0 - lam_init)).astype(o_ref.dtype)


def _params(*sem):
    return pltpu.CompilerParams(dimension_semantics=sem, vmem_limit_bytes=VMEM_LIMIT_BYTES)


def _full(shape):
    return pl.BlockSpec(shape, lambda *_: (0,) * len(shape))


def _router_weights(rg_w, rg_b, re_w, re_b):
    d = rg_w.shape[0]
    n = rg_w.shape[1] + re_w.shape[1]
    w = jnp.concatenate([rg_w, re_w, jnp.zeros((d, LANES - n), _F32)], axis=1)
    b = jnp.concatenate([rg_b, re_b, jnp.zeros((LANES - n,), _F32)])[None, :]
    wh = w.astype(_BF16)
    wl = (w - wh.astype(_F32)).astype(_BF16)
    return wh, wl, b


def _route_outs(t, d, tm):
    per = d // LANES
    shapes = (jax.ShapeDtypeStruct((t, d), _F32),
              jax.ShapeDtypeStruct((t * per, LANES), _F32),
              jax.ShapeDtypeStruct((t, LANES), _F32),
              jax.ShapeDtypeStruct((1, LANES), _F32))
    specs = (pl.BlockSpec((tm, d), lambda i: (i, 0)),
             pl.BlockSpec((tm * per, LANES), lambda i: (i, 0)),
             pl.BlockSpec((tm, LANES), lambda i: (i, 0)),
             pl.BlockSpec((1, LANES), lambda i: (0, 0)))
    return shapes, specs


def _conv_mixer(x, w_in, conv_w, w_out, ln_g, ln_b, router, *, seq, alpha, epg):
    t, d = x.shape
    tm = ROW_TILE
    wrh, wrl, rb = router
    shapes, specs = _route_outs(t, d, tm)
    return pl.pallas_call(
        functools.partial(_conv_mixer_kernel, tiles_per_seq=seq // tm, alpha=alpha, epg=epg),
        grid=(t // tm,),
        in_specs=[pl.BlockSpec((tm, d), lambda i: (i, 0)),
                  _full(w_in.shape), _full(conv_w.shape), _full(w_out.shape),
                  _full((1, d)), _full((1, d)),
                  _full(wrh.shape), _full(wrl.shape), _full(rb.shape)],
        out_specs=specs, out_shape=shapes,
        scratch_shapes=[pltpu.VMEM((tm + SUBLANES, d), _F32)],
        compiler_params=_params("arbitrary"),
        name="conv_mixer_ln_router",
    )(x, w_in, conv_w, w_out, ln_g[None, :], ln_b[None, :], wrh, wrl, rb)


def _attn_out(x, o, w_o, ln_g, ln_b, router, *, alpha, epg):
    t, d = x.shape
    tm = ROW_TILE
    wrh, wrl, rb = router
    shapes, specs = _route_outs(t, d, tm)
    return pl.pallas_call(
        functools.partial(_attn_out_kernel, alpha=alpha, epg=epg),
        grid=(t // tm,),
        in_specs=[pl.BlockSpec((tm, d), lambda i: (i, 0)),
                  pl.BlockSpec((tm, d), lambda i: (i, 0)),
                  _full(w_o.shape), _full((1, d)), _full((1, d)),
                  _full(wrh.shape), _full(wrl.shape), _full(rb.shape)],
        out_specs=specs, out_shape=shapes,
        compiler_params=_params("arbitrary"),
        name="attn_out_ln_router",
    )(x, o, w_o, ln_g[None, :], ln_b[None, :], wrh, wrl, rb)


def _dispatch_plan(route, counts, n_experts):
    t = route.shape[0]
    blk = EXPERT_BLOCK
    cnt = counts[0, :n_experts].astype(jnp.int32)
    padded = (cnt + blk - 1) // blk * blk
    pad_end = jnp.cumsum(padded)
    pad_start = pad_end - padded
    ids = jnp.arange(n_experts, dtype=jnp.int32)

    def sorted_row(e_slot, r_slot):
        e = route[:, e_slot].astype(jnp.int32)
        start = jnp.sum(jnp.where(e[:, None] == ids[None, :], pad_start[None, :], 0), axis=1)
        return start + route[:, r_slot].astype(jnp.int32)

    dest = jnp.concatenate([sorted_row(_E1, _R1), sorted_row(_E2, _R2)])
    n_blocks = (2 * t + n_experts * blk) // blk
    block_start = jnp.arange(n_blocks, dtype=jnp.int32) * blk
    block_expert = jnp.minimum(jnp.sum(pad_end[None, :] <= block_start[:, None], axis=1),
                               n_experts - 1).astype(jnp.int32)
    n_used = (pad_end[-1:] // blk).astype(jnp.int32)
    return dest, block_expert, n_used


def _expert_ffn(x1t, w_in, w_down, layer, dest, block_expert, n_used):
    per = x1t.shape[1]
    _, n_exp, d, de2 = w_in.shape
    de = w_down.shape[2]
    blk = EXPERT_BLOCK
    n_blocks = block_expert.shape[0]
    n_rows = n_blocks * blk
    y = pl.pallas_call(
        _expert_ffn_kernel,
        grid_spec=pltpu.PrefetchScalarGridSpec(
            num_scalar_prefetch=3,
            grid=(n_blocks,),
            in_specs=[pl.BlockSpec(memory_space=pl.ANY),
                      pl.BlockSpec((None, 1, d, de2), lambda i, be, nu, ds: (layer, be[i], 0, 0)),
                      pl.BlockSpec((None, 1, de, d), lambda i, be, nu, ds: (layer, be[i], 0, 0))],
            out_specs=pl.BlockSpec((blk * per, LANES), lambda i, be, nu, ds: (i, 0)),
            scratch_shapes=[pltpu.SMEM((n_rows,), jnp.int32),
                            pltpu.VMEM((2 * blk * per, LANES), _F32),
                            pltpu.SemaphoreType.DMA((2,)),
                            pltpu.VMEM((d, de2), _BF16),
                            pltpu.VMEM((de, d), _BF16)]),
        out_shape=jax.ShapeDtypeStruct((n_rows * per, LANES), _F32),
        compiler_params=_params("arbitrary"),
        name="expert_ffn",
    )(block_expert, n_used, dest, x1t, w_in, w_down)
    return y.reshape(n_rows, per, LANES)


def _combine_scratch(tm, d):
    return [pltpu.VMEM((2 * 2 * tm * (d // LANES), LANES), _F32), pltpu.SemaphoreType.DMA((2,))]


def _combine(dest, x1, route, y_rows, ln_g, ln_b, *, alpha):
    t, d = x1.shape
    tm = ROW_TILE
    return pl.pallas_call(
        functools.partial(_combine_kernel, alpha=alpha),
        grid_spec=pltpu.PrefetchScalarGridSpec(
            num_scalar_prefetch=1,
            grid=(t // tm,),
            in_specs=[pl.BlockSpec((tm, d), lambda i, ds: (i, 0)),
                      pl.BlockSpec((tm, LANES), lambda i, ds: (i, 0)),
                      pl.BlockSpec(memory_space=pl.ANY),
                      pl.BlockSpec((1, d), lambda i, ds: (0, 0)),
                      pl.BlockSpec((1, d), lambda i, ds: (0, 0))],
            out_specs=pl.BlockSpec((tm, d), lambda i, ds: (i, 0)),
            scratch_shapes=_combine_scratch(tm, d)),
        out_shape=jax.ShapeDtypeStruct((t, d), _F32),
        compiler_params=_params("arbitrary"),
        name="combine_ln",
    )(dest, x1, route, y_rows, ln_g[None, :], ln_b[None, :])


def _combine_qkv(dest, x1, route, y_rows, ln_g, ln_b, w_k, w_vt, w_q, *, seq, alpha, q_scale):
    t, d = x1.shape
    tm = ROW_TILE
    per_seq = seq // tm
    wspec = pl.BlockSpec((d, d), lambda i, ds: (0, 0))
    row = pl.BlockSpec((tm, d), lambda i, ds: (i, 0))
    return pl.pallas_call(
        functools.partial(_combine_qkv_kernel, alpha=alpha, q_scale=q_scale),
        grid_spec=pltpu.PrefetchScalarGridSpec(
            num_scalar_prefetch=1,
            grid=(t // tm,),
            in_specs=[row,
                      pl.BlockSpec((tm, LANES), lambda i, ds: (i, 0)),
                      pl.BlockSpec(memory_space=pl.ANY),
                      pl.BlockSpec((1, d), lambda i, ds: (0, 0)),
                      pl.BlockSpec((1, d), lambda i, ds: (0, 0)),
                      wspec, wspec, wspec],
            out_specs=(row, row,
                       pl.BlockSpec((1, d, tm), lambda i, ds: (i // per_seq, 0, i % per_seq)),
                       row),
            scratch_shapes=_combine_scratch(tm, d)),
        out_shape=(jax.ShapeDtypeStruct((t, d), _F32),
                   jax.ShapeDtypeStruct((t, d), _BF16),
                   jax.ShapeDtypeStruct((t // seq, d, seq), _BF16),
                   jax.ShapeDtypeStruct((t, d), _BF16)),
        compiler_params=_params("arbitrary"),
        name="combine_ln_qkv",
    )(dest, x1, route, y_rows, ln_g[None, :], ln_b[None, :], w_k, w_vt, w_q)


def _diff_attention(q, k, vt, lam, subln_g, *, batch, seq, lam_init):
    t, d = q.shape
    tq = ATTN_BLOCK
    hw = d // N_HEADS
    head = pl.BlockSpec((seq, hw), lambda b, h: (b, h))
    return pl.pallas_call(
        functools.partial(_diff_attn_kernel, lam_init=lam_init),
        grid=(batch, N_HEADS),
        in_specs=[_full(lam.shape), _full((1, hw)), head, head,
                  pl.BlockSpec((1, hw, seq), lambda b, h: (b, h, 0))],
        out_specs=head,
        out_shape=jax.ShapeDtypeStruct((t, d), _BF16),
        scratch_shapes=[pltpu.VMEM((1, LANES), _F32),
                        pltpu.VMEM((2, 2, seq // tq, tq, tq), _F32),
                        pltpu.VMEM((2, 2, seq, tq), _BF16)],
        compiler_params=_params("arbitrary", "arbitrary"),
        name="diff_attention",
    )(lam, subln_g[None, :], q, k, vt)


def _moe(x1t, route, counts, w_in, w_down, layer):
    n_exp, d = w_in.shape[1:3]
    dest, block_expert, n_used = _dispatch_plan(route, counts, n_exp)
    x1t = x1t.reshape(-1, d // LANES, LANES)
    y_rows = _expert_ffn(x1t, w_in, w_down, layer, dest, block_expert, n_used)
    return dest, y_rows


def kernel(x, a_w_in, a_conv_w, a_w_out, kv_w, b_w_q, b_lambda, b_subln_g, b_w_o,
           ln1_g, ln1_b, ln2_g, ln2_b, rg_w, rg_b, re_w, re_b, e_w_in, e_w_down):
    batch, seq, d = x.shape
    depth = ln1_g.shape[0]
    assert depth == 2 and a_w_in.shape[0] == 1 and b_w_q.shape[0] == 1
    assert seq % ROW_TILE == 0 and seq % ATTN_BLOCK == 0 and ROW_TILE % ATTN_BLOCK == 0
    t = batch * seq
    alpha = (2.0 * depth) ** 0.25
    epg = re_w.shape[2] // N_GROUPS
    head_dim = d // (2 * N_HEADS)
    xf = x.reshape(t, d)

    router0 = _router_weights(rg_w[0], rg_b[0], re_w[0], re_b[0])
    x1, x1t, route, counts = _conv_mixer(xf, a_w_in[0].astype(_BF16), a_conv_w[0],
                                         a_w_out[0].astype(_BF16), ln1_g[0], ln1_b[0], router0,
                                         seq=seq, alpha=alpha, epg=epg)
    dest, y_rows = _moe(x1t, route, counts, e_w_in, e_w_down, 0)
    q_scale = head_dim ** -0.5 * math.log2(math.e)
    x2, k, vt, q = _combine_qkv(dest, x1, route, y_rows, ln2_g[0], ln2_b[0],
                                kv_w[:, :d].astype(_BF16), kv_w[:, d:].T.astype(_BF16),
                                b_w_q[0].astype(_BF16), seq=seq, alpha=alpha, q_scale=q_scale)

    lam_init = 0.8 - 0.6 * math.exp(-0.3 * 1)
    o = _diff_attention(q, k, vt, b_lambda[0], b_subln_g[0], batch=batch, seq=seq, lam_init=lam_init)
    router1 = _router_weights(rg_w[1], rg_b[1], re_w[1], re_b[1])
    x1, x1t, route, counts = _attn_out(x2, o, b_w_o[0].astype(_BF16), ln1_g[1], ln1_b[1], router1,
                                       alpha=alpha, epg=epg)
    dest, y_rows = _moe(x1t, route, counts, e_w_in, e_w_down, 1)
    out = _combine(dest, x1, route, y_rows, ln2_g[1], ln2_b[1], alpha=alpha)
    return out.reshape(batch, seq, d)
```

```python
import functools
import math

import jax
import jax.numpy as jnp
from jax import lax
from jax.experimental import pallas as pl
from jax.experimental.pallas import tpu as pltpu

N_HEADS = 8
N_GROUPS = 4
LN_EPS = 1e-5
RMS_EPS = 1e-5

LANES = 128
SUBLANES = 8
VMEM_LIMIT_BYTES = 56 * 1024 * 1024

ROW_TILE = 512
EXPERT_BLOCK = 256
EXPERT_SLOTS = 3
ATTN_BLOCK = 256
ISSUE_UNROLL = 8

_E1, _E2, _R1, _R2, _G1, _G2 = range(6)

_F32 = jnp.float32
_BF16 = jnp.bfloat16
_NT = (((1,), (1,)), ((), ()))


def _dot(a, b):
    return jnp.dot(a, b, preferred_element_type=_F32)


def _layer_norm(z, g, b):
    mu = jnp.mean(z, axis=-1, keepdims=True)
    d = z - mu
    var = jnp.mean(d * d, axis=-1, keepdims=True)
    return d * lax.rsqrt(var + LN_EPS) * g + b


def _route_tail(x1, wrh_ref, wrl_ref, rb_ref, route_ref, counts_ref, *, epg):
    tm = x1.shape[0]
    xh = x1.astype(_BF16)
    xl = (x1 - xh.astype(_F32)).astype(_BF16)
    wh = wrh_ref[...]
    logits = _dot(xh, wh) + _dot(xl, wh) + _dot(xh, wrl_ref[...]) + rb_ref[...]

    lane = lax.broadcasted_iota(jnp.int32, (tm, LANES), 1)
    lanef = lane.astype(_F32)
    neg = -jnp.inf
    gl = jnp.where(lane < N_GROUPS, logits, neg)
    gmax = jnp.max(gl, axis=-1, keepdims=True)
    gidx = jnp.min(jnp.where(gl == gmax, lanef, float(LANES)), axis=-1, keepdims=True)
    gtop = 1.0 / jnp.sum(jnp.exp(gl - gmax), axis=-1, keepdims=True)

    lo = float(N_GROUPS) + gidx * float(epg)
    el = jnp.where((lanef >= lo) & (lanef < lo + float(epg)), logits, neg)
    m1 = jnp.max(el, axis=-1, keepdims=True)
    i1 = jnp.min(jnp.where(el == m1, lanef, float(LANES)), axis=-1, keepdims=True)
    el2 = jnp.where(lanef == i1, neg, el)
    m2 = jnp.max(el2, axis=-1, keepdims=True)
    i2 = jnp.min(jnp.where(el2 == m2, lanef, float(LANES)), axis=-1, keepdims=True)
    w2 = jnp.exp(m2 - m1)
    inv = 1.0 / (1.0 + w2)
    g1 = gtop * inv
    g2 = gtop * w2 * inv
    e1 = i1 - float(N_GROUPS)
    e2 = i2 - float(N_GROUPS)

    onehot = jnp.where((lanef == e1) | (lanef == e2), 1.0, 0.0)
    r = lax.broadcasted_iota(jnp.int32, (tm, tm), 0)
    c = lax.broadcasted_iota(jnp.int32, (tm, tm), 1)
    ltri = jnp.where(c < r, 1.0, 0.0).astype(_BF16)
    prefix = _dot(ltri, onehot.astype(_BF16)) + counts_ref[...]
    r1 = jnp.sum(jnp.where(lanef == e1, prefix, 0.0), axis=-1, keepdims=True)
    r2 = jnp.sum(jnp.where(lanef == e2, prefix, 0.0), axis=-1, keepdims=True)
    counts_ref[...] += jnp.sum(onehot, axis=0, keepdims=True)

    rec = jnp.zeros((tm, LANES), _F32)
    for slot, val in ((_E1, e1), (_E2, e2), (_R1, r1), (_R2, r2), (_G1, g1), (_G2, g2)):
        rec = jnp.where(lane == slot, val, rec)
    route_ref[...] = rec


def _store_token_tiles(ref, x):
    n, d = x.shape
    per = d // LANES
    for s in range(per):
        ref[pl.ds(s, n, stride=per), :] = x[:, s * LANES:(s + 1) * LANES]


def _load_token_tiles(ref, first_token, n, d):
    per = d // LANES
    start = first_token * per
    return jnp.concatenate([ref[pl.ds(start + s, n, stride=per), :] for s in range(per)], axis=1)


def _tile_copy(src_hbm, token, dst_vmem, dst_token, sem):
    rows = pl.ds(pl.multiple_of(dst_token * SUBLANES, SUBLANES), SUBLANES)
    return pltpu.make_async_copy(src_hbm.at[token], dst_vmem.at[rows, :], sem)


def _conv_mixer_kernel(x_ref, win_ref, cw_ref, wout_ref, g_ref, b_ref, wrh_ref, wrl_ref, rb_ref,
                       x1_ref, x1t_ref, route_ref, counts_ref, ubuf_ref, *, tiles_per_seq, alpha, epg):
    i = pl.program_id(0)
    tm, d = x_ref.shape

    @pl.when(i == 0)
    def _():
        counts_ref[...] = jnp.zeros_like(counts_ref)

    @pl.when(i % tiles_per_seq == 0)
    def _():
        ubuf_ref[0:SUBLANES, :] = jnp.zeros((SUBLANES, d), _F32)

    x = x_ref[...]
    h = _dot(x.astype(_BF16), win_ref[...])
    u = h[:, d:2 * d] * h[:, 2 * d:]
    ubuf_ref[SUBLANES:SUBLANES + tm, :] = u
    cw = cw_ref[...]
    uc = (cw[0:1, :] * ubuf_ref[SUBLANES - 2:SUBLANES - 2 + tm, :]
          + cw[1:2, :] * ubuf_ref[SUBLANES - 1:SUBLANES - 1 + tm, :]
          + cw[2:3, :] * u)
    ubuf_ref[0:SUBLANES, :] = ubuf_ref[tm:tm + SUBLANES, :]
    y = _dot((h[:, :d] * uc).astype(_BF16), wout_ref[...])
    x1 = _layer_norm(alpha * x + y, g_ref[...], b_ref[...])
    x1_ref[...] = x1
    _store_token_tiles(x1t_ref, x1)
    _route_tail(x1, wrh_ref, wrl_ref, rb_ref, route_ref, counts_ref, epg=epg)


def _attn_out_kernel(x_ref, o_ref, wo_ref, g_ref, b_ref, wrh_ref, wrl_ref, rb_ref,
                     x1_ref, x1t_ref, route_ref, counts_ref, *, alpha, epg):
    @pl.when(pl.program_id(0) == 0)
    def _():
        counts_ref[...] = jnp.zeros_like(counts_ref)

    y = _dot(o_ref[...], wo_ref[...])
    x1 = _layer_norm(alpha * x_ref[...] + y, g_ref[...], b_ref[...])
    x1_ref[...] = x1
    _store_token_tiles(x1t_ref, x1)
    _route_tail(x1, wrh_ref, wrl_ref, rb_ref, route_ref, counts_ref, epg=epg)


def _expert_ffn_kernel(bexp_ref, seg_ref, dest_ref, x_hbm, win_hbm, wdn_hbm, y_ref,
                       rowtok, xbuf, sem, wstage_in, wstage_dn, wsem, winb, wdnb, *, layer):
    i = pl.program_id(0)
    n_exp = (seg_ref.shape[0] - 1) // 2
    n_used = seg_ref[2 * n_exp]
    blk = y_ref.shape[0] // SUBLANES
    d, de2 = wstage_in.shape
    de = de2 // 2
    n_tokens = dest_ref.shape[0] // 2
    slot = i % EXPERT_SLOTS

    def issue(b):
        s = b % EXPERT_SLOTS

        def body(r, carry):
            for p in range(2):
                row = 2 * r + p
                _tile_copy(x_hbm, rowtok[b * blk + row], xbuf, s * blk + row, sem.at[s]).start(priority=p)
            return carry
        lax.fori_loop(0, blk // 2, body, 0, unroll=ISSUE_UNROLL // 2)

    def weight_copies(e):
        return (pltpu.make_async_copy(win_hbm.at[layer, e], wstage_in, wsem.at[0]),
                pltpu.make_async_copy(wdn_hbm.at[layer, e], wstage_dn, wsem.at[1]))

    @pl.when(i == 0)
    def _():
        for c in weight_copies(bexp_ref[0]):
            c.start()

        def fill_expert(e, carry):
            def fill(j, c):
                rowtok[j] = 0
                return c
            return lax.fori_loop(seg_ref[e], seg_ref[n_exp + e], fill, carry)
        lax.fori_loop(0, n_exp, fill_expert, 0)

        def invert(t, carry):
            rowtok[dest_ref[t]] = t
            rowtok[dest_ref[n_tokens + t]] = t
            return carry
        lax.fori_loop(0, n_tokens, invert, 0, unroll=ISSUE_UNROLL)
        issue(0)

        @pl.when(1 < n_used)
        def _():
            issue(1)

    @pl.when(i + 2 < n_used)
    def _():
        issue(i + 2)

    @pl.when(i < n_used)
    def _():
        e = bexp_ref[i]

        @pl.when((i == 0) | (e != bexp_ref[jnp.maximum(i - 1, 0)]))
        def _():
            for c in weight_copies(e):
                c.wait()
            winb[...] = wstage_in[...].astype(_BF16)
            wdnb[...] = wstage_dn[...].astype(_BF16)
            nxt = seg_ref[n_exp + e] // blk

            @pl.when(nxt < n_used)
            def _():
                for c in weight_copies(bexp_ref[nxt]):
                    c.start()

        def wait(r, carry):
            _tile_copy(x_hbm, 0, xbuf, slot * blk + r, sem.at[slot]).wait()
            return carry
        lax.fori_loop(0, blk, wait, 0, unroll=ISSUE_UNROLL)

        x = _load_token_tiles(xbuf, slot * blk, blk, d)
        h = _dot(x.astype(_BF16), winb[...])
        g = h[:, :de]
        a = g * jax.nn.sigmoid(g) * h[:, de:]
        _store_token_tiles(y_ref, _dot(a.astype(_BF16), wdnb[...]))

    @pl.when(i >= n_used)
    def _():
        y_ref[...] = jnp.zeros_like(y_ref)


def _combine_body(dest_ref, x1_ref, route_ref, y_hbm, g_ref, b_ref, ybuf, sem, *, alpha):
    i = pl.program_id(0)
    n = pl.num_programs(0)
    tm, d = x1_ref.shape
    slot = i % 2

    def issue(tile, s):
        def body(r, carry):
            t = tile * tm + r
            _tile_copy(y_hbm, dest_ref[t], ybuf, (2 * s) * tm + r, sem.at[s]).start(priority=0)
            _tile_copy(y_hbm, dest_ref[n * tm + t], ybuf, (2 * s + 1) * tm + r, sem.at[s]).start(priority=1)
            return carry
        lax.fori_loop(0, tm, body, 0, unroll=ISSUE_UNROLL)

    @pl.when(i == 0)
    def _():
        issue(0, 0)

    @pl.when(i + 1 < n)
    def _():
        issue(i + 1, 1 - slot)

    def wait(r, carry):
        _tile_copy(y_hbm, 0, ybuf, (2 * slot) * tm + r, sem.at[slot]).wait()
        _tile_copy(y_hbm, 0, ybuf, (2 * slot + 1) * tm + r, sem.at[slot]).wait()
        return carry
    lax.fori_loop(0, tm, wait, 0, unroll=ISSUE_UNROLL)

    rec = route_ref[...]
    ffn = (rec[:, _G1:_G1 + 1] * _load_token_tiles(ybuf, (2 * slot) * tm, tm, d)
           + rec[:, _G2:_G2 + 1] * _load_token_tiles(ybuf, (2 * slot + 1) * tm, tm, d))
    return _layer_norm(alpha * x1_ref[...] + ffn, g_ref[...], b_ref[...])


def _combine_kernel(dest_ref, x1_ref, route_ref, y_hbm, g_ref, b_ref, x2_ref, ybuf, sem, *, alpha):
    x2_ref[...] = _combine_body(dest_ref, x1_ref, route_ref, y_hbm, g_ref, b_ref, ybuf, sem, alpha=alpha)


def _combine_qkv_kernel(dest_ref, x1_ref, route_ref, y_hbm, g_ref, b_ref, wk_ref, wvt_ref, wq_ref,
                        x2_ref, k_ref, vt_ref, q_ref, ybuf, sem, *, alpha, q_scale):
    x2 = _combine_body(dest_ref, x1_ref, route_ref, y_hbm, g_ref, b_ref, ybuf, sem, alpha=alpha)
    x2_ref[...] = x2
    xb = x2.astype(_BF16)
    k_ref[...] = _dot(xb, wk_ref[...]).astype(_BF16)
    q_ref[...] = (_dot(xb, wq_ref[...]) * q_scale).astype(_BF16)
    vt_ref[0] = lax.dot_general(wvt_ref[...], xb, _NT, preferred_element_type=_F32).astype(_BF16)


def _diff_attn_kernel(lam_ref, g_ref, q_ref, k_ref, vt_ref, o_ref, lamfull_ref, s_ref, p_ref, *, lam_init):
    bi, hi = pl.program_id(0), pl.program_id(1)
    seq, hw = k_ref.shape
    tq = tk = ATTN_BLOCK
    nk = seq // tk
    hd = lam_ref.shape[1]

    @pl.when((bi == 0) & (hi == 0))
    def _():
        lam = lam_ref[...]
        a = jnp.sum(lam[0:1, :] * lam[1:2, :], axis=-1, keepdims=True)
        b = jnp.sum(lam[2:3, :] * lam[3:4, :], axis=-1, keepdims=True)
        lamfull_ref[...] = jnp.broadcast_to(jnp.exp(a) - jnp.exp(b) + lam_init, lamfull_ref.shape)

    lam_full = lamfull_ref[0:1, 0:1]
    lane = lax.broadcasted_iota(jnp.int32, (tq, hw), 1)
    causal = (lax.broadcasted_iota(jnp.int32, (tk, tq), 0) <= lax.broadcasted_iota(jnp.int32, (tk, tq), 1))

    for i in range(nk):
        par = i % 2
        q = q_ref[i * tq:(i + 1) * tq, :]
        zero = jnp.zeros_like(q)
        qs = (jnp.where(lane < hd, q, zero), jnp.where(lane >= hd, q, zero))
        outs = []
        for c in range(2):
            colmax = None
            for j in range(i + 1):
                s = lax.dot_general(k_ref[j * tk:(j + 1) * tk, :], qs[c], _NT,
                                    preferred_element_type=_F32)
                if j == i:
                    s = jnp.where(causal, s, -jnp.inf)
                s_ref[par, c, j] = s
                cm = jnp.max(s, axis=0, keepdims=True)
                colmax = cm if colmax is None else jnp.maximum(colmax, cm)
            l = jnp.zeros((1, tq), _F32)
            for j in range(i + 1):
                p = jnp.exp2(s_ref[par, c, j] - colmax)
                l = l + jnp.sum(p, axis=0, keepdims=True)
                p_ref[par, c, j * tk:(j + 1) * tk, :] = p.astype(_BF16)
            kv = (i + 1) * tk
            outs.append(_dot(vt_ref[0, :, 0:kv], p_ref[par, c, 0:kv, :]) * (1.0 / l))
        ot = outs[0] - lam_full * outs[1]
        ot = ot * lax.rsqrt(jnp.mean(ot * ot, axis=0, keepdims=True) + RMS_EPS)
        o_ref[i * tq:(i + 1) * tq, :] = (ot.T * g_ref[...] * (1.0 - lam_init)).astype(o_ref.dtype)


def _params(*sem):
    return pltpu.CompilerParams(dimension_semantics=sem, vmem_limit_bytes=VMEM_LIMIT_BYTES)


def _full(shape):
    return pl.BlockSpec(shape, lambda *_: (0,) * len(shape))


def _router_weights(rg_w, rg_b, re_w, re_b):
    d = rg_w.shape[0]
    n = rg_w.shape[1] + re_w.shape[1]
    w = jnp.concatenate([rg_w, re_w, jnp.zeros((d, LANES - n), _F32)], axis=1)
    b = jnp.concatenate([rg_b, re_b, jnp.zeros((LANES - n,), _F32)])[None, :]
    wh = w.astype(_BF16)
    wl = (w - wh.astype(_F32)).astype(_BF16)
    return wh, wl, b


def _route_outs(t, d, tm):
    per = d // LANES
    shapes = (jax.ShapeDtypeStruct((t, d), _F32),
              jax.ShapeDtypeStruct((t * per, LANES), _F32),
              jax.ShapeDtypeStruct((t, LANES), _F32),
              jax.ShapeDtypeStruct((1, LANES), _F32))
    specs = (pl.BlockSpec((tm, d), lambda i: (i, 0)),
             pl.BlockSpec((tm * per, LANES), lambda i: (i, 0)),
             pl.BlockSpec((tm, LANES), lambda i: (i, 0)),
             pl.BlockSpec((1, LANES), lambda i: (0, 0)))
    return shapes, specs


def _conv_mixer(x, w_in, conv_w, w_out, ln_g, ln_b, router, *, seq, alpha, epg):
    t, d = x.shape
    tm = ROW_TILE
    wrh, wrl, rb = router
    shapes, specs = _route_outs(t, d, tm)
    return pl.pallas_call(
        functools.partial(_conv_mixer_kernel, tiles_per_seq=seq // tm, alpha=alpha, epg=epg),
        grid=(t // tm,),
        in_specs=[pl.BlockSpec((tm, d), lambda i: (i, 0)),
                  _full(w_in.shape), _full(conv_w.shape), _full(w_out.shape),
                  _full((1, d)), _full((1, d)),
                  _full(wrh.shape), _full(wrl.shape), _full(rb.shape)],
        out_specs=specs, out_shape=shapes,
        scratch_shapes=[pltpu.VMEM((tm + SUBLANES, d), _F32)],
        compiler_params=_params("arbitrary"),
        name="conv_mixer_ln_router",
    )(x, w_in, conv_w, w_out, ln_g[None, :], ln_b[None, :], wrh, wrl, rb)


def _attn_out(x, o, w_o, ln_g, ln_b, router, *, alpha, epg):
    t, d = x.shape
    tm = ROW_TILE
    wrh, wrl, rb = router
    shapes, specs = _route_outs(t, d, tm)
    return pl.pallas_call(
        functools.partial(_attn_out_kernel, alpha=alpha, epg=epg),
        grid=(t // tm,),
        in_specs=[pl.BlockSpec((tm, d), lambda i: (i, 0)),
                  pl.BlockSpec((tm, d), lambda i: (i, 0)),
                  _full(w_o.shape), _full((1, d)), _full((1, d)),
                  _full(wrh.shape), _full(wrl.shape), _full(rb.shape)],
        out_specs=specs, out_shape=shapes,
        compiler_params=_params("arbitrary"),
        name="attn_out_ln_router",
    )(x, o, w_o, ln_g[None, :], ln_b[None, :], wrh, wrl, rb)


def _dispatch_plan(route, counts, n_experts):
    t = route.shape[0]
    blk = EXPERT_BLOCK
    cnt = counts[0, :n_experts].astype(jnp.int32)
    padded = (cnt + blk - 1) // blk * blk
    pad_end = jnp.cumsum(padded)
    pad_start = pad_end - padded
    ids = jnp.arange(n_experts, dtype=jnp.int32)

    def sorted_row(e_slot, r_slot):
        e = route[:, e_slot].astype(jnp.int32)
        start = jnp.sum(jnp.where(e[:, None] == ids[None, :], pad_start[None, :], 0), axis=1)
        return start + route[:, r_slot].astype(jnp.int32)

    dest = jnp.concatenate([sorted_row(_E1, _R1), sorted_row(_E2, _R2)])
    n_blocks = (2 * t + n_experts * blk) // blk
    block_start = jnp.arange(n_blocks, dtype=jnp.int32) * blk
    block_expert = jnp.minimum(jnp.sum(pad_end[None, :] <= block_start[:, None], axis=1),
                               n_experts - 1).astype(jnp.int32)
    n_used = (pad_end[-1:] // blk).astype(jnp.int32)
    seg = jnp.concatenate([pad_start + cnt, pad_end, n_used]).astype(jnp.int32)
    return dest, block_expert, seg


def _expert_ffn(x1t, w_in, w_down, layer, dest, block_expert, seg):
    per = x1t.shape[1]
    _, n_exp, d, de2 = w_in.shape
    de = w_down.shape[2]
    blk = EXPERT_BLOCK
    n_blocks = block_expert.shape[0]
    n_rows = n_blocks * blk
    y = pl.pallas_call(
        functools.partial(_expert_ffn_kernel, layer=layer),
        grid_spec=pltpu.PrefetchScalarGridSpec(
            num_scalar_prefetch=3,
            grid=(n_blocks,),
            in_specs=[pl.BlockSpec(memory_space=pl.ANY),
                      pl.BlockSpec(memory_space=pl.ANY),
                      pl.BlockSpec(memory_space=pl.ANY)],
            out_specs=pl.BlockSpec((blk * per, LANES), lambda i, be, sg, ds: (i, 0)),
            scratch_shapes=[pltpu.SMEM((n_rows,), jnp.int32),
                            pltpu.VMEM((EXPERT_SLOTS * blk * per, LANES), _F32),
                            pltpu.SemaphoreType.DMA((EXPERT_SLOTS,)),
                            pltpu.VMEM((d, de2), _F32),
                            pltpu.VMEM((de, d), _F32),
                            pltpu.SemaphoreType.DMA((2,)),
                            pltpu.VMEM((d, de2), _BF16),
                            pltpu.VMEM((de, d), _BF16)]),
        out_shape=jax.ShapeDtypeStruct((n_rows * per, LANES), _F32),
        compiler_params=_params("arbitrary"),
        name="expert_ffn",
    )(block_expert, seg, dest, x1t, w_in, w_down)
    return y.reshape(n_rows, per, LANES)


def _combine_scratch(tm, d):
    return [pltpu.VMEM((2 * 2 * tm * (d // LANES), LANES), _F32), pltpu.SemaphoreType.DMA((2,))]


def _combine(dest, x1, route, y_rows, ln_g, ln_b, *, alpha):
    t, d = x1.shape
    tm = ROW_TILE
    return pl.pallas_call(
        functools.partial(_combine_kernel, alpha=alpha),
        grid_spec=pltpu.PrefetchScalarGridSpec(
            num_scalar_prefetch=1,
            grid=(t // tm,),
            in_specs=[pl.BlockSpec((tm, d), lambda i, ds: (i, 0)),
                      pl.BlockSpec((tm, LANES), lambda i, ds: (i, 0)),
                      pl.BlockSpec(memory_space=pl.ANY),
                      pl.BlockSpec((1, d), lambda i, ds: (0, 0)),
                      pl.BlockSpec((1, d), lambda i, ds: (0, 0))],
            out_specs=pl.BlockSpec((tm, d), lambda i, ds: (i, 0)),
            scratch_shapes=_combine_scratch(tm, d)),
        out_shape=jax.ShapeDtypeStruct((t, d), _F32),
        compiler_params=_params("arbitrary"),
        name="combine_ln",
    )(dest, x1, route, y_rows, ln_g[None, :], ln_b[None, :])


def _combine_qkv(dest, x1, route, y_rows, ln_g, ln_b, w_k, w_vt, w_q, *, seq, alpha, q_scale):
    t, d = x1.shape
    tm = ROW_TILE
    per_seq = seq // tm
    wspec = pl.BlockSpec((d, d), lambda i, ds: (0, 0))
    row = pl.BlockSpec((tm, d), lambda i, ds: (i, 0))
    return pl.pallas_call(
        functools.partial(_combine_qkv_kernel, alpha=alpha, q_scale=q_scale),
        grid_spec=pltpu.PrefetchScalarGridSpec(
            num_scalar_prefetch=1,
            grid=(t // tm,),
            in_specs=[row,
                      pl.BlockSpec((tm, LANES), lambda i, ds: (i, 0)),
                      pl.BlockSpec(memory_space=pl.ANY),
                      pl.BlockSpec((1, d), lambda i, ds: (0, 0)),
                      pl.BlockSpec((1, d), lambda i, ds: (0, 0)),
                      wspec, wspec, wspec],
            out_specs=(row, row,
                       pl.BlockSpec((1, d, tm), lambda i, ds: (i // per_seq, 0, i % per_seq)),
                       row),
            scratch_shapes=_combine_scratch(tm, d)),
        out_shape=(jax.ShapeDtypeStruct((t, d), _F32),
                   jax.ShapeDtypeStruct((t, d), _BF16),
                   jax.ShapeDtypeStruct((t // seq, d, seq), _BF16),
                   jax.ShapeDtypeStruct((t, d), _BF16)),
        compiler_params=_params("arbitrary"),
        name="combine_ln_qkv",
    )(dest, x1, route, y_rows, ln_g[None, :], ln_b[None, :], w_k, w_vt, w_q)


def _diff_attention(q, k, vt, lam, subln_g, *, batch, seq, lam_init):
    t, d = q.shape
    tq = ATTN_BLOCK
    hw = d // N_HEADS
    head = pl.BlockSpec((seq, hw), lambda b, h: (b, h))
    return pl.pallas_call(
        functools.partial(_diff_attn_kernel, lam_init=lam_init),
        grid=(batch, N_HEADS),
        in_specs=[_full(lam.shape), _full((1, hw)), head, head,
                  pl.BlockSpec((1, hw, seq), lambda b, h: (b, h, 0))],
        out_specs=head,
        out_shape=jax.ShapeDtypeStruct((t, d), _BF16),
        scratch_shapes=[pltpu.VMEM((1, LANES), _F32),
                        pltpu.VMEM((2, 2, seq // tq, tq, tq), _F32),
                        pltpu.VMEM((2, 2, seq, tq), _BF16)],
        compiler_params=_params("arbitrary", "arbitrary"),
        name="diff_attention",
    )(lam, subln_g[None, :], q, k, vt)


def _moe(x1t, route, counts, w_in, w_down, layer):
    n_exp, d = w_in.shape[1:3]
    dest, block_expert, seg = _dispatch_plan(route, counts, n_exp)
    x1t = x1t.reshape(-1, d // LANES, LANES)
    y_rows = _expert_ffn(x1t, w_in, w_down, layer, dest, block_expert, seg)
    return dest, y_rows


def kernel(x, a_w_in, a_conv_w, a_w_out, kv_w, b_w_q, b_lambda, b_subln_g, b_w_o,
           ln1_g, ln1_b, ln2_g, ln2_b, rg_w, rg_b, re_w, re_b, e_w_in, e_w_down):
    batch, seq, d = x.shape
    depth = ln1_g.shape[0]
    assert depth == 2 and a_w_in.shape[0] == 1 and b_w_q.shape[0] == 1
    assert seq % ROW_TILE == 0 and seq % ATTN_BLOCK == 0 and ROW_TILE % ATTN_BLOCK == 0
    t = batch * seq
    alpha = (2.0 * depth) ** 0.25
    epg = re_w.shape[2] // N_GROUPS
    head_dim = d // (2 * N_HEADS)
    xf = x.reshape(t, d)

    router0 = _router_weights(rg_w[0], rg_b[0], re_w[0], re_b[0])
    x1, x1t, route, counts = _conv_mixer(xf, a_w_in[0].astype(_BF16), a_conv_w[0],
                                         a_w_out[0].astype(_BF16), ln1_g[0], ln1_b[0], router0,
                                         seq=seq, alpha=alpha, epg=epg)
    dest, y_rows = _moe(x1t, route, counts, e_w_in, e_w_down, 0)
    q_scale = head_dim ** -0.5 * math.log2(math.e)
    x2, k, vt, q = _combine_qkv(dest, x1, route, y_rows, ln2_g[0], ln2_b[0],
                                kv_w[:, :d].astype(_BF16), kv_w[:, d:].T.astype(_BF16),
                                b_w_q[0].astype(_BF16), seq=seq, alpha=alpha, q_scale=q_scale)

    lam_init = 0.8 - 0.6 * math.exp(-0.3 * 1)
    o = _diff_attention(q, k, vt, b_lambda[0], b_subln_g[0], batch=batch, seq=seq, lam_init=lam_init)
    router1 = _router_weights(rg_w[1], rg_b[1], re_w[1], re_b[1])
    x1, x1t, route, counts = _attn_out(x2, o, b_w_o[0].astype(_BF16), ln1_g[1], ln1_b[1], router1,
                                       alpha=alpha, epg=epg)
    dest, y_rows = _moe(x1t, route, counts, e_w_in, e_w_down, 1)
    out = _combine(dest, x1, route, y_rows, ln2_g[1], ln2_b[1], alpha=alpha)
    return out.reshape(batch, seq, d)
```

```python
import functools
import math

import jax
import jax.numpy as jnp
from jax import lax
from jax.experimental import pallas as pl
from jax.experimental.pallas import tpu as pltpu

N_HEADS = 8
N_GROUPS = 4
LN_EPS = 1e-5
RMS_EPS = 1e-5

LANES = 128
SUBLANES = 8
VMEM_LIMIT_BYTES = 56 * 1024 * 1024

ROW_TILE = 512
EXPERT_BLOCK = 256
EXPERT_SLOTS = 3
ATTN_BLOCK = 256
ISSUE_UNROLL = 8
ONES_ROWS = 16

_E1, _E2, _R1, _R2, _G1, _G2 = range(6)

_F32 = jnp.float32
_BF16 = jnp.bfloat16
_NT = (((1,), (1,)), ((), ()))


def _dot(a, b):
    return jnp.dot(a, b, preferred_element_type=_F32)


def _layer_norm(z, g, b):
    mu = jnp.mean(z, axis=-1, keepdims=True)
    d = z - mu
    var = jnp.mean(d * d, axis=-1, keepdims=True)
    return d * lax.rsqrt(var + LN_EPS) * g + b


def _route_tail(x1, wrh_ref, wrl_ref, rb_ref, route_ref, counts_ref, *, epg):
    tm = x1.shape[0]
    xh = x1.astype(_BF16)
    xl = (x1 - xh.astype(_F32)).astype(_BF16)
    wh = wrh_ref[...]
    logits = _dot(xh, wh) + _dot(xl, wh) + _dot(xh, wrl_ref[...]) + rb_ref[...]

    lane = lax.broadcasted_iota(jnp.int32, (tm, LANES), 1)
    lanef = lane.astype(_F32)
    neg = -jnp.inf
    gl = jnp.where(lane < N_GROUPS, logits, neg)
    gmax = jnp.max(gl, axis=-1, keepdims=True)
    gidx = jnp.min(jnp.where(gl == gmax, lanef, float(LANES)), axis=-1, keepdims=True)
    gtop = 1.0 / jnp.sum(jnp.exp(gl - gmax), axis=-1, keepdims=True)

    lo = float(N_GROUPS) + gidx * float(epg)
    el = jnp.where((lanef >= lo) & (lanef < lo + float(epg)), logits, neg)
    m1 = jnp.max(el, axis=-1, keepdims=True)
    i1 = jnp.min(jnp.where(el == m1, lanef, float(LANES)), axis=-1, keepdims=True)
    el2 = jnp.where(lanef == i1, neg, el)
    m2 = jnp.max(el2, axis=-1, keepdims=True)
    i2 = jnp.min(jnp.where(el2 == m2, lanef, float(LANES)), axis=-1, keepdims=True)
    w2 = jnp.exp(m2 - m1)
    inv = 1.0 / (1.0 + w2)
    g1 = gtop * inv
    g2 = gtop * w2 * inv
    e1 = i1 - float(N_GROUPS)
    e2 = i2 - float(N_GROUPS)

    onehot = jnp.where((lanef == e1) | (lanef == e2), 1.0, 0.0)
    r = lax.broadcasted_iota(jnp.int32, (tm, tm), 0)
    c = lax.broadcasted_iota(jnp.int32, (tm, tm), 1)
    ltri = jnp.where(c < r, 1.0, 0.0).astype(_BF16)
    prefix = _dot(ltri, onehot.astype(_BF16)) + counts_ref[...]
    r1 = jnp.sum(jnp.where(lanef == e1, prefix, 0.0), axis=-1, keepdims=True)
    r2 = jnp.sum(jnp.where(lanef == e2, prefix, 0.0), axis=-1, keepdims=True)
    counts_ref[...] += jnp.sum(onehot, axis=0, keepdims=True)

    rec = jnp.zeros((tm, LANES), _F32)
    for slot, val in ((_E1, e1), (_E2, e2), (_R1, r1), (_R2, r2), (_G1, g1), (_G2, g2)):
        rec = jnp.where(lane == slot, val, rec)
    route_ref[...] = rec


def _store_token_tiles(ref, x):
    n, d = x.shape
    per = d // LANES
    for s in range(per):
        ref[pl.ds(s, n, stride=per), :] = x[:, s * LANES:(s + 1) * LANES]


def _load_token_tiles(ref, first_token, n, d):
    per = d // LANES
    start = first_token * per
    return jnp.concatenate([ref[pl.ds(start + s, n, stride=per), :] for s in range(per)], axis=1)


def _tile_copy(src_hbm, token, dst_vmem, dst_token, sem):
    rows = pl.ds(pl.multiple_of(dst_token * SUBLANES, SUBLANES), SUBLANES)
    return pltpu.make_async_copy(src_hbm.at[token], dst_vmem.at[rows, :], sem)


def _conv_mixer_kernel(x_ref, win_ref, cw_ref, wout_ref, g_ref, b_ref, wrh_ref, wrl_ref, rb_ref,
                       x1_ref, x1t_ref, route_ref, counts_ref, ubuf_ref, *, tiles_per_seq, alpha, epg):
    i = pl.program_id(0)
    tm, d = x_ref.shape

    @pl.when(i == 0)
    def _():
        counts_ref[...] = jnp.zeros_like(counts_ref)

    @pl.when(i % tiles_per_seq == 0)
    def _():
        ubuf_ref[0:SUBLANES, :] = jnp.zeros((SUBLANES, d), _F32)

    x = x_ref[...]
    h = _dot(x.astype(_BF16), win_ref[...])
    u = h[:, d:2 * d] * h[:, 2 * d:]
    ubuf_ref[SUBLANES:SUBLANES + tm, :] = u
    cw = cw_ref[...]
    uc = (cw[0:1, :] * ubuf_ref[SUBLANES - 2:SUBLANES - 2 + tm, :]
          + cw[1:2, :] * ubuf_ref[SUBLANES - 1:SUBLANES - 1 + tm, :]
          + cw[2:3, :] * u)
    ubuf_ref[0:SUBLANES, :] = ubuf_ref[tm:tm + SUBLANES, :]
    y = _dot((h[:, :d] * uc).astype(_BF16), wout_ref[...])
    x1 = _layer_norm(alpha * x + y, g_ref[...], b_ref[...])
    x1_ref[...] = x1
    _store_token_tiles(x1t_ref, x1)
    _route_tail(x1, wrh_ref, wrl_ref, rb_ref, route_ref, counts_ref, epg=epg)


def _attn_out_kernel(x_ref, o_ref, wo_ref, g_ref, b_ref, wrh_ref, wrl_ref, rb_ref,
                     x1_ref, x1t_ref, route_ref, counts_ref, *, alpha, epg):
    @pl.when(pl.program_id(0) == 0)
    def _():
        counts_ref[...] = jnp.zeros_like(counts_ref)

    y = _dot(o_ref[...], wo_ref[...])
    x1 = _layer_norm(alpha * x_ref[...] + y, g_ref[...], b_ref[...])
    x1_ref[...] = x1
    _store_token_tiles(x1t_ref, x1)
    _route_tail(x1, wrh_ref, wrl_ref, rb_ref, route_ref, counts_ref, epg=epg)


def _expert_ffn_kernel(bexp_ref, seg_ref, dest_ref, x_hbm, win_hbm, wdn_hbm, y_ref,
                       rowtok, xbuf, sem, wstage_in, wstage_dn, wsem, winb, wdnb, *, layer):
    i = pl.program_id(0)
    n_exp = (seg_ref.shape[0] - 1) // 2
    n_used = seg_ref[2 * n_exp]
    blk = y_ref.shape[0] // SUBLANES
    d, de2 = wstage_in.shape
    de = de2 // 2
    n_tokens = dest_ref.shape[0] // 2
    slot = i % EXPERT_SLOTS

    def issue(b):
        s = b % EXPERT_SLOTS

        def body(r, carry):
            for p in range(2):
                row = 2 * r + p
                _tile_copy(x_hbm, rowtok[b * blk + row], xbuf, s * blk + row, sem.at[s]).start(priority=p)
            return carry
        lax.fori_loop(0, blk // 2, body, 0, unroll=ISSUE_UNROLL // 2)

    def weight_copies(e):
        return (pltpu.make_async_copy(win_hbm.at[layer, e], wstage_in, wsem.at[0]),
                pltpu.make_async_copy(wdn_hbm.at[layer, e], wstage_dn, wsem.at[1]))

    @pl.when(i == 0)
    def _():
        for c in weight_copies(bexp_ref[0]):
            c.start()

        def fill_expert(e, carry):
            def fill(j, c):
                rowtok[j] = 0
                return c
            return lax.fori_loop(seg_ref[e], seg_ref[n_exp + e], fill, carry)
        lax.fori_loop(0, n_exp, fill_expert, 0)

        def invert(t, carry):
            rowtok[dest_ref[t]] = t
            rowtok[dest_ref[n_tokens + t]] = t
            return carry
        lax.fori_loop(0, n_tokens, invert, 0, unroll=ISSUE_UNROLL)
        issue(0)

        @pl.when(1 < n_used)
        def _():
            issue(1)

    @pl.when(i + 2 < n_used)
    def _():
        issue(i + 2)

    @pl.when(i < n_used)
    def _():
        e = bexp_ref[i]

        @pl.when((i == 0) | (e != bexp_ref[jnp.maximum(i - 1, 0)]))
        def _():
            for c in weight_copies(e):
                c.wait()
            winb[...] = wstage_in[...].astype(_BF16)
            wdnb[...] = wstage_dn[...].astype(_BF16)
            nxt = seg_ref[n_exp + e] // blk

            @pl.when(nxt < n_used)
            def _():
                for c in weight_copies(bexp_ref[nxt]):
                    c.start()

        def wait(r, carry):
            _tile_copy(x_hbm, 0, xbuf, slot * blk + r, sem.at[slot]).wait()
            return carry
        lax.fori_loop(0, blk, wait, 0, unroll=ISSUE_UNROLL)

        x = _load_token_tiles(xbuf, slot * blk, blk, d)
        h = _dot(x.astype(_BF16), winb[...])
        g = h[:, :de]
        a = g * jax.nn.sigmoid(g) * h[:, de:]
        _store_token_tiles(y_ref, _dot(a.astype(_BF16), wdnb[...]))

    @pl.when(i >= n_used)
    def _():
        y_ref[...] = jnp.zeros_like(y_ref)


def _combine_body(dest_ref, x1_ref, route_ref, y_hbm, g_ref, b_ref, ybuf, sem, *, alpha):
    i = pl.program_id(0)
    n = pl.num_programs(0)
    tm, d = x1_ref.shape
    slot = i % 2

    def issue(tile, s):
        def body(r, carry):
            t = tile * tm + r
            _tile_copy(y_hbm, dest_ref[t], ybuf, (2 * s) * tm + r, sem.at[s]).start(priority=0)
            _tile_copy(y_hbm, dest_ref[n * tm + t], ybuf, (2 * s + 1) * tm + r, sem.at[s]).start(priority=1)
            return carry
        lax.fori_loop(0, tm, body, 0, unroll=ISSUE_UNROLL)

    @pl.when(i == 0)
    def _():
        issue(0, 0)

    @pl.when(i + 1 < n)
    def _():
        issue(i + 1, 1 - slot)

    def wait(r, carry):
        _tile_copy(y_hbm, 0, ybuf, (2 * slot) * tm + r, sem.at[slot]).wait()
        _tile_copy(y_hbm, 0, ybuf, (2 * slot + 1) * tm + r, sem.at[slot]).wait()
        return carry
    lax.fori_loop(0, tm, wait, 0, unroll=ISSUE_UNROLL)

    rec = route_ref[...]
    ffn = (rec[:, _G1:_G1 + 1] * _load_token_tiles(ybuf, (2 * slot) * tm, tm, d)
           + rec[:, _G2:_G2 + 1] * _load_token_tiles(ybuf, (2 * slot + 1) * tm, tm, d))
    return _layer_norm(alpha * x1_ref[...] + ffn, g_ref[...], b_ref[...])


def _combine_kernel(dest_ref, x1_ref, route_ref, y_hbm, g_ref, b_ref, x2_ref, ybuf, sem, *, alpha):
    x2_ref[...] = _combine_body(dest_ref, x1_ref, route_ref, y_hbm, g_ref, b_ref, ybuf, sem, alpha=alpha)


def _combine_qkv_kernel(dest_ref, x1_ref, route_ref, y_hbm, g_ref, b_ref, wk_ref, wvt_ref, wq_ref,
                        x2_ref, k_ref, vt_ref, q_ref, ybuf, sem, *, alpha, q_scale):
    x2 = _combine_body(dest_ref, x1_ref, route_ref, y_hbm, g_ref, b_ref, ybuf, sem, alpha=alpha)
    x2_ref[...] = x2
    xb = x2.astype(_BF16)
    k_ref[...] = _dot(xb, wk_ref[...]).astype(_BF16)
    q_ref[...] = (_dot(xb, wq_ref[...]) * q_scale).astype(_BF16)
    vt_ref[0] = lax.dot_general(wvt_ref[...], xb, _NT, preferred_element_type=_F32).astype(_BF16)


def _diff_attn_kernel(lam_ref, g_ref, q_ref, k_ref, vt_ref, o_ref, lamfull_ref, s_ref, p_ref, *, lam_init):
    bi, hi = pl.program_id(0), pl.program_id(1)
    seq, hw = k_ref.shape
    tq = tk = ATTN_BLOCK
    nk = seq // tk
    hd = lam_ref.shape[1]

    @pl.when((bi == 0) & (hi == 0))
    def _():
        lam = lam_ref[...]
        a = jnp.sum(lam[0:1, :] * lam[1:2, :], axis=-1, keepdims=True)
        b = jnp.sum(lam[2:3, :] * lam[3:4, :], axis=-1, keepdims=True)
        lamfull_ref[...] = jnp.broadcast_to(jnp.exp(a) - jnp.exp(b) + lam_init, lamfull_ref.shape)

    lam_full = lamfull_ref[0:1, 0:1]
    lane = lax.broadcasted_iota(jnp.int32, (tq, hw), 1)
    row = lax.broadcasted_iota(jnp.int32, (tk, 2 * tq), 0)
    col = lax.broadcasted_iota(jnp.int32, (tk, 2 * tq), 1)
    causal = row <= jnp.where(col < tq, col, col - tq)
    ones = jnp.ones((ONES_ROWS, seq), _BF16)

    for i in range(nk):
        par = i % 2
        kv = (i + 1) * tk
        q = q_ref[i * tq:(i + 1) * tq, :]
        zero = jnp.zeros_like(q)
        qcat = jnp.concatenate([jnp.where(lane < hd, q, zero), jnp.where(lane >= hd, q, zero)], axis=0)
        s_ref[par, 0:kv, :] = lax.dot_general(k_ref[0:kv, :], qcat, _NT, preferred_element_type=_F32)
        s_ref[par, i * tk:kv, :] = jnp.where(causal, s_ref[par, i * tk:kv, :], -jnp.inf)
        colmax = jnp.max(s_ref[par, 0:kv, :], axis=0, keepdims=True)
        p_ref[par, 0:kv, :] = jnp.exp2(s_ref[par, 0:kv, :] - colmax).astype(_BF16)
        vt_aug = jnp.concatenate([vt_ref[0, :, 0:kv], ones[:, 0:kv]], axis=0)
        acc = _dot(vt_aug, p_ref[par, 0:kv, :])
        o = acc[0:hw, :] * (1.0 / acc[hw:hw + 1, :])
        ot = o[:, 0:tq] - lam_full * o[:, tq:]
        ot = ot * lax.rsqrt(jnp.mean(ot * ot, axis=0, keepdims=True) + RMS_EPS)
        o_ref[i * tq:(i + 1) * tq, :] = (ot.T * g_ref[...] * (1.0 - lam_init)).astype(o_ref.dtype)


def _params(*sem):
    return pltpu.CompilerParams(dimension_semantics=sem, vmem_limit_bytes=VMEM_LIMIT_BYTES)


def _full(shape):
    return pl.BlockSpec(shape, lambda *_: (0,) * len(shape))


def _router_weights(rg_w, rg_b, re_w, re_b):
    d = rg_w.shape[0]
    n = rg_w.shape[1] + re_w.shape[1]
    w = jnp.concatenate([rg_w, re_w, jnp.zeros((d, LANES - n), _F32)], axis=1)
    b = jnp.concatenate([rg_b, re_b, jnp.zeros((LANES - n,), _F32)])[None, :]
    wh = w.astype(_BF16)
    wl = (w - wh.astype(_F32)).astype(_BF16)
    return wh, wl, b


def _route_outs(t, d, tm):
    per = d // LANES
    shapes = (jax.ShapeDtypeStruct((t, d), _F32),
              jax.ShapeDtypeStruct((t * per, LANES), _F32),
              jax.ShapeDtypeStruct((t, LANES), _F32),
              jax.ShapeDtypeStruct((1, LANES), _F32))
    specs = (pl.BlockSpec((tm, d), lambda i: (i, 0)),
             pl.BlockSpec((tm * per, LANES), lambda i: (i, 0)),
             pl.BlockSpec((tm, LANES), lambda i: (i, 0)),
             pl.BlockSpec((1, LANES), lambda i: (0, 0)))
    return shapes, specs


def _conv_mixer(x, w_in, conv_w, w_out, ln_g, ln_b, router, *, seq, alpha, epg):
    t, d = x.shape
    tm = ROW_TILE
    wrh, wrl, rb = router
    shapes, specs = _route_outs(t, d, tm)
    return pl.pallas_call(
        functools.partial(_conv_mixer_kernel, tiles_per_seq=seq // tm, alpha=alpha, epg=epg),
        grid=(t // tm,),
        in_specs=[pl.BlockSpec((tm, d), lambda i: (i, 0)),
                  _full(w_in.shape), _full(conv_w.shape), _full(w_out.shape),
                  _full((1, d)), _full((1, d)),
                  _full(wrh.shape), _full(wrl.shape), _full(rb.shape)],
        out_specs=specs, out_shape=shapes,
        scratch_shapes=[pltpu.VMEM((tm + SUBLANES, d), _F32)],
        compiler_params=_params("arbitrary"),
        name="conv_mixer_ln_router",
    )(x, w_in, conv_w, w_out, ln_g[None, :], ln_b[None, :], wrh, wrl, rb)


def _attn_out(x, o, w_o, ln_g, ln_b, router, *, alpha, epg):
    t, d = x.shape
    tm = ROW_TILE
    wrh, wrl, rb = router
    shapes, specs = _route_outs(t, d, tm)
    return pl.pallas_call(
        functools.partial(_attn_out_kernel, alpha=alpha, epg=epg),
        grid=(t // tm,),
        in_specs=[pl.BlockSpec((tm, d), lambda i: (i, 0)),
                  pl.BlockSpec((tm, d), lambda i: (i, 0)),
                  _full(w_o.shape), _full((1, d)), _full((1, d)),
                  _full(wrh.shape), _full(wrl.shape), _full(rb.shape)],
        out_specs=specs, out_shape=shapes,
        compiler_params=_params("arbitrary"),
        name="attn_out_ln_router",
    )(x, o, w_o, ln_g[None, :], ln_b[None, :], wrh, wrl, rb)


def _dispatch_plan(route, counts, n_experts):
    t = route.shape[0]
    blk = EXPERT_BLOCK
    cnt = counts[0, :n_experts].astype(jnp.int32)
    padded = (cnt + blk - 1) // blk * blk
    pad_end = jnp.cumsum(padded)
    pad_start = pad_end - padded
    ids = jnp.arange(n_experts, dtype=jnp.int32)

    def sorted_row(e_slot, r_slot):
        e = route[:, e_slot].astype(jnp.int32)
        start = jnp.sum(jnp.where(e[:, None] == ids[None, :], pad_start[None, :], 0), axis=1)
        return start + route[:, r_slot].astype(jnp.int32)

    dest = jnp.concatenate([sorted_row(_E1, _R1), sorted_row(_E2, _R2)])
    n_blocks = (2 * t + n_experts * blk) // blk
    block_start = jnp.arange(n_blocks, dtype=jnp.int32) * blk
    block_expert = jnp.minimum(jnp.sum(pad_end[None, :] <= block_start[:, None], axis=1),
                               n_experts - 1).astype(jnp.int32)
    n_used = (pad_end[-1:] // blk).astype(jnp.int32)
    seg = jnp.concatenate([pad_start + cnt, pad_end, n_used]).astype(jnp.int32)
    return dest, block_expert, seg


def _expert_ffn(x1t, w_in, w_down, layer, dest, block_expert, seg):
    per = x1t.shape[1]
    _, n_exp, d, de2 = w_in.shape
    de = w_down.shape[2]
    blk = EXPERT_BLOCK
    n_blocks = block_expert.shape[0]
    n_rows = n_blocks * blk
    y = pl.pallas_call(
        functools.partial(_expert_ffn_kernel, layer=layer),
        grid_spec=pltpu.PrefetchScalarGridSpec(
            num_scalar_prefetch=3,
            grid=(n_blocks,),
            in_specs=[pl.BlockSpec(memory_space=pl.ANY),
                      pl.BlockSpec(memory_space=pl.ANY),
                      pl.BlockSpec(memory_space=pl.ANY)],
            out_specs=pl.BlockSpec((blk * per, LANES), lambda i, be, sg, ds: (i, 0)),
            scratch_shapes=[pltpu.SMEM((n_rows,), jnp.int32),
                            pltpu.VMEM((EXPERT_SLOTS * blk * per, LANES), _F32),
                            pltpu.SemaphoreType.DMA((EXPERT_SLOTS,)),
                            pltpu.VMEM((d, de2), _F32),
                            pltpu.VMEM((de, d), _F32),
                            pltpu.SemaphoreType.DMA((2,)),
                            pltpu.VMEM((d, de2), _BF16),
                            pltpu.VMEM((de, d), _BF16)]),
        out_shape=jax.ShapeDtypeStruct((n_rows * per, LANES), _F32),
        compiler_params=_params("arbitrary"),
        name="expert_ffn",
    )(block_expert, seg, dest, x1t, w_in, w_down)
    return y.reshape(n_rows, per, LANES)


def _combine_scratch(tm, d):
    return [pltpu.VMEM((2 * 2 * tm * (d // LANES), LANES), _F32), pltpu.SemaphoreType.DMA((2,))]


def _combine(dest, x1, route, y_rows, ln_g, ln_b, *, alpha):
    t, d = x1.shape
    tm = ROW_TILE
    return pl.pallas_call(
        functools.partial(_combine_kernel, alpha=alpha),
        grid_spec=pltpu.PrefetchScalarGridSpec(
            num_scalar_prefetch=1,
            grid=(t // tm,),
            in_specs=[pl.BlockSpec((tm, d), lambda i, ds: (i, 0)),
                      pl.BlockSpec((tm, LANES), lambda i, ds: (i, 0)),
                      pl.BlockSpec(memory_space=pl.ANY),
                      pl.BlockSpec((1, d), lambda i, ds: (0, 0)),
                      pl.BlockSpec((1, d), lambda i, ds: (0, 0))],
            out_specs=pl.BlockSpec((tm, d), lambda i, ds: (i, 0)),
            scratch_shapes=_combine_scratch(tm, d)),
        out_shape=jax.ShapeDtypeStruct((t, d), _F32),
        compiler_params=_params("arbitrary"),
        name="combine_ln",
    )(dest, x1, route, y_rows, ln_g[None, :], ln_b[None, :])


def _combine_qkv(dest, x1, route, y_rows, ln_g, ln_b, w_k, w_vt, w_q, *, seq, alpha, q_scale):
    t, d = x1.shape
    tm = ROW_TILE
    per_seq = seq // tm
    wspec = pl.BlockSpec((d, d), lambda i, ds: (0, 0))
    row = pl.BlockSpec((tm, d), lambda i, ds: (i, 0))
    return pl.pallas_call(
        functools.partial(_combine_qkv_kernel, alpha=alpha, q_scale=q_scale),
        grid_spec=pltpu.PrefetchScalarGridSpec(
            num_scalar_prefetch=1,
            grid=(t // tm,),
            in_specs=[row,
                      pl.BlockSpec((tm, LANES), lambda i, ds: (i, 0)),
                      pl.BlockSpec(memory_space=pl.ANY),
                      pl.BlockSpec((1, d), lambda i, ds: (0, 0)),
                      pl.BlockSpec((1, d), lambda i, ds: (0, 0)),
                      wspec, wspec, wspec],
            out_specs=(row, row,
                       pl.BlockSpec((1, d, tm), lambda i, ds: (i // per_seq, 0, i % per_seq)),
                       row),
            scratch_shapes=_combine_scratch(tm, d)),
        out_shape=(jax.ShapeDtypeStruct((t, d), _F32),
                   jax.ShapeDtypeStruct((t, d), _BF16),
                   jax.ShapeDtypeStruct((t // seq, d, seq), _BF16),
                   jax.ShapeDtypeStruct((t, d), _BF16)),
        compiler_params=_params("arbitrary"),
        name="combine_ln_qkv",
    )(dest, x1, route, y_rows, ln_g[None, :], ln_b[None, :], w_k, w_vt, w_q)


def _diff_attention(q, k, vt, lam, subln_g, *, batch, seq, lam_init):
    t, d = q.shape
    tq = ATTN_BLOCK
    hw = d // N_HEADS
    head = pl.BlockSpec((seq, hw), lambda b, h: (b, h))
    return pl.pallas_call(
        functools.partial(_diff_attn_kernel, lam_init=lam_init),
        grid=(batch, N_HEADS),
        in_specs=[_full(lam.shape), _full((1, hw)), head, head,
                  pl.BlockSpec((1, hw, seq), lambda b, h: (b, h, 0))],
        out_specs=head,
        out_shape=jax.ShapeDtypeStruct((t, d), _BF16),
        scratch_shapes=[pltpu.VMEM((1, LANES), _F32),
                        pltpu.VMEM((2, seq, 2 * tq), _F32),
                        pltpu.VMEM((2, seq, 2 * tq), _BF16)],
        compiler_params=_params("arbitrary", "arbitrary"),
        name="diff_attention",
    )(lam, subln_g[None, :], q, k, vt)


def _moe(x1t, route, counts, w_in, w_down, layer):
    n_exp, d = w_in.shape[1:3]
    dest, block_expert, seg = _dispatch_plan(route, counts, n_exp)
    x1t = x1t.reshape(-1, d // LANES, LANES)
    y_rows = _expert_ffn(x1t, w_in, w_down, layer, dest, block_expert, seg)
    return dest, y_rows


def kernel(x, a_w_in, a_conv_w, a_w_out, kv_w, b_w_q, b_lambda, b_subln_g, b_w_o,
           ln1_g, ln1_b, ln2_g, ln2_b, rg_w, rg_b, re_w, re_b, e_w_in, e_w_down):
    batch, seq, d = x.shape
    depth = ln1_g.shape[0]
    assert depth == 2 and a_w_in.shape[0] == 1 and b_w_q.shape[0] == 1
    assert seq % ROW_TILE == 0 and seq % ATTN_BLOCK == 0 and ROW_TILE % ATTN_BLOCK == 0
    t = batch * seq
    alpha = (2.0 * depth) ** 0.25
    epg = re_w.shape[2] // N_GROUPS
    head_dim = d // (2 * N_HEADS)
    xf = x.reshape(t, d)

    router0 = _router_weights(rg_w[0], rg_b[0], re_w[0], re_b[0])
    x1, x1t, route, counts = _conv_mixer(xf, a_w_in[0].astype(_BF16), a_conv_w[0],
                                         a_w_out[0].astype(_BF16), ln1_g[0], ln1_b[0], router0,
                                         seq=seq, alpha=alpha, epg=epg)
    dest, y_rows = _moe(x1t, route, counts, e_w_in, e_w_down, 0)
    q_scale = head_dim ** -0.5 * math.log2(math.e)
    x2, k, vt, q = _combine_qkv(dest, x1, route, y_rows, ln2_g[0], ln2_b[0],
                                kv_w[:, :d].astype(_BF16), kv_w[:, d:].T.astype(_BF16),
                                b_w_q[0].astype(_BF16), seq=seq, alpha=alpha, q_scale=q_scale)

    lam_init = 0.8 - 0.6 * math.exp(-0.3 * 1)
    o = _diff_attention(q, k, vt, b_lambda[0], b_subln_g[0], batch=batch, seq=seq, lam_init=lam_init)
    router1 = _router_weights(rg_w[1], rg_b[1], re_w[1], re_b[1])
    x1, x1t, route, counts = _attn_out(x2, o, b_w_o[0].astype(_BF16), ln1_g[1], ln1_b[1], router1,
                                       alpha=alpha, epg=epg)
    dest, y_rows = _moe(x1t, route, counts, e_w_in, e_w_down, 1)
    out = _combine(dest, x1, route, y_rows, ln2_g[1], ln2_b[1], alpha=alpha)
    return out.reshape(batch, seq, d)
```

```python
import functools
import math

import jax
import jax.numpy as jnp
from jax import lax
from jax.experimental import pallas as pl
from jax.experimental.pallas import tpu as pltpu
from jax.experimental.pallas import tpu_sc as plsc

N_HEADS = 8
N_GROUPS = 4
LN_EPS = 1e-5
RMS_EPS = 1e-5

LANES = 128
SUBLANES = 8
VMEM_LIMIT_BYTES = 56 * 1024 * 1024

SC_CORES = 2
SC_SUBCORES = 16

SC_CHUNK = 64
ROW_TILE = 512
EXPERT_BLOCK = 256
ATTN_BLOCK = 256
ISSUE_UNROLL = 8
ONES_ROWS = 16

_E1, _E2, _R1, _R2, _G1, _G2 = range(6)

_F32 = jnp.float32
_BF16 = jnp.bfloat16
_NT = (((1,), (1,)), ((), ()))


def _dot(a, b):
    return jnp.dot(a, b, preferred_element_type=_F32)


def _layer_norm(z, g, b):
    mu = jnp.mean(z, axis=-1, keepdims=True)
    d = z - mu
    var = jnp.mean(d * d, axis=-1, keepdims=True)
    return d * lax.rsqrt(var + LN_EPS) * g + b


def _route_tail(x1, wrh_ref, wrl_ref, rb_ref, route_ref, counts_ref, *, epg):
    tm = x1.shape[0]
    xh = x1.astype(_BF16)
    xl = (x1 - xh.astype(_F32)).astype(_BF16)
    wh = wrh_ref[...]
    logits = _dot(xh, wh) + _dot(xl, wh) + _dot(xh, wrl_ref[...]) + rb_ref[...]

    lane = lax.broadcasted_iota(jnp.int32, (tm, LANES), 1)
    lanef = lane.astype(_F32)
    neg = -jnp.inf
    gl = jnp.where(lane < N_GROUPS, logits, neg)
    gmax = jnp.max(gl, axis=-1, keepdims=True)
    gidx = jnp.min(jnp.where(gl == gmax, lanef, float(LANES)), axis=-1, keepdims=True)
    gtop = 1.0 / jnp.sum(jnp.exp(gl - gmax), axis=-1, keepdims=True)

    lo = float(N_GROUPS) + gidx * float(epg)
    el = jnp.where((lanef >= lo) & (lanef < lo + float(epg)), logits, neg)
    m1 = jnp.max(el, axis=-1, keepdims=True)
    i1 = jnp.min(jnp.where(el == m1, lanef, float(LANES)), axis=-1, keepdims=True)
    el2 = jnp.where(lanef == i1, neg, el)
    m2 = jnp.max(el2, axis=-1, keepdims=True)
    i2 = jnp.min(jnp.where(el2 == m2, lanef, float(LANES)), axis=-1, keepdims=True)
    w2 = jnp.exp(m2 - m1)
    inv = 1.0 / (1.0 + w2)
    g1 = gtop * inv
    g2 = gtop * w2 * inv
    e1 = i1 - float(N_GROUPS)
    e2 = i2 - float(N_GROUPS)

    onehot = jnp.where((lanef == e1) | (lanef == e2), 1.0, 0.0)
    r = lax.broadcasted_iota(jnp.int32, (tm, tm), 0)
    c = lax.broadcasted_iota(jnp.int32, (tm, tm), 1)
    ltri = jnp.where(c < r, 1.0, 0.0).astype(_BF16)
    prefix = _dot(ltri, onehot.astype(_BF16)) + counts_ref[...]
    r1 = jnp.sum(jnp.where(lanef == e1, prefix, 0.0), axis=-1, keepdims=True)
    r2 = jnp.sum(jnp.where(lanef == e2, prefix, 0.0), axis=-1, keepdims=True)
    counts_ref[...] += jnp.sum(onehot, axis=0, keepdims=True)

    rec = jnp.zeros((tm, LANES), _F32)
    for slot, val in ((_E1, e1), (_E2, e2), (_R1, r1), (_R2, r2), (_G1, g1), (_G2, g2)):
        rec = jnp.where(lane == slot, val, rec)
    route_ref[...] = rec


def _store_token_tiles(ref, x):
    n, d = x.shape
    per = d // LANES
    for s in range(per):
        ref[pl.ds(s, n, stride=per), :] = x[:, s * LANES:(s + 1) * LANES]


def _load_token_tiles(ref, first_token, n, d):
    per = d // LANES
    start = first_token * per
    return jnp.concatenate([ref[pl.ds(start + s, n, stride=per), :] for s in range(per)], axis=1)


def _tile_copy(src_hbm, token, dst_vmem, dst_token, sem):
    rows = pl.ds(pl.multiple_of(dst_token * SUBLANES, SUBLANES), SUBLANES)
    return pltpu.make_async_copy(src_hbm.at[token], dst_vmem.at[rows, :], sem)


def _conv_mixer_kernel(x_ref, win_ref, cw_ref, wout_ref, g_ref, b_ref, wrh_ref, wrl_ref, rb_ref,
                       x1_ref, x1t_ref, route_ref, counts_ref, ubuf_ref, *, tiles_per_seq, alpha, epg):
    i = pl.program_id(0)
    tm, d = x_ref.shape

    @pl.when(i == 0)
    def _():
        counts_ref[...] = jnp.zeros_like(counts_ref)

    @pl.when(i % tiles_per_seq == 0)
    def _():
        ubuf_ref[0:SUBLANES, :] = jnp.zeros((SUBLANES, d), _F32)

    x = x_ref[...]
    h = _dot(x.astype(_BF16), win_ref[...])
    u = h[:, d:2 * d] * h[:, 2 * d:]
    ubuf_ref[SUBLANES:SUBLANES + tm, :] = u
    cw = cw_ref[...]
    uc = (cw[0:1, :] * ubuf_ref[SUBLANES - 2:SUBLANES - 2 + tm, :]
          + cw[1:2, :] * ubuf_ref[SUBLANES - 1:SUBLANES - 1 + tm, :]
          + cw[2:3, :] * u)
    ubuf_ref[0:SUBLANES, :] = ubuf_ref[tm:tm + SUBLANES, :]
    y = _dot((h[:, :d] * uc).astype(_BF16), wout_ref[...])
    x1 = _layer_norm(alpha * x + y, g_ref[...], b_ref[...])
    x1_ref[...] = x1
    _store_token_tiles(x1t_ref, x1)
    _route_tail(x1, wrh_ref, wrl_ref, rb_ref, route_ref, counts_ref, epg=epg)


def _attn_out_kernel(x_ref, o_ref, wo_ref, g_ref, b_ref, wrh_ref, wrl_ref, rb_ref,
                     x1_ref, x1t_ref, route_ref, counts_ref, *, alpha, epg):
    @pl.when(pl.program_id(0) == 0)
    def _():
        counts_ref[...] = jnp.zeros_like(counts_ref)

    y = _dot(o_ref[...], wo_ref[...])
    x1 = _layer_norm(alpha * x_ref[...] + y, g_ref[...], b_ref[...])
    x1_ref[...] = x1
    _store_token_tiles(x1t_ref, x1)
    _route_tail(x1, wrh_ref, wrl_ref, rb_ref, route_ref, counts_ref, epg=epg)


def _expert_ffn_kernel(bexp_ref, seg_ref, x_ref, win_hbm, wdn_hbm, y_ref,
                       wstage_in, wstage_dn, wsem, winb, wdnb, *, layer):
    i = pl.program_id(0)
    n_exp = (seg_ref.shape[0] - 1) // 2
    n_used = seg_ref[2 * n_exp]
    blk = y_ref.shape[0] // SUBLANES
    d, de2 = wstage_in.shape
    de = de2 // 2

    def weight_copies(e):
        return (pltpu.make_async_copy(win_hbm.at[layer, e], wstage_in, wsem.at[0]),
                pltpu.make_async_copy(wdn_hbm.at[layer, e], wstage_dn, wsem.at[1]))

    @pl.when(i == 0)
    def _():
        for c in weight_copies(bexp_ref[0]):
            c.start()

    @pl.when(i < n_used)
    def _():
        e = bexp_ref[i]

        @pl.when((i == 0) | (e != bexp_ref[jnp.maximum(i - 1, 0)]))
        def _():
            for c in weight_copies(e):
                c.wait()
            winb[...] = wstage_in[...].astype(_BF16)
            wdnb[...] = wstage_dn[...].astype(_BF16)
            nxt = seg_ref[n_exp + e] // blk

            @pl.when(nxt < n_used)
            def _():
                for c in weight_copies(bexp_ref[nxt]):
                    c.start()

        x = _load_token_tiles(x_ref, 0, blk, d)
        h = _dot(x.astype(_BF16), winb[...])
        g = h[:, :de]
        a = g * jax.nn.sigmoid(g) * h[:, de:]
        _store_token_tiles(y_ref, _dot(a.astype(_BF16), wdnb[...]))

    @pl.when(i >= n_used)
    def _():
        y_ref[...] = jnp.zeros_like(y_ref)


def _combine_body(dest_ref, x1_ref, route_ref, y_hbm, g_ref, b_ref, ybuf, sem, *, alpha):
    i = pl.program_id(0)
    n = pl.num_programs(0)
    tm, d = x1_ref.shape
    slot = i % 2

    def issue(tile, s):
        def body(r, carry):
            t = tile * tm + r
            _tile_copy(y_hbm, dest_ref[t], ybuf, (2 * s) * tm + r, sem.at[s]).start(priority=0)
            _tile_copy(y_hbm, dest_ref[n * tm + t], ybuf, (2 * s + 1) * tm + r, sem.at[s]).start(priority=1)
            return carry
        lax.fori_loop(0, tm, body, 0, unroll=ISSUE_UNROLL)

    @pl.when(i == 0)
    def _():
        issue(0, 0)

    @pl.when(i + 1 < n)
    def _():
        issue(i + 1, 1 - slot)

    def wait(r, carry):
        _tile_copy(y_hbm, 0, ybuf, (2 * slot) * tm + r, sem.at[slot]).wait()
        _tile_copy(y_hbm, 0, ybuf, (2 * slot + 1) * tm + r, sem.at[slot]).wait()
        return carry
    lax.fori_loop(0, tm, wait, 0, unroll=ISSUE_UNROLL)

    rec = route_ref[...]
    ffn = (rec[:, _G1:_G1 + 1] * _load_token_tiles(ybuf, (2 * slot) * tm, tm, d)
           + rec[:, _G2:_G2 + 1] * _load_token_tiles(ybuf, (2 * slot + 1) * tm, tm, d))
    return _layer_norm(alpha * x1_ref[...] + ffn, g_ref[...], b_ref[...])


def _combine_kernel(dest_ref, x1_ref, route_ref, y_hbm, g_ref, b_ref, x2_ref, ybuf, sem, *, alpha):
    x2_ref[...] = _combine_body(dest_ref, x1_ref, route_ref, y_hbm, g_ref, b_ref, ybuf, sem, alpha=alpha)


def _combine_qkv_kernel(dest_ref, x1_ref, route_ref, y_hbm, g_ref, b_ref, wk_ref, wvt_ref, wq_ref,
                        x2_ref, k_ref, vt_ref, q_ref, ybuf, sem, *, alpha, q_scale):
    x2 = _combine_body(dest_ref, x1_ref, route_ref, y_hbm, g_ref, b_ref, ybuf, sem, alpha=alpha)
    x2_ref[...] = x2
    xb = x2.astype(_BF16)
    k_ref[...] = _dot(xb, wk_ref[...]).astype(_BF16)
    q_ref[...] = (_dot(xb, wq_ref[...]) * q_scale).astype(_BF16)
    vt_ref[0] = lax.dot_general(wvt_ref[...], xb, _NT, preferred_element_type=_F32).astype(_BF16)


def _diff_attn_kernel(lam_ref, g_ref, q_ref, k_ref, vt_ref, o_ref, lamfull_ref, s_ref, p_ref, *, lam_init):
    bi, hi = pl.program_id(0), pl.program_id(1)
    seq, hw = k_ref.shape
    tq = tk = ATTN_BLOCK
    nk = seq // tk
    hd = lam_ref.shape[1]

    @pl.when((bi == 0) & (hi == 0))
    def _():
        lam = lam_ref[...]
        a = jnp.sum(lam[0:1, :] * lam[1:2, :], axis=-1, keepdims=True)
        b = jnp.sum(lam[2:3, :] * lam[3:4, :], axis=-1, keepdims=True)
        lamfull_ref[...] = jnp.broadcast_to(jnp.exp(a) - jnp.exp(b) + lam_init, lamfull_ref.shape)

    lam_full = lamfull_ref[0:1, 0:1]
    lane = lax.broadcasted_iota(jnp.int32, (tq, hw), 1)
    row = lax.broadcasted_iota(jnp.int32, (tk, 2 * tq), 0)
    col = lax.broadcasted_iota(jnp.int32, (tk, 2 * tq), 1)
    causal = row <= jnp.where(col < tq, col, col - tq)
    ones = jnp.ones((ONES_ROWS, seq), _BF16)

    for i in range(nk):
        par = i % 2
        kv = (i + 1) * tk
        q = q_ref[i * tq:(i + 1) * tq, :]
        zero = jnp.zeros_like(q)
        qcat = jnp.concatenate([jnp.where(lane < hd, q, zero), jnp.where(lane >= hd, q, zero)], axis=0)
        s_ref[par, 0:kv, :] = lax.dot_general(k_ref[0:kv, :], qcat, _NT, preferred_element_type=_F32)
        s_ref[par, i * tk:kv, :] = jnp.where(causal, s_ref[par, i * tk:kv, :], -jnp.inf)
        colmax = jnp.max(s_ref[par, 0:kv, :], axis=0, keepdims=True)
        p_ref[par, 0:kv, :] = jnp.exp2(s_ref[par, 0:kv, :] - colmax).astype(_BF16)
        vt_aug = jnp.concatenate([vt_ref[0, :, 0:kv], ones[:, 0:kv]], axis=0)
        acc = _dot(vt_aug, p_ref[par, 0:kv, :])
        o = acc[0:hw, :] * (1.0 / acc[hw:hw + 1, :])
        ot = o[:, 0:tq] - lam_full * o[:, tq:]
        ot = ot * lax.rsqrt(jnp.mean(ot * ot, axis=0, keepdims=True) + RMS_EPS)
        o_ref[i * tq:(i + 1) * tq, :] = (ot.T * g_ref[...] * (1.0 - lam_init)).astype(o_ref.dtype)


def _params(*sem):
    return pltpu.CompilerParams(dimension_semantics=sem, vmem_limit_bytes=VMEM_LIMIT_BYTES)


def _full(shape):
    return pl.BlockSpec(shape, lambda *_: (0,) * len(shape))


def _router_weights(rg_w, rg_b, re_w, re_b):
    d = rg_w.shape[0]
    n = rg_w.shape[1] + re_w.shape[1]
    w = jnp.concatenate([rg_w, re_w, jnp.zeros((d, LANES - n), _F32)], axis=1)
    b = jnp.concatenate([rg_b, re_b, jnp.zeros((LANES - n,), _F32)])[None, :]
    wh = w.astype(_BF16)
    wl = (w - wh.astype(_F32)).astype(_BF16)
    return wh, wl, b


def _route_outs(t, d, tm):
    per = d // LANES
    shapes = (jax.ShapeDtypeStruct((t, d), _F32),
              jax.ShapeDtypeStruct((t * per, LANES), _F32),
              jax.ShapeDtypeStruct((t, LANES), _F32),
              jax.ShapeDtypeStruct((1, LANES), _F32))
    specs = (pl.BlockSpec((tm, d), lambda i: (i, 0)),
             pl.BlockSpec((tm * per, LANES), lambda i: (i, 0)),
             pl.BlockSpec((tm, LANES), lambda i: (i, 0)),
             pl.BlockSpec((1, LANES), lambda i: (0, 0)))
    return shapes, specs


def _conv_mixer(x, w_in, conv_w, w_out, ln_g, ln_b, router, *, seq, alpha, epg):
    t, d = x.shape
    tm = ROW_TILE
    wrh, wrl, rb = router
    shapes, specs = _route_outs(t, d, tm)
    return pl.pallas_call(
        functools.partial(_conv_mixer_kernel, tiles_per_seq=seq // tm, alpha=alpha, epg=epg),
        grid=(t // tm,),
        in_specs=[pl.BlockSpec((tm, d), lambda i: (i, 0)),
                  _full(w_in.shape), _full(conv_w.shape), _full(w_out.shape),
                  _full((1, d)), _full((1, d)),
                  _full(wrh.shape), _full(wrl.shape), _full(rb.shape)],
        out_specs=specs, out_shape=shapes,
        scratch_shapes=[pltpu.VMEM((tm + SUBLANES, d), _F32)],
        compiler_params=_params("arbitrary"),
        name="conv_mixer_ln_router",
    )(x, w_in, conv_w, w_out, ln_g[None, :], ln_b[None, :], wrh, wrl, rb)


def _attn_out(x, o, w_o, ln_g, ln_b, router, *, alpha, epg):
    t, d = x.shape
    tm = ROW_TILE
    wrh, wrl, rb = router
    shapes, specs = _route_outs(t, d, tm)
    return pl.pallas_call(
        functools.partial(_attn_out_kernel, alpha=alpha, epg=epg),
        grid=(t // tm,),
        in_specs=[pl.BlockSpec((tm, d), lambda i: (i, 0)),
                  pl.BlockSpec((tm, d), lambda i: (i, 0)),
                  _full(w_o.shape), _full((1, d)), _full((1, d)),
                  _full(wrh.shape), _full(wrl.shape), _full(rb.shape)],
        out_specs=specs, out_shape=shapes,
        compiler_params=_params("arbitrary"),
        name="attn_out_ln_router",
    )(x, o, w_o, ln_g[None, :], ln_b[None, :], wrh, wrl, rb)


def _dispatch_plan(route, counts, n_experts):
    t = route.shape[0]
    blk = EXPERT_BLOCK
    cnt = counts[0, :n_experts].astype(jnp.int32)
    padded = (cnt + blk - 1) // blk * blk
    pad_end = jnp.cumsum(padded)
    pad_start = pad_end - padded
    ids = jnp.arange(n_experts, dtype=jnp.int32)

    def sorted_row(e_slot, r_slot):
        e = route[:, e_slot].astype(jnp.int32)
        start = jnp.sum(jnp.where(e[:, None] == ids[None, :], pad_start[None, :], 0), axis=1)
        return start + route[:, r_slot].astype(jnp.int32)

    dest = jnp.concatenate([sorted_row(_E1, _R1), sorted_row(_E2, _R2)])
    n_blocks = (2 * t + n_experts * blk) // blk
    block_start = jnp.arange(n_blocks, dtype=jnp.int32) * blk
    block_expert = jnp.minimum(jnp.sum(pad_end[None, :] <= block_start[:, None], axis=1),
                               n_experts - 1).astype(jnp.int32)
    n_used = (pad_end[-1:] // blk).astype(jnp.int32)
    seg = jnp.concatenate([pad_start + cnt, pad_end, n_used]).astype(jnp.int32)
    return dest, block_expert, seg


def _sc_dispatch(x1t, dest, n_rows):
    t, per, lanes = x1t.shape
    workers = SC_CORES * SC_SUBCORES
    chunks = t // (workers * SC_CHUNK)
    mesh = plsc.VectorSubcoreMesh(core_axis_name="c", subcore_axis_name="s",
                                  num_cores=SC_CORES, num_subcores=SC_SUBCORES)

    def body(x_hbm, d_hbm, o_hbm, xbuf, idx1, idx2):
        wid = lax.axis_index("c") * SC_SUBCORES + lax.axis_index("s")

        @pl.loop(0, chunks)
        def _(j):
            off = (wid * chunks + j) * SC_CHUNK
            pltpu.sync_copy(x_hbm.at[pl.ds(off, SC_CHUNK)], xbuf)
            pltpu.sync_copy(d_hbm.at[pl.ds(off, SC_CHUNK)], idx1)
            pltpu.sync_copy(d_hbm.at[pl.ds(t + off, SC_CHUNK)], idx2)
            pltpu.sync_copy(xbuf, o_hbm.at[idx1])
            pltpu.sync_copy(xbuf, o_hbm.at[idx2])

    return pl.kernel(
        body,
        out_type=jax.ShapeDtypeStruct((n_rows, per, lanes), x1t.dtype),
        mesh=mesh,
        scratch_types=[pltpu.VMEM((SC_CHUNK, per, lanes), x1t.dtype),
                       pltpu.VMEM((SC_CHUNK,), jnp.int32),
                       pltpu.VMEM((SC_CHUNK,), jnp.int32)],
        name="sc_dispatch",
    )(x1t, dest)


def _expert_ffn(x_sorted, w_in, w_down, layer, block_expert, seg):
    n_rows, per, _ = x_sorted.shape
    _, n_exp, d, de2 = w_in.shape
    de = w_down.shape[2]
    blk = EXPERT_BLOCK
    n_blocks = block_expert.shape[0]
    tiles = pl.BlockSpec((blk * per, LANES), lambda i, be, sg: (i, 0))
    y = pl.pallas_call(
        functools.partial(_expert_ffn_kernel, layer=layer),
        grid_spec=pltpu.PrefetchScalarGridSpec(
            num_scalar_prefetch=2,
            grid=(n_blocks,),
            in_specs=[tiles,
                      pl.BlockSpec(memory_space=pl.ANY),
                      pl.BlockSpec(memory_space=pl.ANY)],
            out_specs=tiles,
            scratch_shapes=[pltpu.VMEM((d, de2), _F32),
                            pltpu.VMEM((de, d), _F32),
                            pltpu.SemaphoreType.DMA((2,)),
                            pltpu.VMEM((d, de2), _BF16),
                            pltpu.VMEM((de, d), _BF16)]),
        out_shape=jax.ShapeDtypeStruct((n_rows * per, LANES), _F32),
        compiler_params=_params("arbitrary"),
        name="expert_ffn",
    )(block_expert, seg, x_sorted.reshape(n_rows * per, LANES), w_in, w_down)
    return y.reshape(n_rows, per, LANES)


def _combine_scratch(tm, d):
    return [pltpu.VMEM((2 * 2 * tm * (d // LANES), LANES), _F32), pltpu.SemaphoreType.DMA((2,))]


def _combine(dest, x1, route, y_rows, ln_g, ln_b, *, alpha):
    t, d = x1.shape
    tm = ROW_TILE
    return pl.pallas_call(
        functools.partial(_combine_kernel, alpha=alpha),
        grid_spec=pltpu.PrefetchScalarGridSpec(
            num_scalar_prefetch=1,
            grid=(t // tm,),
            in_specs=[pl.BlockSpec((tm, d), lambda i, ds: (i, 0)),
                      pl.BlockSpec((tm, LANES), lambda i, ds: (i, 0)),
                      pl.BlockSpec(memory_space=pl.ANY),
                      pl.BlockSpec((1, d), lambda i, ds: (0, 0)),
                      pl.BlockSpec((1, d), lambda i, ds: (0, 0))],
            out_specs=pl.BlockSpec((tm, d), lambda i, ds: (i, 0)),
            scratch_shapes=_combine_scratch(tm, d)),
        out_shape=jax.ShapeDtypeStruct((t, d), _F32),
        compiler_params=_params("arbitrary"),
        name="combine_ln",
    )(dest, x1, route, y_rows, ln_g[None, :], ln_b[None, :])


def _combine_qkv(dest, x1, route, y_rows, ln_g, ln_b, w_k, w_vt, w_q, *, seq, alpha, q_scale):
    t, d = x1.shape
    tm = ROW_TILE
    per_seq = seq // tm
    wspec = pl.BlockSpec((d, d), lambda i, ds: (0, 0))
    row = pl.BlockSpec((tm, d), lambda i, ds: (i, 0))
    return pl.pallas_call(
        functools.partial(_combine_qkv_kernel, alpha=alpha, q_scale=q_scale),
        grid_spec=pltpu.PrefetchScalarGridSpec(
            num_scalar_prefetch=1,
            grid=(t // tm,),
            in_specs=[row,
                      pl.BlockSpec((tm, LANES), lambda i, ds: (i, 0)),
                      pl.BlockSpec(memory_space=pl.ANY),
                      pl.BlockSpec((1, d), lambda i, ds: (0, 0)),
                      pl.BlockSpec((1, d), lambda i, ds: (0, 0)),
                      wspec, wspec, wspec],
            out_specs=(row, row,
                       pl.BlockSpec((1, d, tm), lambda i, ds: (i // per_seq, 0, i % per_seq)),
                       row),
            scratch_shapes=_combine_scratch(tm, d)),
        out_shape=(jax.ShapeDtypeStruct((t, d), _F32),
                   jax.ShapeDtypeStruct((t, d), _BF16),
                   jax.ShapeDtypeStruct((t // seq, d, seq), _BF16),
                   jax.ShapeDtypeStruct((t, d), _BF16)),
        compiler_params=_params("arbitrary"),
        name="combine_ln_qkv",
    )(dest, x1, route, y_rows, ln_g[None, :], ln_b[None, :], w_k, w_vt, w_q)


def _diff_attention(q, k, vt, lam, subln_g, *, batch, seq, lam_init):
    t, d = q.shape
    tq = ATTN_BLOCK
    hw = d // N_HEADS
    head = pl.BlockSpec((seq, hw), lambda b, h: (b, h))
    return pl.pallas_call(
        functools.partial(_diff_attn_kernel, lam_init=lam_init),
        grid=(batch, N_HEADS),
        in_specs=[_full(lam.shape), _full((1, hw)), head, head,
                  pl.BlockSpec((1, hw, seq), lambda b, h: (b, h, 0))],
        out_specs=head,
        out_shape=jax.ShapeDtypeStruct((t, d), _BF16),
        scratch_shapes=[pltpu.VMEM((1, LANES), _F32),
                        pltpu.VMEM((2, seq, 2 * tq), _F32),
                        pltpu.VMEM((2, seq, 2 * tq), _BF16)],
        compiler_params=_params("arbitrary", "arbitrary"),
        name="diff_attention",
    )(lam, subln_g[None, :], q, k, vt)


def _moe(x1t, route, counts, w_in, w_down, layer):
    n_exp, d = w_in.shape[1:3]
    dest, block_expert, seg = _dispatch_plan(route, counts, n_exp)
    x1t = x1t.reshape(-1, d // LANES, LANES)
    x_sorted = _sc_dispatch(x1t, dest, block_expert.shape[0] * EXPERT_BLOCK)
    y_rows = _expert_ffn(x_sorted, w_in, w_down, layer, block_expert, seg)
    return dest, y_rows


def kernel(x, a_w_in, a_conv_w, a_w_out, kv_w, b_w_q, b_lambda, b_subln_g, b_w_o,
           ln1_g, ln1_b, ln2_g, ln2_b, rg_w, rg_b, re_w, re_b, e_w_in, e_w_down):
    batch, seq, d = x.shape
    depth = ln1_g.shape[0]
    assert depth == 2 and a_w_in.shape[0] == 1 and b_w_q.shape[0] == 1
    assert seq % ROW_TILE == 0 and seq % ATTN_BLOCK == 0 and ROW_TILE % ATTN_BLOCK == 0
    t = batch * seq
    alpha = (2.0 * depth) ** 0.25
    epg = re_w.shape[2] // N_GROUPS
    head_dim = d // (2 * N_HEADS)
    xf = x.reshape(t, d)

    router0 = _router_weights(rg_w[0], rg_b[0], re_w[0], re_b[0])
    x1, x1t, route, counts = _conv_mixer(xf, a_w_in[0].astype(_BF16), a_conv_w[0],
                                         a_w_out[0].astype(_BF16), ln1_g[0], ln1_b[0], router0,
                                         seq=seq, alpha=alpha, epg=epg)
    dest, y_rows = _moe(x1t, route, counts, e_w_in, e_w_down, 0)
    q_scale = head_dim ** -0.5 * math.log2(math.e)
    x2, k, vt, q = _combine_qkv(dest, x1, route, y_rows, ln2_g[0], ln2_b[0],
                                kv_w[:, :d].astype(_BF16), kv_w[:, d:].T.astype(_BF16),
                                b_w_q[0].astype(_BF16), seq=seq, alpha=alpha, q_scale=q_scale)

    lam_init = 0.8 - 0.6 * math.exp(-0.3 * 1)
    o = _diff_attention(q, k, vt, b_lambda[0], b_subln_g[0], batch=batch, seq=seq, lam_init=lam_init)
    router1 = _router_weights(rg_w[1], rg_b[1], re_w[1], re_b[1])
    x1, x1t, route, counts = _attn_out(x2, o, b_w_o[0].astype(_BF16), ln1_g[1], ln1_b[1], router1,
                                       alpha=alpha, epg=epg)
    dest, y_rows = _moe(x1t, route, counts, e_w_in, e_w_down, 1)
    out = _combine(dest, x1, route, y_rows, ln2_g[1], ln2_b[1], alpha=alpha)
    return out.reshape(batch, seq, d)
```

```python
import functools
import math

import jax
import jax.numpy as jnp
from jax import lax
from jax.experimental import pallas as pl
from jax.experimental.pallas import tpu as pltpu
from jax.experimental.pallas import tpu_sc as plsc

N_HEADS = 8
N_GROUPS = 4
LN_EPS = 1e-5
RMS_EPS = 1e-5

LANES = 128
SUBLANES = 8
VMEM_LIMIT_BYTES = 56 * 1024 * 1024

SC_CORES = 2
SC_SUBCORES = 16

SC_CHUNK = 64
MIXER_TILE = 1024
ROW_TILE = 512
EXPERT_BLOCK = 256
ATTN_BLOCK = 256
ISSUE_UNROLL = 8
ONES_ROWS = 16

_E1, _E2, _R1, _R2, _G1, _G2 = range(6)

_F32 = jnp.float32
_BF16 = jnp.bfloat16
_NT = (((1,), (1,)), ((), ()))


def _dot(a, b):
    return jnp.dot(a, b, preferred_element_type=_F32)


def _layer_norm(z, g, b):
    mu = jnp.mean(z, axis=-1, keepdims=True)
    d = z - mu
    var = jnp.mean(d * d, axis=-1, keepdims=True)
    return d * lax.rsqrt(var + LN_EPS) * g + b


def _route_tail(x1, wt_ref, bt_ref, route_ref, counts_ref, umat_ref, *, epg):
    tm = x1.shape[0]
    nr = counts_ref.shape[0]
    half = wt_ref.shape[0] // 2

    @pl.when(pl.program_id(0) == 0)
    def _():
        counts_ref[...] = jnp.zeros_like(counts_ref)
        before = (lax.broadcasted_iota(jnp.int32, (tm, tm), 0) < lax.broadcasted_iota(jnp.int32, (tm, tm), 1))
        umat_ref[...] = jnp.where(before, 1.0, 0.0).astype(_BF16)

    xh = x1.astype(_BF16)
    xl = (x1 - xh.astype(_F32)).astype(_BF16)
    a = lax.dot_general(wt_ref[...], xh, _NT, preferred_element_type=_F32)
    b = lax.dot_general(wt_ref[0:half, :], xl, _NT, preferred_element_type=_F32)
    logits = a[0:nr, :] + a[half:half + nr, :] + b[0:nr, :] + bt_ref[0:nr, :]

    rowf = lax.broadcasted_iota(jnp.int32, (nr, tm), 0).astype(_F32)
    neg = -jnp.inf
    big = float(nr)
    gl = jnp.where(rowf < float(N_GROUPS), logits, neg)
    gmax = jnp.max(gl, axis=0, keepdims=True)
    gidx = jnp.min(jnp.where(gl == gmax, rowf, big), axis=0, keepdims=True)
    gtop = 1.0 / jnp.sum(jnp.exp(gl - gmax), axis=0, keepdims=True)

    lo = float(N_GROUPS) + gidx * float(epg)
    el = jnp.where((rowf >= lo) & (rowf < lo + float(epg)), logits, neg)
    m1 = jnp.max(el, axis=0, keepdims=True)
    i1 = jnp.min(jnp.where(el == m1, rowf, big), axis=0, keepdims=True)
    el2 = jnp.where(rowf == i1, neg, el)
    m2 = jnp.max(el2, axis=0, keepdims=True)
    i2 = jnp.min(jnp.where(el2 == m2, rowf, big), axis=0, keepdims=True)
    w2 = jnp.exp(m2 - m1)
    inv = 1.0 / (1.0 + w2)
    g1 = gtop * inv
    g2 = gtop * w2 * inv

    onehot = jnp.where((rowf == i1) | (rowf == i2), 1.0, 0.0)
    total = _dot(onehot.astype(_BF16), umat_ref[...]) + counts_ref[...]
    r1 = jnp.sum(jnp.where(rowf == i1, total, 0.0), axis=0, keepdims=True)
    r2 = jnp.sum(jnp.where(rowf == i2, total, 0.0), axis=0, keepdims=True)
    counts_ref[...] += jnp.sum(onehot, axis=1, keepdims=True)

    slot = lax.broadcasted_iota(jnp.int32, (SUBLANES, tm), 0)
    rec = jnp.zeros((SUBLANES, tm), _F32)
    fields = ((_E1, i1 - float(N_GROUPS)), (_E2, i2 - float(N_GROUPS)), (_R1, r1), (_R2, r2), (_G1, g1), (_G2, g2))
    for s, val in fields:
        rec = jnp.where(slot == s, val, rec)
    rec = jnp.concatenate([rec, jnp.zeros((LANES - SUBLANES, tm), _F32)], axis=0)
    route_ref[...] = rec.T


def _store_token_tiles(ref, x):
    n, d = x.shape
    per = d // LANES
    for s in range(per):
        ref[pl.ds(s, n, stride=per), :] = x[:, s * LANES:(s + 1) * LANES]


def _load_token_tiles(ref, first_token, n, d):
    per = d // LANES
    start = first_token * per
    return jnp.concatenate([ref[pl.ds(start + s, n, stride=per), :] for s in range(per)], axis=1)


def _tile_copy(src_hbm, token, dst_vmem, dst_token, sem):
    rows = pl.ds(pl.multiple_of(dst_token * SUBLANES, SUBLANES), SUBLANES)
    return pltpu.make_async_copy(src_hbm.at[token], dst_vmem.at[rows, :], sem)


def _conv_mixer_kernel(x_ref, win_ref, cw_ref, wout_ref, g_ref, b_ref, wt_ref, bt_ref,
                       x1_ref, x1t_ref, route_ref, counts_ref, ubuf_ref, umat_ref,
                       *, tiles_per_seq, alpha, epg):
    i = pl.program_id(0)
    tm, d = x_ref.shape

    @pl.when(i % tiles_per_seq == 0)
    def _():
        ubuf_ref[0:SUBLANES, :] = jnp.zeros((SUBLANES, d), _F32)

    x = x_ref[...]
    h = _dot(x.astype(_BF16), win_ref[...])
    u = h[:, d:2 * d] * h[:, 2 * d:]
    ubuf_ref[SUBLANES:SUBLANES + tm, :] = u
    cw = cw_ref[...]
    uc = (cw[0:1, :] * ubuf_ref[SUBLANES - 2:SUBLANES - 2 + tm, :]
          + cw[1:2, :] * ubuf_ref[SUBLANES - 1:SUBLANES - 1 + tm, :]
          + cw[2:3, :] * u)
    ubuf_ref[0:SUBLANES, :] = ubuf_ref[tm:tm + SUBLANES, :]
    y = _dot((h[:, :d] * uc).astype(_BF16), wout_ref[...])
    x1 = _layer_norm(alpha * x + y, g_ref[...], b_ref[...])
    x1_ref[...] = x1
    _store_token_tiles(x1t_ref, x1)
    _route_tail(x1, wt_ref, bt_ref, route_ref, counts_ref, umat_ref, epg=epg)


def _attn_out_kernel(x_ref, o_ref, wo_ref, g_ref, b_ref, wt_ref, bt_ref,
                     x1_ref, x1t_ref, route_ref, counts_ref, umat_ref, *, alpha, epg):
    y = _dot(o_ref[...], wo_ref[...])
    x1 = _layer_norm(alpha * x_ref[...] + y, g_ref[...], b_ref[...])
    x1_ref[...] = x1
    _store_token_tiles(x1t_ref, x1)
    _route_tail(x1, wt_ref, bt_ref, route_ref, counts_ref, umat_ref, epg=epg)


def _expert_ffn_kernel(bexp_ref, seg_ref, x_ref, win_hbm, wdn_hbm, y_ref,
                       wstage_in, wstage_dn, wsem, winb, wdnb, *, layer):
    i = pl.program_id(0)
    n_exp = (seg_ref.shape[0] - 1) // 2
    n_used = seg_ref[2 * n_exp]
    blk = y_ref.shape[0] // SUBLANES
    d, de2 = wstage_in.shape
    de = de2 // 2

    def weight_copies(e):
        return (pltpu.make_async_copy(win_hbm.at[layer, e], wstage_in, wsem.at[0]),
                pltpu.make_async_copy(wdn_hbm.at[layer, e], wstage_dn, wsem.at[1]))

    @pl.when(i == 0)
    def _():
        for c in weight_copies(bexp_ref[0]):
            c.start()

    @pl.when(i < n_used)
    def _():
        e = bexp_ref[i]

        @pl.when((i == 0) | (e != bexp_ref[jnp.maximum(i - 1, 0)]))
        def _():
            for c in weight_copies(e):
                c.wait()
            winb[...] = wstage_in[...].astype(_BF16)
            wdnb[...] = wstage_dn[...].astype(_BF16)
            nxt = seg_ref[n_exp + e] // blk

            @pl.when(nxt < n_used)
            def _():
                for c in weight_copies(bexp_ref[nxt]):
                    c.start()

        x = _load_token_tiles(x_ref, 0, blk, d)
        h = _dot(x.astype(_BF16), winb[...])
        g = h[:, :de]
        a = g * jax.nn.sigmoid(g) * h[:, de:]
        _store_token_tiles(y_ref, _dot(a.astype(_BF16), wdnb[...]))

    @pl.when(i >= n_used)
    def _():
        y_ref[...] = jnp.zeros_like(y_ref)


def _combine_body(dest_ref, x1_ref, route_ref, y_hbm, g_ref, b_ref, ybuf, sem, *, alpha):
    i = pl.program_id(0)
    n = pl.num_programs(0)
    tm, d = x1_ref.shape
    slot = i % 2

    def issue(tile, s):
        def body(r, carry):
            t = tile * tm + r
            _tile_copy(y_hbm, dest_ref[t], ybuf, (2 * s) * tm + r, sem.at[s]).start(priority=0)
            _tile_copy(y_hbm, dest_ref[n * tm + t], ybuf, (2 * s + 1) * tm + r, sem.at[s]).start(priority=1)
            return carry
        lax.fori_loop(0, tm, body, 0, unroll=ISSUE_UNROLL)

    @pl.when(i == 0)
    def _():
        issue(0, 0)

    @pl.when(i + 1 < n)
    def _():
        issue(i + 1, 1 - slot)

    def wait(r, carry):
        _tile_copy(y_hbm, 0, ybuf, (2 * slot) * tm + r, sem.at[slot]).wait()
        _tile_copy(y_hbm, 0, ybuf, (2 * slot + 1) * tm + r, sem.at[slot]).wait()
        return carry
    lax.fori_loop(0, tm, wait, 0, unroll=ISSUE_UNROLL)

    rec = route_ref[...]
    ffn = (rec[:, _G1:_G1 + 1] * _load_token_tiles(ybuf, (2 * slot) * tm, tm, d)
           + rec[:, _G2:_G2 + 1] * _load_token_tiles(ybuf, (2 * slot + 1) * tm, tm, d))
    return _layer_norm(alpha * x1_ref[...] + ffn, g_ref[...], b_ref[...])


def _combine_kernel(dest_ref, x1_ref, route_ref, y_hbm, g_ref, b_ref, x2_ref, ybuf, sem, *, alpha):
    x2_ref[...] = _combine_body(dest_ref, x1_ref, route_ref, y_hbm, g_ref, b_ref, ybuf, sem, alpha=alpha)


def _combine_qkv_kernel(dest_ref, x1_ref, route_ref, y_hbm, g_ref, b_ref, wk_ref, wvt_ref, wq_ref,
                        x2_ref, k_ref, vt_ref, q_ref, ybuf, sem, *, alpha, q_scale):
    x2 = _combine_body(dest_ref, x1_ref, route_ref, y_hbm, g_ref, b_ref, ybuf, sem, alpha=alpha)
    x2_ref[...] = x2
    xb = x2.astype(_BF16)
    k_ref[...] = _dot(xb, wk_ref[...]).astype(_BF16)
    q_ref[...] = (_dot(xb, wq_ref[...]) * q_scale).astype(_BF16)
    vt_ref[0] = lax.dot_general(wvt_ref[...], xb, _NT, preferred_element_type=_F32).astype(_BF16)


def _diff_attn_kernel(lam_ref, g_ref, q_ref, k_ref, vt_ref, o_ref, lamfull_ref, s_ref, p_ref, *, lam_init):
    bi, hi = pl.program_id(0), pl.program_id(1)
    seq, hw = k_ref.shape
    tq = tk = ATTN_BLOCK
    nk = seq // tk
    hd = lam_ref.shape[1]

    @pl.when((bi == 0) & (hi == 0))
    def _():
        lam = lam_ref[...]
        a = jnp.sum(lam[0:1, :] * lam[1:2, :], axis=-1, keepdims=True)
        b = jnp.sum(lam[2:3, :] * lam[3:4, :], axis=-1, keepdims=True)
        lamfull_ref[...] = jnp.broadcast_to(jnp.exp(a) - jnp.exp(b) + lam_init, lamfull_ref.shape)

    lam_full = lamfull_ref[0:1, 0:1]
    lane = lax.broadcasted_iota(jnp.int32, (tq, hw), 1)
    row = lax.broadcasted_iota(jnp.int32, (tk, 2 * tq), 0)
    col = lax.broadcasted_iota(jnp.int32, (tk, 2 * tq), 1)
    causal = row <= jnp.where(col < tq, col, col - tq)
    ones = jnp.ones((ONES_ROWS, seq), _BF16)

    for i in range(nk):
        par = i % 2
        kv = (i + 1) * tk
        q = q_ref[i * tq:(i + 1) * tq, :]
        zero = jnp.zeros_like(q)
        qcat = jnp.concatenate([jnp.where(lane < hd, q, zero), jnp.where(lane >= hd, q, zero)], axis=0)
        s_ref[par, 0:kv, :] = lax.dot_general(k_ref[0:kv, :], qcat, _NT, preferred_element_type=_F32)
        s_ref[par, i * tk:kv, :] = jnp.where(causal, s_ref[par, i * tk:kv, :], -jnp.inf)
        colmax = jnp.max(s_ref[par, 0:kv, :], axis=0, keepdims=True)
        p_ref[par, 0:kv, :] = jnp.exp2(s_ref[par, 0:kv, :] - colmax).astype(_BF16)
        vt_aug = jnp.concatenate([vt_ref[0, :, 0:kv], ones[:, 0:kv]], axis=0)
        acc = _dot(vt_aug, p_ref[par, 0:kv, :])
        o = acc[0:hw, :] * (1.0 / acc[hw:hw + 1, :])
        ot = o[:, 0:tq] - lam_full * o[:, tq:]
        ot = ot * lax.rsqrt(jnp.mean(ot * ot, axis=0, keepdims=True) + RMS_EPS)
        o_ref[i * tq:(i + 1) * tq, :] = (ot.T * g_ref[...] * (1.0 - lam_init)).astype(o_ref.dtype)


def _params(*sem):
    return pltpu.CompilerParams(dimension_semantics=sem, vmem_limit_bytes=VMEM_LIMIT_BYTES)


def _full(shape):
    return pl.BlockSpec(shape, lambda *_: (0,) * len(shape))


def _router_weights(rg_w, rg_b, re_w, re_b):
    d = rg_w.shape[0]
    n = rg_w.shape[1] + re_w.shape[1]
    w = jnp.concatenate([rg_w, re_w, jnp.zeros((d, LANES - n), _F32)], axis=1).T
    b = jnp.concatenate([rg_b, re_b, jnp.zeros((LANES - n,), _F32)])[:, None]
    wh = w.astype(_BF16)
    wl = (w - wh.astype(_F32)).astype(_BF16)
    return jnp.concatenate([wh, wl], axis=0), b


def _router_rows(n_experts):
    return -(-(N_GROUPS + n_experts) // (2 * SUBLANES)) * (2 * SUBLANES)


def _route_outs(t, d, tm, nr):
    per = d // LANES
    shapes = (jax.ShapeDtypeStruct((t, d), _F32),
              jax.ShapeDtypeStruct((t * per, LANES), _F32),
              jax.ShapeDtypeStruct((t, LANES), _F32),
              jax.ShapeDtypeStruct((nr, 1), _F32))
    specs = (pl.BlockSpec((tm, d), lambda i: (i, 0)),
             pl.BlockSpec((tm * per, LANES), lambda i: (i, 0)),
             pl.BlockSpec((tm, LANES), lambda i: (i, 0)),
             pl.BlockSpec((nr, 1), lambda i: (0, 0)))
    return shapes, specs


def _conv_mixer(x, w_in, conv_w, w_out, ln_g, ln_b, router, *, seq, alpha, epg):
    t, d = x.shape
    tm = MIXER_TILE
    wt, bt = router
    shapes, specs = _route_outs(t, d, tm, _router_rows(N_GROUPS * epg))
    return pl.pallas_call(
        functools.partial(_conv_mixer_kernel, tiles_per_seq=seq // tm, alpha=alpha, epg=epg),
        grid=(t // tm,),
        in_specs=[pl.BlockSpec((tm, d), lambda i: (i, 0)),
                  _full(w_in.shape), _full(conv_w.shape), _full(w_out.shape),
                  _full((1, d)), _full((1, d)), _full(wt.shape), _full(bt.shape)],
        out_specs=specs, out_shape=shapes,
        scratch_shapes=[pltpu.VMEM((tm + SUBLANES, d), _F32), pltpu.VMEM((tm, tm), _BF16)],
        compiler_params=_params("arbitrary"),
        name="conv_mixer_ln_router",
    )(x, w_in, conv_w, w_out, ln_g[None, :], ln_b[None, :], wt, bt)


def _attn_out(x, o, w_o, ln_g, ln_b, router, *, alpha, epg):
    t, d = x.shape
    tm = MIXER_TILE
    wt, bt = router
    shapes, specs = _route_outs(t, d, tm, _router_rows(N_GROUPS * epg))
    return pl.pallas_call(
        functools.partial(_attn_out_kernel, alpha=alpha, epg=epg),
        grid=(t // tm,),
        in_specs=[pl.BlockSpec((tm, d), lambda i: (i, 0)),
                  pl.BlockSpec((tm, d), lambda i: (i, 0)),
                  _full(w_o.shape), _full((1, d)), _full((1, d)), _full(wt.shape), _full(bt.shape)],
        out_specs=specs, out_shape=shapes,
        scratch_shapes=[pltpu.VMEM((tm, tm), _BF16)],
        compiler_params=_params("arbitrary"),
        name="attn_out_ln_router",
    )(x, o, w_o, ln_g[None, :], ln_b[None, :], wt, bt)


def _dispatch_plan(route, counts, n_experts):
    t = route.shape[0]
    blk = EXPERT_BLOCK
    cnt = counts[N_GROUPS:N_GROUPS + n_experts, 0].astype(jnp.int32)
    padded = (cnt + blk - 1) // blk * blk
    pad_end = jnp.cumsum(padded)
    pad_start = pad_end - padded
    ids = jnp.arange(n_experts, dtype=jnp.int32)

    def sorted_row(e_slot, r_slot):
        e = route[:, e_slot].astype(jnp.int32)
        start = jnp.sum(jnp.where(e[:, None] == ids[None, :], pad_start[None, :], 0), axis=1)
        return start + route[:, r_slot].astype(jnp.int32)

    dest = jnp.concatenate([sorted_row(_E1, _R1), sorted_row(_E2, _R2)])
    n_blocks = (2 * t + n_experts * blk) // blk
    block_start = jnp.arange(n_blocks, dtype=jnp.int32) * blk
    block_expert = jnp.minimum(jnp.sum(pad_end[None, :] <= block_start[:, None], axis=1),
                               n_experts - 1).astype(jnp.int32)
    n_used = (pad_end[-1:] // blk).astype(jnp.int32)
    seg = jnp.concatenate([pad_start + cnt, pad_end, n_used]).astype(jnp.int32)
    return dest, block_expert, seg


def _sc_dispatch(x1t, dest, n_rows):
    t, per, lanes = x1t.shape
    workers = SC_CORES * SC_SUBCORES
    chunks = t // (workers * SC_CHUNK)
    mesh = plsc.VectorSubcoreMesh(core_axis_name="c", subcore_axis_name="s",
                                  num_cores=SC_CORES, num_subcores=SC_SUBCORES)

    def body(x_hbm, d_hbm, o_hbm, xbuf, idx1, idx2):
        wid = lax.axis_index("c") * SC_SUBCORES + lax.axis_index("s")

        @pl.loop(0, chunks)
        def _(j):
            off = (wid * chunks + j) * SC_CHUNK
            pltpu.sync_copy(x_hbm.at[pl.ds(off, SC_CHUNK)], xbuf)
            pltpu.sync_copy(d_hbm.at[pl.ds(off, SC_CHUNK)], idx1)
            pltpu.sync_copy(d_hbm.at[pl.ds(t + off, SC_CHUNK)], idx2)
            pltpu.sync_copy(xbuf, o_hbm.at[idx1])
            pltpu.sync_copy(xbuf, o_hbm.at[idx2])

    return pl.kernel(
        body,
        out_type=jax.ShapeDtypeStruct((n_rows, per, lanes), x1t.dtype),
        mesh=mesh,
        scratch_types=[pltpu.VMEM((SC_CHUNK, per, lanes), x1t.dtype),
                       pltpu.VMEM((SC_CHUNK,), jnp.int32),
                       pltpu.VMEM((SC_CHUNK,), jnp.int32)],
        name="sc_dispatch",
    )(x1t, dest)


def _expert_ffn(x_sorted, w_in, w_down, layer, block_expert, seg):
    n_rows, per, _ = x_sorted.shape
    _, n_exp, d, de2 = w_in.shape
    de = w_down.shape[2]
    blk = EXPERT_BLOCK
    n_blocks = block_expert.shape[0]
    tiles = pl.BlockSpec((blk * per, LANES), lambda i, be, sg: (i, 0))
    y = pl.pallas_call(
        functools.partial(_expert_ffn_kernel, layer=layer),
        grid_spec=pltpu.PrefetchScalarGridSpec(
            num_scalar_prefetch=2,
            grid=(n_blocks,),
            in_specs=[tiles,
                      pl.BlockSpec(memory_space=pl.ANY),
                      pl.BlockSpec(memory_space=pl.ANY)],
            out_specs=tiles,
            scratch_shapes=[pltpu.VMEM((d, de2), _F32),
                            pltpu.VMEM((de, d), _F32),
                            pltpu.SemaphoreType.DMA((2,)),
                            pltpu.VMEM((d, de2), _BF16),
                            pltpu.VMEM((de, d), _BF16)]),
        out_shape=jax.ShapeDtypeStruct((n_rows * per, LANES), _F32),
        compiler_params=_params("arbitrary"),
        name="expert_ffn",
    )(block_expert, seg, x_sorted.reshape(n_rows * per, LANES), w_in, w_down)
    return y.reshape(n_rows, per, LANES)


def _combine_scratch(tm, d):
    return [pltpu.VMEM((2 * 2 * tm * (d // LANES), LANES), _F32), pltpu.SemaphoreType.DMA((2,))]


def _combine(dest, x1, route, y_rows, ln_g, ln_b, *, alpha):
    t, d = x1.shape
    tm = ROW_TILE
    return pl.pallas_call(
        functools.partial(_combine_kernel, alpha=alpha),
        grid_spec=pltpu.PrefetchScalarGridSpec(
            num_scalar_prefetch=1,
            grid=(t // tm,),
            in_specs=[pl.BlockSpec((tm, d), lambda i, ds: (i, 0)),
                      pl.BlockSpec((tm, LANES), lambda i, ds: (i, 0)),
                      pl.BlockSpec(memory_space=pl.ANY),
                      pl.BlockSpec((1, d), lambda i, ds: (0, 0)),
                      pl.BlockSpec((1, d), lambda i, ds: (0, 0))],
            out_specs=pl.BlockSpec((tm, d), lambda i, ds: (i, 0)),
            scratch_shapes=_combine_scratch(tm, d)),
        out_shape=jax.ShapeDtypeStruct((t, d), _F32),
        compiler_params=_params("arbitrary"),
        name="combine_ln",
    )(dest, x1, route, y_rows, ln_g[None, :], ln_b[None, :])


def _combine_qkv(dest, x1, route, y_rows, ln_g, ln_b, w_k, w_vt, w_q, *, seq, alpha, q_scale):
    t, d = x1.shape
    tm = ROW_TILE
    per_seq = seq // tm
    wspec = pl.BlockSpec((d, d), lambda i, ds: (0, 0))
    row = pl.BlockSpec((tm, d), lambda i, ds: (i, 0))
    return pl.pallas_call(
        functools.partial(_combine_qkv_kernel, alpha=alpha, q_scale=q_scale),
        grid_spec=pltpu.PrefetchScalarGridSpec(
            num_scalar_prefetch=1,
            grid=(t // tm,),
            in_specs=[row,
                      pl.BlockSpec((tm, LANES), lambda i, ds: (i, 0)),
                      pl.BlockSpec(memory_space=pl.ANY),
                      pl.BlockSpec((1, d), lambda i, ds: (0, 0)),
                      pl.BlockSpec((1, d), lambda i, ds: (0, 0)),
                      wspec, wspec, wspec],
            out_specs=(row, row,
                       pl.BlockSpec((1, d, tm), lambda i, ds: (i // per_seq, 0, i % per_seq)),
                       row),
            scratch_shapes=_combine_scratch(tm, d)),
        out_shape=(jax.ShapeDtypeStruct((t, d), _F32),
                   jax.ShapeDtypeStruct((t, d), _BF16),
                   jax.ShapeDtypeStruct((t // seq, d, seq), _BF16),
                   jax.ShapeDtypeStruct((t, d), _BF16)),
        compiler_params=_params("arbitrary"),
        name="combine_ln_qkv",
    )(dest, x1, route, y_rows, ln_g[None, :], ln_b[None, :], w_k, w_vt, w_q)


def _diff_attention(q, k, vt, lam, subln_g, *, batch, seq, lam_init):
    t, d = q.shape
    tq = ATTN_BLOCK
    hw = d // N_HEADS
    head = pl.BlockSpec((seq, hw), lambda b, h: (b, h))
    return pl.pallas_call(
        functools.partial(_diff_attn_kernel, lam_init=lam_init),
        grid=(batch, N_HEADS),
        in_specs=[_full(lam.shape), _full((1, hw)), head, head,
                  pl.BlockSpec((1, hw, seq), lambda b, h: (b, h, 0))],
        out_specs=head,
        out_shape=jax.ShapeDtypeStruct((t, d), _BF16),
        scratch_shapes=[pltpu.VMEM((1, LANES), _F32),
                        pltpu.VMEM((2, seq, 2 * tq), _F32),
                        pltpu.VMEM((2, seq, 2 * tq), _BF16)],
        compiler_params=_params("arbitrary", "arbitrary"),
        name="diff_attention",
    )(lam, subln_g[None, :], q, k, vt)


def _moe(x1t, route, counts, w_in, w_down, layer):
    n_exp, d = w_in.shape[1:3]
    dest, block_expert, seg = _dispatch_plan(route, counts, n_exp)
    x1t = x1t.reshape(-1, d // LANES, LANES)
    x_sorted = _sc_dispatch(x1t, dest, block_expert.shape[0] * EXPERT_BLOCK)
    y_rows = _expert_ffn(x_sorted, w_in, w_down, layer, block_expert, seg)
    return dest, y_rows


def kernel(x, a_w_in, a_conv_w, a_w_out, kv_w, b_w_q, b_lambda, b_subln_g, b_w_o,
           ln1_g, ln1_b, ln2_g, ln2_b, rg_w, rg_b, re_w, re_b, e_w_in, e_w_down):
    batch, seq, d = x.shape
    depth = ln1_g.shape[0]
    assert depth == 2 and a_w_in.shape[0] == 1 and b_w_q.shape[0] == 1
    assert seq % MIXER_TILE == 0 and seq % ROW_TILE == 0 and seq % ATTN_BLOCK == 0
    t = batch * seq
    alpha = (2.0 * depth) ** 0.25
    epg = re_w.shape[2] // N_GROUPS
    head_dim = d // (2 * N_HEADS)
    xf = x.reshape(t, d)

    router0 = _router_weights(rg_w[0], rg_b[0], re_w[0], re_b[0])
    x1, x1t, route, counts = _conv_mixer(xf, a_w_in[0].astype(_BF16), a_conv_w[0],
                                         a_w_out[0].astype(_BF16), ln1_g[0], ln1_b[0], router0,
                                         seq=seq, alpha=alpha, epg=epg)
    dest, y_rows = _moe(x1t, route, counts, e_w_in, e_w_down, 0)
    q_scale = head_dim ** -0.5 * math.log2(math.e)
    x2, k, vt, q = _combine_qkv(dest, x1, route, y_rows, ln2_g[0], ln2_b[0],
                                kv_w[:, :d].astype(_BF16), kv_w[:, d:].T.astype(_BF16),
                                b_w_q[0].astype(_BF16), seq=seq, alpha=alpha, q_scale=q_scale)

    lam_init = 0.8 - 0.6 * math.exp(-0.3 * 1)
    o = _diff_attention(q, k, vt, b_lambda[0], b_subln_g[0], batch=batch, seq=seq, lam_init=lam_init)
    router1 = _router_weights(rg_w[1], rg_b[1], re_w[1], re_b[1])
    x1, x1t, route, counts = _attn_out(x2, o, b_w_o[0].astype(_BF16), ln1_g[1], ln1_b[1], router1,
                                       alpha=alpha, epg=epg)
    dest, y_rows = _moe(x1t, route, counts, e_w_in, e_w_down, 1)
    out = _combine(dest, x1, route, y_rows, ln2_g[1], ln2_b[1], alpha=alpha)
    return out.reshape(batch, seq, d)
```

```python
import functools
import math

import jax
import jax.numpy as jnp
from jax import lax
from jax.experimental import pallas as pl
from jax.experimental.pallas import tpu as pltpu
from jax.experimental.pallas import tpu_sc as plsc

N_HEADS = 8
N_GROUPS = 4
LN_EPS = 1e-5
RMS_EPS = 1e-5

LANES = 128
SUBLANES = 8
VMEM_LIMIT_BYTES = 56 * 1024 * 1024

SC_CORES = 2
SC_SUBCORES = 16

SC_CHUNK = 64
MIXER_TILE = 1024
ROW_TILE = 512
EXPERT_BLOCK = 256
ATTN_BLOCK = 256
ATTN_BUFFERS = 2
ISSUE_UNROLL = 8

_E1, _E2, _R1, _R2, _G1, _G2 = range(6)

_F32 = jnp.float32
_BF16 = jnp.bfloat16
_NT = (((1,), (1,)), ((), ()))


def _dot(a, b):
    return jnp.dot(a, b, preferred_element_type=_F32)


def _layer_norm(z, g, b):
    mu = jnp.mean(z, axis=-1, keepdims=True)
    d = z - mu
    var = jnp.mean(d * d, axis=-1, keepdims=True)
    return d * lax.rsqrt(var + LN_EPS) * g + b


def _route_tail(x1, wt_ref, bt_ref, route_ref, counts_ref, umat_ref, *, epg):
    tm = x1.shape[0]
    nr = counts_ref.shape[0]
    half = wt_ref.shape[0] // 2

    @pl.when(pl.program_id(0) == 0)
    def _():
        counts_ref[...] = jnp.zeros_like(counts_ref)
        before = (lax.broadcasted_iota(jnp.int32, (tm, tm), 0) < lax.broadcasted_iota(jnp.int32, (tm, tm), 1))
        umat_ref[...] = jnp.where(before, 1.0, 0.0).astype(_BF16)

    xh = x1.astype(_BF16)
    xl = (x1 - xh.astype(_F32)).astype(_BF16)
    a = lax.dot_general(wt_ref[...], xh, _NT, preferred_element_type=_F32)
    b = lax.dot_general(wt_ref[0:half, :], xl, _NT, preferred_element_type=_F32)
    logits = a[0:nr, :] + a[half:half + nr, :] + b[0:nr, :] + bt_ref[0:nr, :]

    rowf = lax.broadcasted_iota(jnp.int32, (nr, tm), 0).astype(_F32)
    neg = -jnp.inf
    big = float(nr)
    gl = jnp.where(rowf < float(N_GROUPS), logits, neg)
    gmax = jnp.max(gl, axis=0, keepdims=True)
    gidx = jnp.min(jnp.where(gl == gmax, rowf, big), axis=0, keepdims=True)
    gtop = 1.0 / jnp.sum(jnp.exp(gl - gmax), axis=0, keepdims=True)

    lo = float(N_GROUPS) + gidx * float(epg)
    el = jnp.where((rowf >= lo) & (rowf < lo + float(epg)), logits, neg)
    m1 = jnp.max(el, axis=0, keepdims=True)
    i1 = jnp.min(jnp.where(el == m1, rowf, big), axis=0, keepdims=True)
    el2 = jnp.where(rowf == i1, neg, el)
    m2 = jnp.max(el2, axis=0, keepdims=True)
    i2 = jnp.min(jnp.where(el2 == m2, rowf, big), axis=0, keepdims=True)
    w2 = jnp.exp(m2 - m1)
    inv = 1.0 / (1.0 + w2)
    g1 = gtop * inv
    g2 = gtop * w2 * inv

    onehot = jnp.where((rowf == i1) | (rowf == i2), 1.0, 0.0)
    total = _dot(onehot.astype(_BF16), umat_ref[...]) + counts_ref[...]
    r1 = jnp.sum(jnp.where(rowf == i1, total, 0.0), axis=0, keepdims=True)
    r2 = jnp.sum(jnp.where(rowf == i2, total, 0.0), axis=0, keepdims=True)
    counts_ref[...] += jnp.sum(onehot, axis=1, keepdims=True)

    slot = lax.broadcasted_iota(jnp.int32, (SUBLANES, tm), 0)
    rec = jnp.zeros((SUBLANES, tm), _F32)
    fields = ((_E1, i1 - float(N_GROUPS)), (_E2, i2 - float(N_GROUPS)), (_R1, r1), (_R2, r2), (_G1, g1), (_G2, g2))
    for s, val in fields:
        rec = jnp.where(slot == s, val, rec)
    rec = jnp.concatenate([rec, jnp.zeros((LANES - SUBLANES, tm), _F32)], axis=0)
    route_ref[...] = rec.T


def _store_token_tiles(ref, x):
    n, d = x.shape
    per = d // LANES
    for s in range(per):
        ref[pl.ds(s, n, stride=per), :] = x[:, s * LANES:(s + 1) * LANES]


def _load_token_tiles(ref, first_token, n, d):
    per = d // LANES
    start = first_token * per
    return jnp.concatenate([ref[pl.ds(start + s, n, stride=per), :] for s in range(per)], axis=1)


def _tile_copy(src_hbm, token, dst_vmem, dst_token, sem):
    rows = pl.ds(pl.multiple_of(dst_token * SUBLANES, SUBLANES), SUBLANES)
    return pltpu.make_async_copy(src_hbm.at[token], dst_vmem.at[rows, :], sem)


def _conv_mixer_kernel(x_ref, win_ref, cw_ref, wout_ref, g_ref, b_ref, wt_ref, bt_ref,
                       x1_ref, x1t_ref, route_ref, counts_ref, ubuf_ref, umat_ref,
                       *, tiles_per_seq, alpha, epg):
    i = pl.program_id(0)
    tm, d = x_ref.shape

    @pl.when(i % tiles_per_seq == 0)
    def _():
        ubuf_ref[0:SUBLANES, :] = jnp.zeros((SUBLANES, d), _F32)

    x = x_ref[...]
    h = _dot(x.astype(_BF16), win_ref[...])
    u = h[:, d:2 * d] * h[:, 2 * d:]
    ubuf_ref[SUBLANES:SUBLANES + tm, :] = u
    cw = cw_ref[...]
    uc = (cw[0:1, :] * ubuf_ref[SUBLANES - 2:SUBLANES - 2 + tm, :]
          + cw[1:2, :] * ubuf_ref[SUBLANES - 1:SUBLANES - 1 + tm, :]
          + cw[2:3, :] * u)
    ubuf_ref[0:SUBLANES, :] = ubuf_ref[tm:tm + SUBLANES, :]
    y = _dot((h[:, :d] * uc).astype(_BF16), wout_ref[...])
    x1 = _layer_norm(alpha * x + y, g_ref[...], b_ref[...])
    x1_ref[...] = x1
    _store_token_tiles(x1t_ref, x1)
    _route_tail(x1, wt_ref, bt_ref, route_ref, counts_ref, umat_ref, epg=epg)


def _attn_out_kernel(x_ref, o_ref, wo_ref, g_ref, b_ref, wt_ref, bt_ref,
                     x1_ref, x1t_ref, route_ref, counts_ref, umat_ref, *, alpha, epg):
    y = _dot(o_ref[...], wo_ref[...])
    x1 = _layer_norm(alpha * x_ref[...] + y, g_ref[...], b_ref[...])
    x1_ref[...] = x1
    _store_token_tiles(x1t_ref, x1)
    _route_tail(x1, wt_ref, bt_ref, route_ref, counts_ref, umat_ref, epg=epg)


def _expert_ffn_kernel(bexp_ref, seg_ref, x_ref, win_hbm, wdn_hbm, y_ref,
                       wstage_in, wstage_dn, wsem, winb, wdnb, *, layer):
    i = pl.program_id(0)
    n_exp = (seg_ref.shape[0] - 1) // 2
    n_used = seg_ref[2 * n_exp]
    blk = y_ref.shape[0] // SUBLANES
    d, de2 = wstage_in.shape
    de = de2 // 2

    def weight_copies(e):
        return (pltpu.make_async_copy(win_hbm.at[layer, e], wstage_in, wsem.at[0]),
                pltpu.make_async_copy(wdn_hbm.at[layer, e], wstage_dn, wsem.at[1]))

    @pl.when(i == 0)
    def _():
        for c in weight_copies(bexp_ref[0]):
            c.start()

    @pl.when(i < n_used)
    def _():
        e = bexp_ref[i]

        @pl.when((i == 0) | (e != bexp_ref[jnp.maximum(i - 1, 0)]))
        def _():
            for c in weight_copies(e):
                c.wait()
            winb[...] = wstage_in[...].astype(_BF16)
            wdnb[...] = wstage_dn[...].astype(_BF16)
            nxt = seg_ref[n_exp + e] // blk

            @pl.when(nxt < n_used)
            def _():
                for c in weight_copies(bexp_ref[nxt]):
                    c.start()

        x = _load_token_tiles(x_ref, 0, blk, d)
        h = _dot(x.astype(_BF16), winb[...])
        g = h[:, :de]
        a = g * jax.nn.sigmoid(g) * h[:, de:]
        _store_token_tiles(y_ref, _dot(a.astype(_BF16), wdnb[...]))

    @pl.when(i >= n_used)
    def _():
        y_ref[...] = jnp.zeros_like(y_ref)


def _combine_body(dest_ref, x1_ref, route_ref, y_hbm, g_ref, b_ref, ybuf, sem, *, alpha):
    i = pl.program_id(0)
    n = pl.num_programs(0)
    tm, d = x1_ref.shape
    slot = i % 2

    def issue(tile, s):
        def body(r, carry):
            t = tile * tm + r
            _tile_copy(y_hbm, dest_ref[t], ybuf, (2 * s) * tm + r, sem.at[s]).start(priority=0)
            _tile_copy(y_hbm, dest_ref[n * tm + t], ybuf, (2 * s + 1) * tm + r, sem.at[s]).start(priority=1)
            return carry
        lax.fori_loop(0, tm, body, 0, unroll=ISSUE_UNROLL)

    @pl.when(i == 0)
    def _():
        issue(0, 0)

    @pl.when(i + 1 < n)
    def _():
        issue(i + 1, 1 - slot)

    def wait(r, carry):
        _tile_copy(y_hbm, 0, ybuf, (2 * slot) * tm + r, sem.at[slot]).wait()
        _tile_copy(y_hbm, 0, ybuf, (2 * slot + 1) * tm + r, sem.at[slot]).wait()
        return carry
    lax.fori_loop(0, tm, wait, 0, unroll=ISSUE_UNROLL)

    rec = route_ref[...]
    ffn = (rec[:, _G1:_G1 + 1] * _load_token_tiles(ybuf, (2 * slot) * tm, tm, d)
           + rec[:, _G2:_G2 + 1] * _load_token_tiles(ybuf, (2 * slot + 1) * tm, tm, d))
    return _layer_norm(alpha * x1_ref[...] + ffn, g_ref[...], b_ref[...])


def _combine_kernel(dest_ref, x1_ref, route_ref, y_hbm, g_ref, b_ref, x2_ref, ybuf, sem, *, alpha):
    x2_ref[...] = _combine_body(dest_ref, x1_ref, route_ref, y_hbm, g_ref, b_ref, ybuf, sem, alpha=alpha)


def _combine_qkv_kernel(dest_ref, x1_ref, route_ref, y_hbm, g_ref, b_ref, wk_ref, wv_ref, wq_ref,
                        x2_ref, k_ref, v_ref, q_ref, ybuf, sem, *, alpha, q_scale):
    x2 = _combine_body(dest_ref, x1_ref, route_ref, y_hbm, g_ref, b_ref, ybuf, sem, alpha=alpha)
    x2_ref[...] = x2
    xb = x2.astype(_BF16)
    k_ref[...] = _dot(xb, wk_ref[...]).astype(_BF16)
    q_ref[...] = (_dot(xb, wq_ref[...]) * q_scale).astype(_BF16)
    v_ref[...] = _dot(xb, wv_ref[...]).astype(_BF16)


def _diff_attn_kernel(lam_ref, g_ref, q_ref, k_ref, v_ref, o_ref, lamfull_ref, s_ref, p_ref, *, lam_init):
    bi, hi = pl.program_id(0), pl.program_id(1)
    seq, hw = k_ref.shape
    tq = tk = ATTN_BLOCK
    nk = seq // tk
    hd = lam_ref.shape[1]

    @pl.when((bi == 0) & (hi == 0))
    def _():
        lam = lam_ref[...]
        a = jnp.sum(lam[0:1, :] * lam[1:2, :], axis=-1, keepdims=True)
        b = jnp.sum(lam[2:3, :] * lam[3:4, :], axis=-1, keepdims=True)
        lamfull_ref[...] = jnp.broadcast_to(jnp.exp(a) - jnp.exp(b) + lam_init, lamfull_ref.shape)

    lam_full = lamfull_ref[0:1, 0:1]
    lane = lax.broadcasted_iota(jnp.int32, (tq, hw), 1)
    row = lax.broadcasted_iota(jnp.int32, (2 * tq, tk), 0)
    col = lax.broadcasted_iota(jnp.int32, (2 * tq, tk), 1)
    causal = col <= jnp.where(row < tq, row, row - tq)
    v_aug = jnp.concatenate([v_ref[...], jnp.ones((seq, hw), _BF16)], axis=1)

    for i in reversed(range(nk)):
        par = i % s_ref.shape[0]
        kv = (i + 1) * tk
        q = q_ref[i * tq:(i + 1) * tq, :]
        zero = jnp.zeros_like(q)
        qcat = jnp.concatenate([jnp.where(lane < hd, q, zero), jnp.where(lane >= hd, q, zero)], axis=0)
        s_ref[par, :, 0:kv] = lax.dot_general(qcat, k_ref[0:kv, :], _NT, preferred_element_type=_F32)
        s_ref[par, :, i * tk:kv] = jnp.where(causal, s_ref[par, :, i * tk:kv], -jnp.inf)
        rowmax = jnp.max(s_ref[par, :, 0:kv], axis=1, keepdims=True)
        p_ref[par, :, 0:kv] = jnp.exp2(s_ref[par, :, 0:kv] - rowmax).astype(_BF16)
        acc = _dot(p_ref[par, :, 0:kv], v_aug[0:kv, :])
        o = acc[:, 0:hw] * (1.0 / acc[:, hw:hw + 1])
        od = o[0:tq, :] - lam_full * o[tq:, :]
        od = od * lax.rsqrt(jnp.mean(od * od, axis=1, keepdims=True) + RMS_EPS)
        o_ref[i * tq:(i + 1) * tq, :] = (od * g_ref[...] * (1.0 - lam_init)).astype(o_ref.dtype)


def _params(*sem):
    return pltpu.CompilerParams(dimension_semantics=sem, vmem_limit_bytes=VMEM_LIMIT_BYTES)


def _full(shape):
    return pl.BlockSpec(shape, lambda *_: (0,) * len(shape))


def _router_weights(rg_w, rg_b, re_w, re_b):
    d = rg_w.shape[0]
    n = rg_w.shape[1] + re_w.shape[1]
    w = jnp.concatenate([rg_w, re_w, jnp.zeros((d, LANES - n), _F32)], axis=1).T
    b = jnp.concatenate([rg_b, re_b, jnp.zeros((LANES - n,), _F32)])[:, None]
    wh = w.astype(_BF16)
    wl = (w - wh.astype(_F32)).astype(_BF16)
    return jnp.concatenate([wh, wl], axis=0), b


def _router_rows(n_experts):
    return -(-(N_GROUPS + n_experts) // (2 * SUBLANES)) * (2 * SUBLANES)


def _route_outs(t, d, tm, nr):
    per = d // LANES
    shapes = (jax.ShapeDtypeStruct((t, d), _F32),
              jax.ShapeDtypeStruct((t * per, LANES), _F32),
              jax.ShapeDtypeStruct((t, LANES), _F32),
              jax.ShapeDtypeStruct((nr, 1), _F32))
    specs = (pl.BlockSpec((tm, d), lambda i: (i, 0)),
             pl.BlockSpec((tm * per, LANES), lambda i: (i, 0)),
             pl.BlockSpec((tm, LANES), lambda i: (i, 0)),
             pl.BlockSpec((nr, 1), lambda i: (0, 0)))
    return shapes, specs


def _conv_mixer(x, w_in, conv_w, w_out, ln_g, ln_b, router, *, seq, alpha, epg):
    t, d = x.shape
    tm = MIXER_TILE
    wt, bt = router
    shapes, specs = _route_outs(t, d, tm, _router_rows(N_GROUPS * epg))
    return pl.pallas_call(
        functools.partial(_conv_mixer_kernel, tiles_per_seq=seq // tm, alpha=alpha, epg=epg),
        grid=(t // tm,),
        in_specs=[pl.BlockSpec((tm, d), lambda i: (i, 0)),
                  _full(w_in.shape), _full(conv_w.shape), _full(w_out.shape),
                  _full((1, d)), _full((1, d)), _full(wt.shape), _full(bt.shape)],
        out_specs=specs, out_shape=shapes,
        scratch_shapes=[pltpu.VMEM((tm + SUBLANES, d), _F32), pltpu.VMEM((tm, tm), _BF16)],
        compiler_params=_params("arbitrary"),
        name="conv_mixer_ln_router",
    )(x, w_in, conv_w, w_out, ln_g[None, :], ln_b[None, :], wt, bt)


def _attn_out(x, o, w_o, ln_g, ln_b, router, *, alpha, epg):
    t, d = x.shape
    tm = MIXER_TILE
    wt, bt = router
    shapes, specs = _route_outs(t, d, tm, _router_rows(N_GROUPS * epg))
    return pl.pallas_call(
        functools.partial(_attn_out_kernel, alpha=alpha, epg=epg),
        grid=(t // tm,),
        in_specs=[pl.BlockSpec((tm, d), lambda i: (i, 0)),
                  pl.BlockSpec((tm, d), lambda i: (i, 0)),
                  _full(w_o.shape), _full((1, d)), _full((1, d)), _full(wt.shape), _full(bt.shape)],
        out_specs=specs, out_shape=shapes,
        scratch_shapes=[pltpu.VMEM((tm, tm), _BF16)],
        compiler_params=_params("arbitrary"),
        name="attn_out_ln_router",
    )(x, o, w_o, ln_g[None, :], ln_b[None, :], wt, bt)


def _dispatch_plan(route, counts, n_experts):
    t = route.shape[0]
    blk = EXPERT_BLOCK
    cnt = counts[N_GROUPS:N_GROUPS + n_experts, 0].astype(jnp.int32)
    padded = (cnt + blk - 1) // blk * blk
    pad_end = jnp.cumsum(padded)
    pad_start = pad_end - padded
    ids = jnp.arange(n_experts, dtype=jnp.int32)

    def sorted_row(e_slot, r_slot):
        e = route[:, e_slot].astype(jnp.int32)
        start = jnp.sum(jnp.where(ids[:, None] == e[None, :], pad_start[:, None], 0), axis=0)
        return start + route[:, r_slot].astype(jnp.int32)

    dest = jnp.concatenate([sorted_row(_E1, _R1), sorted_row(_E2, _R2)])
    n_blocks = (2 * t + n_experts * blk) // blk
    block_start = jnp.arange(n_blocks, dtype=jnp.int32) * blk
    block_expert = jnp.minimum(jnp.sum(pad_end[None, :] <= block_start[:, None], axis=1),
                               n_experts - 1).astype(jnp.int32)
    n_used = (pad_end[-1:] // blk).astype(jnp.int32)
    seg = jnp.concatenate([pad_start + cnt, pad_end, n_used]).astype(jnp.int32)
    return dest, block_expert, seg


def _sc_dispatch(x1t, dest, n_rows):
    t, per, lanes = x1t.shape
    workers = SC_CORES * SC_SUBCORES
    chunks = t // (workers * SC_CHUNK)
    mesh = plsc.VectorSubcoreMesh(core_axis_name="c", subcore_axis_name="s",
                                  num_cores=SC_CORES, num_subcores=SC_SUBCORES)

    def body(x_hbm, d_hbm, o_hbm, xbuf, idx1, idx2):
        wid = lax.axis_index("c") * SC_SUBCORES + lax.axis_index("s")

        @pl.loop(0, chunks)
        def _(j):
            off = (wid * chunks + j) * SC_CHUNK
            pltpu.sync_copy(x_hbm.at[pl.ds(off, SC_CHUNK)], xbuf)
            pltpu.sync_copy(d_hbm.at[pl.ds(off, SC_CHUNK)], idx1)
            pltpu.sync_copy(d_hbm.at[pl.ds(t + off, SC_CHUNK)], idx2)
            pltpu.sync_copy(xbuf, o_hbm.at[idx1])
            pltpu.sync_copy(xbuf, o_hbm.at[idx2])

    return pl.kernel(
        body,
        out_type=jax.ShapeDtypeStruct((n_rows, per, lanes), x1t.dtype),
        mesh=mesh,
        scratch_types=[pltpu.VMEM((SC_CHUNK, per, lanes), x1t.dtype),
                       pltpu.VMEM((SC_CHUNK,), jnp.int32),
                       pltpu.VMEM((SC_CHUNK,), jnp.int32)],
        name="sc_dispatch",
    )(x1t, dest)


def _expert_ffn(x_sorted, w_in, w_down, layer, block_expert, seg):
    n_rows, per, _ = x_sorted.shape
    _, n_exp, d, de2 = w_in.shape
    de = w_down.shape[2]
    blk = EXPERT_BLOCK
    n_blocks = block_expert.shape[0]
    tiles = pl.BlockSpec((blk * per, LANES), lambda i, be, sg: (i, 0))
    y = pl.pallas_call(
        functools.partial(_expert_ffn_kernel, layer=layer),
        grid_spec=pltpu.PrefetchScalarGridSpec(
            num_scalar_prefetch=2,
            grid=(n_blocks,),
            in_specs=[tiles,
                      pl.BlockSpec(memory_space=pl.ANY),
                      pl.BlockSpec(memory_space=pl.ANY)],
            out_specs=tiles,
            scratch_shapes=[pltpu.VMEM((d, de2), _F32),
                            pltpu.VMEM((de, d), _F32),
                            pltpu.SemaphoreType.DMA((2,)),
                            pltpu.VMEM((d, de2), _BF16),
                            pltpu.VMEM((de, d), _BF16)]),
        out_shape=jax.ShapeDtypeStruct((n_rows * per, LANES), _F32),
        compiler_params=_params("arbitrary"),
        name="expert_ffn",
    )(block_expert, seg, x_sorted.reshape(n_rows * per, LANES), w_in, w_down)
    return y.reshape(n_rows, per, LANES)


def _combine_scratch(tm, d):
    return [pltpu.VMEM((2 * 2 * tm * (d // LANES), LANES), _F32), pltpu.SemaphoreType.DMA((2,))]


def _combine(dest, x1, route, y_rows, ln_g, ln_b, *, alpha):
    t, d = x1.shape
    tm = ROW_TILE
    return pl.pallas_call(
        functools.partial(_combine_kernel, alpha=alpha),
        grid_spec=pltpu.PrefetchScalarGridSpec(
            num_scalar_prefetch=1,
            grid=(t // tm,),
            in_specs=[pl.BlockSpec((tm, d), lambda i, ds: (i, 0)),
                      pl.BlockSpec((tm, LANES), lambda i, ds: (i, 0)),
                      pl.BlockSpec(memory_space=pl.ANY),
                      pl.BlockSpec((1, d), lambda i, ds: (0, 0)),
                      pl.BlockSpec((1, d), lambda i, ds: (0, 0))],
            out_specs=pl.BlockSpec((tm, d), lambda i, ds: (i, 0)),
            scratch_shapes=_combine_scratch(tm, d)),
        out_shape=jax.ShapeDtypeStruct((t, d), _F32),
        compiler_params=_params("arbitrary"),
        name="combine_ln",
    )(dest, x1, route, y_rows, ln_g[None, :], ln_b[None, :])


def _combine_qkv(dest, x1, route, y_rows, ln_g, ln_b, w_k, w_v, w_q, *, alpha, q_scale):
    t, d = x1.shape
    tm = ROW_TILE
    wspec = pl.BlockSpec((d, d), lambda i, ds: (0, 0))
    row = pl.BlockSpec((tm, d), lambda i, ds: (i, 0))
    return pl.pallas_call(
        functools.partial(_combine_qkv_kernel, alpha=alpha, q_scale=q_scale),
        grid_spec=pltpu.PrefetchScalarGridSpec(
            num_scalar_prefetch=1,
            grid=(t // tm,),
            in_specs=[row,
                      pl.BlockSpec((tm, LANES), lambda i, ds: (i, 0)),
                      pl.BlockSpec(memory_space=pl.ANY),
                      pl.BlockSpec((1, d), lambda i, ds: (0, 0)),
                      pl.BlockSpec((1, d), lambda i, ds: (0, 0)),
                      wspec, wspec, wspec],
            out_specs=(row, row, row, row),
            scratch_shapes=_combine_scratch(tm, d)),
        out_shape=(jax.ShapeDtypeStruct((t, d), _F32),
                   jax.ShapeDtypeStruct((t, d), _BF16),
                   jax.ShapeDtypeStruct((t, d), _BF16),
                   jax.ShapeDtypeStruct((t, d), _BF16)),
        compiler_params=_params("arbitrary"),
        name="combine_ln_qkv",
    )(dest, x1, route, y_rows, ln_g[None, :], ln_b[None, :], w_k, w_v, w_q)


def _diff_attention(q, k, v, lam, subln_g, *, batch, seq, lam_init):
    t, d = q.shape
    tq = ATTN_BLOCK
    hw = d // N_HEADS
    head = pl.BlockSpec((seq, hw), lambda b, h: (b, h))
    return pl.pallas_call(
        functools.partial(_diff_attn_kernel, lam_init=lam_init),
        grid=(batch, N_HEADS),
        in_specs=[_full(lam.shape), _full((1, hw)), head, head, head],
        out_specs=head,
        out_shape=jax.ShapeDtypeStruct((t, d), _BF16),
        scratch_shapes=[pltpu.VMEM((1, LANES), _F32),
                        pltpu.VMEM((ATTN_BUFFERS, 2 * tq, seq), _F32),
                        pltpu.VMEM((ATTN_BUFFERS, 2 * tq, seq), _BF16)],
        compiler_params=_params("arbitrary", "arbitrary"),
        name="diff_attention",
    )(lam, subln_g[None, :], q, k, v)


def _moe(x1t, route, counts, w_in, w_down, layer):
    n_exp, d = w_in.shape[1:3]
    dest, block_expert, seg = _dispatch_plan(route, counts, n_exp)
    x1t = x1t.reshape(-1, d // LANES, LANES)
    x_sorted = _sc_dispatch(x1t, dest, block_expert.shape[0] * EXPERT_BLOCK)
    y_rows = _expert_ffn(x_sorted, w_in, w_down, layer, block_expert, seg)
    return dest, y_rows


def kernel(x, a_w_in, a_conv_w, a_w_out, kv_w, b_w_q, b_lambda, b_subln_g, b_w_o,
           ln1_g, ln1_b, ln2_g, ln2_b, rg_w, rg_b, re_w, re_b, e_w_in, e_w_down):
    batch, seq, d = x.shape
    depth = ln1_g.shape[0]
    assert depth == 2 and a_w_in.shape[0] == 1 and b_w_q.shape[0] == 1
    assert seq % MIXER_TILE == 0 and seq % ROW_TILE == 0 and seq % ATTN_BLOCK == 0
    t = batch * seq
    alpha = (2.0 * depth) ** 0.25
    epg = re_w.shape[2] // N_GROUPS
    head_dim = d // (2 * N_HEADS)
    xf = x.reshape(t, d)

    router0 = _router_weights(rg_w[0], rg_b[0], re_w[0], re_b[0])
    x1, x1t, route, counts = _conv_mixer(xf, a_w_in[0].astype(_BF16), a_conv_w[0],
                                         a_w_out[0].astype(_BF16), ln1_g[0], ln1_b[0], router0,
                                         seq=seq, alpha=alpha, epg=epg)
    dest, y_rows = _moe(x1t, route, counts, e_w_in, e_w_down, 0)
    q_scale = head_dim ** -0.5 * math.log2(math.e)
    x2, k, v, q = _combine_qkv(dest, x1, route, y_rows, ln2_g[0], ln2_b[0],
                               kv_w[:, :d].astype(_BF16), kv_w[:, d:].astype(_BF16),
                               b_w_q[0].astype(_BF16), alpha=alpha, q_scale=q_scale)

    lam_init = 0.8 - 0.6 * math.exp(-0.3 * 1)
    o = _diff_attention(q, k, v, b_lambda[0], b_subln_g[0], batch=batch, seq=seq, lam_init=lam_init)
    router1 = _router_weights(rg_w[1], rg_b[1], re_w[1], re_b[1])
    x1, x1t, route, counts = _attn_out(x2, o, b_w_o[0].astype(_BF16), ln1_g[1], ln1_b[1], router1,
                                       alpha=alpha, epg=epg)
    dest, y_rows = _moe(x1t, route, counts, e_w_in, e_w_down, 1)
    out = _combine(dest, x1, route, y_rows, ln2_g[1], ln2_b[1], alpha=alpha)
    return out.reshape(batch, seq, d)
```

```python
import functools
import math

import jax
import jax.numpy as jnp
from jax import lax
from jax.experimental import pallas as pl
from jax.experimental.pallas import tpu as pltpu
from jax.experimental.pallas import tpu_sc as plsc

N_HEADS = 8
N_GROUPS = 4
LN_EPS = 1e-5
RMS_EPS = 1e-5

LANES = 128
SUBLANES = 8
VMEM_LIMIT_BYTES = 56 * 1024 * 1024

SC_CORES = 2
SC_SUBCORES = 16

SC_CHUNK = 32
MIXER_TILE = 1024
ROW_TILE = 512
EXPERT_BLOCK = 256
ATTN_BLOCK = 256
ATTN_BUFFERS = 2
ISSUE_UNROLL = 8

_E1, _E2, _R1, _R2, _G1, _G2 = range(6)

_F32 = jnp.float32
_BF16 = jnp.bfloat16
_NT = (((1,), (1,)), ((), ()))


def _dot(a, b):
    return jnp.dot(a, b, preferred_element_type=_F32)


def _layer_norm(z, g, b):
    mu = jnp.mean(z, axis=-1, keepdims=True)
    d = z - mu
    var = jnp.mean(d * d, axis=-1, keepdims=True)
    return d * lax.rsqrt(var + LN_EPS) * g + b


def _route_tail(x1, wt_ref, bt_ref, route_ref, routet_ref, counts_ref, umat_ref, *, epg):
    tm = x1.shape[0]
    nr = counts_ref.shape[0]
    half = wt_ref.shape[0] // 2

    @pl.when(pl.program_id(0) == 0)
    def _():
        counts_ref[...] = jnp.zeros_like(counts_ref)
        before = (lax.broadcasted_iota(jnp.int32, (tm, tm), 0) < lax.broadcasted_iota(jnp.int32, (tm, tm), 1))
        umat_ref[...] = jnp.where(before, 1.0, 0.0).astype(_BF16)

    xh = x1.astype(_BF16)
    xl = (x1 - xh.astype(_F32)).astype(_BF16)
    a = lax.dot_general(wt_ref[...], xh, _NT, preferred_element_type=_F32)
    b = lax.dot_general(wt_ref[0:half, :], xl, _NT, preferred_element_type=_F32)
    logits = a[0:nr, :] + a[half:half + nr, :] + b[0:nr, :] + bt_ref[0:nr, :]

    rowf = lax.broadcasted_iota(jnp.int32, (nr, tm), 0).astype(_F32)
    neg = -jnp.inf
    big = float(nr)
    gl = jnp.where(rowf < float(N_GROUPS), logits, neg)
    gmax = jnp.max(gl, axis=0, keepdims=True)
    gidx = jnp.min(jnp.where(gl == gmax, rowf, big), axis=0, keepdims=True)
    gtop = 1.0 / jnp.sum(jnp.exp(gl - gmax), axis=0, keepdims=True)

    lo = float(N_GROUPS) + gidx * float(epg)
    el = jnp.where((rowf >= lo) & (rowf < lo + float(epg)), logits, neg)
    m1 = jnp.max(el, axis=0, keepdims=True)
    i1 = jnp.min(jnp.where(el == m1, rowf, big), axis=0, keepdims=True)
    el2 = jnp.where(rowf == i1, neg, el)
    m2 = jnp.max(el2, axis=0, keepdims=True)
    i2 = jnp.min(jnp.where(el2 == m2, rowf, big), axis=0, keepdims=True)
    w2 = jnp.exp(m2 - m1)
    inv = 1.0 / (1.0 + w2)
    g1 = gtop * inv
    g2 = gtop * w2 * inv

    onehot = jnp.where((rowf == i1) | (rowf == i2), 1.0, 0.0)
    total = _dot(onehot.astype(_BF16), umat_ref[...]) + counts_ref[...]
    r1 = jnp.sum(jnp.where(rowf == i1, total, 0.0), axis=0, keepdims=True)
    r2 = jnp.sum(jnp.where(rowf == i2, total, 0.0), axis=0, keepdims=True)
    counts_ref[...] += jnp.sum(onehot, axis=1, keepdims=True)

    slot = lax.broadcasted_iota(jnp.int32, (SUBLANES, tm), 0)
    rec = jnp.zeros((SUBLANES, tm), _F32)
    fields = ((_E1, i1 - float(N_GROUPS)), (_E2, i2 - float(N_GROUPS)), (_R1, r1), (_R2, r2), (_G1, g1), (_G2, g2))
    for s, val in fields:
        rec = jnp.where(slot == s, val, rec)
    routet_ref[...] = rec
    rec = jnp.concatenate([rec, jnp.zeros((LANES - SUBLANES, tm), _F32)], axis=0)
    route_ref[...] = rec.T


def _store_token_tiles(ref, x):
    n, d = x.shape
    per = d // LANES
    for s in range(per):
        ref[pl.ds(s, n, stride=per), :] = x[:, s * LANES:(s + 1) * LANES]


def _load_token_tiles(ref, first_token, n, d):
    per = d // LANES
    start = first_token * per
    return jnp.concatenate([ref[pl.ds(start + s, n, stride=per), :] for s in range(per)], axis=1)


def _tile_copy(src_hbm, token, dst_vmem, dst_token, sem):
    rows = pl.ds(pl.multiple_of(dst_token * SUBLANES, SUBLANES), SUBLANES)
    return pltpu.make_async_copy(src_hbm.at[token], dst_vmem.at[rows, :], sem)


def _conv_mixer_kernel(x_ref, win_ref, cw_ref, wout_ref, g_ref, b_ref, wt_ref, bt_ref,
                       x1_ref, x1t_ref, route_ref, routet_ref, counts_ref, ubuf_ref, umat_ref,
                       *, tiles_per_seq, alpha, epg):
    i = pl.program_id(0)
    tm, d = x_ref.shape

    @pl.when(i % tiles_per_seq == 0)
    def _():
        ubuf_ref[0:SUBLANES, :] = jnp.zeros((SUBLANES, d), _F32)

    x = x_ref[...]
    h = _dot(x.astype(_BF16), win_ref[...])
    u = h[:, d:2 * d] * h[:, 2 * d:]
    ubuf_ref[SUBLANES:SUBLANES + tm, :] = u
    cw = cw_ref[...]
    uc = (cw[0:1, :] * ubuf_ref[SUBLANES - 2:SUBLANES - 2 + tm, :]
          + cw[1:2, :] * ubuf_ref[SUBLANES - 1:SUBLANES - 1 + tm, :]
          + cw[2:3, :] * u)
    ubuf_ref[0:SUBLANES, :] = ubuf_ref[tm:tm + SUBLANES, :]
    y = _dot((h[:, :d] * uc).astype(_BF16), wout_ref[...])
    x1 = _layer_norm(alpha * x + y, g_ref[...], b_ref[...])
    x1_ref[...] = x1
    _store_token_tiles(x1t_ref, x1)
    _route_tail(x1, wt_ref, bt_ref, route_ref, routet_ref, counts_ref, umat_ref, epg=epg)


def _attn_out_kernel(x_ref, o_ref, wo_ref, g_ref, b_ref, wt_ref, bt_ref,
                     x1_ref, x1t_ref, route_ref, routet_ref, counts_ref, umat_ref, *, alpha, epg):
    y = _dot(o_ref[...], wo_ref[...])
    x1 = _layer_norm(alpha * x_ref[...] + y, g_ref[...], b_ref[...])
    x1_ref[...] = x1
    _store_token_tiles(x1t_ref, x1)
    _route_tail(x1, wt_ref, bt_ref, route_ref, routet_ref, counts_ref, umat_ref, epg=epg)


def _expert_ffn_kernel(bexp_ref, seg_ref, x_ref, win_hbm, wdn_hbm, y_ref,
                       wstage_in, wstage_dn, wsem, winb, wdnb, *, layer):
    i = pl.program_id(0)
    n_exp = (seg_ref.shape[0] - 1) // 2
    n_used = seg_ref[2 * n_exp]
    blk = y_ref.shape[0] // SUBLANES
    d, de2 = wstage_in.shape
    de = de2 // 2

    def weight_copies(e):
        return (pltpu.make_async_copy(win_hbm.at[layer, e], wstage_in, wsem.at[0]),
                pltpu.make_async_copy(wdn_hbm.at[layer, e], wstage_dn, wsem.at[1]))

    @pl.when(i == 0)
    def _():
        for c in weight_copies(bexp_ref[0]):
            c.start()

    @pl.when(i < n_used)
    def _():
        e = bexp_ref[i]

        @pl.when((i == 0) | (e != bexp_ref[jnp.maximum(i - 1, 0)]))
        def _():
            for c in weight_copies(e):
                c.wait()
            winb[...] = wstage_in[...].astype(_BF16)
            wdnb[...] = wstage_dn[...].astype(_BF16)
            nxt = seg_ref[n_exp + e] // blk

            @pl.when(nxt < n_used)
            def _():
                for c in weight_copies(bexp_ref[nxt]):
                    c.start()

        x = _load_token_tiles(x_ref, 0, blk, d)
        h = _dot(x.astype(_BF16), winb[...])
        g = h[:, :de]
        a = g * jax.nn.sigmoid(g) * h[:, de:]
        _store_token_tiles(y_ref, _dot(a.astype(_BF16), wdnb[...]))

    @pl.when(i >= n_used)
    def _():
        y_ref[...] = jnp.zeros_like(y_ref)


def _combine_body(dest_ref, x1_ref, route_ref, y_hbm, g_ref, b_ref, ybuf, sem, *, alpha):
    i = pl.program_id(0)
    n = pl.num_programs(0)
    tm, d = x1_ref.shape
    slot = i % 2

    def issue(tile, s):
        def body(r, carry):
            t = tile * tm + r
            _tile_copy(y_hbm, dest_ref[t], ybuf, (2 * s) * tm + r, sem.at[s]).start(priority=0)
            _tile_copy(y_hbm, dest_ref[n * tm + t], ybuf, (2 * s + 1) * tm + r, sem.at[s]).start(priority=1)
            return carry
        lax.fori_loop(0, tm, body, 0, unroll=ISSUE_UNROLL)

    @pl.when(i == 0)
    def _():
        issue(0, 0)

    @pl.when(i + 1 < n)
    def _():
        issue(i + 1, 1 - slot)

    def wait(r, carry):
        _tile_copy(y_hbm, 0, ybuf, (2 * slot) * tm + r, sem.at[slot]).wait()
        _tile_copy(y_hbm, 0, ybuf, (2 * slot + 1) * tm + r, sem.at[slot]).wait()
        return carry
    lax.fori_loop(0, tm, wait, 0, unroll=ISSUE_UNROLL)

    rec = route_ref[...]
    ffn = (rec[:, _G1:_G1 + 1] * _load_token_tiles(ybuf, (2 * slot) * tm, tm, d)
           + rec[:, _G2:_G2 + 1] * _load_token_tiles(ybuf, (2 * slot + 1) * tm, tm, d))
    return _layer_norm(alpha * x1_ref[...] + ffn, g_ref[...], b_ref[...])


def _combine_kernel(dest_ref, x1_ref, route_ref, y_hbm, g_ref, b_ref, x2_ref, ybuf, sem, *, alpha):
    x2_ref[...] = _combine_body(dest_ref, x1_ref, route_ref, y_hbm, g_ref, b_ref, ybuf, sem, alpha=alpha)


def _combine_qkv_kernel(dest_ref, x1_ref, route_ref, y_hbm, g_ref, b_ref, wk_ref, wv_ref, wq_ref,
                        x2_ref, k_ref, v_ref, q_ref, ybuf, sem, *, alpha, q_scale):
    x2 = _combine_body(dest_ref, x1_ref, route_ref, y_hbm, g_ref, b_ref, ybuf, sem, alpha=alpha)
    x2_ref[...] = x2
    xb = x2.astype(_BF16)
    k_ref[...] = _dot(xb, wk_ref[...]).astype(_BF16)
    q_ref[...] = (_dot(xb, wq_ref[...]) * q_scale).astype(_BF16)
    v_ref[...] = _dot(xb, wv_ref[...]).astype(_BF16)


def _diff_attn_kernel(lam_ref, g_ref, q_ref, k_ref, v_ref, o_ref, lamfull_ref, s_ref, p_ref, *, lam_init):
    bi, hi = pl.program_id(0), pl.program_id(1)
    seq, hw = k_ref.shape
    tq = tk = ATTN_BLOCK
    nk = seq // tk
    hd = lam_ref.shape[1]

    @pl.when((bi == 0) & (hi == 0))
    def _():
        lam = lam_ref[...]
        a = jnp.sum(lam[0:1, :] * lam[1:2, :], axis=-1, keepdims=True)
        b = jnp.sum(lam[2:3, :] * lam[3:4, :], axis=-1, keepdims=True)
        lamfull_ref[...] = jnp.broadcast_to(jnp.exp(a) - jnp.exp(b) + lam_init, lamfull_ref.shape)

    lam_full = lamfull_ref[0:1, 0:1]
    lane = lax.broadcasted_iota(jnp.int32, (tq, hw), 1)
    row = lax.broadcasted_iota(jnp.int32, (2 * tq, tk), 0)
    col = lax.broadcasted_iota(jnp.int32, (2 * tq, tk), 1)
    causal = col <= jnp.where(row < tq, row, row - tq)
    v_aug = jnp.concatenate([v_ref[...], jnp.ones((seq, hw), _BF16)], axis=1)

    for i in reversed(range(nk)):
        par = i % s_ref.shape[0]
        kv = (i + 1) * tk
        q = q_ref[i * tq:(i + 1) * tq, :]
        zero = jnp.zeros_like(q)
        qcat = jnp.concatenate([jnp.where(lane < hd, q, zero), jnp.where(lane >= hd, q, zero)], axis=0)
        s_ref[par, :, 0:kv] = lax.dot_general(qcat, k_ref[0:kv, :], _NT, preferred_element_type=_F32)
        s_ref[par, :, i * tk:kv] = jnp.where(causal, s_ref[par, :, i * tk:kv], -jnp.inf)
        rowmax = jnp.max(s_ref[par, :, 0:kv], axis=1, keepdims=True)
        p_ref[par, :, 0:kv] = jnp.exp2(s_ref[par, :, 0:kv] - rowmax).astype(_BF16)
        acc = _dot(p_ref[par, :, 0:kv], v_aug[0:kv, :])
        o = acc[:, 0:hw] * (1.0 / acc[:, hw:hw + 1])
        od = o[0:tq, :] - lam_full * o[tq:, :]
        od = od * lax.rsqrt(jnp.mean(od * od, axis=1, keepdims=True) + RMS_EPS)
        o_ref[i * tq:(i + 1) * tq, :] = (od * g_ref[...] * (1.0 - lam_init)).astype(o_ref.dtype)


def _params(*sem):
    return pltpu.CompilerParams(dimension_semantics=sem, vmem_limit_bytes=VMEM_LIMIT_BYTES)


def _full(shape):
    return pl.BlockSpec(shape, lambda *_: (0,) * len(shape))


def _router_weights(rg_w, rg_b, re_w, re_b):
    d = rg_w.shape[0]
    n = rg_w.shape[1] + re_w.shape[1]
    w = jnp.concatenate([rg_w, re_w, jnp.zeros((d, LANES - n), _F32)], axis=1).T
    b = jnp.concatenate([rg_b, re_b, jnp.zeros((LANES - n,), _F32)])[:, None]
    wh = w.astype(_BF16)
    wl = (w - wh.astype(_F32)).astype(_BF16)
    return jnp.concatenate([wh, wl], axis=0), b


def _router_rows(n_experts):
    return -(-(N_GROUPS + n_experts) // (2 * SUBLANES)) * (2 * SUBLANES)


def _route_outs(t, d, tm, nr):
    per = d // LANES
    shapes = (jax.ShapeDtypeStruct((t, d), _F32),
              jax.ShapeDtypeStruct((t * per, LANES), _F32),
              jax.ShapeDtypeStruct((t, LANES), _F32),
              jax.ShapeDtypeStruct((SUBLANES, t), _F32),
              jax.ShapeDtypeStruct((nr, 1), _F32))
    specs = (pl.BlockSpec((tm, d), lambda i: (i, 0)),
             pl.BlockSpec((tm * per, LANES), lambda i: (i, 0)),
             pl.BlockSpec((tm, LANES), lambda i: (i, 0)),
             pl.BlockSpec((SUBLANES, tm), lambda i: (0, i)),
             pl.BlockSpec((nr, 1), lambda i: (0, 0)))
    return shapes, specs


def _conv_mixer(x, w_in, conv_w, w_out, ln_g, ln_b, router, *, seq, alpha, epg):
    t, d = x.shape
    tm = MIXER_TILE
    wt, bt = router
    shapes, specs = _route_outs(t, d, tm, _router_rows(N_GROUPS * epg))
    return pl.pallas_call(
        functools.partial(_conv_mixer_kernel, tiles_per_seq=seq // tm, alpha=alpha, epg=epg),
        grid=(t // tm,),
        in_specs=[pl.BlockSpec((tm, d), lambda i: (i, 0)),
                  _full(w_in.shape), _full(conv_w.shape), _full(w_out.shape),
                  _full((1, d)), _full((1, d)), _full(wt.shape), _full(bt.shape)],
        out_specs=specs, out_shape=shapes,
        scratch_shapes=[pltpu.VMEM((tm + SUBLANES, d), _F32), pltpu.VMEM((tm, tm), _BF16)],
        compiler_params=_params("arbitrary"),
        name="conv_mixer_ln_router",
    )(x, w_in, conv_w, w_out, ln_g[None, :], ln_b[None, :], wt, bt)


def _attn_out(x, o, w_o, ln_g, ln_b, router, *, alpha, epg):
    t, d = x.shape
    tm = MIXER_TILE
    wt, bt = router
    shapes, specs = _route_outs(t, d, tm, _router_rows(N_GROUPS * epg))
    return pl.pallas_call(
        functools.partial(_attn_out_kernel, alpha=alpha, epg=epg),
        grid=(t // tm,),
        in_specs=[pl.BlockSpec((tm, d), lambda i: (i, 0)),
                  pl.BlockSpec((tm, d), lambda i: (i, 0)),
                  _full(w_o.shape), _full((1, d)), _full((1, d)), _full(wt.shape), _full(bt.shape)],
        out_specs=specs, out_shape=shapes,
        scratch_shapes=[pltpu.VMEM((tm, tm), _BF16)],
        compiler_params=_params("arbitrary"),
        name="attn_out_ln_router",
    )(x, o, w_o, ln_g[None, :], ln_b[None, :], wt, bt)


def _dispatch_plan(route_t, counts, n_experts):
    t = route_t.shape[1]
    blk = EXPERT_BLOCK
    cnt = counts[N_GROUPS:N_GROUPS + n_experts, 0].astype(jnp.int32)
    padded = (cnt + blk - 1) // blk * blk
    pad_end = jnp.cumsum(padded)
    pad_start = pad_end - padded
    ids = jnp.arange(n_experts, dtype=jnp.int32)

    def sorted_row(e_slot, r_slot):
        e = route_t[e_slot].astype(jnp.int32)
        start = jnp.sum(jnp.where(ids[:, None] == e[None, :], pad_start[:, None], 0), axis=0)
        return start + route_t[r_slot].astype(jnp.int32)

    dest = jnp.concatenate([sorted_row(_E1, _R1), sorted_row(_E2, _R2)])
    n_blocks = (2 * t + n_experts * blk) // blk
    block_start = jnp.arange(n_blocks, dtype=jnp.int32) * blk
    block_expert = jnp.minimum(jnp.sum(pad_end[None, :] <= block_start[:, None], axis=1),
                               n_experts - 1).astype(jnp.int32)
    n_used = (pad_end[-1:] // blk).astype(jnp.int32)
    seg = jnp.concatenate([pad_start + cnt, pad_end, n_used]).astype(jnp.int32)
    return dest, block_expert, seg


def _sc_dispatch(x1t, dest, n_rows):
    t, per, lanes = x1t.shape
    workers = SC_CORES * SC_SUBCORES
    chunks = t // (workers * SC_CHUNK)
    mesh = plsc.VectorSubcoreMesh(core_axis_name="c", subcore_axis_name="s",
                                  num_cores=SC_CORES, num_subcores=SC_SUBCORES)

    def body(x_hbm, d_hbm, o_hbm, xbuf, idx1, idx2, sem):
        wid = lax.axis_index("c") * SC_SUBCORES + lax.axis_index("s")
        first = wid * chunks

        def load(j, slot):
            return pltpu.make_async_copy(x_hbm.at[pl.ds((first + j) * SC_CHUNK, SC_CHUNK)],
                                         xbuf.at[slot], sem.at[slot])

        load(0, 0).start()

        @pl.loop(0, chunks)
        def _(j):
            slot = j % 2

            @pl.when(j + 1 < chunks)
            def _():
                load(j + 1, 1 - slot).start()

            off = (first + j) * SC_CHUNK
            pltpu.sync_copy(d_hbm.at[pl.ds(off, SC_CHUNK)], idx1)
            pltpu.sync_copy(d_hbm.at[pl.ds(t + off, SC_CHUNK)], idx2)
            load(j, slot).wait()
            pltpu.sync_copy(xbuf.at[slot], o_hbm.at[idx1])
            pltpu.sync_copy(xbuf.at[slot], o_hbm.at[idx2])

    return pl.kernel(
        body,
        out_type=jax.ShapeDtypeStruct((n_rows, per, lanes), x1t.dtype),
        mesh=mesh,
        scratch_types=[pltpu.VMEM((2, SC_CHUNK, per, lanes), x1t.dtype),
                       pltpu.VMEM((SC_CHUNK,), jnp.int32),
                       pltpu.VMEM((SC_CHUNK,), jnp.int32),
                       pltpu.SemaphoreType.DMA((2,))],
        name="sc_dispatch",
    )(x1t, dest)


def _expert_ffn(x_sorted, w_in, w_down, layer, block_expert, seg):
    n_rows, per, _ = x_sorted.shape
    _, n_exp, d, de2 = w_in.shape
    de = w_down.shape[2]
    blk = EXPERT_BLOCK
    n_blocks = block_expert.shape[0]
    tiles = pl.BlockSpec((blk * per, LANES), lambda i, be, sg: (i, 0))
    y = pl.pallas_call(
        functools.partial(_expert_ffn_kernel, layer=layer),
        grid_spec=pltpu.PrefetchScalarGridSpec(
            num_scalar_prefetch=2,
            grid=(n_blocks,),
            in_specs=[tiles,
                      pl.BlockSpec(memory_space=pl.ANY),
                      pl.BlockSpec(memory_space=pl.ANY)],
            out_specs=tiles,
            scratch_shapes=[pltpu.VMEM((d, de2), _F32),
                            pltpu.VMEM((de, d), _F32),
                            pltpu.SemaphoreType.DMA((2,)),
                            pltpu.VMEM((d, de2), _BF16),
                            pltpu.VMEM((de, d), _BF16)]),
        out_shape=jax.ShapeDtypeStruct((n_rows * per, LANES), _F32),
        compiler_params=_params("arbitrary"),
        name="expert_ffn",
    )(block_expert, seg, x_sorted.reshape(n_rows * per, LANES), w_in, w_down)
    return y.reshape(n_rows, per, LANES)


def _combine_scratch(tm, d):
    return [pltpu.VMEM((2 * 2 * tm * (d // LANES), LANES), _F32), pltpu.SemaphoreType.DMA((2,))]


def _combine(dest, x1, route, y_rows, ln_g, ln_b, *, alpha):
    t, d = x1.shape
    tm = ROW_TILE
    return pl.pallas_call(
        functools.partial(_combine_kernel, alpha=alpha),
        grid_spec=pltpu.PrefetchScalarGridSpec(
            num_scalar_prefetch=1,
            grid=(t // tm,),
            in_specs=[pl.BlockSpec((tm, d), lambda i, ds: (i, 0)),
                      pl.BlockSpec((tm, LANES), lambda i, ds: (i, 0)),
                      pl.BlockSpec(memory_space=pl.ANY),
                      pl.BlockSpec((1, d), lambda i, ds: (0, 0)),
                      pl.BlockSpec((1, d), lambda i, ds: (0, 0))],
            out_specs=pl.BlockSpec((tm, d), lambda i, ds: (i, 0)),
            scratch_shapes=_combine_scratch(tm, d)),
        out_shape=jax.ShapeDtypeStruct((t, d), _F32),
        compiler_params=_params("arbitrary"),
        name="combine_ln",
    )(dest, x1, route, y_rows, ln_g[None, :], ln_b[None, :])


def _combine_qkv(dest, x1, route, y_rows, ln_g, ln_b, w_k, w_v, w_q, *, alpha, q_scale):
    t, d = x1.shape
    tm = ROW_TILE
    wspec = pl.BlockSpec((d, d), lambda i, ds: (0, 0))
    row = pl.BlockSpec((tm, d), lambda i, ds: (i, 0))
    return pl.pallas_call(
        functools.partial(_combine_qkv_kernel, alpha=alpha, q_scale=q_scale),
        grid_spec=pltpu.PrefetchScalarGridSpec(
            num_scalar_prefetch=1,
            grid=(t // tm,),
            in_specs=[row,
                      pl.BlockSpec((tm, LANES), lambda i, ds: (i, 0)),
                      pl.BlockSpec(memory_space=pl.ANY),
                      pl.BlockSpec((1, d), lambda i, ds: (0, 0)),
                      pl.BlockSpec((1, d), lambda i, ds: (0, 0)),
                      wspec, wspec, wspec],
            out_specs=(row, row, row, row),
            scratch_shapes=_combine_scratch(tm, d)),
        out_shape=(jax.ShapeDtypeStruct((t, d), _F32),
                   jax.ShapeDtypeStruct((t, d), _BF16),
                   jax.ShapeDtypeStruct((t, d), _BF16),
                   jax.ShapeDtypeStruct((t, d), _BF16)),
        compiler_params=_params("arbitrary"),
        name="combine_ln_qkv",
    )(dest, x1, route, y_rows, ln_g[None, :], ln_b[None, :], w_k, w_v, w_q)


def _diff_attention(q, k, v, lam, subln_g, *, batch, seq, lam_init):
    t, d = q.shape
    tq = ATTN_BLOCK
    hw = d // N_HEADS
    head = pl.BlockSpec((seq, hw), lambda b, h: (b, h))
    return pl.pallas_call(
        functools.partial(_diff_attn_kernel, lam_init=lam_init),
        grid=(batch, N_HEADS),
        in_specs=[_full(lam.shape), _full((1, hw)), head, head, head],
        out_specs=head,
        out_shape=jax.ShapeDtypeStruct((t, d), _BF16),
        scratch_shapes=[pltpu.VMEM((1, LANES), _F32),
                        pltpu.VMEM((ATTN_BUFFERS, 2 * tq, seq), _F32),
                        pltpu.VMEM((ATTN_BUFFERS, 2 * tq, seq), _BF16)],
        compiler_params=_params("arbitrary", "arbitrary"),
        name="diff_attention",
    )(lam, subln_g[None, :], q, k, v)


def _moe(x1t, route_t, counts, w_in, w_down, layer):
    n_exp, d = w_in.shape[1:3]
    dest, block_expert, seg = _dispatch_plan(route_t, counts, n_exp)
    x1t = x1t.reshape(-1, d // LANES, LANES)
    x_sorted = _sc_dispatch(x1t, dest, block_expert.shape[0] * EXPERT_BLOCK)
    y_rows = _expert_ffn(x_sorted, w_in, w_down, layer, block_expert, seg)
    return dest, y_rows


def kernel(x, a_w_in, a_conv_w, a_w_out, kv_w, b_w_q, b_lambda, b_subln_g, b_w_o,
           ln1_g, ln1_b, ln2_g, ln2_b, rg_w, rg_b, re_w, re_b, e_w_in, e_w_down):
    batch, seq, d = x.shape
    depth = ln1_g.shape[0]
    assert depth == 2 and a_w_in.shape[0] == 1 and b_w_q.shape[0] == 1
    assert seq % MIXER_TILE == 0 and seq % ROW_TILE == 0 and seq % ATTN_BLOCK == 0
    t = batch * seq
    alpha = (2.0 * depth) ** 0.25
    epg = re_w.shape[2] // N_GROUPS
    head_dim = d // (2 * N_HEADS)
    xf = x.reshape(t, d)

    router0 = _router_weights(rg_w[0], rg_b[0], re_w[0], re_b[0])
    x1, x1t, route, route_t, counts = _conv_mixer(xf, a_w_in[0].astype(_BF16), a_conv_w[0],
                                         a_w_out[0].astype(_BF16), ln1_g[0], ln1_b[0], router0,
                                         seq=seq, alpha=alpha, epg=epg)
    dest, y_rows = _moe(x1t, route_t, counts, e_w_in, e_w_down, 0)
    q_scale = head_dim ** -0.5 * math.log2(math.e)
    x2, k, v, q = _combine_qkv(dest, x1, route, y_rows, ln2_g[0], ln2_b[0],
                               kv_w[:, :d].astype(_BF16), kv_w[:, d:].astype(_BF16),
                               b_w_q[0].astype(_BF16), alpha=alpha, q_scale=q_scale)

    lam_init = 0.8 - 0.6 * math.exp(-0.3 * 1)
    o = _diff_attention(q, k, v, b_lambda[0], b_subln_g[0], batch=batch, seq=seq, lam_init=lam_init)
    router1 = _router_weights(rg_w[1], rg_b[1], re_w[1], re_b[1])
    x1, x1t, route, route_t, counts = _attn_out(x2, o, b_w_o[0].astype(_BF16), ln1_g[1], ln1_b[1], router1,
                                       alpha=alpha, epg=epg)
    dest, y_rows = _moe(x1t, route_t, counts, e_w_in, e_w_down, 1)
    out = _combine(dest, x1, route, y_rows, ln2_g[1], ln2_b[1], alpha=alpha)
    return out.reshape(batch, seq, d)
```

```python
import functools
import math

import jax
import jax.numpy as jnp
from jax import lax
from jax.experimental import pallas as pl
from jax.experimental.pallas import tpu as pltpu
from jax.experimental.pallas import tpu_sc as plsc

N_HEADS = 8
N_GROUPS = 4
LN_EPS = 1e-5
RMS_EPS = 1e-5

LANES = 128
SUBLANES = 8
VMEM_LIMIT_BYTES = 56 * 1024 * 1024

SC_CORES = 2
SC_SUBCORES = 16

SC_CHUNK = 32
MIXER_TILE = 1024
ROW_TILE = 512
EXPERT_BLOCK = 256
ATTN_BLOCK = 256
ATTN_BUFFERS = 2

_E1, _E2, _R1, _R2, _G1, _G2 = range(6)

_F32 = jnp.float32
_BF16 = jnp.bfloat16
_NT = (((1,), (1,)), ((), ()))


def _dot(a, b):
    return jnp.dot(a, b, preferred_element_type=_F32)


def _layer_norm(z, g, b):
    mu = jnp.mean(z, axis=-1, keepdims=True)
    d = z - mu
    var = jnp.mean(d * d, axis=-1, keepdims=True)
    return d * lax.rsqrt(var + LN_EPS) * g + b


def _route_tail(x1, wt_ref, bt_ref, route_ref, routet_ref, counts_ref, umat_ref, *, epg):
    tm = x1.shape[0]
    nr = counts_ref.shape[0]
    half = wt_ref.shape[0] // 2

    @pl.when(pl.program_id(0) == 0)
    def _():
        counts_ref[...] = jnp.zeros_like(counts_ref)
        before = (lax.broadcasted_iota(jnp.int32, (tm, tm), 0) < lax.broadcasted_iota(jnp.int32, (tm, tm), 1))
        umat_ref[...] = jnp.where(before, 1.0, 0.0).astype(_BF16)

    xh = x1.astype(_BF16)
    xl = (x1 - xh.astype(_F32)).astype(_BF16)
    a = lax.dot_general(wt_ref[...], xh, _NT, preferred_element_type=_F32)
    b = lax.dot_general(wt_ref[0:half, :], xl, _NT, preferred_element_type=_F32)
    logits = a[0:nr, :] + a[half:half + nr, :] + b[0:nr, :] + bt_ref[0:nr, :]

    rowf = lax.broadcasted_iota(jnp.int32, (nr, tm), 0).astype(_F32)
    neg = -jnp.inf
    big = float(nr)
    gl = jnp.where(rowf < float(N_GROUPS), logits, neg)
    gmax = jnp.max(gl, axis=0, keepdims=True)
    gidx = jnp.min(jnp.where(gl == gmax, rowf, big), axis=0, keepdims=True)
    gtop = 1.0 / jnp.sum(jnp.exp(gl - gmax), axis=0, keepdims=True)

    lo = float(N_GROUPS) + gidx * float(epg)
    el = jnp.where((rowf >= lo) & (rowf < lo + float(epg)), logits, neg)
    m1 = jnp.max(el, axis=0, keepdims=True)
    i1 = jnp.min(jnp.where(el == m1, rowf, big), axis=0, keepdims=True)
    el2 = jnp.where(rowf == i1, neg, el)
    m2 = jnp.max(el2, axis=0, keepdims=True)
    i2 = jnp.min(jnp.where(el2 == m2, rowf, big), axis=0, keepdims=True)
    w2 = jnp.exp(m2 - m1)
    inv = 1.0 / (1.0 + w2)
    g1 = gtop * inv
    g2 = gtop * w2 * inv

    onehot = jnp.where((rowf == i1) | (rowf == i2), 1.0, 0.0)
    total = _dot(onehot.astype(_BF16), umat_ref[...]) + counts_ref[...]
    r1 = jnp.sum(jnp.where(rowf == i1, total, 0.0), axis=0, keepdims=True)
    r2 = jnp.sum(jnp.where(rowf == i2, total, 0.0), axis=0, keepdims=True)
    counts_ref[...] += jnp.sum(onehot, axis=1, keepdims=True)

    slot = lax.broadcasted_iota(jnp.int32, (SUBLANES, tm), 0)
    rec = jnp.zeros((SUBLANES, tm), _F32)
    fields = ((_E1, i1 - float(N_GROUPS)), (_E2, i2 - float(N_GROUPS)), (_R1, r1), (_R2, r2), (_G1, g1), (_G2, g2))
    for s, val in fields:
        rec = jnp.where(slot == s, val, rec)
    routet_ref[...] = rec
    rec = jnp.concatenate([rec, jnp.zeros((LANES - SUBLANES, tm), _F32)], axis=0)
    route_ref[...] = rec.T


def _store_token_tiles(ref, x):
    n, d = x.shape
    per = d // LANES
    for s in range(per):
        ref[pl.ds(s, n, stride=per), :] = x[:, s * LANES:(s + 1) * LANES]


def _load_token_tiles(ref, first_token, n, d):
    per = d // LANES
    start = first_token * per
    return jnp.concatenate([ref[pl.ds(start + s, n, stride=per), :] for s in range(per)], axis=1)


def _conv_mixer_kernel(x_ref, win_ref, cw_ref, wout_ref, g_ref, b_ref, wt_ref, bt_ref,
                       x1_ref, x1t_ref, route_ref, routet_ref, counts_ref, ubuf_ref, umat_ref,
                       *, tiles_per_seq, alpha, epg):
    i = pl.program_id(0)
    tm, d = x_ref.shape

    @pl.when(i % tiles_per_seq == 0)
    def _():
        ubuf_ref[0:SUBLANES, :] = jnp.zeros((SUBLANES, d), _F32)

    x = x_ref[...]
    h = _dot(x.astype(_BF16), win_ref[...])
    u = h[:, d:2 * d] * h[:, 2 * d:]
    ubuf_ref[SUBLANES:SUBLANES + tm, :] = u
    cw = cw_ref[...]
    uc = (cw[0:1, :] * ubuf_ref[SUBLANES - 2:SUBLANES - 2 + tm, :]
          + cw[1:2, :] * ubuf_ref[SUBLANES - 1:SUBLANES - 1 + tm, :]
          + cw[2:3, :] * u)
    ubuf_ref[0:SUBLANES, :] = ubuf_ref[tm:tm + SUBLANES, :]
    y = _dot((h[:, :d] * uc).astype(_BF16), wout_ref[...])
    x1 = _layer_norm(alpha * x + y, g_ref[...], b_ref[...])
    x1_ref[...] = x1
    _store_token_tiles(x1t_ref, x1)
    _route_tail(x1, wt_ref, bt_ref, route_ref, routet_ref, counts_ref, umat_ref, epg=epg)


def _attn_out_kernel(x_ref, o_ref, wo_ref, g_ref, b_ref, wt_ref, bt_ref,
                     x1_ref, x1t_ref, route_ref, routet_ref, counts_ref, umat_ref, *, alpha, epg):
    y = _dot(o_ref[...], wo_ref[...])
    x1 = _layer_norm(alpha * x_ref[...] + y, g_ref[...], b_ref[...])
    x1_ref[...] = x1
    _store_token_tiles(x1t_ref, x1)
    _route_tail(x1, wt_ref, bt_ref, route_ref, routet_ref, counts_ref, umat_ref, epg=epg)


def _expert_ffn_kernel(bexp_ref, seg_ref, x_ref, rec_ref, win_hbm, wdn_hbm, y_ref,
                       wstage_in, wstage_dn, wsem, winb, wdnb, *, layer):
    i = pl.program_id(0)
    n_exp = (seg_ref.shape[0] - 1) // 2
    n_used = seg_ref[2 * n_exp]
    blk = y_ref.shape[0] // SUBLANES
    d, de2 = wstage_in.shape
    de = de2 // 2

    def weight_copies(e):
        return (pltpu.make_async_copy(win_hbm.at[layer, e], wstage_in, wsem.at[0]),
                pltpu.make_async_copy(wdn_hbm.at[layer, e], wstage_dn, wsem.at[1]))

    @pl.when(i == 0)
    def _():
        for c in weight_copies(bexp_ref[0]):
            c.start()

    @pl.when(i < n_used)
    def _():
        e = bexp_ref[i]

        @pl.when((i == 0) | (e != bexp_ref[jnp.maximum(i - 1, 0)]))
        def _():
            for c in weight_copies(e):
                c.wait()
            winb[...] = wstage_in[...].astype(_BF16)
            wdnb[...] = wstage_dn[...].astype(_BF16)
            nxt = seg_ref[n_exp + e] // blk

            @pl.when(nxt < n_used)
            def _():
                for c in weight_copies(bexp_ref[nxt]):
                    c.start()

        x = _load_token_tiles(x_ref, 0, blk, d)
        h = _dot(x.astype(_BF16), winb[...])
        g = h[:, :de]
        a = g * jax.nn.sigmoid(g) * h[:, de:]
        rec = rec_ref[...]
        gate = jnp.where(rec[:, _E1:_E1 + 1] == e.astype(_F32), rec[:, _G1:_G1 + 1], rec[:, _G2:_G2 + 1])
        _store_token_tiles(y_ref, _dot(a.astype(_BF16), wdnb[...]) * gate)


def _combine_body(x1_ref, ffn_ref, g_ref, b_ref, *, alpha):
    tm, d = x1_ref.shape
    return _layer_norm(alpha * x1_ref[...] + _load_token_tiles(ffn_ref, 0, tm, d), g_ref[...], b_ref[...])


def _combine_kernel(x1_ref, ffn_ref, g_ref, b_ref, x2_ref, *, alpha):
    x2_ref[...] = _combine_body(x1_ref, ffn_ref, g_ref, b_ref, alpha=alpha)


def _combine_qkv_kernel(x1_ref, ffn_ref, g_ref, b_ref, wk_ref, wv_ref, wq_ref,
                        x2_ref, k_ref, v_ref, q_ref, *, alpha, q_scale):
    x2 = _combine_body(x1_ref, ffn_ref, g_ref, b_ref, alpha=alpha)
    x2_ref[...] = x2
    xb = x2.astype(_BF16)
    k_ref[...] = _dot(xb, wk_ref[...]).astype(_BF16)
    q_ref[...] = (_dot(xb, wq_ref[...]) * q_scale).astype(_BF16)
    v_ref[...] = _dot(xb, wv_ref[...]).astype(_BF16)


def _diff_attn_kernel(lam_ref, g_ref, q_ref, k_ref, v_ref, o_ref, lamfull_ref, s_ref, p_ref, *, lam_init):
    bi, hi = pl.program_id(0), pl.program_id(1)
    seq, hw = k_ref.shape
    tq = tk = ATTN_BLOCK
    nk = seq // tk
    hd = lam_ref.shape[1]

    @pl.when((bi == 0) & (hi == 0))
    def _():
        lam = lam_ref[...]
        a = jnp.sum(lam[0:1, :] * lam[1:2, :], axis=-1, keepdims=True)
        b = jnp.sum(lam[2:3, :] * lam[3:4, :], axis=-1, keepdims=True)
        lamfull_ref[...] = jnp.broadcast_to(jnp.exp(a) - jnp.exp(b) + lam_init, lamfull_ref.shape)

    lam_full = lamfull_ref[0:1, 0:1]
    lane = lax.broadcasted_iota(jnp.int32, (tq, hw), 1)
    row = lax.broadcasted_iota(jnp.int32, (2 * tq, tk), 0)
    col = lax.broadcasted_iota(jnp.int32, (2 * tq, tk), 1)
    causal = col <= jnp.where(row < tq, row, row - tq)
    v_aug = jnp.concatenate([v_ref[...], jnp.ones((seq, hw), _BF16)], axis=1)

    for i in reversed(range(nk)):
        par = i % s_ref.shape[0]
        kv = (i + 1) * tk
        q = q_ref[i * tq:(i + 1) * tq, :]
        zero = jnp.zeros_like(q)
        qcat = jnp.concatenate([jnp.where(lane < hd, q, zero), jnp.where(lane >= hd, q, zero)], axis=0)
        s_ref[par, :, 0:kv] = lax.dot_general(qcat, k_ref[0:kv, :], _NT, preferred_element_type=_F32)
        s_ref[par, :, i * tk:kv] = jnp.where(causal, s_ref[par, :, i * tk:kv], -jnp.inf)
        rowmax = jnp.max(s_ref[par, :, 0:kv], axis=1, keepdims=True)
        p_ref[par, :, 0:kv] = jnp.exp2(s_ref[par, :, 0:kv] - rowmax).astype(_BF16)
        acc = _dot(p_ref[par, :, 0:kv], v_aug[0:kv, :])
        o = acc[:, 0:hw] * (1.0 / acc[:, hw:hw + 1])
        od = o[0:tq, :] - lam_full * o[tq:, :]
        od = od * lax.rsqrt(jnp.mean(od * od, axis=1, keepdims=True) + RMS_EPS)
        o_ref[i * tq:(i + 1) * tq, :] = (od * g_ref[...] * (1.0 - lam_init)).astype(o_ref.dtype)


def _params(*sem):
    return pltpu.CompilerParams(dimension_semantics=sem, vmem_limit_bytes=VMEM_LIMIT_BYTES)


def _full(shape):
    return pl.BlockSpec(shape, lambda *_: (0,) * len(shape))


def _router_weights(rg_w, rg_b, re_w, re_b):
    d = rg_w.shape[0]
    n = rg_w.shape[1] + re_w.shape[1]
    w = jnp.concatenate([rg_w, re_w, jnp.zeros((d, LANES - n), _F32)], axis=1).T
    b = jnp.concatenate([rg_b, re_b, jnp.zeros((LANES - n,), _F32)])[:, None]
    wh = w.astype(_BF16)
    wl = (w - wh.astype(_F32)).astype(_BF16)
    return jnp.concatenate([wh, wl], axis=0), b


def _router_rows(n_experts):
    return -(-(N_GROUPS + n_experts) // (2 * SUBLANES)) * (2 * SUBLANES)


def _route_outs(t, d, tm, nr):
    per = d // LANES
    shapes = (jax.ShapeDtypeStruct((t, d), _F32),
              jax.ShapeDtypeStruct((t * per, LANES), _F32),
              jax.ShapeDtypeStruct((t, LANES), _F32),
              jax.ShapeDtypeStruct((SUBLANES, t), _F32),
              jax.ShapeDtypeStruct((nr, 1), _F32))
    specs = (pl.BlockSpec((tm, d), lambda i: (i, 0)),
             pl.BlockSpec((tm * per, LANES), lambda i: (i, 0)),
             pl.BlockSpec((tm, LANES), lambda i: (i, 0)),
             pl.BlockSpec((SUBLANES, tm), lambda i: (0, i)),
             pl.BlockSpec((nr, 1), lambda i: (0, 0)))
    return shapes, specs


def _conv_mixer(x, w_in, conv_w, w_out, ln_g, ln_b, router, *, seq, alpha, epg):
    t, d = x.shape
    tm = MIXER_TILE
    wt, bt = router
    shapes, specs = _route_outs(t, d, tm, _router_rows(N_GROUPS * epg))
    return pl.pallas_call(
        functools.partial(_conv_mixer_kernel, tiles_per_seq=seq // tm, alpha=alpha, epg=epg),
        grid=(t // tm,),
        in_specs=[pl.BlockSpec((tm, d), lambda i: (i, 0)),
                  _full(w_in.shape), _full(conv_w.shape), _full(w_out.shape),
                  _full((1, d)), _full((1, d)), _full(wt.shape), _full(bt.shape)],
        out_specs=specs, out_shape=shapes,
        scratch_shapes=[pltpu.VMEM((tm + SUBLANES, d), _F32), pltpu.VMEM((tm, tm), _BF16)],
        compiler_params=_params("arbitrary"),
        name="conv_mixer_ln_router",
    )(x, w_in, conv_w, w_out, ln_g[None, :], ln_b[None, :], wt, bt)


def _attn_out(x, o, w_o, ln_g, ln_b, router, *, alpha, epg):
    t, d = x.shape
    tm = MIXER_TILE
    wt, bt = router
    shapes, specs = _route_outs(t, d, tm, _router_rows(N_GROUPS * epg))
    return pl.pallas_call(
        functools.partial(_attn_out_kernel, alpha=alpha, epg=epg),
        grid=(t // tm,),
        in_specs=[pl.BlockSpec((tm, d), lambda i: (i, 0)),
                  pl.BlockSpec((tm, d), lambda i: (i, 0)),
                  _full(w_o.shape), _full((1, d)), _full((1, d)), _full(wt.shape), _full(bt.shape)],
        out_specs=specs, out_shape=shapes,
        scratch_shapes=[pltpu.VMEM((tm, tm), _BF16)],
        compiler_params=_params("arbitrary"),
        name="attn_out_ln_router",
    )(x, o, w_o, ln_g[None, :], ln_b[None, :], wt, bt)


def _dispatch_plan(route_t, counts, n_experts):
    t = route_t.shape[1]
    blk = EXPERT_BLOCK
    cnt = counts[N_GROUPS:N_GROUPS + n_experts, 0].astype(jnp.int32)
    padded = (cnt + blk - 1) // blk * blk
    pad_end = jnp.cumsum(padded)
    pad_start = pad_end - padded
    ids = jnp.arange(n_experts, dtype=jnp.int32)

    def sorted_row(e_slot, r_slot):
        e = route_t[e_slot].astype(jnp.int32)
        start = jnp.sum(jnp.where(ids[:, None] == e[None, :], pad_start[:, None], 0), axis=0)
        return start + route_t[r_slot].astype(jnp.int32)

    dest = jnp.concatenate([sorted_row(_E1, _R1), sorted_row(_E2, _R2)])
    n_blocks = (2 * t + n_experts * blk) // blk
    block_start = jnp.arange(n_blocks, dtype=jnp.int32) * blk
    block_expert = jnp.minimum(jnp.sum(pad_end[None, :] <= block_start[:, None], axis=1),
                               n_experts - 1).astype(jnp.int32)
    n_used = (pad_end[-1:] // blk).astype(jnp.int32)
    seg = jnp.concatenate([pad_start + cnt, pad_end, n_used]).astype(jnp.int32)
    return dest, block_expert, seg


def _sc_mesh():
    return plsc.VectorSubcoreMesh(core_axis_name="c", subcore_axis_name="s",
                                  num_cores=SC_CORES, num_subcores=SC_SUBCORES)


def _sc_worker_chunks(t, chunk):
    chunks = t // (SC_CORES * SC_SUBCORES * chunk)
    wid = lax.axis_index("c") * SC_SUBCORES + lax.axis_index("s")
    return wid * chunks, chunks


def _sc_dispatch(x1t, route, dest, n_rows):
    t, per, lanes = x1t.shape
    c = SC_CHUNK

    def body(x_hbm, r_hbm, d_hbm, xo_hbm, ro_hbm, xbuf, rbuf, idx1, idx2, lsem, ssem):
        first, chunks = _sc_worker_chunks(t, c)
        pltpu.sync_copy(d_hbm.at[pl.ds(first * c, chunks * c)], idx1)
        pltpu.sync_copy(d_hbm.at[pl.ds(t + first * c, chunks * c)], idx2)

        def loads(j, slot):
            rows = pl.ds((first + j) * c, c)
            return (pltpu.make_async_copy(x_hbm.at[rows], xbuf.at[slot], lsem.at[2 * slot]),
                    pltpu.make_async_copy(r_hbm.at[rows], rbuf.at[slot], lsem.at[2 * slot + 1]))

        for cp in loads(0, 0):
            cp.start()

        @pl.loop(0, chunks)
        def _(j):
            slot = j % 2

            @pl.when(j + 1 < chunks)
            def _():
                for cp in loads(j + 1, 1 - slot):
                    cp.start()

            for cp in loads(j, slot):
                cp.wait()
            i1 = idx1.at[pl.ds(j * c, c)]
            i2 = idx2.at[pl.ds(j * c, c)]
            scatters = (pltpu.make_async_copy(xbuf.at[slot], xo_hbm.at[i1], ssem.at[0]),
                        pltpu.make_async_copy(xbuf.at[slot], xo_hbm.at[i2], ssem.at[1]),
                        pltpu.make_async_copy(rbuf.at[slot], ro_hbm.at[i1], ssem.at[2]),
                        pltpu.make_async_copy(rbuf.at[slot], ro_hbm.at[i2], ssem.at[3]))
            for cp in scatters:
                cp.start()
            for cp in scatters:
                cp.wait()

    per_worker = t // (SC_CORES * SC_SUBCORES)
    return pl.kernel(
        body,
        out_type=(jax.ShapeDtypeStruct((n_rows, per, lanes), x1t.dtype),
                  jax.ShapeDtypeStruct((n_rows, lanes), route.dtype)),
        mesh=_sc_mesh(),
        scratch_types=[pltpu.VMEM((2, c, per, lanes), x1t.dtype),
                       pltpu.VMEM((2, c, lanes), route.dtype),
                       pltpu.VMEM((per_worker,), jnp.int32),
                       pltpu.VMEM((per_worker,), jnp.int32),
                       pltpu.SemaphoreType.DMA((4,)),
                       pltpu.SemaphoreType.DMA((4,))],
        name="sc_dispatch",
    )(x1t, route, dest)


def _sc_combine(y_rows, dest, t):
    _, per, lanes = y_rows.shape
    c = SC_CHUNK

    def body(y_hbm, d_hbm, o_hbm, buf, idx1, idx2, osem):
        first, chunks = _sc_worker_chunks(t, c)
        pltpu.sync_copy(d_hbm.at[pl.ds(first * c, chunks * c)], idx1)
        pltpu.sync_copy(d_hbm.at[pl.ds(t + first * c, chunks * c)], idx2)

        def store(j, slot):
            return pltpu.make_async_copy(buf.at[slot], o_hbm.at[pl.ds((first + j) * c, c)], osem.at[slot])

        @pl.loop(0, chunks)
        def _(j):
            slot = j % 2

            @pl.when(j >= 2)
            def _():
                store(j - 2, slot).wait()

            pltpu.sync_copy(y_hbm.at[idx1.at[pl.ds(j * c, c)]], buf.at[slot])
            pltpu.sync_copy(y_hbm.at[idx2.at[pl.ds(j * c, c)]], buf.at[slot], add=True)
            store(j, slot).start()

        for j in range(max(chunks - 2, 0), chunks):
            store(j, j % 2).wait()

    per_worker = t // (SC_CORES * SC_SUBCORES)
    return pl.kernel(
        body,
        out_type=jax.ShapeDtypeStruct((t, per, lanes), y_rows.dtype),
        mesh=_sc_mesh(),
        scratch_types=[pltpu.VMEM((2, c, per, lanes), y_rows.dtype),
                       pltpu.VMEM((per_worker,), jnp.int32),
                       pltpu.VMEM((per_worker,), jnp.int32),
                       pltpu.SemaphoreType.DMA((2,))],
        name="sc_combine",
    )(y_rows, dest)


def _expert_ffn(x_sorted, r_sorted, w_in, w_down, layer, block_expert, seg):
    n_rows, per, _ = x_sorted.shape
    _, n_exp, d, de2 = w_in.shape
    de = w_down.shape[2]
    blk = EXPERT_BLOCK
    n_blocks = block_expert.shape[0]

    def used(i, be, sg):
        return jnp.minimum(i, sg[2 * n_exp] - 1), 0

    tiles = pl.BlockSpec((blk * per, LANES), used)
    y = pl.pallas_call(
        functools.partial(_expert_ffn_kernel, layer=layer),
        grid_spec=pltpu.PrefetchScalarGridSpec(
            num_scalar_prefetch=2,
            grid=(n_blocks,),
            in_specs=[tiles,
                      pl.BlockSpec((blk, LANES), used),
                      pl.BlockSpec(memory_space=pl.ANY),
                      pl.BlockSpec(memory_space=pl.ANY)],
            out_specs=tiles,
            scratch_shapes=[pltpu.VMEM((d, de2), _F32),
                            pltpu.VMEM((de, d), _F32),
                            pltpu.SemaphoreType.DMA((2,)),
                            pltpu.VMEM((d, de2), _BF16),
                            pltpu.VMEM((de, d), _BF16)]),
        out_shape=jax.ShapeDtypeStruct((n_rows * per, LANES), _F32),
        compiler_params=_params("arbitrary"),
        name="expert_ffn",
    )(block_expert, seg, x_sorted.reshape(n_rows * per, LANES), r_sorted, w_in, w_down)
    return y.reshape(n_rows, per, LANES)


def _combine(x1, ffn, ln_g, ln_b, *, alpha):
    t, d = x1.shape
    tm = ROW_TILE
    per = d // LANES
    row = pl.BlockSpec((tm, d), lambda i: (i, 0))
    return pl.pallas_call(
        functools.partial(_combine_kernel, alpha=alpha),
        grid=(t // tm,),
        in_specs=[row, pl.BlockSpec((tm * per, LANES), lambda i: (i, 0)), _full((1, d)), _full((1, d))],
        out_specs=row,
        out_shape=jax.ShapeDtypeStruct((t, d), _F32),
        compiler_params=_params("arbitrary"),
        name="combine_ln",
    )(x1, ffn.reshape(t * per, LANES), ln_g[None, :], ln_b[None, :])


def _combine_qkv(x1, ffn, ln_g, ln_b, w_k, w_v, w_q, *, alpha, q_scale):
    t, d = x1.shape
    tm = ROW_TILE
    per = d // LANES
    row = pl.BlockSpec((tm, d), lambda i: (i, 0))
    return pl.pallas_call(
        functools.partial(_combine_qkv_kernel, alpha=alpha, q_scale=q_scale),
        grid=(t // tm,),
        in_specs=[row, pl.BlockSpec((tm * per, LANES), lambda i: (i, 0)), _full((1, d)), _full((1, d)),
                  _full((d, d)), _full((d, d)), _full((d, d))],
        out_specs=(row, row, row, row),
        out_shape=(jax.ShapeDtypeStruct((t, d), _F32),
                   jax.ShapeDtypeStruct((t, d), _BF16),
                   jax.ShapeDtypeStruct((t, d), _BF16),
                   jax.ShapeDtypeStruct((t, d), _BF16)),
        compiler_params=_params("arbitrary"),
        name="combine_ln_qkv",
    )(x1, ffn.reshape(t * per, LANES), ln_g[None, :], ln_b[None, :], w_k, w_v, w_q)


def _diff_attention(q, k, v, lam, subln_g, *, batch, seq, lam_init):
    t, d = q.shape
    tq = ATTN_BLOCK
    hw = d // N_HEADS
    head = pl.BlockSpec((seq, hw), lambda b, h: (b, h))
    return pl.pallas_call(
        functools.partial(_diff_attn_kernel, lam_init=lam_init),
        grid=(batch, N_HEADS),
        in_specs=[_full(lam.shape), _full((1, hw)), head, head, head],
        out_specs=head,
        out_shape=jax.ShapeDtypeStruct((t, d), _BF16),
        scratch_shapes=[pltpu.VMEM((1, LANES), _F32),
                        pltpu.VMEM((ATTN_BUFFERS, 2 * tq, seq), _F32),
                        pltpu.VMEM((ATTN_BUFFERS, 2 * tq, seq), _BF16)],
        compiler_params=_params("arbitrary", "arbitrary"),
        name="diff_attention",
    )(lam, subln_g[None, :], q, k, v)


def _moe(x1t, route, route_t, counts, w_in, w_down, layer):
    n_exp, d = w_in.shape[1:3]
    dest, block_expert, seg = _dispatch_plan(route_t, counts, n_exp)
    x1t = x1t.reshape(-1, d // LANES, LANES)
    x_sorted, r_sorted = _sc_dispatch(x1t, route, dest, block_expert.shape[0] * EXPERT_BLOCK)
    y_rows = _expert_ffn(x_sorted, r_sorted, w_in, w_down, layer, block_expert, seg)
    return _sc_combine(y_rows, dest, x1t.shape[0])


def kernel(x, a_w_in, a_conv_w, a_w_out, kv_w, b_w_q, b_lambda, b_subln_g, b_w_o,
           ln1_g, ln1_b, ln2_g, ln2_b, rg_w, rg_b, re_w, re_b, e_w_in, e_w_down):
    batch, seq, d = x.shape
    depth = ln1_g.shape[0]
    assert depth == 2 and a_w_in.shape[0] == 1 and b_w_q.shape[0] == 1
    assert seq % MIXER_TILE == 0 and seq % ROW_TILE == 0 and seq % ATTN_BLOCK == 0
    t = batch * seq
    assert t % (SC_CORES * SC_SUBCORES * SC_CHUNK) == 0
    alpha = (2.0 * depth) ** 0.25
    epg = re_w.shape[2] // N_GROUPS
    head_dim = d // (2 * N_HEADS)
    xf = x.reshape(t, d)

    router0 = _router_weights(rg_w[0], rg_b[0], re_w[0], re_b[0])
    x1, x1t, route, route_t, counts = _conv_mixer(xf, a_w_in[0].astype(_BF16), a_conv_w[0],
                                                  a_w_out[0].astype(_BF16), ln1_g[0], ln1_b[0], router0,
                                                  seq=seq, alpha=alpha, epg=epg)
    ffn = _moe(x1t, route, route_t, counts, e_w_in, e_w_down, 0)
    q_scale = head_dim ** -0.5 * math.log2(math.e)
    x2, k, v, q = _combine_qkv(x1, ffn, ln2_g[0], ln2_b[0],
                               kv_w[:, :d].astype(_BF16), kv_w[:, d:].astype(_BF16),
                               b_w_q[0].astype(_BF16), alpha=alpha, q_scale=q_scale)

    lam_init = 0.8 - 0.6 * math.exp(-0.3 * 1)
    o = _diff_attention(q, k, v, b_lambda[0], b_subln_g[0], batch=batch, seq=seq, lam_init=lam_init)
    router1 = _router_weights(rg_w[1], rg_b[1], re_w[1], re_b[1])
    x1, x1t, route, route_t, counts = _attn_out(x2, o, b_w_o[0].astype(_BF16), ln1_g[1], ln1_b[1],
                                                router1, alpha=alpha, epg=epg)
    ffn = _moe(x1t, route, route_t, counts, e_w_in, e_w_down, 1)
    out = _combine(x1, ffn, ln2_g[1], ln2_b[1], alpha=alpha)
    return out.reshape(batch, seq, d)
```

```python
import functools
import math

import jax
import jax.numpy as jnp
from jax import lax
from jax.experimental import pallas as pl
from jax.experimental.pallas import tpu as pltpu
from jax.experimental.pallas import tpu_sc as plsc

N_HEADS = 8
N_GROUPS = 4
LN_EPS = 1e-5
RMS_EPS = 1e-5

LANES = 128
SUBLANES = 8
VMEM_LIMIT_BYTES = 56 * 1024 * 1024

SC_CORES = 2
SC_SUBCORES = 16

SC_CHUNK = 32
MIXER_TILE = 1024
ROW_TILE = 512
EXPERT_BLOCK = 256
ATTN_BLOCK = 256
ATTN_BUFFERS = 2

_E1, _E2, _R1, _R2, _G1, _G2 = range(6)

_F32 = jnp.float32
_BF16 = jnp.bfloat16
_NT = (((1,), (1,)), ((), ()))


def _dot(a, b):
    return jnp.dot(a, b, preferred_element_type=_F32)


def _layer_norm(z, g, b):
    mu = jnp.mean(z, axis=-1, keepdims=True)
    d = z - mu
    var = jnp.mean(d * d, axis=-1, keepdims=True)
    return d * lax.rsqrt(var + LN_EPS) * g + b


def _route_tail(x1, wt_ref, bt_ref, route_ref, routet_ref, counts_ref, umat_ref, *, epg):
    tm = x1.shape[0]
    nr = counts_ref.shape[0]
    half = wt_ref.shape[0] // 2

    @pl.when(pl.program_id(0) == 0)
    def _():
        counts_ref[...] = jnp.zeros_like(counts_ref)
        before = (lax.broadcasted_iota(jnp.int32, (tm, tm), 0) < lax.broadcasted_iota(jnp.int32, (tm, tm), 1))
        umat_ref[...] = jnp.where(before, 1.0, 0.0).astype(_BF16)

    xh = x1.astype(_BF16)
    xl = (x1 - xh.astype(_F32)).astype(_BF16)
    a = lax.dot_general(wt_ref[...], xh, _NT, preferred_element_type=_F32)
    b = lax.dot_general(wt_ref[0:half, :], xl, _NT, preferred_element_type=_F32)
    logits = a[0:nr, :] + a[half:half + nr, :] + b[0:nr, :] + bt_ref[0:nr, :]

    rowf = lax.broadcasted_iota(jnp.int32, (nr, tm), 0).astype(_F32)
    neg = -jnp.inf
    big = float(nr)
    gl = jnp.where(rowf < float(N_GROUPS), logits, neg)
    gmax = jnp.max(gl, axis=0, keepdims=True)
    gidx = jnp.min(jnp.where(gl == gmax, rowf, big), axis=0, keepdims=True)
    gtop = 1.0 / jnp.sum(jnp.exp(gl - gmax), axis=0, keepdims=True)

    lo = float(N_GROUPS) + gidx * float(epg)
    el = jnp.where((rowf >= lo) & (rowf < lo + float(epg)), logits, neg)
    m1 = jnp.max(el, axis=0, keepdims=True)
    i1 = jnp.min(jnp.where(el == m1, rowf, big), axis=0, keepdims=True)
    el2 = jnp.where(rowf == i1, neg, el)
    m2 = jnp.max(el2, axis=0, keepdims=True)
    i2 = jnp.min(jnp.where(el2 == m2, rowf, big), axis=0, keepdims=True)
    w2 = jnp.exp(m2 - m1)
    inv = 1.0 / (1.0 + w2)
    g1 = gtop * inv
    g2 = gtop * w2 * inv

    onehot = jnp.where((rowf == i1) | (rowf == i2), 1.0, 0.0)
    total = _dot(onehot.astype(_BF16), umat_ref[...]) + counts_ref[...]
    r1 = jnp.sum(jnp.where(rowf == i1, total, 0.0), axis=0, keepdims=True)
    r2 = jnp.sum(jnp.where(rowf == i2, total, 0.0), axis=0, keepdims=True)
    counts_ref[...] += jnp.sum(onehot, axis=1, keepdims=True)

    slot = lax.broadcasted_iota(jnp.int32, (SUBLANES, tm), 0)
    rec = jnp.zeros((SUBLANES, tm), _F32)
    fields = ((_E1, i1 - float(N_GROUPS)), (_E2, i2 - float(N_GROUPS)), (_R1, r1), (_R2, r2), (_G1, g1), (_G2, g2))
    for s, val in fields:
        rec = jnp.where(slot == s, val, rec)
    routet_ref[...] = rec
    rec = jnp.concatenate([rec, jnp.zeros((LANES - SUBLANES, tm), _F32)], axis=0)
    route_ref[...] = rec.T


def _store_token_tiles(ref, x):
    n, d = x.shape
    per = d // LANES
    for s in range(per):
        ref[pl.ds(s, n, stride=per), :] = x[:, s * LANES:(s + 1) * LANES]


def _store_token_words(ref, x):
    n, d = x.shape
    rows = d // (2 * LANES)
    bits = lax.bitcast_convert_type(x.astype(_BF16).astype(_F32), jnp.uint32)
    for s in range(rows):
        lo = bits[:, s * LANES:(s + 1) * LANES] >> 16
        hi = bits[:, d // 2 + s * LANES:d // 2 + (s + 1) * LANES] & jnp.uint32(0xFFFF0000)
        ref[pl.ds(s, n, stride=rows), :] = lo | hi


def _load_token_words(ref, n, d):
    rows = d // (2 * LANES)
    words = [ref[pl.ds(s, n, stride=rows), :] for s in range(rows)]
    lo = [lax.bitcast_convert_type(w << 16, _F32) for w in words]
    hi = [lax.bitcast_convert_type(w & jnp.uint32(0xFFFF0000), _F32) for w in words]
    return jnp.concatenate(lo + hi, axis=1).astype(_BF16)


def _load_token_tiles(ref, first_token, n, d):
    per = d // LANES
    start = first_token * per
    return jnp.concatenate([ref[pl.ds(start + s, n, stride=per), :] for s in range(per)], axis=1)


def _conv_mixer_kernel(x_ref, win_ref, cw_ref, wout_ref, g_ref, b_ref, wt_ref, bt_ref,
                       x1_ref, x1t_ref, route_ref, routet_ref, counts_ref, ubuf_ref, umat_ref,
                       *, tiles_per_seq, alpha, epg):
    i = pl.program_id(0)
    tm, d = x_ref.shape

    @pl.when(i % tiles_per_seq == 0)
    def _():
        ubuf_ref[0:SUBLANES, :] = jnp.zeros((SUBLANES, d), _F32)

    x = x_ref[...]
    h = _dot(x.astype(_BF16), win_ref[...])
    u = h[:, d:2 * d] * h[:, 2 * d:]
    ubuf_ref[SUBLANES:SUBLANES + tm, :] = u
    cw = cw_ref[...]
    uc = (cw[0:1, :] * ubuf_ref[SUBLANES - 2:SUBLANES - 2 + tm, :]
          + cw[1:2, :] * ubuf_ref[SUBLANES - 1:SUBLANES - 1 + tm, :]
          + cw[2:3, :] * u)
    ubuf_ref[0:SUBLANES, :] = ubuf_ref[tm:tm + SUBLANES, :]
    y = _dot((h[:, :d] * uc).astype(_BF16), wout_ref[...])
    x1 = _layer_norm(alpha * x + y, g_ref[...], b_ref[...])
    x1_ref[...] = x1
    _store_token_words(x1t_ref, x1)
    _route_tail(x1, wt_ref, bt_ref, route_ref, routet_ref, counts_ref, umat_ref, epg=epg)


def _attn_out_kernel(x_ref, o_ref, wo_ref, g_ref, b_ref, wt_ref, bt_ref,
                     x1_ref, x1t_ref, route_ref, routet_ref, counts_ref, umat_ref, *, alpha, epg):
    y = _dot(o_ref[...], wo_ref[...])
    x1 = _layer_norm(alpha * x_ref[...] + y, g_ref[...], b_ref[...])
    x1_ref[...] = x1
    _store_token_words(x1t_ref, x1)
    _route_tail(x1, wt_ref, bt_ref, route_ref, routet_ref, counts_ref, umat_ref, epg=epg)


def _expert_ffn_kernel(bexp_ref, seg_ref, x_ref, rec_ref, win_hbm, wdn_hbm, y_ref,
                       wstage_in, wstage_dn, wsem, winb, wdnb, *, layer):
    i = pl.program_id(0)
    n_exp = (seg_ref.shape[0] - 1) // 2
    n_used = seg_ref[2 * n_exp]
    d, de2 = wstage_in.shape
    blk = y_ref.shape[0] // (d // LANES)
    de = de2 // 2

    def weight_copies(e):
        return (pltpu.make_async_copy(win_hbm.at[layer, e], wstage_in, wsem.at[0]),
                pltpu.make_async_copy(wdn_hbm.at[layer, e], wstage_dn, wsem.at[1]))

    @pl.when(i == 0)
    def _():
        for c in weight_copies(bexp_ref[0]):
            c.start()

    @pl.when(i < n_used)
    def _():
        e = bexp_ref[i]

        @pl.when((i == 0) | (e != bexp_ref[jnp.maximum(i - 1, 0)]))
        def _():
            for c in weight_copies(e):
                c.wait()
            winb[...] = wstage_in[...].astype(_BF16)
            wdnb[...] = wstage_dn[...].astype(_BF16)
            nxt = seg_ref[n_exp + e] // blk

            @pl.when(nxt < n_used)
            def _():
                for c in weight_copies(bexp_ref[nxt]):
                    c.start()

        h = _dot(_load_token_words(x_ref, blk, d), winb[...])
        g = h[:, :de]
        a = g * jax.nn.sigmoid(g) * h[:, de:]
        rec = rec_ref[...]
        gate = jnp.where(rec[:, _E1:_E1 + 1] == e.astype(_F32), rec[:, _G1:_G1 + 1], rec[:, _G2:_G2 + 1])
        _store_token_tiles(y_ref, _dot(a.astype(_BF16), wdnb[...]) * gate)


def _combine_body(x1_ref, ffn_ref, g_ref, b_ref, *, alpha):
    tm, d = x1_ref.shape
    return _layer_norm(alpha * x1_ref[...] + _load_token_tiles(ffn_ref, 0, tm, d), g_ref[...], b_ref[...])


def _combine_kernel(x1_ref, ffn_ref, g_ref, b_ref, x2_ref, *, alpha):
    x2_ref[...] = _combine_body(x1_ref, ffn_ref, g_ref, b_ref, alpha=alpha)


def _combine_qkv_kernel(x1_ref, ffn_ref, g_ref, b_ref, wk_ref, wv_ref, wq_ref,
                        x2_ref, k_ref, v_ref, q_ref, *, alpha, q_scale):
    x2 = _combine_body(x1_ref, ffn_ref, g_ref, b_ref, alpha=alpha)
    x2_ref[...] = x2
    xb = x2.astype(_BF16)
    k_ref[...] = _dot(xb, wk_ref[...]).astype(_BF16)
    q_ref[...] = (_dot(xb, wq_ref[...]) * q_scale).astype(_BF16)
    v_ref[...] = _dot(xb, wv_ref[...]).astype(_BF16)


def _diff_attn_kernel(lam_ref, g_ref, q_ref, k_ref, v_ref, o_ref, lamfull_ref, s_ref, p_ref, *, lam_init):
    bi, hi = pl.program_id(0), pl.program_id(1)
    seq, hw = k_ref.shape
    tq = tk = ATTN_BLOCK
    nk = seq // tk
    hd = lam_ref.shape[1]

    @pl.when((bi == 0) & (hi == 0))
    def _():
        lam = lam_ref[...]
        a = jnp.sum(lam[0:1, :] * lam[1:2, :], axis=-1, keepdims=True)
        b = jnp.sum(lam[2:3, :] * lam[3:4, :], axis=-1, keepdims=True)
        lamfull_ref[...] = jnp.broadcast_to(jnp.exp(a) - jnp.exp(b) + lam_init, lamfull_ref.shape)

    lam_full = lamfull_ref[0:1, 0:1]
    lane = lax.broadcasted_iota(jnp.int32, (tq, hw), 1)
    row = lax.broadcasted_iota(jnp.int32, (2 * tq, tk), 0)
    col = lax.broadcasted_iota(jnp.int32, (2 * tq, tk), 1)
    causal = col <= jnp.where(row < tq, row, row - tq)
    v_aug = jnp.concatenate([v_ref[...], jnp.ones((seq, hw), _BF16)], axis=1)

    for i in reversed(range(nk)):
        par = i % s_ref.shape[0]
        kv = (i + 1) * tk
        q = q_ref[i * tq:(i + 1) * tq, :]
        zero = jnp.zeros_like(q)
        qcat = jnp.concatenate([jnp.where(lane < hd, q, zero), jnp.where(lane >= hd, q, zero)], axis=0)
        s_ref[par, :, 0:kv] = lax.dot_general(qcat, k_ref[0:kv, :], _NT, preferred_element_type=_F32)
        s_ref[par, :, i * tk:kv] = jnp.where(causal, s_ref[par, :, i * tk:kv], -jnp.inf)
        rowmax = jnp.max(s_ref[par, :, 0:kv], axis=1, keepdims=True)
        p_ref[par, :, 0:kv] = jnp.exp2(s_ref[par, :, 0:kv] - rowmax).astype(_BF16)
        acc = _dot(p_ref[par, :, 0:kv], v_aug[0:kv, :])
        o = acc[:, 0:hw] * (1.0 / acc[:, hw:hw + 1])
        od = o[0:tq, :] - lam_full * o[tq:, :]
        od = od * lax.rsqrt(jnp.mean(od * od, axis=1, keepdims=True) + RMS_EPS)
        o_ref[i * tq:(i + 1) * tq, :] = (od * g_ref[...] * (1.0 - lam_init)).astype(o_ref.dtype)


def _params(*sem):
    return pltpu.CompilerParams(dimension_semantics=sem, vmem_limit_bytes=VMEM_LIMIT_BYTES)


def _full(shape):
    return pl.BlockSpec(shape, lambda *_: (0,) * len(shape))


def _router_weights(rg_w, rg_b, re_w, re_b):
    d = rg_w.shape[0]
    n = rg_w.shape[1] + re_w.shape[1]
    w = jnp.concatenate([rg_w, re_w, jnp.zeros((d, LANES - n), _F32)], axis=1).T
    b = jnp.concatenate([rg_b, re_b, jnp.zeros((LANES - n,), _F32)])[:, None]
    wh = w.astype(_BF16)
    wl = (w - wh.astype(_F32)).astype(_BF16)
    return jnp.concatenate([wh, wl], axis=0), b


def _router_rows(n_experts):
    return -(-(N_GROUPS + n_experts) // (2 * SUBLANES)) * (2 * SUBLANES)


def _route_outs(t, d, tm, nr):
    per = d // LANES
    shapes = (jax.ShapeDtypeStruct((t, d), _F32),
              jax.ShapeDtypeStruct((t * per // 2, LANES), jnp.uint32),
              jax.ShapeDtypeStruct((t, LANES), _F32),
              jax.ShapeDtypeStruct((SUBLANES, t), _F32),
              jax.ShapeDtypeStruct((nr, 1), _F32))
    specs = (pl.BlockSpec((tm, d), lambda i: (i, 0)),
             pl.BlockSpec((tm * per // 2, LANES), lambda i: (i, 0)),
             pl.BlockSpec((tm, LANES), lambda i: (i, 0)),
             pl.BlockSpec((SUBLANES, tm), lambda i: (0, i)),
             pl.BlockSpec((nr, 1), lambda i: (0, 0)))
    return shapes, specs


def _conv_mixer(x, w_in, conv_w, w_out, ln_g, ln_b, router, *, seq, alpha, epg):
    t, d = x.shape
    tm = MIXER_TILE
    wt, bt = router
    shapes, specs = _route_outs(t, d, tm, _router_rows(N_GROUPS * epg))
    return pl.pallas_call(
        functools.partial(_conv_mixer_kernel, tiles_per_seq=seq // tm, alpha=alpha, epg=epg),
        grid=(t // tm,),
        in_specs=[pl.BlockSpec((tm, d), lambda i: (i, 0)),
                  _full(w_in.shape), _full(conv_w.shape), _full(w_out.shape),
                  _full((1, d)), _full((1, d)), _full(wt.shape), _full(bt.shape)],
        out_specs=specs, out_shape=shapes,
        scratch_shapes=[pltpu.VMEM((tm + SUBLANES, d), _F32), pltpu.VMEM((tm, tm), _BF16)],
        compiler_params=_params("arbitrary"),
        name="conv_mixer_ln_router",
    )(x, w_in, conv_w, w_out, ln_g[None, :], ln_b[None, :], wt, bt)


def _attn_out(x, o, w_o, ln_g, ln_b, router, *, alpha, epg):
    t, d = x.shape
    tm = MIXER_TILE
    wt, bt = router
    shapes, specs = _route_outs(t, d, tm, _router_rows(N_GROUPS * epg))
    return pl.pallas_call(
        functools.partial(_attn_out_kernel, alpha=alpha, epg=epg),
        grid=(t // tm,),
        in_specs=[pl.BlockSpec((tm, d), lambda i: (i, 0)),
                  pl.BlockSpec((tm, d), lambda i: (i, 0)),
                  _full(w_o.shape), _full((1, d)), _full((1, d)), _full(wt.shape), _full(bt.shape)],
        out_specs=specs, out_shape=shapes,
        scratch_shapes=[pltpu.VMEM((tm, tm), _BF16)],
        compiler_params=_params("arbitrary"),
        name="attn_out_ln_router",
    )(x, o, w_o, ln_g[None, :], ln_b[None, :], wt, bt)


def _dispatch_plan(route_t, counts, n_experts):
    t = route_t.shape[1]
    blk = EXPERT_BLOCK
    cnt = counts[N_GROUPS:N_GROUPS + n_experts, 0].astype(jnp.int32)
    padded = (cnt + blk - 1) // blk * blk
    pad_end = jnp.cumsum(padded)
    pad_start = pad_end - padded
    ids = jnp.arange(n_experts, dtype=jnp.int32)

    def sorted_row(e_slot, r_slot):
        e = route_t[e_slot].astype(jnp.int32)
        start = jnp.sum(jnp.where(ids[:, None] == e[None, :], pad_start[:, None], 0), axis=0)
        return start + route_t[r_slot].astype(jnp.int32)

    dest = jnp.concatenate([sorted_row(_E1, _R1), sorted_row(_E2, _R2)])
    n_blocks = (2 * t + n_experts * blk) // blk
    block_start = jnp.arange(n_blocks, dtype=jnp.int32) * blk
    block_expert = jnp.minimum(jnp.sum(pad_end[None, :] <= block_start[:, None], axis=1),
                               n_experts - 1).astype(jnp.int32)
    n_used = (pad_end[-1:] // blk).astype(jnp.int32)
    seg = jnp.concatenate([pad_start + cnt, pad_end, n_used]).astype(jnp.int32)
    return dest, block_expert, seg


def _sc_mesh():
    return plsc.VectorSubcoreMesh(core_axis_name="c", subcore_axis_name="s",
                                  num_cores=SC_CORES, num_subcores=SC_SUBCORES)


def _sc_worker_chunks(t, chunk):
    chunks = t // (SC_CORES * SC_SUBCORES * chunk)
    wid = lax.axis_index("c") * SC_SUBCORES + lax.axis_index("s")
    return wid * chunks, chunks


def _sc_dispatch(x1t, route, dest, n_rows):
    t, per, lanes = x1t.shape
    c = SC_CHUNK

    def body(x_hbm, r_hbm, d_hbm, xo_hbm, ro_hbm, xbuf, rbuf, idx1, idx2, lsem, ssem):
        first, chunks = _sc_worker_chunks(t, c)
        pltpu.sync_copy(d_hbm.at[pl.ds(first * c, chunks * c)], idx1)
        pltpu.sync_copy(d_hbm.at[pl.ds(t + first * c, chunks * c)], idx2)

        def loads(j, slot):
            rows = pl.ds((first + j) * c, c)
            return (pltpu.make_async_copy(x_hbm.at[rows], xbuf.at[slot], lsem.at[2 * slot]),
                    pltpu.make_async_copy(r_hbm.at[rows], rbuf.at[slot], lsem.at[2 * slot + 1]))

        for cp in loads(0, 0):
            cp.start()

        @pl.loop(0, chunks)
        def _(j):
            slot = j % 2

            @pl.when(j + 1 < chunks)
            def _():
                for cp in loads(j + 1, 1 - slot):
                    cp.start()

            for cp in loads(j, slot):
                cp.wait()
            i1 = idx1.at[pl.ds(j * c, c)]
            i2 = idx2.at[pl.ds(j * c, c)]
            scatters = (pltpu.make_async_copy(xbuf.at[slot], xo_hbm.at[i1], ssem.at[0]),
                        pltpu.make_async_copy(xbuf.at[slot], xo_hbm.at[i2], ssem.at[1]),
                        pltpu.make_async_copy(rbuf.at[slot], ro_hbm.at[i1], ssem.at[2]),
                        pltpu.make_async_copy(rbuf.at[slot], ro_hbm.at[i2], ssem.at[3]))
            for cp in scatters:
                cp.start()
            for cp in scatters:
                cp.wait()

    per_worker = t // (SC_CORES * SC_SUBCORES)
    return pl.kernel(
        body,
        out_type=(jax.ShapeDtypeStruct((n_rows, per, lanes), x1t.dtype),
                  jax.ShapeDtypeStruct((n_rows, lanes), route.dtype)),
        mesh=_sc_mesh(),
        scratch_types=[pltpu.VMEM((2, c, per, lanes), x1t.dtype),
                       pltpu.VMEM((2, c, lanes), route.dtype),
                       pltpu.VMEM((per_worker,), jnp.int32),
                       pltpu.VMEM((per_worker,), jnp.int32),
                       pltpu.SemaphoreType.DMA((4,)),
                       pltpu.SemaphoreType.DMA((4,))],
        name="sc_dispatch",
    )(x1t, route, dest)


def _sc_combine(y_rows, dest, t):
    _, per, lanes = y_rows.shape
    c = SC_CHUNK

    def body(y_hbm, d_hbm, o_hbm, buf, idx1, idx2, osem):
        first, chunks = _sc_worker_chunks(t, c)
        pltpu.sync_copy(d_hbm.at[pl.ds(first * c, chunks * c)], idx1)
        pltpu.sync_copy(d_hbm.at[pl.ds(t + first * c, chunks * c)], idx2)

        def store(j, slot):
            return pltpu.make_async_copy(buf.at[slot], o_hbm.at[pl.ds((first + j) * c, c)], osem.at[slot])

        @pl.loop(0, chunks)
        def _(j):
            slot = j % 2

            @pl.when(j >= 2)
            def _():
                store(j - 2, slot).wait()

            pltpu.sync_copy(y_hbm.at[idx1.at[pl.ds(j * c, c)]], buf.at[slot])
            pltpu.sync_copy(y_hbm.at[idx2.at[pl.ds(j * c, c)]], buf.at[slot], add=True)
            store(j, slot).start()

        for j in range(max(chunks - 2, 0), chunks):
            store(j, j % 2).wait()

    per_worker = t // (SC_CORES * SC_SUBCORES)
    return pl.kernel(
        body,
        out_type=jax.ShapeDtypeStruct((t, per, lanes), y_rows.dtype),
        mesh=_sc_mesh(),
        scratch_types=[pltpu.VMEM((2, c, per, lanes), y_rows.dtype),
                       pltpu.VMEM((per_worker,), jnp.int32),
                       pltpu.VMEM((per_worker,), jnp.int32),
                       pltpu.SemaphoreType.DMA((2,))],
        name="sc_combine",
    )(y_rows, dest)


def _expert_ffn(x_sorted, r_sorted, w_in, w_down, layer, block_expert, seg):
    n_rows, xper, _ = x_sorted.shape
    _, n_exp, d, de2 = w_in.shape
    de = w_down.shape[2]
    per = d // LANES
    blk = EXPERT_BLOCK
    n_blocks = block_expert.shape[0]

    def used(i, be, sg):
        return jnp.minimum(i, sg[2 * n_exp] - 1), 0

    y = pl.pallas_call(
        functools.partial(_expert_ffn_kernel, layer=layer),
        grid_spec=pltpu.PrefetchScalarGridSpec(
            num_scalar_prefetch=2,
            grid=(n_blocks,),
            in_specs=[pl.BlockSpec((blk * xper, LANES), used),
                      pl.BlockSpec((blk, LANES), used),
                      pl.BlockSpec(memory_space=pl.ANY),
                      pl.BlockSpec(memory_space=pl.ANY)],
            out_specs=pl.BlockSpec((blk * per, LANES), used),
            scratch_shapes=[pltpu.VMEM((d, de2), _F32),
                            pltpu.VMEM((de, d), _F32),
                            pltpu.SemaphoreType.DMA((2,)),
                            pltpu.VMEM((d, de2), _BF16),
                            pltpu.VMEM((de, d), _BF16)]),
        out_shape=jax.ShapeDtypeStruct((n_rows * per, LANES), _F32),
        compiler_params=_params("arbitrary"),
        name="expert_ffn",
    )(block_expert, seg, x_sorted.reshape(n_rows * xper, LANES), r_sorted, w_in, w_down)
    return y.reshape(n_rows, per, LANES)


def _combine(x1, ffn, ln_g, ln_b, *, alpha):
    t, d = x1.shape
    tm = ROW_TILE
    per = d // LANES
    row = pl.BlockSpec((tm, d), lambda i: (i, 0))
    return pl.pallas_call(
        functools.partial(_combine_kernel, alpha=alpha),
        grid=(t // tm,),
        in_specs=[row, pl.BlockSpec((tm * per, LANES), lambda i: (i, 0)), _full((1, d)), _full((1, d))],
        out_specs=row,
        out_shape=jax.ShapeDtypeStruct((t, d), _F32),
        compiler_params=_params("arbitrary"),
        name="combine_ln",
    )(x1, ffn.reshape(t * per, LANES), ln_g[None, :], ln_b[None, :])


def _combine_qkv(x1, ffn, ln_g, ln_b, w_k, w_v, w_q, *, alpha, q_scale):
    t, d = x1.shape
    tm = ROW_TILE
    per = d // LANES
    row = pl.BlockSpec((tm, d), lambda i: (i, 0))
    return pl.pallas_call(
        functools.partial(_combine_qkv_kernel, alpha=alpha, q_scale=q_scale),
        grid=(t // tm,),
        in_specs=[row, pl.BlockSpec((tm * per, LANES), lambda i: (i, 0)), _full((1, d)), _full((1, d)),
                  _full((d, d)), _full((d, d)), _full((d, d))],
        out_specs=(row, row, row, row),
        out_shape=(jax.ShapeDtypeStruct((t, d), _F32),
                   jax.ShapeDtypeStruct((t, d), _BF16),
                   jax.ShapeDtypeStruct((t, d), _BF16),
                   jax.ShapeDtypeStruct((t, d), _BF16)),
        compiler_params=_params("arbitrary"),
        name="combine_ln_qkv",
    )(x1, ffn.reshape(t * per, LANES), ln_g[None, :], ln_b[None, :], w_k, w_v, w_q)


def _diff_attention(q, k, v, lam, subln_g, *, batch, seq, lam_init):
    t, d = q.shape
    tq = ATTN_BLOCK
    hw = d // N_HEADS
    head = pl.BlockSpec((seq, hw), lambda b, h: (b, h))
    return pl.pallas_call(
        functools.partial(_diff_attn_kernel, lam_init=lam_init),
        grid=(batch, N_HEADS),
        in_specs=[_full(lam.shape), _full((1, hw)), head, head, head],
        out_specs=head,
        out_shape=jax.ShapeDtypeStruct((t, d), _BF16),
        scratch_shapes=[pltpu.VMEM((1, LANES), _F32),
                        pltpu.VMEM((ATTN_BUFFERS, 2 * tq, seq), _F32),
                        pltpu.VMEM((ATTN_BUFFERS, 2 * tq, seq), _BF16)],
        compiler_params=_params("arbitrary", "arbitrary"),
        name="diff_attention",
    )(lam, subln_g[None, :], q, k, v)


def _moe(x1t, route, route_t, counts, w_in, w_down, layer):
    n_exp, d = w_in.shape[1:3]
    dest, block_expert, seg = _dispatch_plan(route_t, counts, n_exp)
    x1t = x1t.reshape(-1, d // (2 * LANES), LANES)
    x_sorted, r_sorted = _sc_dispatch(x1t, route, dest, block_expert.shape[0] * EXPERT_BLOCK)
    y_rows = _expert_ffn(x_sorted, r_sorted, w_in, w_down, layer, block_expert, seg)
    return _sc_combine(y_rows, dest, x1t.shape[0])


def kernel(x, a_w_in, a_conv_w, a_w_out, kv_w, b_w_q, b_lambda, b_subln_g, b_w_o,
           ln1_g, ln1_b, ln2_g, ln2_b, rg_w, rg_b, re_w, re_b, e_w_in, e_w_down):
    batch, seq, d = x.shape
    depth = ln1_g.shape[0]
    assert depth == 2 and a_w_in.shape[0] == 1 and b_w_q.shape[0] == 1
    assert seq % MIXER_TILE == 0 and seq % ROW_TILE == 0 and seq % ATTN_BLOCK == 0
    t = batch * seq
    assert t % (SC_CORES * SC_SUBCORES * SC_CHUNK) == 0
    alpha = (2.0 * depth) ** 0.25
    epg = re_w.shape[2] // N_GROUPS
    head_dim = d // (2 * N_HEADS)
    xf = x.reshape(t, d)

    router0 = _router_weights(rg_w[0], rg_b[0], re_w[0], re_b[0])
    x1, x1t, route, route_t, counts = _conv_mixer(xf, a_w_in[0].astype(_BF16), a_conv_w[0],
                                                  a_w_out[0].astype(_BF16), ln1_g[0], ln1_b[0], router0,
                                                  seq=seq, alpha=alpha, epg=epg)
    ffn = _moe(x1t, route, route_t, counts, e_w_in, e_w_down, 0)
    q_scale = head_dim ** -0.5 * math.log2(math.e)
    x2, k, v, q = _combine_qkv(x1, ffn, ln2_g[0], ln2_b[0],
                               kv_w[:, :d].astype(_BF16), kv_w[:, d:].astype(_BF16),
                               b_w_q[0].astype(_BF16), alpha=alpha, q_scale=q_scale)

    lam_init = 0.8 - 0.6 * math.exp(-0.3 * 1)
    o = _diff_attention(q, k, v, b_lambda[0], b_subln_g[0], batch=batch, seq=seq, lam_init=lam_init)
    router1 = _router_weights(rg_w[1], rg_b[1], re_w[1], re_b[1])
    x1, x1t, route, route_t, counts = _attn_out(x2, o, b_w_o[0].astype(_BF16), ln1_g[1], ln1_b[1],
                                                router1, alpha=alpha, epg=epg)
    ffn = _moe(x1t, route, route_t, counts, e_w_in, e_w_down, 1)
    out = _combine(x1, ffn, ln2_g[1], ln2_b[1], alpha=alpha)
    return out.reshape(batch, seq, d)
```

```python
import functools
import math

import jax
import jax.numpy as jnp
from jax import lax
from jax.experimental import pallas as pl
from jax.experimental.pallas import tpu as pltpu
from jax.experimental.pallas import tpu_sc as plsc

N_HEADS = 8
N_GROUPS = 4
LN_EPS = 1e-5
RMS_EPS = 1e-5

LANES = 128
SUBLANES = 8
VMEM_LIMIT_BYTES = 56 * 1024 * 1024

SC_CORES = 2
SC_SUBCORES = 16

SC_CHUNK = 32
MIXER_TILE = 1024
ROW_TILE = 512
EXPERT_BLOCK = 256
ATTN_BLOCK = 256
ATTN_BUFFERS = 2

_E1, _E2, _R1, _R2, _G1, _G2 = range(6)

_F32 = jnp.float32
_BF16 = jnp.bfloat16
_NT = (((1,), (1,)), ((), ()))


def _dot(a, b):
    return jnp.dot(a, b, preferred_element_type=_F32)


def _layer_norm(z, g, b):
    mu = jnp.mean(z, axis=-1, keepdims=True)
    d = z - mu
    var = jnp.mean(d * d, axis=-1, keepdims=True)
    return d * lax.rsqrt(var + LN_EPS) * g + b


def _route_tail(x1, wt_ref, bt_ref, route_ref, routet_ref, counts_ref, umat_ref, *, epg):
    tm = x1.shape[0]
    nr = counts_ref.shape[0]
    half = wt_ref.shape[0] // 2

    @pl.when(pl.program_id(0) == 0)
    def _():
        counts_ref[...] = jnp.zeros_like(counts_ref)
        before = (lax.broadcasted_iota(jnp.int32, (tm, tm), 0) < lax.broadcasted_iota(jnp.int32, (tm, tm), 1))
        umat_ref[...] = jnp.where(before, 1.0, 0.0).astype(_BF16)

    xh = x1.astype(_BF16)
    xl = (x1 - xh.astype(_F32)).astype(_BF16)
    a = lax.dot_general(wt_ref[...], xh, _NT, preferred_element_type=_F32)
    b = lax.dot_general(wt_ref[0:half, :], xl, _NT, preferred_element_type=_F32)
    logits = a[0:nr, :] + a[half:half + nr, :] + b[0:nr, :] + bt_ref[0:nr, :]

    rowf = lax.broadcasted_iota(jnp.int32, (nr, tm), 0).astype(_F32)
    neg = -jnp.inf
    big = float(nr)
    gl = jnp.where(rowf < float(N_GROUPS), logits, neg)
    gmax = jnp.max(gl, axis=0, keepdims=True)
    gidx = jnp.min(jnp.where(gl == gmax, rowf, big), axis=0, keepdims=True)
    gtop = 1.0 / jnp.sum(jnp.exp(gl - gmax), axis=0, keepdims=True)

    lo = float(N_GROUPS) + gidx * float(epg)
    el = jnp.where((rowf >= lo) & (rowf < lo + float(epg)), logits, neg)
    m1 = jnp.max(el, axis=0, keepdims=True)
    i1 = jnp.min(jnp.where(el == m1, rowf, big), axis=0, keepdims=True)
    el2 = jnp.where(rowf == i1, neg, el)
    m2 = jnp.max(el2, axis=0, keepdims=True)
    i2 = jnp.min(jnp.where(el2 == m2, rowf, big), axis=0, keepdims=True)
    w2 = jnp.exp(m2 - m1)
    inv = 1.0 / (1.0 + w2)
    g1 = gtop * inv
    g2 = gtop * w2 * inv

    onehot = jnp.where((rowf == i1) | (rowf == i2), 1.0, 0.0)
    total = _dot(onehot.astype(_BF16), umat_ref[...]) + counts_ref[...]
    r1 = jnp.sum(jnp.where(rowf == i1, total, 0.0), axis=0, keepdims=True)
    r2 = jnp.sum(jnp.where(rowf == i2, total, 0.0), axis=0, keepdims=True)
    counts_ref[...] += jnp.sum(onehot, axis=1, keepdims=True)

    slot = lax.broadcasted_iota(jnp.int32, (SUBLANES, tm), 0)
    rec = jnp.zeros((SUBLANES, tm), _F32)
    fields = ((_E1, i1 - float(N_GROUPS)), (_E2, i2 - float(N_GROUPS)), (_R1, r1), (_R2, r2), (_G1, g1), (_G2, g2))
    for s, val in fields:
        rec = jnp.where(slot == s, val, rec)
    routet_ref[...] = rec
    rec = jnp.concatenate([rec, jnp.zeros((LANES - SUBLANES, tm), _F32)], axis=0)
    route_ref[...] = rec.T


def _store_token_words(ref, x):
    n, d = x.shape
    rows = d // (2 * LANES)
    bits = lax.bitcast_convert_type(x.astype(_BF16).astype(_F32), jnp.uint32)
    for s in range(rows):
        lo = bits[:, s * LANES:(s + 1) * LANES] >> 16
        hi = bits[:, d // 2 + s * LANES:d // 2 + (s + 1) * LANES] & jnp.uint32(0xFFFF0000)
        ref[pl.ds(s, n, stride=rows), :] = lo | hi


def _load_token_words(ref, n, d):
    rows = d // (2 * LANES)
    words = [ref[pl.ds(s, n, stride=rows), :] for s in range(rows)]
    lo = [lax.bitcast_convert_type(w << 16, _F32) for w in words]
    hi = [lax.bitcast_convert_type(w & jnp.uint32(0xFFFF0000), _F32) for w in words]
    return jnp.concatenate(lo + hi, axis=1)


def _conv_mixer_kernel(x_ref, win_ref, cw_ref, wout_ref, g_ref, b_ref, wt_ref, bt_ref,
                       x1_ref, x1t_ref, route_ref, routet_ref, counts_ref, ubuf_ref, umat_ref,
                       *, tiles_per_seq, alpha, epg):
    i = pl.program_id(0)
    tm, d = x_ref.shape

    @pl.when(i % tiles_per_seq == 0)
    def _():
        ubuf_ref[0:SUBLANES, :] = jnp.zeros((SUBLANES, d), _F32)

    x = x_ref[...]
    h = _dot(x.astype(_BF16), win_ref[...])
    u = h[:, d:2 * d] * h[:, 2 * d:]
    ubuf_ref[SUBLANES:SUBLANES + tm, :] = u
    cw = cw_ref[...]
    uc = (cw[0:1, :] * ubuf_ref[SUBLANES - 2:SUBLANES - 2 + tm, :]
          + cw[1:2, :] * ubuf_ref[SUBLANES - 1:SUBLANES - 1 + tm, :]
          + cw[2:3, :] * u)
    ubuf_ref[0:SUBLANES, :] = ubuf_ref[tm:tm + SUBLANES, :]
    y = _dot((h[:, :d] * uc).astype(_BF16), wout_ref[...])
    x1 = _layer_norm(alpha * x + y, g_ref[...], b_ref[...])
    x1_ref[...] = x1
    _store_token_words(x1t_ref, x1)
    _route_tail(x1, wt_ref, bt_ref, route_ref, routet_ref, counts_ref, umat_ref, epg=epg)


def _attn_out_kernel(x_ref, o_ref, wo_ref, g_ref, b_ref, wt_ref, bt_ref,
                     x1_ref, x1t_ref, route_ref, routet_ref, counts_ref, umat_ref, *, alpha, epg):
    y = _dot(o_ref[...], wo_ref[...])
    x1 = _layer_norm(alpha * x_ref[...] + y, g_ref[...], b_ref[...])
    x1_ref[...] = x1
    _store_token_words(x1t_ref, x1)
    _route_tail(x1, wt_ref, bt_ref, route_ref, routet_ref, counts_ref, umat_ref, epg=epg)


def _expert_ffn_kernel(bexp_ref, seg_ref, x_ref, rec_ref, win_hbm, wdn_hbm, y_ref,
                       wstage_in, wstage_dn, wsem, winb, wdnb, *, layer):
    i = pl.program_id(0)
    n_exp = (seg_ref.shape[0] - 1) // 2
    n_used = seg_ref[2 * n_exp]
    d, de2 = wstage_in.shape
    blk = rec_ref.shape[0]
    de = de2 // 2

    def weight_copies(e):
        return (pltpu.make_async_copy(win_hbm.at[layer, e], wstage_in, wsem.at[0]),
                pltpu.make_async_copy(wdn_hbm.at[layer, e], wstage_dn, wsem.at[1]))

    @pl.when(i == 0)
    def _():
        for c in weight_copies(bexp_ref[0]):
            c.start()

    @pl.when(i < n_used)
    def _():
        e = bexp_ref[i]

        @pl.when((i == 0) | (e != bexp_ref[jnp.maximum(i - 1, 0)]))
        def _():
            for c in weight_copies(e):
                c.wait()
            winb[...] = wstage_in[...].astype(_BF16)
            wdnb[...] = wstage_dn[...].astype(_BF16)
            nxt = seg_ref[n_exp + e] // blk

            @pl.when(nxt < n_used)
            def _():
                for c in weight_copies(bexp_ref[nxt]):
                    c.start()

        h = _dot(_load_token_words(x_ref, blk, d).astype(_BF16), winb[...])
        g = h[:, :de]
        a = g * jax.nn.sigmoid(g) * h[:, de:]
        rec = rec_ref[...]
        gate = jnp.where(rec[:, _E1:_E1 + 1] == e.astype(_F32), rec[:, _G1:_G1 + 1], rec[:, _G2:_G2 + 1])
        _store_token_words(y_ref, _dot(a.astype(_BF16), wdnb[...]) * gate)


def _combine_body(x1_ref, ya_ref, yb_ref, g_ref, b_ref, *, alpha):
    tm, d = x1_ref.shape
    ffn = _load_token_words(ya_ref, tm, d) + _load_token_words(yb_ref, tm, d)
    return _layer_norm(alpha * x1_ref[...] + ffn, g_ref[...], b_ref[...])


def _combine_kernel(x1_ref, ya_ref, yb_ref, g_ref, b_ref, x2_ref, *, alpha):
    x2_ref[...] = _combine_body(x1_ref, ya_ref, yb_ref, g_ref, b_ref, alpha=alpha)


def _combine_qkv_kernel(x1_ref, ya_ref, yb_ref, g_ref, b_ref, wk_ref, wv_ref, wq_ref,
                        x2_ref, k_ref, v_ref, q_ref, *, alpha, q_scale):
    x2 = _combine_body(x1_ref, ya_ref, yb_ref, g_ref, b_ref, alpha=alpha)
    x2_ref[...] = x2
    xb = x2.astype(_BF16)
    k_ref[...] = _dot(xb, wk_ref[...]).astype(_BF16)
    q_ref[...] = (_dot(xb, wq_ref[...]) * q_scale).astype(_BF16)
    v_ref[...] = _dot(xb, wv_ref[...]).astype(_BF16)


def _diff_attn_kernel(lam_ref, g_ref, q_ref, k_ref, v_ref, o_ref, lamfull_ref, s_ref, p_ref, *, lam_init):
    bi, hi = pl.program_id(0), pl.program_id(1)
    seq, hw = k_ref.shape
    tq = tk = ATTN_BLOCK
    nk = seq // tk
    hd = lam_ref.shape[1]

    @pl.when((bi == 0) & (hi == 0))
    def _():
        lam = lam_ref[...]
        a = jnp.sum(lam[0:1, :] * lam[1:2, :], axis=-1, keepdims=True)
        b = jnp.sum(lam[2:3, :] * lam[3:4, :], axis=-1, keepdims=True)
        lamfull_ref[...] = jnp.broadcast_to(jnp.exp(a) - jnp.exp(b) + lam_init, lamfull_ref.shape)

    lam_full = lamfull_ref[0:1, 0:1]
    lane = lax.broadcasted_iota(jnp.int32, (tq, hw), 1)
    row = lax.broadcasted_iota(jnp.int32, (2 * tq, tk), 0)
    col = lax.broadcasted_iota(jnp.int32, (2 * tq, tk), 1)
    causal = col <= jnp.where(row < tq, row, row - tq)
    v_aug = jnp.concatenate([v_ref[...], jnp.ones((seq, hw), _BF16)], axis=1)

    for i in reversed(range(nk)):
        par = i % s_ref.shape[0]
        kv = (i + 1) * tk
        q = q_ref[i * tq:(i + 1) * tq, :]
        zero = jnp.zeros_like(q)
        qcat = jnp.concatenate([jnp.where(lane < hd, q, zero), jnp.where(lane >= hd, q, zero)], axis=0)
        s_ref[par, :, 0:kv] = lax.dot_general(qcat, k_ref[0:kv, :], _NT, preferred_element_type=_F32)
        s_ref[par, :, i * tk:kv] = jnp.where(causal, s_ref[par, :, i * tk:kv], -jnp.inf)
        rowmax = jnp.max(s_ref[par, :, 0:kv], axis=1, keepdims=True)
        p_ref[par, :, 0:kv] = jnp.exp2(s_ref[par, :, 0:kv] - rowmax).astype(_BF16)
        acc = _dot(p_ref[par, :, 0:kv], v_aug[0:kv, :])
        o = acc[:, 0:hw] * (1.0 / acc[:, hw:hw + 1])
        od = o[0:tq, :] - lam_full * o[tq:, :]
        od = od * lax.rsqrt(jnp.mean(od * od, axis=1, keepdims=True) + RMS_EPS)
        o_ref[i * tq:(i + 1) * tq, :] = (od * g_ref[...] * (1.0 - lam_init)).astype(o_ref.dtype)


def _params(*sem):
    return pltpu.CompilerParams(dimension_semantics=sem, vmem_limit_bytes=VMEM_LIMIT_BYTES)


def _full(shape):
    return pl.BlockSpec(shape, lambda *_: (0,) * len(shape))


def _router_weights(rg_w, rg_b, re_w, re_b):
    d = rg_w.shape[0]
    n = rg_w.shape[1] + re_w.shape[1]
    w = jnp.concatenate([rg_w, re_w, jnp.zeros((d, LANES - n), _F32)], axis=1).T
    b = jnp.concatenate([rg_b, re_b, jnp.zeros((LANES - n,), _F32)])[:, None]
    wh = w.astype(_BF16)
    wl = (w - wh.astype(_F32)).astype(_BF16)
    return jnp.concatenate([wh, wl], axis=0), b


def _router_rows(n_experts):
    return -(-(N_GROUPS + n_experts) // (2 * SUBLANES)) * (2 * SUBLANES)


def _route_outs(t, d, tm, nr):
    per = d // LANES
    shapes = (jax.ShapeDtypeStruct((t, d), _F32),
              jax.ShapeDtypeStruct((t * per // 2, LANES), jnp.uint32),
              jax.ShapeDtypeStruct((t, LANES), _F32),
              jax.ShapeDtypeStruct((SUBLANES, t), _F32),
              jax.ShapeDtypeStruct((nr, 1), _F32))
    specs = (pl.BlockSpec((tm, d), lambda i: (i, 0)),
             pl.BlockSpec((tm * per // 2, LANES), lambda i: (i, 0)),
             pl.BlockSpec((tm, LANES), lambda i: (i, 0)),
             pl.BlockSpec((SUBLANES, tm), lambda i: (0, i)),
             pl.BlockSpec((nr, 1), lambda i: (0, 0)))
    return shapes, specs


def _conv_mixer(x, w_in, conv_w, w_out, ln_g, ln_b, router, *, seq, alpha, epg):
    t, d = x.shape
    tm = MIXER_TILE
    wt, bt = router
    shapes, specs = _route_outs(t, d, tm, _router_rows(N_GROUPS * epg))
    return pl.pallas_call(
        functools.partial(_conv_mixer_kernel, tiles_per_seq=seq // tm, alpha=alpha, epg=epg),
        grid=(t // tm,),
        in_specs=[pl.BlockSpec((tm, d), lambda i: (i, 0)),
                  _full(w_in.shape), _full(conv_w.shape), _full(w_out.shape),
                  _full((1, d)), _full((1, d)), _full(wt.shape), _full(bt.shape)],
        out_specs=specs, out_shape=shapes,
        scratch_shapes=[pltpu.VMEM((tm + SUBLANES, d), _F32), pltpu.VMEM((tm, tm), _BF16)],
        compiler_params=_params("arbitrary"),
        name="conv_mixer_ln_router",
    )(x, w_in, conv_w, w_out, ln_g[None, :], ln_b[None, :], wt, bt)


def _attn_out(x, o, w_o, ln_g, ln_b, router, *, alpha, epg):
    t, d = x.shape
    tm = MIXER_TILE
    wt, bt = router
    shapes, specs = _route_outs(t, d, tm, _router_rows(N_GROUPS * epg))
    return pl.pallas_call(
        functools.partial(_attn_out_kernel, alpha=alpha, epg=epg),
        grid=(t // tm,),
        in_specs=[pl.BlockSpec((tm, d), lambda i: (i, 0)),
                  pl.BlockSpec((tm, d), lambda i: (i, 0)),
                  _full(w_o.shape), _full((1, d)), _full((1, d)), _full(wt.shape), _full(bt.shape)],
        out_specs=specs, out_shape=shapes,
        scratch_shapes=[pltpu.VMEM((tm, tm), _BF16)],
        compiler_params=_params("arbitrary"),
        name="attn_out_ln_router",
    )(x, o, w_o, ln_g[None, :], ln_b[None, :], wt, bt)


def _dispatch_plan(route_t, counts, n_experts):
    t = route_t.shape[1]
    blk = EXPERT_BLOCK
    cnt = counts[N_GROUPS:N_GROUPS + n_experts, 0].astype(jnp.int32)
    padded = (cnt + blk - 1) // blk * blk
    pad_end = jnp.cumsum(padded)
    pad_start = pad_end - padded
    ids = jnp.arange(n_experts, dtype=jnp.int32)

    def sorted_row(e_slot, r_slot):
        e = route_t[e_slot].astype(jnp.int32)
        start = jnp.sum(jnp.where(ids[:, None] == e[None, :], pad_start[:, None], 0), axis=0)
        return start + route_t[r_slot].astype(jnp.int32)

    dest = jnp.concatenate([sorted_row(_E1, _R1), sorted_row(_E2, _R2)])
    n_blocks = (2 * t + n_experts * blk) // blk
    block_start = jnp.arange(n_blocks, dtype=jnp.int32) * blk
    block_expert = jnp.minimum(jnp.sum(pad_end[None, :] <= block_start[:, None], axis=1),
                               n_experts - 1).astype(jnp.int32)
    n_used = (pad_end[-1:] // blk).astype(jnp.int32)
    seg = jnp.concatenate([pad_start + cnt, pad_end, n_used]).astype(jnp.int32)
    return dest, block_expert, seg


def _sc_mesh():
    return plsc.VectorSubcoreMesh(core_axis_name="c", subcore_axis_name="s",
                                  num_cores=SC_CORES, num_subcores=SC_SUBCORES)


def _sc_worker_chunks(t, chunk):
    chunks = t // (SC_CORES * SC_SUBCORES * chunk)
    wid = lax.axis_index("c") * SC_SUBCORES + lax.axis_index("s")
    return wid * chunks, chunks


def _sc_dispatch(x1t, route, dest, n_rows):
    t, per, lanes = x1t.shape
    c = SC_CHUNK

    def body(x_hbm, r_hbm, d_hbm, xo_hbm, ro_hbm, xbuf, rbuf, idx1, idx2, lsem, ssem):
        first, chunks = _sc_worker_chunks(t, c)
        pltpu.sync_copy(d_hbm.at[pl.ds(first * c, chunks * c)], idx1)
        pltpu.sync_copy(d_hbm.at[pl.ds(t + first * c, chunks * c)], idx2)

        def loads(j, slot):
            rows = pl.ds((first + j) * c, c)
            return (pltpu.make_async_copy(x_hbm.at[rows], xbuf.at[slot], lsem.at[2 * slot]),
                    pltpu.make_async_copy(r_hbm.at[rows], rbuf.at[slot], lsem.at[2 * slot + 1]))

        for cp in loads(0, 0):
            cp.start()

        @pl.loop(0, chunks)
        def _(j):
            slot = j % 2

            @pl.when(j + 1 < chunks)
            def _():
                for cp in loads(j + 1, 1 - slot):
                    cp.start()

            for cp in loads(j, slot):
                cp.wait()
            i1 = idx1.at[pl.ds(j * c, c)]
            i2 = idx2.at[pl.ds(j * c, c)]
            scatters = (pltpu.make_async_copy(xbuf.at[slot], xo_hbm.at[i1], ssem.at[0]),
                        pltpu.make_async_copy(xbuf.at[slot], xo_hbm.at[i2], ssem.at[1]),
                        pltpu.make_async_copy(rbuf.at[slot], ro_hbm.at[i1], ssem.at[2]),
                        pltpu.make_async_copy(rbuf.at[slot], ro_hbm.at[i2], ssem.at[3]))
            for cp in scatters:
                cp.start()
            for cp in scatters:
                cp.wait()

    per_worker = t // (SC_CORES * SC_SUBCORES)
    return pl.kernel(
        body,
        out_type=(jax.ShapeDtypeStruct((n_rows, per, lanes), x1t.dtype),
                  jax.ShapeDtypeStruct((n_rows, lanes), route.dtype)),
        mesh=_sc_mesh(),
        scratch_types=[pltpu.VMEM((2, c, per, lanes), x1t.dtype),
                       pltpu.VMEM((2, c, lanes), route.dtype),
                       pltpu.VMEM((per_worker,), jnp.int32),
                       pltpu.VMEM((per_worker,), jnp.int32),
                       pltpu.SemaphoreType.DMA((4,)),
                       pltpu.SemaphoreType.DMA((4,))],
        name="sc_dispatch",
    )(x1t, route, dest)


def _sc_combine(y_rows, dest, t):
    _, per, lanes = y_rows.shape
    c = SC_CHUNK

    def body(y_hbm, d_hbm, oa_hbm, ob_hbm, buf, idx1, idx2, gsem, osem):
        first, chunks = _sc_worker_chunks(t, c)
        pltpu.sync_copy(d_hbm.at[pl.ds(first * c, chunks * c)], idx1)
        pltpu.sync_copy(d_hbm.at[pl.ds(t + first * c, chunks * c)], idx2)

        def stores(j, slot):
            rows = pl.ds((first + j) * c, c)
            return (pltpu.make_async_copy(buf.at[slot, 0], oa_hbm.at[rows], osem.at[2 * slot]),
                    pltpu.make_async_copy(buf.at[slot, 1], ob_hbm.at[rows], osem.at[2 * slot + 1]))

        @pl.loop(0, chunks)
        def _(j):
            slot = j % 2

            @pl.when(j >= 2)
            def _():
                for cp in stores(j - 2, slot):
                    cp.wait()

            gathers = (pltpu.make_async_copy(y_hbm.at[idx1.at[pl.ds(j * c, c)]], buf.at[slot, 0], gsem.at[0]),
                       pltpu.make_async_copy(y_hbm.at[idx2.at[pl.ds(j * c, c)]], buf.at[slot, 1], gsem.at[1]))
            for cp in gathers:
                cp.start()
            for cp in gathers:
                cp.wait()
            for cp in stores(j, slot):
                cp.start()

        for j in range(max(chunks - 2, 0), chunks):
            for cp in stores(j, j % 2):
                cp.wait()

    per_worker = t // (SC_CORES * SC_SUBCORES)
    out = jax.ShapeDtypeStruct((t, per, lanes), y_rows.dtype)
    return pl.kernel(
        body,
        out_type=(out, out),
        mesh=_sc_mesh(),
        scratch_types=[pltpu.VMEM((2, 2, c, per, lanes), y_rows.dtype),
                       pltpu.VMEM((per_worker,), jnp.int32),
                       pltpu.VMEM((per_worker,), jnp.int32),
                       pltpu.SemaphoreType.DMA((2,)),
                       pltpu.SemaphoreType.DMA((4,))],
        name="sc_combine",
    )(y_rows, dest)


def _expert_ffn(x_sorted, r_sorted, w_in, w_down, layer, block_expert, seg):
    n_rows, per, _ = x_sorted.shape
    _, n_exp, d, de2 = w_in.shape
    de = w_down.shape[2]
    blk = EXPERT_BLOCK
    n_blocks = block_expert.shape[0]

    def used(i, be, sg):
        return jnp.minimum(i, sg[2 * n_exp] - 1), 0

    y = pl.pallas_call(
        functools.partial(_expert_ffn_kernel, layer=layer),
        grid_spec=pltpu.PrefetchScalarGridSpec(
            num_scalar_prefetch=2,
            grid=(n_blocks,),
            in_specs=[pl.BlockSpec((blk * per, LANES), used),
                      pl.BlockSpec((blk, LANES), used),
                      pl.BlockSpec(memory_space=pl.ANY),
                      pl.BlockSpec(memory_space=pl.ANY)],
            out_specs=pl.BlockSpec((blk * per, LANES), used),
            scratch_shapes=[pltpu.VMEM((d, de2), _F32),
                            pltpu.VMEM((de, d), _F32),
                            pltpu.SemaphoreType.DMA((2,)),
                            pltpu.VMEM((d, de2), _BF16),
                            pltpu.VMEM((de, d), _BF16)]),
        out_shape=jax.ShapeDtypeStruct((n_rows * per, LANES), jnp.uint32),
        compiler_params=_params("arbitrary"),
        name="expert_ffn",
    )(block_expert, seg, x_sorted.reshape(n_rows * per, LANES), r_sorted, w_in, w_down)
    return y.reshape(n_rows, per, LANES)


def _combine(x1, ya, yb, ln_g, ln_b, *, alpha):
    t, d = x1.shape
    tm = ROW_TILE
    per = ya.shape[1]
    row = pl.BlockSpec((tm, d), lambda i: (i, 0))
    words = pl.BlockSpec((tm * per, LANES), lambda i: (i, 0))
    return pl.pallas_call(
        functools.partial(_combine_kernel, alpha=alpha),
        grid=(t // tm,),
        in_specs=[row, words, words, _full((1, d)), _full((1, d))],
        out_specs=row,
        out_shape=jax.ShapeDtypeStruct((t, d), _F32),
        compiler_params=_params("arbitrary"),
        name="combine_ln",
    )(x1, ya.reshape(t * per, LANES), yb.reshape(t * per, LANES), ln_g[None, :], ln_b[None, :])


def _combine_qkv(x1, ya, yb, ln_g, ln_b, w_k, w_v, w_q, *, alpha, q_scale):
    t, d = x1.shape
    tm = ROW_TILE
    per = ya.shape[1]
    row = pl.BlockSpec((tm, d), lambda i: (i, 0))
    words = pl.BlockSpec((tm * per, LANES), lambda i: (i, 0))
    return pl.pallas_call(
        functools.partial(_combine_qkv_kernel, alpha=alpha, q_scale=q_scale),
        grid=(t // tm,),
        in_specs=[row, words, words, _full((1, d)), _full((1, d)),
                  _full((d, d)), _full((d, d)), _full((d, d))],
        out_specs=(row, row, row, row),
        out_shape=(jax.ShapeDtypeStruct((t, d), _F32),
                   jax.ShapeDtypeStruct((t, d), _BF16),
                   jax.ShapeDtypeStruct((t, d), _BF16),
                   jax.ShapeDtypeStruct((t, d), _BF16)),
        compiler_params=_params("arbitrary"),
        name="combine_ln_qkv",
    )(x1, ya.reshape(t * per, LANES), yb.reshape(t * per, LANES), ln_g[None, :], ln_b[None, :],
      w_k, w_v, w_q)


def _diff_attention(q, k, v, lam, subln_g, *, batch, seq, lam_init):
    t, d = q.shape
    tq = ATTN_BLOCK
    hw = d // N_HEADS
    head = pl.BlockSpec((seq, hw), lambda b, h: (b, h))
    return pl.pallas_call(
        functools.partial(_diff_attn_kernel, lam_init=lam_init),
        grid=(batch, N_HEADS),
        in_specs=[_full(lam.shape), _full((1, hw)), head, head, head],
        out_specs=head,
        out_shape=jax.ShapeDtypeStruct((t, d), _BF16),
        scratch_shapes=[pltpu.VMEM((1, LANES), _F32),
                        pltpu.VMEM((ATTN_BUFFERS, 2 * tq, seq), _F32),
                        pltpu.VMEM((ATTN_BUFFERS, 2 * tq, seq), _BF16)],
        compiler_params=_params("arbitrary", "arbitrary"),
        name="diff_attention",
    )(lam, subln_g[None, :], q, k, v)


def _moe(x1t, route, route_t, counts, w_in, w_down, layer):
    n_exp, d = w_in.shape[1:3]
    dest, block_expert, seg = _dispatch_plan(route_t, counts, n_exp)
    x1t = x1t.reshape(-1, d // (2 * LANES), LANES)
    x_sorted, r_sorted = _sc_dispatch(x1t, route, dest, block_expert.shape[0] * EXPERT_BLOCK)
    y_rows = _expert_ffn(x_sorted, r_sorted, w_in, w_down, layer, block_expert, seg)
    return _sc_combine(y_rows, dest, x1t.shape[0])


def kernel(x, a_w_in, a_conv_w, a_w_out, kv_w, b_w_q, b_lambda, b_subln_g, b_w_o,
           ln1_g, ln1_b, ln2_g, ln2_b, rg_w, rg_b, re_w, re_b, e_w_in, e_w_down):
    batch, seq, d = x.shape
    depth = ln1_g.shape[0]
    assert depth == 2 and a_w_in.shape[0] == 1 and b_w_q.shape[0] == 1
    assert seq % MIXER_TILE == 0 and seq % ROW_TILE == 0 and seq % ATTN_BLOCK == 0
    t = batch * seq
    assert t % (SC_CORES * SC_SUBCORES * SC_CHUNK) == 0
    alpha = (2.0 * depth) ** 0.25
    epg = re_w.shape[2] // N_GROUPS
    head_dim = d // (2 * N_HEADS)
    xf = x.reshape(t, d)

    router0 = _router_weights(rg_w[0], rg_b[0], re_w[0], re_b[0])
    x1, x1t, route, route_t, counts = _conv_mixer(xf, a_w_in[0].astype(_BF16), a_conv_w[0],
                                                  a_w_out[0].astype(_BF16), ln1_g[0], ln1_b[0], router0,
                                                  seq=seq, alpha=alpha, epg=epg)
    ya, yb = _moe(x1t, route, route_t, counts, e_w_in, e_w_down, 0)
    q_scale = head_dim ** -0.5 * math.log2(math.e)
    x2, k, v, q = _combine_qkv(x1, ya, yb, ln2_g[0], ln2_b[0],
                               kv_w[:, :d].astype(_BF16), kv_w[:, d:].astype(_BF16),
                               b_w_q[0].astype(_BF16), alpha=alpha, q_scale=q_scale)

    lam_init = 0.8 - 0.6 * math.exp(-0.3 * 1)
    o = _diff_attention(q, k, v, b_lambda[0], b_subln_g[0], batch=batch, seq=seq, lam_init=lam_init)
    router1 = _router_weights(rg_w[1], rg_b[1], re_w[1], re_b[1])
    x1, x1t, route, route_t, counts = _attn_out(x2, o, b_w_o[0].astype(_BF16), ln1_g[1], ln1_b[1],
                                                router1, alpha=alpha, epg=epg)
    ya, yb = _moe(x1t, route, route_t, counts, e_w_in, e_w_down, 1)
    out = _combine(x1, ya, yb, ln2_g[1], ln2_b[1], alpha=alpha)
    return out.reshape(batch, seq, d)
```

```python
import functools
import math

import jax
import jax.numpy as jnp
from jax import lax
from jax.experimental import pallas as pl
from jax.experimental.pallas import tpu as pltpu
from jax.experimental.pallas import tpu_sc as plsc

N_HEADS = 8
N_GROUPS = 4
LN_EPS = 1e-5
RMS_EPS = 1e-5

LANES = 128
SUBLANES = 8
VMEM_LIMIT_BYTES = 56 * 1024 * 1024

SC_CORES = 2
SC_SUBCORES = 16

SC_CHUNK = 32
MIXER_TILE = 1024
ROW_TILE = 1024
EXPERT_BLOCK = 512
ATTN_BLOCK = 256
ATTN_BUFFERS = 2

_E1, _E2, _R1, _R2, _G1, _G2 = range(6)

_F32 = jnp.float32
_BF16 = jnp.bfloat16
_NT = (((1,), (1,)), ((), ()))


def _dot(a, b):
    return jnp.dot(a, b, preferred_element_type=_F32)


def _layer_norm(z, g, b):
    mu = jnp.mean(z, axis=-1, keepdims=True)
    d = z - mu
    var = jnp.mean(d * d, axis=-1, keepdims=True)
    return d * lax.rsqrt(var + LN_EPS) * g + b


def _route_tail(x1, wt_ref, bt_ref, route_ref, routet_ref, counts_ref, umat_ref, *, epg):
    tm = x1.shape[0]
    nr = counts_ref.shape[0]
    half = wt_ref.shape[0] // 2

    @pl.when(pl.program_id(0) == 0)
    def _():
        counts_ref[...] = jnp.zeros_like(counts_ref)
        before = (lax.broadcasted_iota(jnp.int32, (tm, tm), 0) < lax.broadcasted_iota(jnp.int32, (tm, tm), 1))
        umat_ref[...] = jnp.where(before, 1.0, 0.0).astype(_BF16)

    xh = x1.astype(_BF16)
    xl = (x1 - xh.astype(_F32)).astype(_BF16)
    a = lax.dot_general(wt_ref[...], xh, _NT, preferred_element_type=_F32)
    b = lax.dot_general(wt_ref[0:half, :], xl, _NT, preferred_element_type=_F32)
    logits = a[0:nr, :] + a[half:half + nr, :] + b[0:nr, :] + bt_ref[0:nr, :]

    rowf = lax.broadcasted_iota(jnp.int32, (nr, tm), 0).astype(_F32)
    neg = -jnp.inf
    big = float(nr)
    gl = jnp.where(rowf < float(N_GROUPS), logits, neg)
    gmax = jnp.max(gl, axis=0, keepdims=True)
    gidx = jnp.min(jnp.where(gl == gmax, rowf, big), axis=0, keepdims=True)
    gtop = 1.0 / jnp.sum(jnp.exp(gl - gmax), axis=0, keepdims=True)

    lo = float(N_GROUPS) + gidx * float(epg)
    el = jnp.where((rowf >= lo) & (rowf < lo + float(epg)), logits, neg)
    m1 = jnp.max(el, axis=0, keepdims=True)
    i1 = jnp.min(jnp.where(el == m1, rowf, big), axis=0, keepdims=True)
    el2 = jnp.where(rowf == i1, neg, el)
    m2 = jnp.max(el2, axis=0, keepdims=True)
    i2 = jnp.min(jnp.where(el2 == m2, rowf, big), axis=0, keepdims=True)
    w2 = jnp.exp(m2 - m1)
    inv = 1.0 / (1.0 + w2)
    g1 = gtop * inv
    g2 = gtop * w2 * inv

    onehot = jnp.where((rowf == i1) | (rowf == i2), 1.0, 0.0)
    total = _dot(onehot.astype(_BF16), umat_ref[...]) + counts_ref[...]
    r1 = jnp.sum(jnp.where(rowf == i1, total, 0.0), axis=0, keepdims=True)
    r2 = jnp.sum(jnp.where(rowf == i2, total, 0.0), axis=0, keepdims=True)
    counts_ref[...] += jnp.sum(onehot, axis=1, keepdims=True)

    slot = lax.broadcasted_iota(jnp.int32, (SUBLANES, tm), 0)
    rec = jnp.zeros((SUBLANES, tm), _F32)
    fields = ((_E1, i1 - float(N_GROUPS)), (_E2, i2 - float(N_GROUPS)), (_R1, r1), (_R2, r2), (_G1, g1), (_G2, g2))
    for s, val in fields:
        rec = jnp.where(slot == s, val, rec)
    routet_ref[...] = rec
    rec = jnp.concatenate([rec, jnp.zeros((LANES - SUBLANES, tm), _F32)], axis=0)
    route_ref[...] = rec.T


def _store_token_words(ref, x):
    n, d = x.shape
    rows = d // (2 * LANES)
    bits = lax.bitcast_convert_type(x.astype(_BF16).astype(_F32), jnp.uint32)
    for s in range(rows):
        lo = bits[:, s * LANES:(s + 1) * LANES] >> 16
        hi = bits[:, d // 2 + s * LANES:d // 2 + (s + 1) * LANES] & jnp.uint32(0xFFFF0000)
        ref[pl.ds(s, n, stride=rows), :] = lo | hi


def _load_token_words(ref, n, d):
    rows = d // (2 * LANES)
    words = [ref[pl.ds(s, n, stride=rows), :] for s in range(rows)]
    lo = [lax.bitcast_convert_type(w << 16, _F32) for w in words]
    hi = [lax.bitcast_convert_type(w & jnp.uint32(0xFFFF0000), _F32) for w in words]
    return jnp.concatenate(lo + hi, axis=1)


def _conv_mixer_kernel(x_ref, win_ref, cw_ref, wout_ref, g_ref, b_ref, wt_ref, bt_ref,
                       x1_ref, x1t_ref, route_ref, routet_ref, counts_ref, ubuf_ref, umat_ref,
                       *, tiles_per_seq, alpha, epg):
    i = pl.program_id(0)
    tm, d = x_ref.shape

    @pl.when(i % tiles_per_seq == 0)
    def _():
        ubuf_ref[0:SUBLANES, :] = jnp.zeros((SUBLANES, d), _F32)

    x = x_ref[...]
    h = _dot(x.astype(_BF16), win_ref[...])
    u = h[:, d:2 * d] * h[:, 2 * d:]
    ubuf_ref[SUBLANES:SUBLANES + tm, :] = u
    cw = cw_ref[...]
    uc = (cw[0:1, :] * ubuf_ref[SUBLANES - 2:SUBLANES - 2 + tm, :]
          + cw[1:2, :] * ubuf_ref[SUBLANES - 1:SUBLANES - 1 + tm, :]
          + cw[2:3, :] * u)
    ubuf_ref[0:SUBLANES, :] = ubuf_ref[tm:tm + SUBLANES, :]
    y = _dot((h[:, :d] * uc).astype(_BF16), wout_ref[...])
    x1 = _layer_norm(alpha * x + y, g_ref[...], b_ref[...])
    x1_ref[...] = x1
    _store_token_words(x1t_ref, x1)
    _route_tail(x1, wt_ref, bt_ref, route_ref, routet_ref, counts_ref, umat_ref, epg=epg)


def _attn_out_kernel(x_ref, o_ref, wo_ref, g_ref, b_ref, wt_ref, bt_ref,
                     x1_ref, x1t_ref, route_ref, routet_ref, counts_ref, umat_ref, *, alpha, epg):
    y = _dot(o_ref[...], wo_ref[...])
    x1 = _layer_norm(alpha * x_ref[...] + y, g_ref[...], b_ref[...])
    x1_ref[...] = x1
    _store_token_words(x1t_ref, x1)
    _route_tail(x1, wt_ref, bt_ref, route_ref, routet_ref, counts_ref, umat_ref, epg=epg)


def _expert_ffn_kernel(bexp_ref, seg_ref, x_ref, rec_ref, win_hbm, wdn_hbm, y_ref,
                       wstage_in, wstage_dn, wsem, winb, wdnb, *, layer):
    i = pl.program_id(0)
    n_exp = (seg_ref.shape[0] - 1) // 2
    n_used = seg_ref[2 * n_exp]
    d, de2 = wstage_in.shape
    blk = rec_ref.shape[0]
    de = de2 // 2

    def weight_copies(e):
        return (pltpu.make_async_copy(win_hbm.at[layer, e], wstage_in, wsem.at[0]),
                pltpu.make_async_copy(wdn_hbm.at[layer, e], wstage_dn, wsem.at[1]))

    @pl.when(i == 0)
    def _():
        for c in weight_copies(bexp_ref[0]):
            c.start()

    @pl.when(i < n_used)
    def _():
        e = bexp_ref[i]

        @pl.when((i == 0) | (e != bexp_ref[jnp.maximum(i - 1, 0)]))
        def _():
            for c in weight_copies(e):
                c.wait()
            winb[...] = wstage_in[...].astype(_BF16)
            wdnb[...] = wstage_dn[...].astype(_BF16)
            nxt = seg_ref[n_exp + e] // blk

            @pl.when(nxt < n_used)
            def _():
                for c in weight_copies(bexp_ref[nxt]):
                    c.start()

        h = _dot(_load_token_words(x_ref, blk, d).astype(_BF16), winb[...])
        g = h[:, :de]
        a = g * jax.nn.sigmoid(g) * h[:, de:]
        rec = rec_ref[...]
        gate = jnp.where(rec[:, _E1:_E1 + 1] == e.astype(_F32), rec[:, _G1:_G1 + 1], rec[:, _G2:_G2 + 1])
        _store_token_words(y_ref, _dot(a.astype(_BF16), wdnb[...]) * gate)


def _combine_body(x1_ref, ya_ref, yb_ref, g_ref, b_ref, *, alpha):
    tm, d = x1_ref.shape
    ffn = _load_token_words(ya_ref, tm, d) + _load_token_words(yb_ref, tm, d)
    return _layer_norm(alpha * x1_ref[...] + ffn, g_ref[...], b_ref[...])


def _combine_kernel(x1_ref, ya_ref, yb_ref, g_ref, b_ref, x2_ref, *, alpha):
    x2_ref[...] = _combine_body(x1_ref, ya_ref, yb_ref, g_ref, b_ref, alpha=alpha)


def _combine_qkv_kernel(x1_ref, ya_ref, yb_ref, g_ref, b_ref, wk_ref, wv_ref, wq_ref,
                        x2_ref, k_ref, v_ref, q_ref, *, alpha, q_scale):
    x2 = _combine_body(x1_ref, ya_ref, yb_ref, g_ref, b_ref, alpha=alpha)
    x2_ref[...] = x2
    xb = x2.astype(_BF16)
    k_ref[...] = _dot(xb, wk_ref[...]).astype(_BF16)
    q_ref[...] = (_dot(xb, wq_ref[...]) * q_scale).astype(_BF16)
    v_ref[...] = _dot(xb, wv_ref[...]).astype(_BF16)


def _diff_attn_kernel(lam_ref, g_ref, q_ref, k_ref, v_ref, o_ref, lamfull_ref, s_ref, p_ref, *, lam_init):
    bi, hi = pl.program_id(0), pl.program_id(1)
    seq, hw = k_ref.shape
    tq = tk = ATTN_BLOCK
    nk = seq // tk
    hd = lam_ref.shape[1]

    @pl.when((bi == 0) & (hi == 0))
    def _():
        lam = lam_ref[...]
        a = jnp.sum(lam[0:1, :] * lam[1:2, :], axis=-1, keepdims=True)
        b = jnp.sum(lam[2:3, :] * lam[3:4, :], axis=-1, keepdims=True)
        lamfull_ref[...] = jnp.broadcast_to(jnp.exp(a) - jnp.exp(b) + lam_init, lamfull_ref.shape)

    lam_full = lamfull_ref[0:1, 0:1]
    lane = lax.broadcasted_iota(jnp.int32, (tq, hw), 1)
    row = lax.broadcasted_iota(jnp.int32, (2 * tq, tk), 0)
    col = lax.broadcasted_iota(jnp.int32, (2 * tq, tk), 1)
    causal = col <= jnp.where(row < tq, row, row - tq)
    v_aug = jnp.concatenate([v_ref[...], jnp.ones((seq, hw), _BF16)], axis=1)

    for i in reversed(range(nk)):
        par = i % s_ref.shape[0]
        kv = (i + 1) * tk
        q = q_ref[i * tq:(i + 1) * tq, :]
        zero = jnp.zeros_like(q)
        qcat = jnp.concatenate([jnp.where(lane < hd, q, zero), jnp.where(lane >= hd, q, zero)], axis=0)
        s_ref[par, :, 0:kv] = lax.dot_general(qcat, k_ref[0:kv, :], _NT, preferred_element_type=_F32)
        s_ref[par, :, i * tk:kv] = jnp.where(causal, s_ref[par, :, i * tk:kv], -jnp.inf)
        rowmax = jnp.max(s_ref[par, :, 0:kv], axis=1, keepdims=True)
        p_ref[par, :, 0:kv] = jnp.exp2(s_ref[par, :, 0:kv] - rowmax).astype(_BF16)
        acc = _dot(p_ref[par, :, 0:kv], v_aug[0:kv, :])
        o = acc[:, 0:hw] * (1.0 / acc[:, hw:hw + 1])
        od = o[0:tq, :] - lam_full * o[tq:, :]
        od = od * lax.rsqrt(jnp.mean(od * od, axis=1, keepdims=True) + RMS_EPS)
        o_ref[i * tq:(i + 1) * tq, :] = (od * g_ref[...] * (1.0 - lam_init)).astype(o_ref.dtype)


def _params(*sem):
    return pltpu.CompilerParams(dimension_semantics=sem, vmem_limit_bytes=VMEM_LIMIT_BYTES)


def _full(shape):
    return pl.BlockSpec(shape, lambda *_: (0,) * len(shape))


def _router_weights(rg_w, rg_b, re_w, re_b):
    d = rg_w.shape[0]
    n = rg_w.shape[1] + re_w.shape[1]
    w = jnp.concatenate([rg_w, re_w, jnp.zeros((d, LANES - n), _F32)], axis=1).T
    b = jnp.concatenate([rg_b, re_b, jnp.zeros((LANES - n,), _F32)])[:, None]
    wh = w.astype(_BF16)
    wl = (w - wh.astype(_F32)).astype(_BF16)
    return jnp.concatenate([wh, wl], axis=0), b


def _router_rows(n_experts):
    return -(-(N_GROUPS + n_experts) // (2 * SUBLANES)) * (2 * SUBLANES)


def _route_outs(t, d, tm, nr):
    per = d // LANES
    shapes = (jax.ShapeDtypeStruct((t, d), _F32),
              jax.ShapeDtypeStruct((t * per // 2, LANES), jnp.uint32),
              jax.ShapeDtypeStruct((t, LANES), _F32),
              jax.ShapeDtypeStruct((SUBLANES, t), _F32),
              jax.ShapeDtypeStruct((nr, 1), _F32))
    specs = (pl.BlockSpec((tm, d), lambda i: (i, 0)),
             pl.BlockSpec((tm * per // 2, LANES), lambda i: (i, 0)),
             pl.BlockSpec((tm, LANES), lambda i: (i, 0)),
             pl.BlockSpec((SUBLANES, tm), lambda i: (0, i)),
             pl.BlockSpec((nr, 1), lambda i: (0, 0)))
    return shapes, specs


def _conv_mixer(x, w_in, conv_w, w_out, ln_g, ln_b, router, *, seq, alpha, epg):
    t, d = x.shape
    tm = MIXER_TILE
    wt, bt = router
    shapes, specs = _route_outs(t, d, tm, _router_rows(N_GROUPS * epg))
    return pl.pallas_call(
        functools.partial(_conv_mixer_kernel, tiles_per_seq=seq // tm, alpha=alpha, epg=epg),
        grid=(t // tm,),
        in_specs=[pl.BlockSpec((tm, d), lambda i: (i, 0)),
                  _full(w_in.shape), _full(conv_w.shape), _full(w_out.shape),
                  _full((1, d)), _full((1, d)), _full(wt.shape), _full(bt.shape)],
        out_specs=specs, out_shape=shapes,
        scratch_shapes=[pltpu.VMEM((tm + SUBLANES, d), _F32), pltpu.VMEM((tm, tm), _BF16)],
        compiler_params=_params("arbitrary"),
        name="conv_mixer_ln_router",
    )(x, w_in, conv_w, w_out, ln_g[None, :], ln_b[None, :], wt, bt)


def _attn_out(x, o, w_o, ln_g, ln_b, router, *, alpha, epg):
    t, d = x.shape
    tm = MIXER_TILE
    wt, bt = router
    shapes, specs = _route_outs(t, d, tm, _router_rows(N_GROUPS * epg))
    return pl.pallas_call(
        functools.partial(_attn_out_kernel, alpha=alpha, epg=epg),
        grid=(t // tm,),
        in_specs=[pl.BlockSpec((tm, d), lambda i: (i, 0)),
                  pl.BlockSpec((tm, d), lambda i: (i, 0)),
                  _full(w_o.shape), _full((1, d)), _full((1, d)), _full(wt.shape), _full(bt.shape)],
        out_specs=specs, out_shape=shapes,
        scratch_shapes=[pltpu.VMEM((tm, tm), _BF16)],
        compiler_params=_params("arbitrary"),
        name="attn_out_ln_router",
    )(x, o, w_o, ln_g[None, :], ln_b[None, :], wt, bt)


def _dispatch_plan(route_t, counts, n_experts):
    t = route_t.shape[1]
    blk = EXPERT_BLOCK
    cnt = counts[N_GROUPS:N_GROUPS + n_experts, 0].astype(jnp.int32)
    padded = (cnt + blk - 1) // blk * blk
    pad_end = jnp.cumsum(padded)
    pad_start = pad_end - padded
    ids = jnp.arange(n_experts, dtype=jnp.int32)

    def sorted_row(e_slot, r_slot):
        e = route_t[e_slot].astype(jnp.int32)
        start = jnp.sum(jnp.where(ids[:, None] == e[None, :], pad_start[:, None], 0), axis=0)
        return start + route_t[r_slot].astype(jnp.int32)

    dest = jnp.concatenate([sorted_row(_E1, _R1), sorted_row(_E2, _R2)])
    n_blocks = (2 * t + n_experts * blk) // blk
    block_start = jnp.arange(n_blocks, dtype=jnp.int32) * blk
    block_expert = jnp.minimum(jnp.sum(pad_end[None, :] <= block_start[:, None], axis=1),
                               n_experts - 1).astype(jnp.int32)
    n_used = (pad_end[-1:] // blk).astype(jnp.int32)
    seg = jnp.concatenate([pad_start + cnt, pad_end, n_used]).astype(jnp.int32)
    return dest, block_expert, seg


def _sc_mesh():
    return plsc.VectorSubcoreMesh(core_axis_name="c", subcore_axis_name="s",
                                  num_cores=SC_CORES, num_subcores=SC_SUBCORES)


def _sc_worker_chunks(t, chunk):
    chunks = t // (SC_CORES * SC_SUBCORES * chunk)
    wid = lax.axis_index("c") * SC_SUBCORES + lax.axis_index("s")
    return wid * chunks, chunks


def _sc_dispatch(x1t, route, dest, n_rows):
    t, per, lanes = x1t.shape
    c = SC_CHUNK

    def body(x_hbm, r_hbm, d_hbm, xo_hbm, ro_hbm, xbuf, rbuf, idx1, idx2, lsem, ssem):
        first, chunks = _sc_worker_chunks(t, c)
        pltpu.sync_copy(d_hbm.at[pl.ds(first * c, chunks * c)], idx1)
        pltpu.sync_copy(d_hbm.at[pl.ds(t + first * c, chunks * c)], idx2)

        def loads(j, slot):
            rows = pl.ds((first + j) * c, c)
            return (pltpu.make_async_copy(x_hbm.at[rows], xbuf.at[slot], lsem.at[2 * slot]),
                    pltpu.make_async_copy(r_hbm.at[rows], rbuf.at[slot], lsem.at[2 * slot + 1]))

        for cp in loads(0, 0):
            cp.start()

        @pl.loop(0, chunks)
        def _(j):
            slot = j % 2

            @pl.when(j + 1 < chunks)
            def _():
                for cp in loads(j + 1, 1 - slot):
                    cp.start()

            for cp in loads(j, slot):
                cp.wait()
            i1 = idx1.at[pl.ds(j * c, c)]
            i2 = idx2.at[pl.ds(j * c, c)]
            scatters = (pltpu.make_async_copy(xbuf.at[slot], xo_hbm.at[i1], ssem.at[0]),
                        pltpu.make_async_copy(xbuf.at[slot], xo_hbm.at[i2], ssem.at[1]),
                        pltpu.make_async_copy(rbuf.at[slot], ro_hbm.at[i1], ssem.at[2]),
                        pltpu.make_async_copy(rbuf.at[slot], ro_hbm.at[i2], ssem.at[3]))
            for cp in scatters:
                cp.start()
            for cp in scatters:
                cp.wait()

    per_worker = t // (SC_CORES * SC_SUBCORES)
    return pl.kernel(
        body,
        out_type=(jax.ShapeDtypeStruct((n_rows, per, lanes), x1t.dtype),
                  jax.ShapeDtypeStruct((n_rows, lanes), route.dtype)),
        mesh=_sc_mesh(),
        scratch_types=[pltpu.VMEM((2, c, per, lanes), x1t.dtype),
                       pltpu.VMEM((2, c, lanes), route.dtype),
                       pltpu.VMEM((per_worker,), jnp.int32),
                       pltpu.VMEM((per_worker,), jnp.int32),
                       pltpu.SemaphoreType.DMA((4,)),
                       pltpu.SemaphoreType.DMA((4,))],
        name="sc_dispatch",
    )(x1t, route, dest)


def _sc_combine(y_rows, dest, t):
    _, per, lanes = y_rows.shape
    c = SC_CHUNK

    def body(y_hbm, d_hbm, oa_hbm, ob_hbm, buf, idx1, idx2, gsem, osem):
        first, chunks = _sc_worker_chunks(t, c)
        pltpu.sync_copy(d_hbm.at[pl.ds(first * c, chunks * c)], idx1)
        pltpu.sync_copy(d_hbm.at[pl.ds(t + first * c, chunks * c)], idx2)

        def stores(j, slot):
            rows = pl.ds((first + j) * c, c)
            return (pltpu.make_async_copy(buf.at[slot, 0], oa_hbm.at[rows], osem.at[2 * slot]),
                    pltpu.make_async_copy(buf.at[slot, 1], ob_hbm.at[rows], osem.at[2 * slot + 1]))

        @pl.loop(0, chunks)
        def _(j):
            slot = j % 2

            @pl.when(j >= 2)
            def _():
                for cp in stores(j - 2, slot):
                    cp.wait()

            gathers = (pltpu.make_async_copy(y_hbm.at[idx1.at[pl.ds(j * c, c)]], buf.at[slot, 0], gsem.at[0]),
                       pltpu.make_async_copy(y_hbm.at[idx2.at[pl.ds(j * c, c)]], buf.at[slot, 1], gsem.at[1]))
            for cp in gathers:
                cp.start()
            for cp in gathers:
                cp.wait()
            for cp in stores(j, slot):
                cp.start()

        for j in range(max(chunks - 2, 0), chunks):
            for cp in stores(j, j % 2):
                cp.wait()

    per_worker = t // (SC_CORES * SC_SUBCORES)
    out = jax.ShapeDtypeStruct((t, per, lanes), y_rows.dtype)
    return pl.kernel(
        body,
        out_type=(out, out),
        mesh=_sc_mesh(),
        scratch_types=[pltpu.VMEM((2, 2, c, per, lanes), y_rows.dtype),
                       pltpu.VMEM((per_worker,), jnp.int32),
                       pltpu.VMEM((per_worker,), jnp.int32),
                       pltpu.SemaphoreType.DMA((2,)),
                       pltpu.SemaphoreType.DMA((4,))],
        name="sc_combine",
    )(y_rows, dest)


def _expert_ffn(x_sorted, r_sorted, w_in, w_down, layer, block_expert, seg):
    n_rows, per, _ = x_sorted.shape
    _, n_exp, d, de2 = w_in.shape
    de = w_down.shape[2]
    blk = EXPERT_BLOCK
    n_blocks = block_expert.shape[0]

    def used(i, be, sg):
        return jnp.minimum(i, sg[2 * n_exp] - 1), 0

    y = pl.pallas_call(
        functools.partial(_expert_ffn_kernel, layer=layer),
        grid_spec=pltpu.PrefetchScalarGridSpec(
            num_scalar_prefetch=2,
            grid=(n_blocks,),
            in_specs=[pl.BlockSpec((blk * per, LANES), used),
                      pl.BlockSpec((blk, LANES), used),
                      pl.BlockSpec(memory_space=pl.ANY),
                      pl.BlockSpec(memory_space=pl.ANY)],
            out_specs=pl.BlockSpec((blk * per, LANES), used),
            scratch_shapes=[pltpu.VMEM((d, de2), _F32),
                            pltpu.VMEM((de, d), _F32),
                            pltpu.SemaphoreType.DMA((2,)),
                            pltpu.VMEM((d, de2), _BF16),
                            pltpu.VMEM((de, d), _BF16)]),
        out_shape=jax.ShapeDtypeStruct((n_rows * per, LANES), jnp.uint32),
        compiler_params=_params("arbitrary"),
        name="expert_ffn",
    )(block_expert, seg, x_sorted.reshape(n_rows * per, LANES), r_sorted, w_in, w_down)
    return y.reshape(n_rows, per, LANES)


def _combine(x1, ya, yb, ln_g, ln_b, *, alpha):
    t, d = x1.shape
    tm = ROW_TILE
    per = ya.shape[1]
    row = pl.BlockSpec((tm, d), lambda i: (i, 0))
    words = pl.BlockSpec((tm * per, LANES), lambda i: (i, 0))
    return pl.pallas_call(
        functools.partial(_combine_kernel, alpha=alpha),
        grid=(t // tm,),
        in_specs=[row, words, words, _full((1, d)), _full((1, d))],
        out_specs=row,
        out_shape=jax.ShapeDtypeStruct((t, d), _F32),
        compiler_params=_params("arbitrary"),
        name="combine_ln",
    )(x1, ya.reshape(t * per, LANES), yb.reshape(t * per, LANES), ln_g[None, :], ln_b[None, :])


def _combine_qkv(x1, ya, yb, ln_g, ln_b, w_k, w_v, w_q, *, alpha, q_scale):
    t, d = x1.shape
    tm = ROW_TILE
    per = ya.shape[1]
    row = pl.BlockSpec((tm, d), lambda i: (i, 0))
    words = pl.BlockSpec((tm * per, LANES), lambda i: (i, 0))
    return pl.pallas_call(
        functools.partial(_combine_qkv_kernel, alpha=alpha, q_scale=q_scale),
        grid=(t // tm,),
        in_specs=[row, words, words, _full((1, d)), _full((1, d)),
                  _full((d, d)), _full((d, d)), _full((d, d))],
        out_specs=(row, row, row, row),
        out_shape=(jax.ShapeDtypeStruct((t, d), _F32),
                   jax.ShapeDtypeStruct((t, d), _BF16),
                   jax.ShapeDtypeStruct((t, d), _BF16),
                   jax.ShapeDtypeStruct((t, d), _BF16)),
        compiler_params=_params("arbitrary"),
        name="combine_ln_qkv",
    )(x1, ya.reshape(t * per, LANES), yb.reshape(t * per, LANES), ln_g[None, :], ln_b[None, :],
      w_k, w_v, w_q)


def _diff_attention(q, k, v, lam, subln_g, *, batch, seq, lam_init):
    t, d = q.shape
    tq = ATTN_BLOCK
    hw = d // N_HEADS
    head = pl.BlockSpec((seq, hw), lambda b, h: (b, h))
    return pl.pallas_call(
        functools.partial(_diff_attn_kernel, lam_init=lam_init),
        grid=(batch, N_HEADS),
        in_specs=[_full(lam.shape), _full((1, hw)), head, head, head],
        out_specs=head,
        out_shape=jax.ShapeDtypeStruct((t, d), _BF16),
        scratch_shapes=[pltpu.VMEM((1, LANES), _F32),
                        pltpu.VMEM((ATTN_BUFFERS, 2 * tq, seq), _F32),
                        pltpu.VMEM((ATTN_BUFFERS, 2 * tq, seq), _BF16)],
        compiler_params=_params("arbitrary", "arbitrary"),
        name="diff_attention",
    )(lam, subln_g[None, :], q, k, v)


def _moe(x1t, route, route_t, counts, w_in, w_down, layer):
    n_exp, d = w_in.shape[1:3]
    dest, block_expert, seg = _dispatch_plan(route_t, counts, n_exp)
    x1t = x1t.reshape(-1, d // (2 * LANES), LANES)
    x_sorted, r_sorted = _sc_dispatch(x1t, route, dest, block_expert.shape[0] * EXPERT_BLOCK)
    y_rows = _expert_ffn(x_sorted, r_sorted, w_in, w_down, layer, block_expert, seg)
    return _sc_combine(y_rows, dest, x1t.shape[0])


def kernel(x, a_w_in, a_conv_w, a_w_out, kv_w, b_w_q, b_lambda, b_subln_g, b_w_o,
           ln1_g, ln1_b, ln2_g, ln2_b, rg_w, rg_b, re_w, re_b, e_w_in, e_w_down):
    batch, seq, d = x.shape
    depth = ln1_g.shape[0]
    assert depth == 2 and a_w_in.shape[0] == 1 and b_w_q.shape[0] == 1
    assert seq % MIXER_TILE == 0 and seq % ROW_TILE == 0 and seq % ATTN_BLOCK == 0
    t = batch * seq
    assert t % (SC_CORES * SC_SUBCORES * SC_CHUNK) == 0
    alpha = (2.0 * depth) ** 0.25
    epg = re_w.shape[2] // N_GROUPS
    head_dim = d // (2 * N_HEADS)
    xf = x.reshape(t, d)

    router0 = _router_weights(rg_w[0], rg_b[0], re_w[0], re_b[0])
    x1, x1t, route, route_t, counts = _conv_mixer(xf, a_w_in[0].astype(_BF16), a_conv_w[0],
                                                  a_w_out[0].astype(_BF16), ln1_g[0], ln1_b[0], router0,
                                                  seq=seq, alpha=alpha, epg=epg)
    ya, yb = _moe(x1t, route, route_t, counts, e_w_in, e_w_down, 0)
    q_scale = head_dim ** -0.5 * math.log2(math.e)
    x2, k, v, q = _combine_qkv(x1, ya, yb, ln2_g[0], ln2_b[0],
                               kv_w[:, :d].astype(_BF16), kv_w[:, d:].astype(_BF16),
                               b_w_q[0].astype(_BF16), alpha=alpha, q_scale=q_scale)

    lam_init = 0.8 - 0.6 * math.exp(-0.3 * 1)
    o = _diff_attention(q, k, v, b_lambda[0], b_subln_g[0], batch=batch, seq=seq, lam_init=lam_init)
    router1 = _router_weights(rg_w[1], rg_b[1], re_w[1], re_b[1])
    x1, x1t, route, route_t, counts = _attn_out(x2, o, b_w_o[0].astype(_BF16), ln1_g[1], ln1_b[1],
                                                router1, alpha=alpha, epg=epg)
    ya, yb = _moe(x1t, route, route_t, counts, e_w_in, e_w_down, 1)
    out = _combine(x1, ya, yb, ln2_g[1], ln2_b[1], alpha=alpha)
    return out.reshape(batch, seq, d)
```

```python
import functools
import math

import jax
import jax.numpy as jnp
from jax import lax
from jax.experimental import pallas as pl
from jax.experimental.pallas import tpu as pltpu
from jax.experimental.pallas import tpu_sc as plsc

N_HEADS = 8
N_GROUPS = 4
LN_EPS = 1e-5
RMS_EPS = 1e-5

LANES = 128
SUBLANES = 8
VMEM_LIMIT_BYTES = 56 * 1024 * 1024

SC_CORES = 2
SC_SUBCORES = 16

SC_CHUNK = 32
MIXER_TILE = 1024
ROW_TILE = 1024
EXPERT_BLOCK = 512
ATTN_BLOCK = 256
ATTN_HEADS = 2
ATTN_BUFFERS = 4

_E1, _E2, _R1, _R2, _G1, _G2 = range(6)

_F32 = jnp.float32
_BF16 = jnp.bfloat16
_NT = (((1,), (1,)), ((), ()))


def _dot(a, b):
    return jnp.dot(a, b, preferred_element_type=_F32)


def _layer_norm(z, g, b):
    mu = jnp.mean(z, axis=-1, keepdims=True)
    d = z - mu
    var = jnp.mean(d * d, axis=-1, keepdims=True)
    return d * lax.rsqrt(var + LN_EPS) * g + b


def _route_tail(x1, wt_ref, bt_ref, route_ref, routet_ref, counts_ref, umat_ref, *, epg):
    tm = x1.shape[0]
    nr = counts_ref.shape[0]
    half = wt_ref.shape[0] // 2

    @pl.when(pl.program_id(0) == 0)
    def _():
        counts_ref[...] = jnp.zeros_like(counts_ref)
        before = (lax.broadcasted_iota(jnp.int32, (tm, tm), 0) < lax.broadcasted_iota(jnp.int32, (tm, tm), 1))
        umat_ref[...] = jnp.where(before, 1.0, 0.0).astype(_BF16)

    xh = x1.astype(_BF16)
    xl = (x1 - xh.astype(_F32)).astype(_BF16)
    a = lax.dot_general(wt_ref[...], xh, _NT, preferred_element_type=_F32)
    b = lax.dot_general(wt_ref[0:half, :], xl, _NT, preferred_element_type=_F32)
    logits = a[0:nr, :] + a[half:half + nr, :] + b[0:nr, :] + bt_ref[0:nr, :]

    rowf = lax.broadcasted_iota(jnp.int32, (nr, tm), 0).astype(_F32)
    neg = -jnp.inf
    big = float(nr)
    gl = jnp.where(rowf < float(N_GROUPS), logits, neg)
    gmax = jnp.max(gl, axis=0, keepdims=True)
    gidx = jnp.min(jnp.where(gl == gmax, rowf, big), axis=0, keepdims=True)
    gtop = 1.0 / jnp.sum(jnp.exp(gl - gmax), axis=0, keepdims=True)

    lo = float(N_GROUPS) + gidx * float(epg)
    el = jnp.where((rowf >= lo) & (rowf < lo + float(epg)), logits, neg)
    m1 = jnp.max(el, axis=0, keepdims=True)
    i1 = jnp.min(jnp.where(el == m1, rowf, big), axis=0, keepdims=True)
    el2 = jnp.where(rowf == i1, neg, el)
    m2 = jnp.max(el2, axis=0, keepdims=True)
    i2 = jnp.min(jnp.where(el2 == m2, rowf, big), axis=0, keepdims=True)
    w2 = jnp.exp(m2 - m1)
    inv = 1.0 / (1.0 + w2)
    g1 = gtop * inv
    g2 = gtop * w2 * inv

    onehot = jnp.where((rowf == i1) | (rowf == i2), 1.0, 0.0)
    total = _dot(onehot.astype(_BF16), umat_ref[...]) + counts_ref[...]
    r1 = jnp.sum(jnp.where(rowf == i1, total, 0.0), axis=0, keepdims=True)
    r2 = jnp.sum(jnp.where(rowf == i2, total, 0.0), axis=0, keepdims=True)
    counts_ref[...] += jnp.sum(onehot, axis=1, keepdims=True)

    slot = lax.broadcasted_iota(jnp.int32, (SUBLANES, tm), 0)
    rec = jnp.zeros((SUBLANES, tm), _F32)
    fields = ((_E1, i1 - float(N_GROUPS)), (_E2, i2 - float(N_GROUPS)), (_R1, r1), (_R2, r2), (_G1, g1), (_G2, g2))
    for s, val in fields:
        rec = jnp.where(slot == s, val, rec)
    routet_ref[...] = rec
    rec = jnp.concatenate([rec, jnp.zeros((LANES - SUBLANES, tm), _F32)], axis=0)
    route_ref[...] = rec.T


def _store_token_words(ref, x):
    n, d = x.shape
    rows = d // (2 * LANES)
    bits = lax.bitcast_convert_type(x.astype(_BF16).astype(_F32), jnp.uint32)
    for s in range(rows):
        lo = bits[:, s * LANES:(s + 1) * LANES] >> 16
        hi = bits[:, d // 2 + s * LANES:d // 2 + (s + 1) * LANES] & jnp.uint32(0xFFFF0000)
        ref[pl.ds(s, n, stride=rows), :] = lo | hi


def _load_token_words(ref, n, d):
    rows = d // (2 * LANES)
    words = [ref[pl.ds(s, n, stride=rows), :] for s in range(rows)]
    lo = [lax.bitcast_convert_type(w << 16, _F32) for w in words]
    hi = [lax.bitcast_convert_type(w & jnp.uint32(0xFFFF0000), _F32) for w in words]
    return jnp.concatenate(lo + hi, axis=1)


def _conv_mixer_kernel(x_ref, win_ref, cw_ref, wout_ref, g_ref, b_ref, wt_ref, bt_ref,
                       x1_ref, x1t_ref, route_ref, routet_ref, counts_ref, ubuf_ref, umat_ref,
                       *, tiles_per_seq, alpha, epg):
    i = pl.program_id(0)
    tm, d = x_ref.shape

    @pl.when(i % tiles_per_seq == 0)
    def _():
        ubuf_ref[0:SUBLANES, :] = jnp.zeros((SUBLANES, d), _F32)

    x = x_ref[...]
    h = _dot(x.astype(_BF16), win_ref[...])
    u = h[:, d:2 * d] * h[:, 2 * d:]
    ubuf_ref[SUBLANES:SUBLANES + tm, :] = u
    cw = cw_ref[...]
    uc = (cw[0:1, :] * ubuf_ref[SUBLANES - 2:SUBLANES - 2 + tm, :]
          + cw[1:2, :] * ubuf_ref[SUBLANES - 1:SUBLANES - 1 + tm, :]
          + cw[2:3, :] * u)
    ubuf_ref[0:SUBLANES, :] = ubuf_ref[tm:tm + SUBLANES, :]
    y = _dot((h[:, :d] * uc).astype(_BF16), wout_ref[...])
    x1 = _layer_norm(alpha * x + y, g_ref[...], b_ref[...])
    x1_ref[...] = x1
    _store_token_words(x1t_ref, x1)
    _route_tail(x1, wt_ref, bt_ref, route_ref, routet_ref, counts_ref, umat_ref, epg=epg)


def _attn_out_kernel(x_ref, o_ref, wo_ref, g_ref, b_ref, wt_ref, bt_ref,
                     x1_ref, x1t_ref, route_ref, routet_ref, counts_ref, umat_ref, *, alpha, epg):
    y = _dot(o_ref[...], wo_ref[...])
    x1 = _layer_norm(alpha * x_ref[...] + y, g_ref[...], b_ref[...])
    x1_ref[...] = x1
    _store_token_words(x1t_ref, x1)
    _route_tail(x1, wt_ref, bt_ref, route_ref, routet_ref, counts_ref, umat_ref, epg=epg)


def _expert_ffn_kernel(bexp_ref, seg_ref, x_ref, rec_ref, win_hbm, wdn_hbm, y_ref,
                       wstage_in, wstage_dn, wsem, winb, wdnb, *, layer):
    i = pl.program_id(0)
    n_exp = (seg_ref.shape[0] - 1) // 2
    n_used = seg_ref[2 * n_exp]
    d, de2 = wstage_in.shape
    blk = rec_ref.shape[0]
    de = de2 // 2

    def weight_copies(e):
        return (pltpu.make_async_copy(win_hbm.at[layer, e], wstage_in, wsem.at[0]),
                pltpu.make_async_copy(wdn_hbm.at[layer, e], wstage_dn, wsem.at[1]))

    @pl.when(i == 0)
    def _():
        for c in weight_copies(bexp_ref[0]):
            c.start()

    @pl.when(i < n_used)
    def _():
        e = bexp_ref[i]

        @pl.when((i == 0) | (e != bexp_ref[jnp.maximum(i - 1, 0)]))
        def _():
            for c in weight_copies(e):
                c.wait()
            winb[...] = wstage_in[...].astype(_BF16)
            wdnb[...] = wstage_dn[...].astype(_BF16)
            nxt = seg_ref[n_exp + e] // blk

            @pl.when(nxt < n_used)
            def _():
                for c in weight_copies(bexp_ref[nxt]):
                    c.start()

        h = _dot(_load_token_words(x_ref, blk, d).astype(_BF16), winb[...])
        g = h[:, :de]
        a = g * jax.nn.sigmoid(g) * h[:, de:]
        rec = rec_ref[...]
        gate = jnp.where(rec[:, _E1:_E1 + 1] == e.astype(_F32), rec[:, _G1:_G1 + 1], rec[:, _G2:_G2 + 1])
        _store_token_words(y_ref, _dot(a.astype(_BF16), wdnb[...]) * gate)


def _combine_body(x1_ref, ya_ref, yb_ref, g_ref, b_ref, *, alpha):
    tm, d = x1_ref.shape
    ffn = _load_token_words(ya_ref, tm, d) + _load_token_words(yb_ref, tm, d)
    return _layer_norm(alpha * x1_ref[...] + ffn, g_ref[...], b_ref[...])


def _combine_kernel(x1_ref, ya_ref, yb_ref, g_ref, b_ref, x2_ref, *, alpha):
    x2_ref[...] = _combine_body(x1_ref, ya_ref, yb_ref, g_ref, b_ref, alpha=alpha)


def _combine_qkv_kernel(x1_ref, ya_ref, yb_ref, g_ref, b_ref, wk_ref, wv_ref, wq_ref,
                        x2_ref, k_ref, v_ref, q_ref, *, alpha, q_scale):
    x2 = _combine_body(x1_ref, ya_ref, yb_ref, g_ref, b_ref, alpha=alpha)
    x2_ref[...] = x2
    xb = x2.astype(_BF16)
    k_ref[...] = _dot(xb, wk_ref[...]).astype(_BF16)
    q_ref[...] = (_dot(xb, wq_ref[...]) * q_scale).astype(_BF16)
    v_ref[...] = _dot(xb, wv_ref[...]).astype(_BF16)


def _diff_attn_kernel(lam_ref, g_ref, q_ref, k_ref, v_ref, o_ref, lamfull_ref, s_ref, p_ref, *, lam_init):
    bi, hi = pl.program_id(0), pl.program_id(1)
    seq = k_ref.shape[0]
    hw = g_ref.shape[1]
    heads = k_ref.shape[1] // hw
    tq = tk = ATTN_BLOCK
    nk = seq // tk
    hd = lam_ref.shape[1]

    @pl.when((bi == 0) & (hi == 0))
    def _():
        lam = lam_ref[...]
        a = jnp.sum(lam[0:1, :] * lam[1:2, :], axis=-1, keepdims=True)
        b = jnp.sum(lam[2:3, :] * lam[3:4, :], axis=-1, keepdims=True)
        lamfull_ref[...] = jnp.broadcast_to(jnp.exp(a) - jnp.exp(b) + lam_init, lamfull_ref.shape)

    lam_full = lamfull_ref[0:1, 0:1]
    lane = lax.broadcasted_iota(jnp.int32, (tq, hw), 1)
    row = lax.broadcasted_iota(jnp.int32, (2 * tq, tk), 0)
    col = lax.broadcasted_iota(jnp.int32, (2 * tq, tk), 1)
    causal = col <= jnp.where(row < tq, row, row - tq)
    ones = jnp.ones((seq, hw), _BF16)

    step = 0
    for i in reversed(range(nk)):
        for h in range(heads):
            cols = slice(h * hw, (h + 1) * hw)
            par = step % s_ref.shape[0]
            step += 1
            kv = (i + 1) * tk
            q = q_ref[i * tq:(i + 1) * tq, cols]
            zero = jnp.zeros_like(q)
            qcat = jnp.concatenate([jnp.where(lane < hd, q, zero), jnp.where(lane >= hd, q, zero)], axis=0)
            s_ref[par, :, 0:kv] = lax.dot_general(qcat, k_ref[0:kv, cols], _NT, preferred_element_type=_F32)
            s_ref[par, :, i * tk:kv] = jnp.where(causal, s_ref[par, :, i * tk:kv], -jnp.inf)
            rowmax = jnp.max(s_ref[par, :, 0:kv], axis=1, keepdims=True)
            p_ref[par, :, 0:kv] = jnp.exp2(s_ref[par, :, 0:kv] - rowmax).astype(_BF16)
            v_aug = jnp.concatenate([v_ref[0:kv, cols], ones[0:kv, :]], axis=1)
            acc = _dot(p_ref[par, :, 0:kv], v_aug)
            o = acc[:, 0:hw] * (1.0 / acc[:, hw:hw + 1])
            od = o[0:tq, :] - lam_full * o[tq:, :]
            od = od * lax.rsqrt(jnp.mean(od * od, axis=1, keepdims=True) + RMS_EPS)
            o_ref[i * tq:(i + 1) * tq, cols] = (od * g_ref[...] * (1.0 - lam_init)).astype(o_ref.dtype)


def _params(*sem):
    return pltpu.CompilerParams(dimension_semantics=sem, vmem_limit_bytes=VMEM_LIMIT_BYTES)


def _full(shape):
    return pl.BlockSpec(shape, lambda *_: (0,) * len(shape))


def _router_weights(rg_w, rg_b, re_w, re_b):
    d = rg_w.shape[0]
    n = rg_w.shape[1] + re_w.shape[1]
    w = jnp.concatenate([rg_w, re_w, jnp.zeros((d, LANES - n), _F32)], axis=1).T
    b = jnp.concatenate([rg_b, re_b, jnp.zeros((LANES - n,), _F32)])[:, None]
    wh = w.astype(_BF16)
    wl = (w - wh.astype(_F32)).astype(_BF16)
    return jnp.concatenate([wh, wl], axis=0), b


def _router_rows(n_experts):
    return -(-(N_GROUPS + n_experts) // (2 * SUBLANES)) * (2 * SUBLANES)


def _route_outs(t, d, tm, nr):
    per = d // LANES
    shapes = (jax.ShapeDtypeStruct((t, d), _F32),
              jax.ShapeDtypeStruct((t * per // 2, LANES), jnp.uint32),
              jax.ShapeDtypeStruct((t, LANES), _F32),
              jax.ShapeDtypeStruct((SUBLANES, t), _F32),
              jax.ShapeDtypeStruct((nr, 1), _F32))
    specs = (pl.BlockSpec((tm, d), lambda i: (i, 0)),
             pl.BlockSpec((tm * per // 2, LANES), lambda i: (i, 0)),
             pl.BlockSpec((tm, LANES), lambda i: (i, 0)),
             pl.BlockSpec((SUBLANES, tm), lambda i: (0, i)),
             pl.BlockSpec((nr, 1), lambda i: (0, 0)))
    return shapes, specs


def _conv_mixer(x, w_in, conv_w, w_out, ln_g, ln_b, router, *, seq, alpha, epg):
    t, d = x.shape
    tm = MIXER_TILE
    wt, bt = router
    shapes, specs = _route_outs(t, d, tm, _router_rows(N_GROUPS * epg))
    return pl.pallas_call(
        functools.partial(_conv_mixer_kernel, tiles_per_seq=seq // tm, alpha=alpha, epg=epg),
        grid=(t // tm,),
        in_specs=[pl.BlockSpec((tm, d), lambda i: (i, 0)),
                  _full(w_in.shape), _full(conv_w.shape), _full(w_out.shape),
                  _full((1, d)), _full((1, d)), _full(wt.shape), _full(bt.shape)],
        out_specs=specs, out_shape=shapes,
        scratch_shapes=[pltpu.VMEM((tm + SUBLANES, d), _F32), pltpu.VMEM((tm, tm), _BF16)],
        compiler_params=_params("arbitrary"),
        name="conv_mixer_ln_router",
    )(x, w_in, conv_w, w_out, ln_g[None, :], ln_b[None, :], wt, bt)


def _attn_out(x, o, w_o, ln_g, ln_b, router, *, alpha, epg):
    t, d = x.shape
    tm = MIXER_TILE
    wt, bt = router
    shapes, specs = _route_outs(t, d, tm, _router_rows(N_GROUPS * epg))
    return pl.pallas_call(
        functools.partial(_attn_out_kernel, alpha=alpha, epg=epg),
        grid=(t // tm,),
        in_specs=[pl.BlockSpec((tm, d), lambda i: (i, 0)),
                  pl.BlockSpec((tm, d), lambda i: (i, 0)),
                  _full(w_o.shape), _full((1, d)), _full((1, d)), _full(wt.shape), _full(bt.shape)],
        out_specs=specs, out_shape=shapes,
        scratch_shapes=[pltpu.VMEM((tm, tm), _BF16)],
        compiler_params=_params("arbitrary"),
        name="attn_out_ln_router",
    )(x, o, w_o, ln_g[None, :], ln_b[None, :], wt, bt)


def _dispatch_plan(route_t, counts, n_experts):
    t = route_t.shape[1]
    blk = EXPERT_BLOCK
    cnt = counts[N_GROUPS:N_GROUPS + n_experts, 0].astype(jnp.int32)
    padded = (cnt + blk - 1) // blk * blk
    pad_end = jnp.cumsum(padded)
    pad_start = pad_end - padded
    ids = jnp.arange(n_experts, dtype=jnp.int32)

    def sorted_row(e_slot, r_slot):
        e = route_t[e_slot].astype(jnp.int32)
        start = jnp.sum(jnp.where(ids[:, None] == e[None, :], pad_start[:, None], 0), axis=0)
        return start + route_t[r_slot].astype(jnp.int32)

    dest = jnp.concatenate([sorted_row(_E1, _R1), sorted_row(_E2, _R2)])
    n_blocks = (2 * t + n_experts * blk) // blk
    block_start = jnp.arange(n_blocks, dtype=jnp.int32) * blk
    block_expert = jnp.minimum(jnp.sum(pad_end[None, :] <= block_start[:, None], axis=1),
                               n_experts - 1).astype(jnp.int32)
    n_used = (pad_end[-1:] // blk).astype(jnp.int32)
    seg = jnp.concatenate([pad_start + cnt, pad_end, n_used]).astype(jnp.int32)
    return dest, block_expert, seg


def _sc_mesh():
    return plsc.VectorSubcoreMesh(core_axis_name="c", subcore_axis_name="s",
                                  num_cores=SC_CORES, num_subcores=SC_SUBCORES)


def _sc_worker_chunks(t, chunk):
    chunks = t // (SC_CORES * SC_SUBCORES * chunk)
    wid = lax.axis_index("c") * SC_SUBCORES + lax.axis_index("s")
    return wid * chunks, chunks


def _sc_dispatch(x1t, route, dest, n_rows):
    t, per, lanes = x1t.shape
    c = SC_CHUNK

    def body(x_hbm, r_hbm, d_hbm, xo_hbm, ro_hbm, xbuf, rbuf, idx1, idx2, lsem, ssem):
        first, chunks = _sc_worker_chunks(t, c)
        pltpu.sync_copy(d_hbm.at[pl.ds(first * c, chunks * c)], idx1)
        pltpu.sync_copy(d_hbm.at[pl.ds(t + first * c, chunks * c)], idx2)

        def loads(j, slot):
            rows = pl.ds((first + j) * c, c)
            return (pltpu.make_async_copy(x_hbm.at[rows], xbuf.at[slot], lsem.at[2 * slot]),
                    pltpu.make_async_copy(r_hbm.at[rows], rbuf.at[slot], lsem.at[2 * slot + 1]))

        for cp in loads(0, 0):
            cp.start()

        @pl.loop(0, chunks)
        def _(j):
            slot = j % 2

            @pl.when(j + 1 < chunks)
            def _():
                for cp in loads(j + 1, 1 - slot):
                    cp.start()

            for cp in loads(j, slot):
                cp.wait()
            i1 = idx1.at[pl.ds(j * c, c)]
            i2 = idx2.at[pl.ds(j * c, c)]
            scatters = (pltpu.make_async_copy(xbuf.at[slot], xo_hbm.at[i1], ssem.at[0]),
                        pltpu.make_async_copy(xbuf.at[slot], xo_hbm.at[i2], ssem.at[1]),
                        pltpu.make_async_copy(rbuf.at[slot], ro_hbm.at[i1], ssem.at[2]),
                        pltpu.make_async_copy(rbuf.at[slot], ro_hbm.at[i2], ssem.at[3]))
            for cp in scatters:
                cp.start()
            for cp in scatters:
                cp.wait()

    per_worker = t // (SC_CORES * SC_SUBCORES)
    return pl.kernel(
        body,
        out_type=(jax.ShapeDtypeStruct((n_rows, per, lanes), x1t.dtype),
                  jax.ShapeDtypeStruct((n_rows, lanes), route.dtype)),
        mesh=_sc_mesh(),
        scratch_types=[pltpu.VMEM((2, c, per, lanes), x1t.dtype),
                       pltpu.VMEM((2, c, lanes), route.dtype),
                       pltpu.VMEM((per_worker,), jnp.int32),
                       pltpu.VMEM((per_worker,), jnp.int32),
                       pltpu.SemaphoreType.DMA((4,)),
                       pltpu.SemaphoreType.DMA((4,))],
        name="sc_dispatch",
    )(x1t, route, dest)


def _sc_combine(y_rows, dest, t):
    _, per, lanes = y_rows.shape
    c = SC_CHUNK

    def body(y_hbm, d_hbm, oa_hbm, ob_hbm, buf, idx1, idx2, gsem, osem):
        first, chunks = _sc_worker_chunks(t, c)
        pltpu.sync_copy(d_hbm.at[pl.ds(first * c, chunks * c)], idx1)
        pltpu.sync_copy(d_hbm.at[pl.ds(t + first * c, chunks * c)], idx2)

        def stores(j, slot):
            rows = pl.ds((first + j) * c, c)
            return (pltpu.make_async_copy(buf.at[slot, 0], oa_hbm.at[rows], osem.at[2 * slot]),
                    pltpu.make_async_copy(buf.at[slot, 1], ob_hbm.at[rows], osem.at[2 * slot + 1]))

        @pl.loop(0, chunks)
        def _(j):
            slot = j % 2

            @pl.when(j >= 2)
            def _():
                for cp in stores(j - 2, slot):
                    cp.wait()

            gathers = (pltpu.make_async_copy(y_hbm.at[idx1.at[pl.ds(j * c, c)]], buf.at[slot, 0], gsem.at[0]),
                       pltpu.make_async_copy(y_hbm.at[idx2.at[pl.ds(j * c, c)]], buf.at[slot, 1], gsem.at[1]))
            for cp in gathers:
                cp.start()
            for cp in gathers:
                cp.wait()
            for cp in stores(j, slot):
                cp.start()

        for j in range(max(chunks - 2, 0), chunks):
            for cp in stores(j, j % 2):
                cp.wait()

    per_worker = t // (SC_CORES * SC_SUBCORES)
    out = jax.ShapeDtypeStruct((t, per, lanes), y_rows.dtype)
    return pl.kernel(
        body,
        out_type=(out, out),
        mesh=_sc_mesh(),
        scratch_types=[pltpu.VMEM((2, 2, c, per, lanes), y_rows.dtype),
                       pltpu.VMEM((per_worker,), jnp.int32),
                       pltpu.VMEM((per_worker,), jnp.int32),
                       pltpu.SemaphoreType.DMA((2,)),
                       pltpu.SemaphoreType.DMA((4,))],
        name="sc_combine",
    )(y_rows, dest)


def _expert_ffn(x_sorted, r_sorted, w_in, w_down, layer, block_expert, seg):
    n_rows, per, _ = x_sorted.shape
    _, n_exp, d, de2 = w_in.shape
    de = w_down.shape[2]
    blk = EXPERT_BLOCK
    n_blocks = block_expert.shape[0]

    def used(i, be, sg):
        return jnp.minimum(i, sg[2 * n_exp] - 1), 0

    y = pl.pallas_call(
        functools.partial(_expert_ffn_kernel, layer=layer),
        grid_spec=pltpu.PrefetchScalarGridSpec(
            num_scalar_prefetch=2,
            grid=(n_blocks,),
            in_specs=[pl.BlockSpec((blk * per, LANES), used),
                      pl.BlockSpec((blk, LANES), used),
                      pl.BlockSpec(memory_space=pl.ANY),
                      pl.BlockSpec(memory_space=pl.ANY)],
            out_specs=pl.BlockSpec((blk * per, LANES), used),
            scratch_shapes=[pltpu.VMEM((d, de2), _F32),
                            pltpu.VMEM((de, d), _F32),
                            pltpu.SemaphoreType.DMA((2,)),
                            pltpu.VMEM((d, de2), _BF16),
                            pltpu.VMEM((de, d), _BF16)]),
        out_shape=jax.ShapeDtypeStruct((n_rows * per, LANES), jnp.uint32),
        compiler_params=_params("arbitrary"),
        name="expert_ffn",
    )(block_expert, seg, x_sorted.reshape(n_rows * per, LANES), r_sorted, w_in, w_down)
    return y.reshape(n_rows, per, LANES)


def _combine(x1, ya, yb, ln_g, ln_b, *, alpha):
    t, d = x1.shape
    tm = ROW_TILE
    per = ya.shape[1]
    row = pl.BlockSpec((tm, d), lambda i: (i, 0))
    words = pl.BlockSpec((tm * per, LANES), lambda i: (i, 0))
    return pl.pallas_call(
        functools.partial(_combine_kernel, alpha=alpha),
        grid=(t // tm,),
        in_specs=[row, words, words, _full((1, d)), _full((1, d))],
        out_specs=row,
        out_shape=jax.ShapeDtypeStruct((t, d), _F32),
        compiler_params=_params("arbitrary"),
        name="combine_ln",
    )(x1, ya.reshape(t * per, LANES), yb.reshape(t * per, LANES), ln_g[None, :], ln_b[None, :])


def _combine_qkv(x1, ya, yb, ln_g, ln_b, w_k, w_v, w_q, *, alpha, q_scale):
    t, d = x1.shape
    tm = ROW_TILE
    per = ya.shape[1]
    row = pl.BlockSpec((tm, d), lambda i: (i, 0))
    words = pl.BlockSpec((tm * per, LANES), lambda i: (i, 0))
    return pl.pallas_call(
        functools.partial(_combine_qkv_kernel, alpha=alpha, q_scale=q_scale),
        grid=(t // tm,),
        in_specs=[row, words, words, _full((1, d)), _full((1, d)),
                  _full((d, d)), _full((d, d)), _full((d, d))],
        out_specs=(row, row, row, row),
        out_shape=(jax.ShapeDtypeStruct((t, d), _F32),
                   jax.ShapeDtypeStruct((t, d), _BF16),
                   jax.ShapeDtypeStruct((t, d), _BF16),
                   jax.ShapeDtypeStruct((t, d), _BF16)),
        compiler_params=_params("arbitrary"),
        name="combine_ln_qkv",
    )(x1, ya.reshape(t * per, LANES), yb.reshape(t * per, LANES), ln_g[None, :], ln_b[None, :],
      w_k, w_v, w_q)


def _diff_attention(q, k, v, lam, subln_g, *, batch, seq, lam_init):
    t, d = q.shape
    tq = ATTN_BLOCK
    hw = d // N_HEADS
    head = pl.BlockSpec((seq, ATTN_HEADS * hw), lambda b, h: (b, h))
    return pl.pallas_call(
        functools.partial(_diff_attn_kernel, lam_init=lam_init),
        grid=(batch, N_HEADS // ATTN_HEADS),
        in_specs=[_full(lam.shape), _full((1, hw)), head, head, head],
        out_specs=head,
        out_shape=jax.ShapeDtypeStruct((t, d), _BF16),
        scratch_shapes=[pltpu.VMEM((1, LANES), _F32),
                        pltpu.VMEM((ATTN_BUFFERS, 2 * tq, seq), _F32),
                        pltpu.VMEM((ATTN_BUFFERS, 2 * tq, seq), _BF16)],
        compiler_params=_params("arbitrary", "arbitrary"),
        name="diff_attention",
    )(lam, subln_g[None, :], q, k, v)


def _moe(x1t, route, route_t, counts, w_in, w_down, layer):
    n_exp, d = w_in.shape[1:3]
    dest, block_expert, seg = _dispatch_plan(route_t, counts, n_exp)
    x1t = x1t.reshape(-1, d // (2 * LANES), LANES)
    x_sorted, r_sorted = _sc_dispatch(x1t, route, dest, block_expert.shape[0] * EXPERT_BLOCK)
    y_rows = _expert_ffn(x_sorted, r_sorted, w_in, w_down, layer, block_expert, seg)
    return _sc_combine(y_rows, dest, x1t.shape[0])


def kernel(x, a_w_in, a_conv_w, a_w_out, kv_w, b_w_q, b_lambda, b_subln_g, b_w_o,
           ln1_g, ln1_b, ln2_g, ln2_b, rg_w, rg_b, re_w, re_b, e_w_in, e_w_down):
    batch, seq, d = x.shape
    depth = ln1_g.shape[0]
    assert depth == 2 and a_w_in.shape[0] == 1 and b_w_q.shape[0] == 1
    assert seq % MIXER_TILE == 0 and seq % ROW_TILE == 0 and seq % ATTN_BLOCK == 0
    t = batch * seq
    assert t % (SC_CORES * SC_SUBCORES * SC_CHUNK) == 0
    alpha = (2.0 * depth) ** 0.25
    epg = re_w.shape[2] // N_GROUPS
    head_dim = d // (2 * N_HEADS)
    xf = x.reshape(t, d)

    router0 = _router_weights(rg_w[0], rg_b[0], re_w[0], re_b[0])
    x1, x1t, route, route_t, counts = _conv_mixer(xf, a_w_in[0].astype(_BF16), a_conv_w[0],
                                                  a_w_out[0].astype(_BF16), ln1_g[0], ln1_b[0], router0,
                                                  seq=seq, alpha=alpha, epg=epg)
    ya, yb = _moe(x1t, route, route_t, counts, e_w_in, e_w_down, 0)
    q_scale = head_dim ** -0.5 * math.log2(math.e)
    x2, k, v, q = _combine_qkv(x1, ya, yb, ln2_g[0], ln2_b[0],
                               kv_w[:, :d].astype(_BF16), kv_w[:, d:].astype(_BF16),
                               b_w_q[0].astype(_BF16), alpha=alpha, q_scale=q_scale)

    lam_init = 0.8 - 0.6 * math.exp(-0.3 * 1)
    o = _diff_attention(q, k, v, b_lambda[0], b_subln_g[0], batch=batch, seq=seq, lam_init=lam_init)
    router1 = _router_weights(rg_w[1], rg_b[1], re_w[1], re_b[1])
    x1, x1t, route, route_t, counts = _attn_out(x2, o, b_w_o[0].astype(_BF16), ln1_g[1], ln1_b[1],
                                                router1, alpha=alpha, epg=epg)
    ya, yb = _moe(x1t, route, route_t, counts, e_w_in, e_w_down, 1)
    out = _combine(x1, ya, yb, ln2_g[1], ln2_b[1], alpha=alpha)
    return out.reshape(batch, seq, d)
```

```python
import functools
import math

import jax
import jax.numpy as jnp
from jax import lax
from jax.experimental import pallas as pl
from jax.experimental.pallas import tpu as pltpu
from jax.experimental.pallas import tpu_sc as plsc

N_HEADS = 8
N_GROUPS = 4
LN_EPS = 1e-5
RMS_EPS = 1e-5

LANES = 128
SUBLANES = 8
VMEM_LIMIT_BYTES = 56 * 1024 * 1024

SC_CORES = 2
SC_SUBCORES = 16

SC_CHUNK = 32
MIXER_TILE = 1024
MIXER_SPLIT = 2
ROW_TILE = 1024
EXPERT_BLOCK = 512
ATTN_BLOCK = 256
ATTN_HEADS = 2
ATTN_BUFFERS = 4

_E1, _E2, _R1, _R2, _G1, _G2 = range(6)

_F32 = jnp.float32
_BF16 = jnp.bfloat16
_NT = (((1,), (1,)), ((), ()))


def _dot(a, b):
    return jnp.dot(a, b, preferred_element_type=_F32)


def _layer_norm(z, g, b):
    mu = jnp.mean(z, axis=-1, keepdims=True)
    d = z - mu
    var = jnp.mean(d * d, axis=-1, keepdims=True)
    return d * lax.rsqrt(var + LN_EPS) * g + b


def _route_tail(x1, wt_ref, bt_ref, route_ref, routet_ref, counts_ref, umat_ref, *, epg, row0=0):
    tm = x1.shape[0]
    nr = counts_ref.shape[0]
    half = wt_ref.shape[0] // 2

    if row0 == 0:
        @pl.when(pl.program_id(0) == 0)
        def _():
            counts_ref[...] = jnp.zeros_like(counts_ref)
            before = (lax.broadcasted_iota(jnp.int32, (tm, tm), 0) < lax.broadcasted_iota(jnp.int32, (tm, tm), 1))
            umat_ref[...] = jnp.where(before, 1.0, 0.0).astype(_BF16)

    xh = x1.astype(_BF16)
    xl = (x1 - xh.astype(_F32)).astype(_BF16)
    a = lax.dot_general(wt_ref[...], xh, _NT, preferred_element_type=_F32)
    b = lax.dot_general(wt_ref[0:half, :], xl, _NT, preferred_element_type=_F32)
    logits = a[0:nr, :] + a[half:half + nr, :] + b[0:nr, :] + bt_ref[0:nr, :]

    rowf = lax.broadcasted_iota(jnp.int32, (nr, tm), 0).astype(_F32)
    neg = -jnp.inf
    big = float(nr)
    gl = jnp.where(rowf < float(N_GROUPS), logits, neg)
    gmax = jnp.max(gl, axis=0, keepdims=True)
    gidx = jnp.min(jnp.where(gl == gmax, rowf, big), axis=0, keepdims=True)
    gtop = 1.0 / jnp.sum(jnp.exp(gl - gmax), axis=0, keepdims=True)

    lo = float(N_GROUPS) + gidx * float(epg)
    el = jnp.where((rowf >= lo) & (rowf < lo + float(epg)), logits, neg)
    m1 = jnp.max(el, axis=0, keepdims=True)
    i1 = jnp.min(jnp.where(el == m1, rowf, big), axis=0, keepdims=True)
    el2 = jnp.where(rowf == i1, neg, el)
    m2 = jnp.max(el2, axis=0, keepdims=True)
    i2 = jnp.min(jnp.where(el2 == m2, rowf, big), axis=0, keepdims=True)
    w2 = jnp.exp(m2 - m1)
    inv = 1.0 / (1.0 + w2)
    g1 = gtop * inv
    g2 = gtop * w2 * inv

    onehot = jnp.where((rowf == i1) | (rowf == i2), 1.0, 0.0)
    total = _dot(onehot.astype(_BF16), umat_ref[...]) + counts_ref[...]
    r1 = jnp.sum(jnp.where(rowf == i1, total, 0.0), axis=0, keepdims=True)
    r2 = jnp.sum(jnp.where(rowf == i2, total, 0.0), axis=0, keepdims=True)
    counts_ref[...] += jnp.sum(onehot, axis=1, keepdims=True)

    slot = lax.broadcasted_iota(jnp.int32, (SUBLANES, tm), 0)
    rec = jnp.zeros((SUBLANES, tm), _F32)
    fields = ((_E1, i1 - float(N_GROUPS)), (_E2, i2 - float(N_GROUPS)), (_R1, r1), (_R2, r2), (_G1, g1), (_G2, g2))
    for s, val in fields:
        rec = jnp.where(slot == s, val, rec)
    routet_ref[:, row0:row0 + tm] = rec
    rec = jnp.concatenate([rec, jnp.zeros((LANES - SUBLANES, tm), _F32)], axis=0)
    route_ref[row0:row0 + tm, :] = rec.T


def _store_token_words(ref, x):
    n, d = x.shape
    rows = d // (2 * LANES)
    bits = lax.bitcast_convert_type(x.astype(_BF16).astype(_F32), jnp.uint32)
    for s in range(rows):
        lo = bits[:, s * LANES:(s + 1) * LANES] >> 16
        hi = bits[:, d // 2 + s * LANES:d // 2 + (s + 1) * LANES] & jnp.uint32(0xFFFF0000)
        ref[pl.ds(s, n, stride=rows), :] = lo | hi


def _load_token_words(ref, n, d):
    rows = d // (2 * LANES)
    words = [ref[pl.ds(s, n, stride=rows), :] for s in range(rows)]
    lo = [lax.bitcast_convert_type(w << 16, _F32) for w in words]
    hi = [lax.bitcast_convert_type(w & jnp.uint32(0xFFFF0000), _F32) for w in words]
    return jnp.concatenate(lo + hi, axis=1)


def _conv_mixer_kernel(x_ref, win_ref, cw_ref, wout_ref, g_ref, b_ref, wt_ref, bt_ref,
                       x1_ref, x1t_ref, route_ref, routet_ref, counts_ref, ubuf_ref, umat_ref,
                       *, tiles_per_seq, alpha, epg):
    i = pl.program_id(0)
    tm, d = x_ref.shape

    @pl.when(i % tiles_per_seq == 0)
    def _():
        ubuf_ref[0:SUBLANES, :] = jnp.zeros((SUBLANES, d), _F32)

    x = x_ref[...]
    h = _dot(x.astype(_BF16), win_ref[...])
    u = h[:, d:2 * d] * h[:, 2 * d:]
    ubuf_ref[SUBLANES:SUBLANES + tm, :] = u
    cw = cw_ref[...]
    uc = (cw[0:1, :] * ubuf_ref[SUBLANES - 2:SUBLANES - 2 + tm, :]
          + cw[1:2, :] * ubuf_ref[SUBLANES - 1:SUBLANES - 1 + tm, :]
          + cw[2:3, :] * u)
    ubuf_ref[0:SUBLANES, :] = ubuf_ref[tm:tm + SUBLANES, :]
    gated = (h[:, :d] * uc).astype(_BF16)
    sub = tm // MIXER_SPLIT
    ys = [_dot(gated[r0:r0 + sub, :], wout_ref[...]) for r0 in range(0, tm, sub)]
    for k, r0 in enumerate(range(0, tm, sub)):
        x1 = _layer_norm(alpha * x[r0:r0 + sub, :] + ys[k], g_ref[...], b_ref[...])
        x1_ref[r0:r0 + sub, :] = x1
        _store_token_words(x1t_ref.at[pl.ds(r0 * x1t_ref.shape[0] // tm, sub * x1t_ref.shape[0] // tm), :], x1)
        _route_tail(x1, wt_ref, bt_ref, route_ref, routet_ref, counts_ref, umat_ref, epg=epg, row0=r0)


def _attn_out_kernel(x_ref, o_ref, wo_ref, g_ref, b_ref, wt_ref, bt_ref,
                     x1_ref, x1t_ref, route_ref, routet_ref, counts_ref, umat_ref, *, alpha, epg):
    tm = x_ref.shape[0]
    sub = tm // MIXER_SPLIT
    ys = [_dot(o_ref[r0:r0 + sub, :], wo_ref[...]) for r0 in range(0, tm, sub)]
    for k, r0 in enumerate(range(0, tm, sub)):
        x1 = _layer_norm(alpha * x_ref[r0:r0 + sub, :] + ys[k], g_ref[...], b_ref[...])
        x1_ref[r0:r0 + sub, :] = x1
        _store_token_words(x1t_ref.at[pl.ds(r0 * x1t_ref.shape[0] // tm, sub * x1t_ref.shape[0] // tm), :], x1)
        _route_tail(x1, wt_ref, bt_ref, route_ref, routet_ref, counts_ref, umat_ref, epg=epg, row0=r0)


def _expert_ffn_kernel(bexp_ref, seg_ref, x_ref, rec_ref, win_hbm, wdn_hbm, y_ref,
                       wstage_in, wstage_dn, wsem, winb, wdnb, *, layer):
    i = pl.program_id(0)
    n_exp = (seg_ref.shape[0] - 1) // 2
    n_used = seg_ref[2 * n_exp]
    d, de2 = wstage_in.shape
    blk = rec_ref.shape[0]
    de = de2 // 2

    def weight_copies(e):
        return (pltpu.make_async_copy(win_hbm.at[layer, e], wstage_in, wsem.at[0]),
                pltpu.make_async_copy(wdn_hbm.at[layer, e], wstage_dn, wsem.at[1]))

    @pl.when(i == 0)
    def _():
        for c in weight_copies(bexp_ref[0]):
            c.start()

    @pl.when(i < n_used)
    def _():
        e = bexp_ref[i]

        @pl.when((i == 0) | (e != bexp_ref[jnp.maximum(i - 1, 0)]))
        def _():
            for c in weight_copies(e):
                c.wait()
            winb[...] = wstage_in[...].astype(_BF16)
            wdnb[...] = wstage_dn[...].astype(_BF16)
            nxt = seg_ref[n_exp + e] // blk

            @pl.when(nxt < n_used)
            def _():
                for c in weight_copies(bexp_ref[nxt]):
                    c.start()

        h = _dot(_load_token_words(x_ref, blk, d).astype(_BF16), winb[...])
        g = h[:, :de]
        a = g * jax.nn.sigmoid(g) * h[:, de:]
        rec = rec_ref[...]
        gate = jnp.where(rec[:, _E1:_E1 + 1] == e.astype(_F32), rec[:, _G1:_G1 + 1], rec[:, _G2:_G2 + 1])
        _store_token_words(y_ref, _dot(a.astype(_BF16), wdnb[...]) * gate)


def _combine_body(x1_ref, ya_ref, yb_ref, g_ref, b_ref, *, alpha):
    tm, d = x1_ref.shape
    ffn = _load_token_words(ya_ref, tm, d) + _load_token_words(yb_ref, tm, d)
    return _layer_norm(alpha * x1_ref[...] + ffn, g_ref[...], b_ref[...])


def _combine_kernel(x1_ref, ya_ref, yb_ref, g_ref, b_ref, x2_ref, *, alpha):
    x2_ref[...] = _combine_body(x1_ref, ya_ref, yb_ref, g_ref, b_ref, alpha=alpha)


def _combine_qkv_kernel(x1_ref, ya_ref, yb_ref, g_ref, b_ref, wk_ref, wv_ref, wq_ref,
                        x2_ref, k_ref, v_ref, q_ref, *, alpha, q_scale):
    x2 = _combine_body(x1_ref, ya_ref, yb_ref, g_ref, b_ref, alpha=alpha)
    x2_ref[...] = x2
    xb = x2.astype(_BF16)
    k_ref[...] = _dot(xb, wk_ref[...]).astype(_BF16)
    q_ref[...] = (_dot(xb, wq_ref[...]) * q_scale).astype(_BF16)
    v_ref[...] = _dot(xb, wv_ref[...]).astype(_BF16)


def _diff_attn_kernel(lam_ref, g_ref, q_ref, k_ref, v_ref, o_ref, lamfull_ref, s_ref, p_ref, *, lam_init):
    bi, hi = pl.program_id(0), pl.program_id(1)
    seq = k_ref.shape[0]
    hw = g_ref.shape[1]
    heads = k_ref.shape[1] // hw
    tq = tk = ATTN_BLOCK
    nk = seq // tk
    hd = lam_ref.shape[1]

    @pl.when((bi == 0) & (hi == 0))
    def _():
        lam = lam_ref[...]
        a = jnp.sum(lam[0:1, :] * lam[1:2, :], axis=-1, keepdims=True)
        b = jnp.sum(lam[2:3, :] * lam[3:4, :], axis=-1, keepdims=True)
        lamfull_ref[...] = jnp.broadcast_to(jnp.exp(a) - jnp.exp(b) + lam_init, lamfull_ref.shape)

    lam_full = lamfull_ref[0:1, 0:1]
    lane = lax.broadcasted_iota(jnp.int32, (tq, hw), 1)
    row = lax.broadcasted_iota(jnp.int32, (2 * tq, tk), 0)
    col = lax.broadcasted_iota(jnp.int32, (2 * tq, tk), 1)
    causal = col <= jnp.where(row < tq, row, row - tq)
    ones = jnp.ones((seq, hw), _BF16)

    step = 0
    for i in reversed(range(nk)):
        for h in range(heads):
            cols = slice(h * hw, (h + 1) * hw)
            par = step % s_ref.shape[0]
            step += 1
            kv = (i + 1) * tk
            q = q_ref[i * tq:(i + 1) * tq, cols]
            zero = jnp.zeros_like(q)
            qcat = jnp.concatenate([jnp.where(lane < hd, q, zero), jnp.where(lane >= hd, q, zero)], axis=0)
            s_ref[par, :, 0:kv] = lax.dot_general(qcat, k_ref[0:kv, cols], _NT, preferred_element_type=_F32)
            s_ref[par, :, i * tk:kv] = jnp.where(causal, s_ref[par, :, i * tk:kv], -jnp.inf)
            rowmax = jnp.max(s_ref[par, :, 0:kv], axis=1, keepdims=True)
            p_ref[par, :, 0:kv] = jnp.exp2(s_ref[par, :, 0:kv] - rowmax).astype(_BF16)
            v_aug = jnp.concatenate([v_ref[0:kv, cols], ones[0:kv, :]], axis=1)
            acc = _dot(p_ref[par, :, 0:kv], v_aug)
            o = acc[:, 0:hw] * (1.0 / acc[:, hw:hw + 1])
            od = o[0:tq, :] - lam_full * o[tq:, :]
            od = od * lax.rsqrt(jnp.mean(od * od, axis=1, keepdims=True) + RMS_EPS)
            o_ref[i * tq:(i + 1) * tq, cols] = (od * g_ref[...] * (1.0 - lam_init)).astype(o_ref.dtype)


def _params(*sem):
    return pltpu.CompilerParams(dimension_semantics=sem, vmem_limit_bytes=VMEM_LIMIT_BYTES)


def _full(shape):
    return pl.BlockSpec(shape, lambda *_: (0,) * len(shape))


def _router_weights(rg_w, rg_b, re_w, re_b):
    d = rg_w.shape[0]
    n = rg_w.shape[1] + re_w.shape[1]
    w = jnp.concatenate([rg_w, re_w, jnp.zeros((d, LANES - n), _F32)], axis=1).T
    b = jnp.concatenate([rg_b, re_b, jnp.zeros((LANES - n,), _F32)])[:, None]
    wh = w.astype(_BF16)
    wl = (w - wh.astype(_F32)).astype(_BF16)
    return jnp.concatenate([wh, wl], axis=0), b


def _router_rows(n_experts):
    return -(-(N_GROUPS + n_experts) // (2 * SUBLANES)) * (2 * SUBLANES)


def _route_outs(t, d, tm, nr):
    per = d // LANES
    shapes = (jax.ShapeDtypeStruct((t, d), _F32),
              jax.ShapeDtypeStruct((t * per // 2, LANES), jnp.uint32),
              jax.ShapeDtypeStruct((t, LANES), _F32),
              jax.ShapeDtypeStruct((SUBLANES, t), _F32),
              jax.ShapeDtypeStruct((nr, 1), _F32))
    specs = (pl.BlockSpec((tm, d), lambda i: (i, 0)),
             pl.BlockSpec((tm * per // 2, LANES), lambda i: (i, 0)),
             pl.BlockSpec((tm, LANES), lambda i: (i, 0)),
             pl.BlockSpec((SUBLANES, tm), lambda i: (0, i)),
             pl.BlockSpec((nr, 1), lambda i: (0, 0)))
    return shapes, specs


def _conv_mixer(x, w_in, conv_w, w_out, ln_g, ln_b, router, *, seq, alpha, epg):
    t, d = x.shape
    tm = MIXER_TILE
    wt, bt = router
    shapes, specs = _route_outs(t, d, tm, _router_rows(N_GROUPS * epg))
    return pl.pallas_call(
        functools.partial(_conv_mixer_kernel, tiles_per_seq=seq // tm, alpha=alpha, epg=epg),
        grid=(t // tm,),
        in_specs=[pl.BlockSpec((tm, d), lambda i: (i, 0)),
                  _full(w_in.shape), _full(conv_w.shape), _full(w_out.shape),
                  _full((1, d)), _full((1, d)), _full(wt.shape), _full(bt.shape)],
        out_specs=specs, out_shape=shapes,
        scratch_shapes=[pltpu.VMEM((tm + SUBLANES, d), _F32),
                        pltpu.VMEM((tm // MIXER_SPLIT, tm // MIXER_SPLIT), _BF16)],
        compiler_params=_params("arbitrary"),
        name="conv_mixer_ln_router",
    )(x, w_in, conv_w, w_out, ln_g[None, :], ln_b[None, :], wt, bt)


def _attn_out(x, o, w_o, ln_g, ln_b, router, *, alpha, epg):
    t, d = x.shape
    tm = MIXER_TILE
    wt, bt = router
    shapes, specs = _route_outs(t, d, tm, _router_rows(N_GROUPS * epg))
    return pl.pallas_call(
        functools.partial(_attn_out_kernel, alpha=alpha, epg=epg),
        grid=(t // tm,),
        in_specs=[pl.BlockSpec((tm, d), lambda i: (i, 0)),
                  pl.BlockSpec((tm, d), lambda i: (i, 0)),
                  _full(w_o.shape), _full((1, d)), _full((1, d)), _full(wt.shape), _full(bt.shape)],
        out_specs=specs, out_shape=shapes,
        scratch_shapes=[pltpu.VMEM((tm // MIXER_SPLIT, tm // MIXER_SPLIT), _BF16)],
        compiler_params=_params("arbitrary"),
        name="attn_out_ln_router",
    )(x, o, w_o, ln_g[None, :], ln_b[None, :], wt, bt)


def _dispatch_plan(route_t, counts, n_experts):
    t = route_t.shape[1]
    blk = EXPERT_BLOCK
    cnt = counts[N_GROUPS:N_GROUPS + n_experts, 0].astype(jnp.int32)
    padded = (cnt + blk - 1) // blk * blk
    pad_end = jnp.cumsum(padded)
    pad_start = pad_end - padded
    ids = jnp.arange(n_experts, dtype=jnp.int32)

    def sorted_row(e_slot, r_slot):
        e = route_t[e_slot].astype(jnp.int32)
        start = jnp.sum(jnp.where(ids[:, None] == e[None, :], pad_start[:, None], 0), axis=0)
        return start + route_t[r_slot].astype(jnp.int32)

    dest = jnp.concatenate([sorted_row(_E1, _R1), sorted_row(_E2, _R2)])
    n_blocks = (2 * t + n_experts * blk) // blk
    block_start = jnp.arange(n_blocks, dtype=jnp.int32) * blk
    block_expert = jnp.minimum(jnp.sum(pad_end[None, :] <= block_start[:, None], axis=1),
                               n_experts - 1).astype(jnp.int32)
    n_used = (pad_end[-1:] // blk).astype(jnp.int32)
    seg = jnp.concatenate([pad_start + cnt, pad_end, n_used]).astype(jnp.int32)
    return dest, block_expert, seg


def _sc_mesh():
    return plsc.VectorSubcoreMesh(core_axis_name="c", subcore_axis_name="s",
                                  num_cores=SC_CORES, num_subcores=SC_SUBCORES)


def _sc_worker_chunks(t, chunk):
    chunks = t // (SC_CORES * SC_SUBCORES * chunk)
    wid = lax.axis_index("c") * SC_SUBCORES + lax.axis_index("s")
    return wid * chunks, chunks


def _sc_dispatch(x1t, route, dest, n_rows):
    t, per, lanes = x1t.shape
    c = SC_CHUNK

    def body(x_hbm, r_hbm, d_hbm, xo_hbm, ro_hbm, xbuf, rbuf, idx1, idx2, lsem, ssem):
        first, chunks = _sc_worker_chunks(t, c)
        pltpu.sync_copy(d_hbm.at[pl.ds(first * c, chunks * c)], idx1)
        pltpu.sync_copy(d_hbm.at[pl.ds(t + first * c, chunks * c)], idx2)

        def loads(j, slot):
            rows = pl.ds((first + j) * c, c)
            return (pltpu.make_async_copy(x_hbm.at[rows], xbuf.at[slot], lsem.at[2 * slot]),
                    pltpu.make_async_copy(r_hbm.at[rows], rbuf.at[slot], lsem.at[2 * slot + 1]))

        for cp in loads(0, 0):
            cp.start()

        @pl.loop(0, chunks)
        def _(j):
            slot = j % 2

            @pl.when(j + 1 < chunks)
            def _():
                for cp in loads(j + 1, 1 - slot):
                    cp.start()

            for cp in loads(j, slot):
                cp.wait()
            i1 = idx1.at[pl.ds(j * c, c)]
            i2 = idx2.at[pl.ds(j * c, c)]
            scatters = (pltpu.make_async_copy(xbuf.at[slot], xo_hbm.at[i1], ssem.at[0]),
                        pltpu.make_async_copy(xbuf.at[slot], xo_hbm.at[i2], ssem.at[1]),
                        pltpu.make_async_copy(rbuf.at[slot], ro_hbm.at[i1], ssem.at[2]),
                        pltpu.make_async_copy(rbuf.at[slot], ro_hbm.at[i2], ssem.at[3]))
            for cp in scatters:
                cp.start()
            for cp in scatters:
                cp.wait()

    per_worker = t // (SC_CORES * SC_SUBCORES)
    return pl.kernel(
        body,
        out_type=(jax.ShapeDtypeStruct((n_rows, per, lanes), x1t.dtype),
                  jax.ShapeDtypeStruct((n_rows, lanes), route.dtype)),
        mesh=_sc_mesh(),
        scratch_types=[pltpu.VMEM((2, c, per, lanes), x1t.dtype),
                       pltpu.VMEM((2, c, lanes), route.dtype),
                       pltpu.VMEM((per_worker,), jnp.int32),
                       pltpu.VMEM((per_worker,), jnp.int32),
                       pltpu.SemaphoreType.DMA((4,)),
                       pltpu.SemaphoreType.DMA((4,))],
        name="sc_dispatch",
    )(x1t, route, dest)


def _sc_combine(y_rows, dest, t):
    _, per, lanes = y_rows.shape
    c = SC_CHUNK

    def body(y_hbm, d_hbm, oa_hbm, ob_hbm, buf, idx1, idx2, gsem, osem):
        first, chunks = _sc_worker_chunks(t, c)
        pltpu.sync_copy(d_hbm.at[pl.ds(first * c, chunks * c)], idx1)
        pltpu.sync_copy(d_hbm.at[pl.ds(t + first * c, chunks * c)], idx2)

        def stores(j, slot):
            rows = pl.ds((first + j) * c, c)
            return (pltpu.make_async_copy(buf.at[slot, 0], oa_hbm.at[rows], osem.at[2 * slot]),
                    pltpu.make_async_copy(buf.at[slot, 1], ob_hbm.at[rows], osem.at[2 * slot + 1]))

        @pl.loop(0, chunks)
        def _(j):
            slot = j % 2

            @pl.when(j >= 2)
            def _():
                for cp in stores(j - 2, slot):
                    cp.wait()

            gathers = (pltpu.make_async_copy(y_hbm.at[idx1.at[pl.ds(j * c, c)]], buf.at[slot, 0], gsem.at[0]),
                       pltpu.make_async_copy(y_hbm.at[idx2.at[pl.ds(j * c, c)]], buf.at[slot, 1], gsem.at[1]))
            for cp in gathers:
                cp.start()
            for cp in gathers:
                cp.wait()
            for cp in stores(j, slot):
                cp.start()

        for j in range(max(chunks - 2, 0), chunks):
            for cp in stores(j, j % 2):
                cp.wait()

    per_worker = t // (SC_CORES * SC_SUBCORES)
    out = jax.ShapeDtypeStruct((t, per, lanes), y_rows.dtype)
    return pl.kernel(
        body,
        out_type=(out, out),
        mesh=_sc_mesh(),
        scratch_types=[pltpu.VMEM((2, 2, c, per, lanes), y_rows.dtype),
                       pltpu.VMEM((per_worker,), jnp.int32),
                       pltpu.VMEM((per_worker,), jnp.int32),
                       pltpu.SemaphoreType.DMA((2,)),
                       pltpu.SemaphoreType.DMA((4,))],
        name="sc_combine",
    )(y_rows, dest)


def _expert_ffn(x_sorted, r_sorted, w_in, w_down, layer, block_expert, seg):
    n_rows, per, _ = x_sorted.shape
    _, n_exp, d, de2 = w_in.shape
    de = w_down.shape[2]
    blk = EXPERT_BLOCK
    n_blocks = block_expert.shape[0]

    def used(i, be, sg):
        return jnp.minimum(i, sg[2 * n_exp] - 1), 0

    y = pl.pallas_call(
        functools.partial(_expert_ffn_kernel, layer=layer),
        grid_spec=pltpu.PrefetchScalarGridSpec(
            num_scalar_prefetch=2,
            grid=(n_blocks,),
            in_specs=[pl.BlockSpec((blk * per, LANES), used),
                      pl.BlockSpec((blk, LANES), used),
                      pl.BlockSpec(memory_space=pl.ANY),
                      pl.BlockSpec(memory_space=pl.ANY)],
            out_specs=pl.BlockSpec((blk * per, LANES), used),
            scratch_shapes=[pltpu.VMEM((d, de2), _F32),
                            pltpu.VMEM((de, d), _F32),
                            pltpu.SemaphoreType.DMA((2,)),
                            pltpu.VMEM((d, de2), _BF16),
                            pltpu.VMEM((de, d), _BF16)]),
        out_shape=jax.ShapeDtypeStruct((n_rows * per, LANES), jnp.uint32),
        compiler_params=_params("arbitrary"),
        name="expert_ffn",
    )(block_expert, seg, x_sorted.reshape(n_rows * per, LANES), r_sorted, w_in, w_down)
    return y.reshape(n_rows, per, LANES)


def _combine(x1, ya, yb, ln_g, ln_b, *, alpha):
    t, d = x1.shape
    tm = ROW_TILE
    per = ya.shape[1]
    row = pl.BlockSpec((tm, d), lambda i: (i, 0))
    words = pl.BlockSpec((tm * per, LANES), lambda i: (i, 0))
    return pl.pallas_call(
        functools.partial(_combine_kernel, alpha=alpha),
        grid=(t // tm,),
        in_specs=[row, words, words, _full((1, d)), _full((1, d))],
        out_specs=row,
        out_shape=jax.ShapeDtypeStruct((t, d), _F32),
        compiler_params=_params("arbitrary"),
        name="combine_ln",
    )(x1, ya.reshape(t * per, LANES), yb.reshape(t * per, LANES), ln_g[None, :], ln_b[None, :])


def _combine_qkv(x1, ya, yb, ln_g, ln_b, w_k, w_v, w_q, *, alpha, q_scale):
    t, d = x1.shape
    tm = ROW_TILE
    per = ya.shape[1]
    row = pl.BlockSpec((tm, d), lambda i: (i, 0))
    words = pl.BlockSpec((tm * per, LANES), lambda i: (i, 0))
    return pl.pallas_call(
        functools.partial(_combine_qkv_kernel, alpha=alpha, q_scale=q_scale),
        grid=(t // tm,),
        in_specs=[row, words, words, _full((1, d)), _full((1, d)),
                  _full((d, d)), _full((d, d)), _full((d, d))],
        out_specs=(row, row, row, row),
        out_shape=(jax.ShapeDtypeStruct((t, d), _F32),
                   jax.ShapeDtypeStruct((t, d), _BF16),
                   jax.ShapeDtypeStruct((t, d), _BF16),
                   jax.ShapeDtypeStruct((t, d), _BF16)),
        compiler_params=_params("arbitrary"),
        name="combine_ln_qkv",
    )(x1, ya.reshape(t * per, LANES), yb.reshape(t * per, LANES), ln_g[None, :], ln_b[None, :],
      w_k, w_v, w_q)


def _diff_attention(q, k, v, lam, subln_g, *, batch, seq, lam_init):
    t, d = q.shape
    tq = ATTN_BLOCK
    hw = d // N_HEADS
    head = pl.BlockSpec((seq, ATTN_HEADS * hw), lambda b, h: (b, h))
    return pl.pallas_call(
        functools.partial(_diff_attn_kernel, lam_init=lam_init),
        grid=(batch, N_HEADS // ATTN_HEADS),
        in_specs=[_full(lam.shape), _full((1, hw)), head, head, head],
        out_specs=head,
        out_shape=jax.ShapeDtypeStruct((t, d), _BF16),
        scratch_shapes=[pltpu.VMEM((1, LANES), _F32),
                        pltpu.VMEM((ATTN_BUFFERS, 2 * tq, seq), _F32),
                        pltpu.VMEM((ATTN_BUFFERS, 2 * tq, seq), _BF16)],
        compiler_params=_params("arbitrary", "arbitrary"),
        name="diff_attention",
    )(lam, subln_g[None, :], q, k, v)


def _moe(x1t, route, route_t, counts, w_in, w_down, layer):
    n_exp, d = w_in.shape[1:3]
    dest, block_expert, seg = _dispatch_plan(route_t, counts, n_exp)
    x1t = x1t.reshape(-1, d // (2 * LANES), LANES)
    x_sorted, r_sorted = _sc_dispatch(x1t, route, dest, block_expert.shape[0] * EXPERT_BLOCK)
    y_rows = _expert_ffn(x_sorted, r_sorted, w_in, w_down, layer, block_expert, seg)
    return _sc_combine(y_rows, dest, x1t.shape[0])


def kernel(x, a_w_in, a_conv_w, a_w_out, kv_w, b_w_q, b_lambda, b_subln_g, b_w_o,
           ln1_g, ln1_b, ln2_g, ln2_b, rg_w, rg_b, re_w, re_b, e_w_in, e_w_down):
    batch, seq, d = x.shape
    depth = ln1_g.shape[0]
    assert depth == 2 and a_w_in.shape[0] == 1 and b_w_q.shape[0] == 1
    assert seq % MIXER_TILE == 0 and seq % ROW_TILE == 0 and seq % ATTN_BLOCK == 0
    t = batch * seq
    assert t % (SC_CORES * SC_SUBCORES * SC_CHUNK) == 0
    alpha = (2.0 * depth) ** 0.25
    epg = re_w.shape[2] // N_GROUPS
    head_dim = d // (2 * N_HEADS)
    xf = x.reshape(t, d)

    router0 = _router_weights(rg_w[0], rg_b[0], re_w[0], re_b[0])
    x1, x1t, route, route_t, counts = _conv_mixer(xf, a_w_in[0].astype(_BF16), a_conv_w[0],
                                                  a_w_out[0].astype(_BF16), ln1_g[0], ln1_b[0], router0,
                                                  seq=seq, alpha=alpha, epg=epg)
    ya, yb = _moe(x1t, route, route_t, counts, e_w_in, e_w_down, 0)
    q_scale = head_dim ** -0.5 * math.log2(math.e)
    x2, k, v, q = _combine_qkv(x1, ya, yb, ln2_g[0], ln2_b[0],
                               kv_w[:, :d].astype(_BF16), kv_w[:, d:].astype(_BF16),
                               b_w_q[0].astype(_BF16), alpha=alpha, q_scale=q_scale)

    lam_init = 0.8 - 0.6 * math.exp(-0.3 * 1)
    o = _diff_attention(q, k, v, b_lambda[0], b_subln_g[0], batch=batch, seq=seq, lam_init=lam_init)
    router1 = _router_weights(rg_w[1], rg_b[1], re_w[1], re_b[1])
    x1, x1t, route, route_t, counts = _attn_out(x2, o, b_w_o[0].astype(_BF16), ln1_g[1], ln1_b[1],
                                                router1, alpha=alpha, epg=epg)
    ya, yb = _moe(x1t, route, route_t, counts, e_w_in, e_w_down, 1)
    out = _combine(x1, ya, yb, ln2_g[1], ln2_b[1], alpha=alpha)
    return out.reshape(batch, seq, d)
```

```python
import functools
import math

import jax
import jax.numpy as jnp
from jax import lax
from jax.experimental import pallas as pl
from jax.experimental.pallas import tpu as pltpu
from jax.experimental.pallas import tpu_sc as plsc

N_HEADS = 8
N_GROUPS = 4
LN_EPS = 1e-5
RMS_EPS = 1e-5

LANES = 128
SUBLANES = 8
VMEM_LIMIT_BYTES = 56 * 1024 * 1024

SC_CORES = 2
SC_SUBCORES = 16

SC_CHUNK = 32
MIXER_TILE = 1024
ROW_TILE = 1024
EXPERT_BLOCK = 512
ATTN_BLOCK = 256
ATTN_HEADS = 2
ATTN_BUFFERS = 4

_E1, _E2, _R1, _R2, _G1, _G2 = range(6)

_F32 = jnp.float32
_BF16 = jnp.bfloat16
_NT = (((1,), (1,)), ((), ()))


def _dot(a, b):
    return jnp.dot(a, b, preferred_element_type=_F32)


def _layer_norm(z, g, b):
    mu = jnp.mean(z, axis=-1, keepdims=True)
    d = z - mu
    var = jnp.mean(d * d, axis=-1, keepdims=True)
    return d * lax.rsqrt(var + LN_EPS) * g + b


def _route_tail(x1, wt_ref, bt_ref, route_ref, routet_ref, counts_ref, umat_ref, *, epg):
    tm = x1.shape[0]
    nr = counts_ref.shape[0]
    half = wt_ref.shape[0] // 2

    @pl.when(pl.program_id(0) == 0)
    def _():
        counts_ref[...] = jnp.zeros_like(counts_ref)
        before = (lax.broadcasted_iota(jnp.int32, (tm, tm), 0) < lax.broadcasted_iota(jnp.int32, (tm, tm), 1))
        umat_ref[...] = jnp.where(before, 1.0, 0.0).astype(_BF16)

    xh = x1.astype(_BF16)
    xl = (x1 - xh.astype(_F32)).astype(_BF16)
    a = lax.dot_general(wt_ref[...], xh, _NT, preferred_element_type=_F32)
    b = lax.dot_general(wt_ref[0:half, :], xl, _NT, preferred_element_type=_F32)
    logits = a[0:nr, :] + a[half:half + nr, :] + b[0:nr, :] + bt_ref[0:nr, :]

    rowf = lax.broadcasted_iota(jnp.int32, (nr, tm), 0).astype(_F32)
    neg = -jnp.inf
    big = float(nr)
    gl = jnp.where(rowf < float(N_GROUPS), logits, neg)
    gmax = jnp.max(gl, axis=0, keepdims=True)
    gidx = jnp.min(jnp.where(gl == gmax, rowf, big), axis=0, keepdims=True)
    gtop = 1.0 / jnp.sum(jnp.exp(gl - gmax), axis=0, keepdims=True)

    lo = float(N_GROUPS) + gidx * float(epg)
    el = jnp.where((rowf >= lo) & (rowf < lo + float(epg)), logits, neg)
    m1 = jnp.max(el, axis=0, keepdims=True)
    i1 = jnp.min(jnp.where(el == m1, rowf, big), axis=0, keepdims=True)
    el2 = jnp.where(rowf == i1, neg, el)
    m2 = jnp.max(el2, axis=0, keepdims=True)
    i2 = jnp.min(jnp.where(el2 == m2, rowf, big), axis=0, keepdims=True)
    w2 = jnp.exp(m2 - m1)
    inv = 1.0 / (1.0 + w2)
    g1 = gtop * inv
    g2 = gtop * w2 * inv

    onehot = jnp.where((rowf == i1) | (rowf == i2), 1.0, 0.0)
    total = _dot(onehot.astype(_BF16), umat_ref[...]) + counts_ref[...]
    r1 = jnp.sum(jnp.where(rowf == i1, total, 0.0), axis=0, keepdims=True)
    r2 = jnp.sum(jnp.where(rowf == i2, total, 0.0), axis=0, keepdims=True)
    counts_ref[...] += jnp.sum(onehot, axis=1, keepdims=True)

    slot = lax.broadcasted_iota(jnp.int32, (SUBLANES, tm), 0)
    rec = jnp.zeros((SUBLANES, tm), _F32)
    fields = ((_E1, i1 - float(N_GROUPS)), (_E2, i2 - float(N_GROUPS)), (_R1, r1), (_R2, r2), (_G1, g1), (_G2, g2))
    for s, val in fields:
        rec = jnp.where(slot == s, val, rec)
    routet_ref[...] = rec
    rec = jnp.concatenate([rec, jnp.zeros((LANES - SUBLANES, tm), _F32)], axis=0)
    route_ref[...] = rec.T


def _store_token_words(ref, x):
    n, d = x.shape
    rows = d // (2 * LANES)
    bits = lax.bitcast_convert_type(x.astype(_BF16).astype(_F32), jnp.uint32)
    for s in range(rows):
        lo = bits[:, s * LANES:(s + 1) * LANES] >> 16
        hi = bits[:, d // 2 + s * LANES:d // 2 + (s + 1) * LANES] & jnp.uint32(0xFFFF0000)
        ref[pl.ds(s, n, stride=rows), :] = lo | hi


def _load_token_words(ref, n, d):
    rows = d // (2 * LANES)
    words = [ref[pl.ds(s, n, stride=rows), :] for s in range(rows)]
    lo = [lax.bitcast_convert_type(w << 16, _F32) for w in words]
    hi = [lax.bitcast_convert_type(w & jnp.uint32(0xFFFF0000), _F32) for w in words]
    return jnp.concatenate(lo + hi, axis=1)


def _conv_mixer_kernel(x_ref, win_ref, cw_ref, wout_ref, g_ref, b_ref, wt_ref, bt_ref,
                       x1_ref, x1t_ref, route_ref, routet_ref, counts_ref, ubuf_ref, umat_ref,
                       *, tiles_per_seq, alpha, epg):
    i = pl.program_id(0)
    tm, d = x_ref.shape

    @pl.when(i % tiles_per_seq == 0)
    def _():
        ubuf_ref[0:SUBLANES, :] = jnp.zeros((SUBLANES, d), _F32)

    x = x_ref[...]
    h = _dot(x.astype(_BF16), win_ref[...])
    u = h[:, d:2 * d] * h[:, 2 * d:]
    ubuf_ref[SUBLANES:SUBLANES + tm, :] = u
    cw = cw_ref[...]
    uc = (cw[0:1, :] * ubuf_ref[SUBLANES - 2:SUBLANES - 2 + tm, :]
          + cw[1:2, :] * ubuf_ref[SUBLANES - 1:SUBLANES - 1 + tm, :]
          + cw[2:3, :] * u)
    ubuf_ref[0:SUBLANES, :] = ubuf_ref[tm:tm + SUBLANES, :]
    y = _dot((h[:, :d] * uc).astype(_BF16), wout_ref[...])
    x1 = _layer_norm(alpha * x + y, g_ref[...], b_ref[...])
    x1_ref[...] = x1
    _store_token_words(x1t_ref, x1)
    _route_tail(x1, wt_ref, bt_ref, route_ref, routet_ref, counts_ref, umat_ref, epg=epg)


def _attn_out_kernel(x_ref, o_ref, wo_ref, g_ref, b_ref, wt_ref, bt_ref,
                     x1_ref, x1t_ref, route_ref, routet_ref, counts_ref, umat_ref, *, alpha, epg):
    y = _dot(o_ref[...], wo_ref[...])
    x1 = _layer_norm(alpha * x_ref[...] + y, g_ref[...], b_ref[...])
    x1_ref[...] = x1
    _store_token_words(x1t_ref, x1)
    _route_tail(x1, wt_ref, bt_ref, route_ref, routet_ref, counts_ref, umat_ref, epg=epg)


def _expert_ffn_kernel(bexp_ref, seg_ref, x_ref, rec_ref, win_hbm, wdn_hbm, y_ref,
                       wstage_in, wstage_dn, wsem, winb, wdnb, *, layer):
    i = pl.program_id(0)
    n_exp = (seg_ref.shape[0] - 1) // 2
    n_used = seg_ref[2 * n_exp]
    d, de2 = wstage_in.shape
    blk = rec_ref.shape[0]
    de = de2 // 2

    def weight_copies(e):
        return (pltpu.make_async_copy(win_hbm.at[layer, e], wstage_in, wsem.at[0]),
                pltpu.make_async_copy(wdn_hbm.at[layer, e], wstage_dn, wsem.at[1]))

    @pl.when(i == 0)
    def _():
        for c in weight_copies(bexp_ref[0]):
            c.start()

    @pl.when(i < n_used)
    def _():
        e = bexp_ref[i]

        @pl.when((i == 0) | (e != bexp_ref[jnp.maximum(i - 1, 0)]))
        def _():
            for c in weight_copies(e):
                c.wait()
            winb[...] = wstage_in[...].astype(_BF16)
            wdnb[...] = wstage_dn[...].astype(_BF16)
            nxt = seg_ref[n_exp + e] // blk

            @pl.when(nxt < n_used)
            def _():
                for c in weight_copies(bexp_ref[nxt]):
                    c.start()

        valid = i * blk + lax.broadcasted_iota(jnp.int32, (blk, 1), 0) < seg_ref[e]
        x = jnp.where(valid, _load_token_words(x_ref, blk, d), 0.0)
        h = _dot(x.astype(_BF16), winb[...])
        g = h[:, :de]
        a = g * jax.nn.sigmoid(g) * h[:, de:]
        rec = rec_ref[...]
        gate = jnp.where(rec[:, _E1:_E1 + 1] == e.astype(_F32), rec[:, _G1:_G1 + 1], rec[:, _G2:_G2 + 1])
        gate = jnp.where(valid, gate, 0.0)
        _store_token_words(y_ref, _dot(a.astype(_BF16), wdnb[...]) * gate)


def _combine_body(x1_ref, ya_ref, yb_ref, g_ref, b_ref, *, alpha):
    tm, d = x1_ref.shape
    ffn = _load_token_words(ya_ref, tm, d) + _load_token_words(yb_ref, tm, d)
    return _layer_norm(alpha * x1_ref[...] + ffn, g_ref[...], b_ref[...])


def _combine_kernel(x1_ref, ya_ref, yb_ref, g_ref, b_ref, x2_ref, *, alpha):
    x2_ref[...] = _combine_body(x1_ref, ya_ref, yb_ref, g_ref, b_ref, alpha=alpha)


def _combine_qkv_kernel(x1_ref, ya_ref, yb_ref, g_ref, b_ref, wk_ref, wv_ref, wq_ref,
                        x2_ref, k_ref, v_ref, q_ref, *, alpha, q_scale):
    x2 = _combine_body(x1_ref, ya_ref, yb_ref, g_ref, b_ref, alpha=alpha)
    x2_ref[...] = x2
    xb = x2.astype(_BF16)
    k_ref[...] = _dot(xb, wk_ref[...]).astype(_BF16)
    q_ref[...] = (_dot(xb, wq_ref[...]) * q_scale).astype(_BF16)
    v_ref[...] = _dot(xb, wv_ref[...]).astype(_BF16)


def _diff_attn_kernel(lam_ref, g_ref, q_ref, k_ref, v_ref, o_ref, lamfull_ref, s_ref, p_ref, *, lam_init):
    bi, hi = pl.program_id(0), pl.program_id(1)
    seq = k_ref.shape[0]
    hw = g_ref.shape[1]
    heads = k_ref.shape[1] // hw
    tq = tk = ATTN_BLOCK
    nk = seq // tk
    hd = lam_ref.shape[1]

    @pl.when((bi == 0) & (hi == 0))
    def _():
        lam = lam_ref[...]
        a = jnp.sum(lam[0:1, :] * lam[1:2, :], axis=-1, keepdims=True)
        b = jnp.sum(lam[2:3, :] * lam[3:4, :], axis=-1, keepdims=True)
        lamfull_ref[...] = jnp.broadcast_to(jnp.exp(a) - jnp.exp(b) + lam_init, lamfull_ref.shape)

    lam_full = lamfull_ref[0:1, 0:1]
    lane = lax.broadcasted_iota(jnp.int32, (tq, hw), 1)
    row = lax.broadcasted_iota(jnp.int32, (2 * tq, tk), 0)
    col = lax.broadcasted_iota(jnp.int32, (2 * tq, tk), 1)
    causal = col <= jnp.where(row < tq, row, row - tq)
    ones = jnp.ones((seq, hw), _BF16)

    step = 0
    for i in reversed(range(nk)):
        for h in range(heads):
            cols = slice(h * hw, (h + 1) * hw)
            par = step % s_ref.shape[0]
            step += 1
            kv = (i + 1) * tk
            q = q_ref[i * tq:(i + 1) * tq, cols]
            zero = jnp.zeros_like(q)
            qcat = jnp.concatenate([jnp.where(lane < hd, q, zero), jnp.where(lane >= hd, q, zero)], axis=0)
            s_ref[par, :, 0:kv] = lax.dot_general(qcat, k_ref[0:kv, cols], _NT, preferred_element_type=_F32)
            s_ref[par, :, i * tk:kv] = jnp.where(causal, s_ref[par, :, i * tk:kv], -jnp.inf)
            rowmax = jnp.max(s_ref[par, :, 0:kv], axis=1, keepdims=True)
            p_ref[par, :, 0:kv] = jnp.exp2(s_ref[par, :, 0:kv] - rowmax).astype(_BF16)
            v_aug = jnp.concatenate([v_ref[0:kv, cols], ones[0:kv, :]], axis=1)
            acc = _dot(p_ref[par, :, 0:kv], v_aug)
            o = acc[:, 0:hw] * (1.0 / acc[:, hw:hw + 1])
            od = o[0:tq, :] - lam_full * o[tq:, :]
            od = od * lax.rsqrt(jnp.mean(od * od, axis=1, keepdims=True) + RMS_EPS)
            o_ref[i * tq:(i + 1) * tq, cols] = (od * g_ref[...] * (1.0 - lam_init)).astype(o_ref.dtype)


def _params(*sem):
    return pltpu.CompilerParams(dimension_semantics=sem, vmem_limit_bytes=VMEM_LIMIT_BYTES)


def _full(shape):
    return pl.BlockSpec(shape, lambda *_: (0,) * len(shape))


def _router_weights(rg_w, rg_b, re_w, re_b):
    d = rg_w.shape[0]
    n = rg_w.shape[1] + re_w.shape[1]
    w = jnp.concatenate([rg_w, re_w, jnp.zeros((d, LANES - n), _F32)], axis=1).T
    b = jnp.concatenate([rg_b, re_b, jnp.zeros((LANES - n,), _F32)])[:, None]
    wh = w.astype(_BF16)
    wl = (w - wh.astype(_F32)).astype(_BF16)
    return jnp.concatenate([wh, wl], axis=0), b


def _router_rows(n_experts):
    return -(-(N_GROUPS + n_experts) // (2 * SUBLANES)) * (2 * SUBLANES)


def _route_outs(t, d, tm, nr):
    per = d // LANES
    shapes = (jax.ShapeDtypeStruct((t, d), _F32),
              jax.ShapeDtypeStruct((t * per // 2, LANES), jnp.uint32),
              jax.ShapeDtypeStruct((t, LANES), _F32),
              jax.ShapeDtypeStruct((SUBLANES, t), _F32),
              jax.ShapeDtypeStruct((nr, 1), _F32))
    specs = (pl.BlockSpec((tm, d), lambda i: (i, 0)),
             pl.BlockSpec((tm * per // 2, LANES), lambda i: (i, 0)),
             pl.BlockSpec((tm, LANES), lambda i: (i, 0)),
             pl.BlockSpec((SUBLANES, tm), lambda i: (0, i)),
             pl.BlockSpec((nr, 1), lambda i: (0, 0)))
    return shapes, specs


def _conv_mixer(x, w_in, conv_w, w_out, ln_g, ln_b, router, *, seq, alpha, epg):
    t, d = x.shape
    tm = MIXER_TILE
    wt, bt = router
    shapes, specs = _route_outs(t, d, tm, _router_rows(N_GROUPS * epg))
    return pl.pallas_call(
        functools.partial(_conv_mixer_kernel, tiles_per_seq=seq // tm, alpha=alpha, epg=epg),
        grid=(t // tm,),
        in_specs=[pl.BlockSpec((tm, d), lambda i: (i, 0)),
                  _full(w_in.shape), _full(conv_w.shape), _full(w_out.shape),
                  _full((1, d)), _full((1, d)), _full(wt.shape), _full(bt.shape)],
        out_specs=specs, out_shape=shapes,
        scratch_shapes=[pltpu.VMEM((tm + SUBLANES, d), _F32), pltpu.VMEM((tm, tm), _BF16)],
        compiler_params=_params("arbitrary"),
        name="conv_mixer_ln_router",
    )(x, w_in, conv_w, w_out, ln_g[None, :], ln_b[None, :], wt, bt)


def _attn_out(x, o, w_o, ln_g, ln_b, router, *, alpha, epg):
    t, d = x.shape
    tm = MIXER_TILE
    wt, bt = router
    shapes, specs = _route_outs(t, d, tm, _router_rows(N_GROUPS * epg))
    return pl.pallas_call(
        functools.partial(_attn_out_kernel, alpha=alpha, epg=epg),
        grid=(t // tm,),
        in_specs=[pl.BlockSpec((tm, d), lambda i: (i, 0)),
                  pl.BlockSpec((tm, d), lambda i: (i, 0)),
                  _full(w_o.shape), _full((1, d)), _full((1, d)), _full(wt.shape), _full(bt.shape)],
        out_specs=specs, out_shape=shapes,
        scratch_shapes=[pltpu.VMEM((tm, tm), _BF16)],
        compiler_params=_params("arbitrary"),
        name="attn_out_ln_router",
    )(x, o, w_o, ln_g[None, :], ln_b[None, :], wt, bt)


def _dispatch_plan(route_t, counts, n_experts):
    t = route_t.shape[1]
    blk = EXPERT_BLOCK
    cnt = counts[N_GROUPS:N_GROUPS + n_experts, 0].astype(jnp.int32)
    padded = (cnt + blk - 1) // blk * blk
    pad_end = jnp.cumsum(padded)
    pad_start = pad_end - padded
    ids = jnp.arange(n_experts, dtype=jnp.int32)

    def sorted_row(e_slot, r_slot):
        e = route_t[e_slot].astype(jnp.int32)
        start = jnp.sum(jnp.where(ids[:, None] == e[None, :], pad_start[:, None], 0), axis=0)
        return start + route_t[r_slot].astype(jnp.int32)

    dest = jnp.concatenate([sorted_row(_E1, _R1), sorted_row(_E2, _R2)])
    n_blocks = (2 * t + n_experts * blk) // blk
    block_start = jnp.arange(n_blocks, dtype=jnp.int32) * blk
    block_expert = jnp.minimum(jnp.sum(pad_end[None, :] <= block_start[:, None], axis=1),
                               n_experts - 1).astype(jnp.int32)
    n_used = (pad_end[-1:] // blk).astype(jnp.int32)
    seg = jnp.concatenate([pad_start + cnt, pad_end, n_used]).astype(jnp.int32)
    return dest, block_expert, seg


def _sc_mesh():
    return plsc.VectorSubcoreMesh(core_axis_name="c", subcore_axis_name="s",
                                  num_cores=SC_CORES, num_subcores=SC_SUBCORES)


def _sc_worker_chunks(t, chunk):
    chunks = t // (SC_CORES * SC_SUBCORES * chunk)
    wid = lax.axis_index("c") * SC_SUBCORES + lax.axis_index("s")
    return wid * chunks, chunks


def _sc_dispatch(x1t, route, dest, n_rows):
    t, per, lanes = x1t.shape
    c = SC_CHUNK

    def body(x_hbm, r_hbm, d_hbm, xo_hbm, ro_hbm, xbuf, rbuf, idx1, idx2, lsem, ssem):
        first, chunks = _sc_worker_chunks(t, c)
        pltpu.sync_copy(d_hbm.at[pl.ds(first * c, chunks * c)], idx1)
        pltpu.sync_copy(d_hbm.at[pl.ds(t + first * c, chunks * c)], idx2)

        def loads(j, slot):
            rows = pl.ds((first + j) * c, c)
            return (pltpu.make_async_copy(x_hbm.at[rows], xbuf.at[slot], lsem.at[2 * slot]),
                    pltpu.make_async_copy(r_hbm.at[rows], rbuf.at[slot], lsem.at[2 * slot + 1]))

        for cp in loads(0, 0):
            cp.start()

        @pl.loop(0, chunks)
        def _(j):
            slot = j % 2

            @pl.when(j + 1 < chunks)
            def _():
                for cp in loads(j + 1, 1 - slot):
                    cp.start()

            for cp in loads(j, slot):
                cp.wait()
            i1 = idx1.at[pl.ds(j * c, c)]
            i2 = idx2.at[pl.ds(j * c, c)]
            scatters = (pltpu.make_async_copy(xbuf.at[slot], xo_hbm.at[i1], ssem.at[0]),
                        pltpu.make_async_copy(xbuf.at[slot], xo_hbm.at[i2], ssem.at[1]),
                        pltpu.make_async_copy(rbuf.at[slot], ro_hbm.at[i1], ssem.at[2]),
                        pltpu.make_async_copy(rbuf.at[slot], ro_hbm.at[i2], ssem.at[3]))
            for cp in scatters:
                cp.start()
            for cp in scatters:
                cp.wait()

    per_worker = t // (SC_CORES * SC_SUBCORES)
    return pl.kernel(
        body,
        out_type=(jax.ShapeDtypeStruct((n_rows, per, lanes), x1t.dtype),
                  jax.ShapeDtypeStruct((n_rows, lanes), route.dtype)),
        mesh=_sc_mesh(),
        scratch_types=[pltpu.VMEM((2, c, per, lanes), x1t.dtype),
                       pltpu.VMEM((2, c, lanes), route.dtype),
                       pltpu.VMEM((per_worker,), jnp.int32),
                       pltpu.VMEM((per_worker,), jnp.int32),
                       pltpu.SemaphoreType.DMA((4,)),
                       pltpu.SemaphoreType.DMA((4,))],
        name="sc_dispatch",
    )(x1t, route, dest)


def _sc_combine(y_rows, dest, t):
    _, per, lanes = y_rows.shape
    c = SC_CHUNK

    def body(y_hbm, d_hbm, oa_hbm, ob_hbm, buf, idx1, idx2, gsem, osem):
        first, chunks = _sc_worker_chunks(t, c)
        pltpu.sync_copy(d_hbm.at[pl.ds(first * c, chunks * c)], idx1)
        pltpu.sync_copy(d_hbm.at[pl.ds(t + first * c, chunks * c)], idx2)

        def stores(j, slot):
            rows = pl.ds((first + j) * c, c)
            return (pltpu.make_async_copy(buf.at[slot, 0], oa_hbm.at[rows], osem.at[2 * slot]),
                    pltpu.make_async_copy(buf.at[slot, 1], ob_hbm.at[rows], osem.at[2 * slot + 1]))

        @pl.loop(0, chunks)
        def _(j):
            slot = j % 2

            @pl.when(j >= 2)
            def _():
                for cp in stores(j - 2, slot):
                    cp.wait()

            gathers = (pltpu.make_async_copy(y_hbm.at[idx1.at[pl.ds(j * c, c)]], buf.at[slot, 0], gsem.at[0]),
                       pltpu.make_async_copy(y_hbm.at[idx2.at[pl.ds(j * c, c)]], buf.at[slot, 1], gsem.at[1]))
            for cp in gathers:
                cp.start()
            for cp in gathers:
                cp.wait()
            for cp in stores(j, slot):
                cp.start()

        for j in range(max(chunks - 2, 0), chunks):
            for cp in stores(j, j % 2):
                cp.wait()

    per_worker = t // (SC_CORES * SC_SUBCORES)
    out = jax.ShapeDtypeStruct((t, per, lanes), y_rows.dtype)
    return pl.kernel(
        body,
        out_type=(out, out),
        mesh=_sc_mesh(),
        scratch_types=[pltpu.VMEM((2, 2, c, per, lanes), y_rows.dtype),
                       pltpu.VMEM((per_worker,), jnp.int32),
                       pltpu.VMEM((per_worker,), jnp.int32),
                       pltpu.SemaphoreType.DMA((2,)),
                       pltpu.SemaphoreType.DMA((4,))],
        name="sc_combine",
    )(y_rows, dest)


def _expert_ffn(x_sorted, r_sorted, w_in, w_down, layer, block_expert, seg):
    n_rows, per, _ = x_sorted.shape
    _, n_exp, d, de2 = w_in.shape
    de = w_down.shape[2]
    blk = EXPERT_BLOCK
    n_blocks = block_expert.shape[0]

    def used(i, be, sg):
        return jnp.minimum(i, sg[2 * n_exp] - 1), 0

    y = pl.pallas_call(
        functools.partial(_expert_ffn_kernel, layer=layer),
        grid_spec=pltpu.PrefetchScalarGridSpec(
            num_scalar_prefetch=2,
            grid=(n_blocks,),
            in_specs=[pl.BlockSpec((blk * per, LANES), used),
                      pl.BlockSpec((blk, LANES), used),
                      pl.BlockSpec(memory_space=pl.ANY),
                      pl.BlockSpec(memory_space=pl.ANY)],
            out_specs=pl.BlockSpec((blk * per, LANES), used),
            scratch_shapes=[pltpu.VMEM((d, de2), _F32),
                            pltpu.VMEM((de, d), _F32),
                            pltpu.SemaphoreType.DMA((2,)),
                            pltpu.VMEM((d, de2), _BF16),
                            pltpu.VMEM((de, d), _BF16)]),
        out_shape=jax.ShapeDtypeStruct((n_rows * per, LANES), jnp.uint32),
        compiler_params=_params("arbitrary"),
        name="expert_ffn",
    )(block_expert, seg, x_sorted.reshape(n_rows * per, LANES), r_sorted, w_in, w_down)
    return y.reshape(n_rows, per, LANES)


def _combine(x1, ya, yb, ln_g, ln_b, *, alpha):
    t, d = x1.shape
    tm = ROW_TILE
    per = ya.shape[1]
    row = pl.BlockSpec((tm, d), lambda i: (i, 0))
    words = pl.BlockSpec((tm * per, LANES), lambda i: (i, 0))
    return pl.pallas_call(
        functools.partial(_combine_kernel, alpha=alpha),
        grid=(t // tm,),
        in_specs=[row, words, words, _full((1, d)), _full((1, d))],
        out_specs=row,
        out_shape=jax.ShapeDtypeStruct((t, d), _F32),
        compiler_params=_params("arbitrary"),
        name="combine_ln",
    )(x1, ya.reshape(t * per, LANES), yb.reshape(t * per, LANES), ln_g[None, :], ln_b[None, :])


def _combine_qkv(x1, ya, yb, ln_g, ln_b, w_k, w_v, w_q, *, alpha, q_scale):
    t, d = x1.shape
    tm = ROW_TILE
    per = ya.shape[1]
    row = pl.BlockSpec((tm, d), lambda i: (i, 0))
    words = pl.BlockSpec((tm * per, LANES), lambda i: (i, 0))
    return pl.pallas_call(
        functools.partial(_combine_qkv_kernel, alpha=alpha, q_scale=q_scale),
        grid=(t // tm,),
        in_specs=[row, words, words, _full((1, d)), _full((1, d)),
                  _full((d, d)), _full((d, d)), _full((d, d))],
        out_specs=(row, row, row, row),
        out_shape=(jax.ShapeDtypeStruct((t, d), _F32),
                   jax.ShapeDtypeStruct((t, d), _BF16),
                   jax.ShapeDtypeStruct((t, d), _BF16),
                   jax.ShapeDtypeStruct((t, d), _BF16)),
        compiler_params=_params("arbitrary"),
        name="combine_ln_qkv",
    )(x1, ya.reshape(t * per, LANES), yb.reshape(t * per, LANES), ln_g[None, :], ln_b[None, :],
      w_k, w_v, w_q)


def _diff_attention(q, k, v, lam, subln_g, *, batch, seq, lam_init):
    t, d = q.shape
    tq = ATTN_BLOCK
    hw = d // N_HEADS
    head = pl.BlockSpec((seq, ATTN_HEADS * hw), lambda b, h: (b, h))
    return pl.pallas_call(
        functools.partial(_diff_attn_kernel, lam_init=lam_init),
        grid=(batch, N_HEADS // ATTN_HEADS),
        in_specs=[_full(lam.shape), _full((1, hw)), head, head, head],
        out_specs=head,
        out_shape=jax.ShapeDtypeStruct((t, d), _BF16),
        scratch_shapes=[pltpu.VMEM((1, LANES), _F32),
                        pltpu.VMEM((ATTN_BUFFERS, 2 * tq, seq), _F32),
                        pltpu.VMEM((ATTN_BUFFERS, 2 * tq, seq), _BF16)],
        compiler_params=_params("arbitrary", "arbitrary"),
        name="diff_attention",
    )(lam, subln_g[None, :], q, k, v)


def _moe(x1t, route, route_t, counts, w_in, w_down, layer):
    n_exp, d = w_in.shape[1:3]
    dest, block_expert, seg = _dispatch_plan(route_t, counts, n_exp)
    x1t = x1t.reshape(-1, d // (2 * LANES), LANES)
    x_sorted, r_sorted = _sc_dispatch(x1t, route, dest, block_expert.shape[0] * EXPERT_BLOCK)
    y_rows = _expert_ffn(x_sorted, r_sorted, w_in, w_down, layer, block_expert, seg)
    return _sc_combine(y_rows, dest, x1t.shape[0])


def kernel(x, a_w_in, a_conv_w, a_w_out, kv_w, b_w_q, b_lambda, b_subln_g, b_w_o,
           ln1_g, ln1_b, ln2_g, ln2_b, rg_w, rg_b, re_w, re_b, e_w_in, e_w_down):
    batch, seq, d = x.shape
    depth = ln1_g.shape[0]
    assert depth == 2 and a_w_in.shape[0] == 1 and b_w_q.shape[0] == 1
    assert seq % MIXER_TILE == 0 and seq % ROW_TILE == 0 and seq % ATTN_BLOCK == 0
    t = batch * seq
    assert t % (SC_CORES * SC_SUBCORES * SC_CHUNK) == 0
    alpha = (2.0 * depth) ** 0.25
    epg = re_w.shape[2] // N_GROUPS
    head_dim = d // (2 * N_HEADS)
    xf = x.reshape(t, d)

    router0 = _router_weights(rg_w[0], rg_b[0], re_w[0], re_b[0])
    x1, x1t, route, route_t, counts = _conv_mixer(xf, a_w_in[0].astype(_BF16), a_conv_w[0],
                                                  a_w_out[0].astype(_BF16), ln1_g[0], ln1_b[0], router0,
                                                  seq=seq, alpha=alpha, epg=epg)
    ya, yb = _moe(x1t, route, route_t, counts, e_w_in, e_w_down, 0)
    q_scale = head_dim ** -0.5 * math.log2(math.e)
    x2, k, v, q = _combine_qkv(x1, ya, yb, ln2_g[0], ln2_b[0],
                               kv_w[:, :d].astype(_BF16), kv_w[:, d:].astype(_BF16),
                               b_w_q[0].astype(_BF16), alpha=alpha, q_scale=q_scale)

    lam_init = 0.8 - 0.6 * math.exp(-0.3 * 1)
    o = _diff_attention(q, k, v, b_lambda[0], b_subln_g[0], batch=batch, seq=seq, lam_init=lam_init)
    router1 = _router_weights(rg_w[1], rg_b[1], re_w[1], re_b[1])
    x1, x1t, route, route_t, counts = _attn_out(x2, o, b_w_o[0].astype(_BF16), ln1_g[1], ln1_b[1],
                                                router1, alpha=alpha, epg=epg)
    ya, yb = _moe(x1t, route, route_t, counts, e_w_in, e_w_down, 1)
    out = _combine(x1, ya, yb, ln2_g[1], ln2_b[1], alpha=alpha)
    return out.reshape(batch, seq, d)
```

```python
import functools
import math

import jax
import jax.numpy as jnp
from jax import lax
from jax.experimental import pallas as pl
from jax.experimental.pallas import tpu as pltpu
from jax.experimental.pallas import tpu_sc as plsc

N_HEADS = 8
N_GROUPS = 4
LN_EPS = 1e-5
RMS_EPS = 1e-5

LANES = 128
SUBLANES = 8
VMEM_LIMIT_BYTES = 56 * 1024 * 1024

SC_CORES = 2
SC_SUBCORES = 16

SC_CHUNK = 32
MIXER_TILE = 1024
ROW_TILE = 1024
EXPERT_BLOCK = 512
ATTN_BLOCK = 256
ATTN_HEADS = 2
ATTN_BUFFERS = 4

_E1, _E2, _R1, _R2, _G1, _G2 = range(6)

_F32 = jnp.float32
_BF16 = jnp.bfloat16
_NT = (((1,), (1,)), ((), ()))


def _dot(a, b):
    return jnp.dot(a, b, preferred_element_type=_F32)


def _layer_norm(z, g, b):
    mu = jnp.mean(z, axis=-1, keepdims=True)
    d = z - mu
    var = jnp.mean(d * d, axis=-1, keepdims=True)
    return d * lax.rsqrt(var + LN_EPS) * g + b


def _route_tail(x1, wt_ref, bt_ref, route_ref, routet_ref, counts_ref, umat_ref, *, epg):
    tm = x1.shape[0]
    nr = counts_ref.shape[0]
    half = wt_ref.shape[0] // 2

    @pl.when(pl.program_id(0) == 0)
    def _():
        counts_ref[...] = jnp.zeros_like(counts_ref)
        before = (lax.broadcasted_iota(jnp.int32, (tm, tm), 0) < lax.broadcasted_iota(jnp.int32, (tm, tm), 1))
        umat_ref[...] = jnp.where(before, 1.0, 0.0).astype(_BF16)

    xh = x1.astype(_BF16)
    xl = (x1 - xh.astype(_F32)).astype(_BF16)
    a = lax.dot_general(wt_ref[...], xh, _NT, preferred_element_type=_F32)
    b = lax.dot_general(wt_ref[0:half, :], xl, _NT, preferred_element_type=_F32)
    logits = a[0:nr, :] + a[half:half + nr, :] + b[0:nr, :] + bt_ref[0:nr, :]

    rowf = lax.broadcasted_iota(jnp.int32, (nr, tm), 0).astype(_F32)
    neg = -jnp.inf
    big = float(nr)
    gl = jnp.where(rowf < float(N_GROUPS), logits, neg)
    gmax = jnp.max(gl, axis=0, keepdims=True)
    gidx = jnp.min(jnp.where(gl == gmax, rowf, big), axis=0, keepdims=True)
    gtop = 1.0 / jnp.sum(jnp.exp(gl - gmax), axis=0, keepdims=True)

    lo = float(N_GROUPS) + gidx * float(epg)
    el = jnp.where((rowf >= lo) & (rowf < lo + float(epg)), logits, neg)
    m1 = jnp.max(el, axis=0, keepdims=True)
    i1 = jnp.min(jnp.where(el == m1, rowf, big), axis=0, keepdims=True)
    el2 = jnp.where(rowf == i1, neg, el)
    m2 = jnp.max(el2, axis=0, keepdims=True)
    i2 = jnp.min(jnp.where(el2 == m2, rowf, big), axis=0, keepdims=True)
    w2 = jnp.exp(m2 - m1)
    inv = 1.0 / (1.0 + w2)
    g1 = gtop * inv
    g2 = gtop * w2 * inv

    onehot = jnp.where((rowf == i1) | (rowf == i2), 1.0, 0.0)
    total = _dot(onehot.astype(_BF16), umat_ref[...]) + counts_ref[...]
    r1 = jnp.sum(jnp.where(rowf == i1, total, 0.0), axis=0, keepdims=True)
    r2 = jnp.sum(jnp.where(rowf == i2, total, 0.0), axis=0, keepdims=True)
    counts_ref[...] += jnp.sum(onehot, axis=1, keepdims=True)

    slot = lax.broadcasted_iota(jnp.int32, (SUBLANES, tm), 0)
    rec = jnp.zeros((SUBLANES, tm), _F32)
    fields = ((_E1, i1 - float(N_GROUPS)), (_E2, i2 - float(N_GROUPS)), (_R1, r1), (_R2, r2), (_G1, g1), (_G2, g2))
    for s, val in fields:
        rec = jnp.where(slot == s, val, rec)
    routet_ref[...] = rec
    rec = jnp.concatenate([rec, jnp.zeros((LANES - SUBLANES, tm), _F32)], axis=0)
    route_ref[...] = rec.T


def _store_token_words(ref, x):
    n, d = x.shape
    rows = d // (2 * LANES)
    bits = lax.bitcast_convert_type(x.astype(_BF16).astype(_F32), jnp.uint32)
    for s in range(rows):
        lo = bits[:, s * LANES:(s + 1) * LANES] >> 16
        hi = bits[:, d // 2 + s * LANES:d // 2 + (s + 1) * LANES] & jnp.uint32(0xFFFF0000)
        ref[pl.ds(s, n, stride=rows), :] = lo | hi


def _load_token_words(ref, n, d):
    rows = d // (2 * LANES)
    words = [ref[pl.ds(s, n, stride=rows), :] for s in range(rows)]
    lo = [lax.bitcast_convert_type(w << 16, _F32) for w in words]
    hi = [lax.bitcast_convert_type(w & jnp.uint32(0xFFFF0000), _F32) for w in words]
    return jnp.concatenate(lo + hi, axis=1)


def _conv_mixer_kernel(x_ref, win_ref, cw_ref, wout_ref, g_ref, b_ref, wt_ref, bt_ref,
                       x1_ref, x1t_ref, route_ref, routet_ref, counts_ref, ubuf_ref, umat_ref,
                       *, tiles_per_seq, alpha, epg):
    i = pl.program_id(0)
    tm, d = x_ref.shape

    @pl.when(i % tiles_per_seq == 0)
    def _():
        ubuf_ref[0:SUBLANES, :] = jnp.zeros((SUBLANES, d), _F32)

    x = x_ref[...]
    h = _dot(x.astype(_BF16), win_ref[...])
    u = h[:, d:2 * d] * h[:, 2 * d:]
    ubuf_ref[SUBLANES:SUBLANES + tm, :] = u
    cw = cw_ref[...]
    uc = (cw[0:1, :] * ubuf_ref[SUBLANES - 2:SUBLANES - 2 + tm, :]
          + cw[1:2, :] * ubuf_ref[SUBLANES - 1:SUBLANES - 1 + tm, :]
          + cw[2:3, :] * u)
    ubuf_ref[0:SUBLANES, :] = ubuf_ref[tm:tm + SUBLANES, :]
    y = _dot((h[:, :d] * uc).astype(_BF16), wout_ref[...])
    x1 = _layer_norm(alpha * x + y, g_ref[...], b_ref[...])
    x1_ref[...] = x1
    _store_token_words(x1t_ref, x1)
    _route_tail(x1, wt_ref, bt_ref, route_ref, routet_ref, counts_ref, umat_ref, epg=epg)


def _attn_out_kernel(x_ref, o_ref, wo_ref, g_ref, b_ref, wt_ref, bt_ref,
                     x1_ref, x1t_ref, route_ref, routet_ref, counts_ref, umat_ref, *, alpha, epg):
    y = _dot(o_ref[...], wo_ref[...])
    x1 = _layer_norm(alpha * x_ref[...] + y, g_ref[...], b_ref[...])
    x1_ref[...] = x1
    _store_token_words(x1t_ref, x1)
    _route_tail(x1, wt_ref, bt_ref, route_ref, routet_ref, counts_ref, umat_ref, epg=epg)


def _expert_ffn_kernel(bexp_ref, seg_ref, x_ref, rec_ref, win_hbm, wdn_hbm, y_ref,
                       wstage_in, wstage_dn, wsem, winb, wdnb, *, layer):
    i = pl.program_id(0)
    n_exp = (seg_ref.shape[0] - 1) // 2
    n_used = seg_ref[2 * n_exp]
    d, de2 = wstage_in.shape
    blk = rec_ref.shape[0]
    de = de2 // 2

    def weight_copies(e):
        return (pltpu.make_async_copy(win_hbm.at[layer, e], wstage_in, wsem.at[0]),
                pltpu.make_async_copy(wdn_hbm.at[layer, e], wstage_dn, wsem.at[1]))

    @pl.when(i == 0)
    def _():
        for c in weight_copies(bexp_ref[0]):
            c.start()

    @pl.when(i < n_used)
    def _():
        e = bexp_ref[i]

        @pl.when((i == 0) | (e != bexp_ref[jnp.maximum(i - 1, 0)]))
        def _():
            for c in weight_copies(e):
                c.wait()
            winb[...] = wstage_in[...].astype(_BF16)
            wdnb[...] = wstage_dn[...].astype(_BF16)
            nxt = seg_ref[n_exp + e] // blk

            @pl.when(nxt < n_used)
            def _():
                for c in weight_copies(bexp_ref[nxt]):
                    c.start()

        n_valid = seg_ref[e] - i * blk

        def ffn(rows):
            valid = lax.broadcasted_iota(jnp.int32, (rows, 1), 0) < n_valid
            x = jnp.where(valid, _load_token_words(x_ref, rows, d), 0.0)
            h = _dot(x.astype(_BF16), winb[...])
            g = h[:, :de]
            a = g * jax.nn.sigmoid(g) * h[:, de:]
            rec = rec_ref[0:rows, :]
            gate = jnp.where(rec[:, _E1:_E1 + 1] == e.astype(_F32), rec[:, _G1:_G1 + 1], rec[:, _G2:_G2 + 1])
            gate = jnp.where(valid, gate, 0.0)
            _store_token_words(y_ref, _dot(a.astype(_BF16), wdnb[...]) * gate)

        half = blk // 2

        @pl.when(n_valid > half)
        def _():
            ffn(blk)

        @pl.when(n_valid <= half)
        def _():
            ffn(half)
            y_ref[half * (y_ref.shape[0] // blk):, :] = jnp.zeros(
                (half * (y_ref.shape[0] // blk), y_ref.shape[1]), y_ref.dtype)


def _combine_body(x1_ref, ya_ref, yb_ref, g_ref, b_ref, *, alpha):
    tm, d = x1_ref.shape
    ffn = _load_token_words(ya_ref, tm, d) + _load_token_words(yb_ref, tm, d)
    return _layer_norm(alpha * x1_ref[...] + ffn, g_ref[...], b_ref[...])


def _combine_kernel(x1_ref, ya_ref, yb_ref, g_ref, b_ref, x2_ref, *, alpha):
    x2_ref[...] = _combine_body(x1_ref, ya_ref, yb_ref, g_ref, b_ref, alpha=alpha)


def _combine_qkv_kernel(x1_ref, ya_ref, yb_ref, g_ref, b_ref, wk_ref, wv_ref, wq_ref,
                        x2_ref, k_ref, v_ref, q_ref, *, alpha, q_scale):
    x2 = _combine_body(x1_ref, ya_ref, yb_ref, g_ref, b_ref, alpha=alpha)
    x2_ref[...] = x2
    xb = x2.astype(_BF16)
    k_ref[...] = _dot(xb, wk_ref[...]).astype(_BF16)
    q_ref[...] = (_dot(xb, wq_ref[...]) * q_scale).astype(_BF16)
    v_ref[...] = _dot(xb, wv_ref[...]).astype(_BF16)


def _diff_attn_kernel(lam_ref, g_ref, q_ref, k_ref, v_ref, o_ref, lamfull_ref, s_ref, p_ref, *, lam_init):
    bi, hi = pl.program_id(0), pl.program_id(1)
    seq = k_ref.shape[0]
    hw = g_ref.shape[1]
    heads = k_ref.shape[1] // hw
    tq = tk = ATTN_BLOCK
    nk = seq // tk
    hd = lam_ref.shape[1]

    @pl.when((bi == 0) & (hi == 0))
    def _():
        lam = lam_ref[...]
        a = jnp.sum(lam[0:1, :] * lam[1:2, :], axis=-1, keepdims=True)
        b = jnp.sum(lam[2:3, :] * lam[3:4, :], axis=-1, keepdims=True)
        lamfull_ref[...] = jnp.broadcast_to(jnp.exp(a) - jnp.exp(b) + lam_init, lamfull_ref.shape)

    lam_full = lamfull_ref[0:1, 0:1]
    lane = lax.broadcasted_iota(jnp.int32, (tq, hw), 1)
    row = lax.broadcasted_iota(jnp.int32, (2 * tq, tk), 0)
    col = lax.broadcasted_iota(jnp.int32, (2 * tq, tk), 1)
    causal = col <= jnp.where(row < tq, row, row - tq)
    ones = jnp.ones((seq, hw), _BF16)

    step = 0
    for i in reversed(range(nk)):
        for h in range(heads):
            cols = slice(h * hw, (h + 1) * hw)
            par = step % s_ref.shape[0]
            step += 1
            kv = (i + 1) * tk
            q = q_ref[i * tq:(i + 1) * tq, cols]
            zero = jnp.zeros_like(q)
            qcat = jnp.concatenate([jnp.where(lane < hd, q, zero), jnp.where(lane >= hd, q, zero)], axis=0)
            s_ref[par, :, 0:kv] = lax.dot_general(qcat, k_ref[0:kv, cols], _NT, preferred_element_type=_F32)
            s_ref[par, :, i * tk:kv] = jnp.where(causal, s_ref[par, :, i * tk:kv], -jnp.inf)
            rowmax = jnp.max(s_ref[par, :, 0:kv], axis=1, keepdims=True)
            p_ref[par, :, 0:kv] = jnp.exp2(s_ref[par, :, 0:kv] - rowmax).astype(_BF16)
            v_aug = jnp.concatenate([v_ref[0:kv, cols], ones[0:kv, :]], axis=1)
            acc = _dot(p_ref[par, :, 0:kv], v_aug)
            o = acc[:, 0:hw] * (1.0 / acc[:, hw:hw + 1])
            od = o[0:tq, :] - lam_full * o[tq:, :]
            od = od * lax.rsqrt(jnp.mean(od * od, axis=1, keepdims=True) + RMS_EPS)
            o_ref[i * tq:(i + 1) * tq, cols] = (od * g_ref[...] * (1.0 - lam_init)).astype(o_ref.dtype)


def _params(*sem):
    return pltpu.CompilerParams(dimension_semantics=sem, vmem_limit_bytes=VMEM_LIMIT_BYTES)


def _full(shape):
    return pl.BlockSpec(shape, lambda *_: (0,) * len(shape))


def _router_weights(rg_w, rg_b, re_w, re_b):
    d = rg_w.shape[0]
    n = rg_w.shape[1] + re_w.shape[1]
    w = jnp.concatenate([rg_w, re_w, jnp.zeros((d, LANES - n), _F32)], axis=1).T
    b = jnp.concatenate([rg_b, re_b, jnp.zeros((LANES - n,), _F32)])[:, None]
    wh = w.astype(_BF16)
    wl = (w - wh.astype(_F32)).astype(_BF16)
    return jnp.concatenate([wh, wl], axis=0), b


def _router_rows(n_experts):
    return -(-(N_GROUPS + n_experts) // (2 * SUBLANES)) * (2 * SUBLANES)


def _route_outs(t, d, tm, nr):
    per = d // LANES
    shapes = (jax.ShapeDtypeStruct((t, d), _F32),
              jax.ShapeDtypeStruct((t * per // 2, LANES), jnp.uint32),
              jax.ShapeDtypeStruct((t, LANES), _F32),
              jax.ShapeDtypeStruct((SUBLANES, t), _F32),
              jax.ShapeDtypeStruct((nr, 1), _F32))
    specs = (pl.BlockSpec((tm, d), lambda i: (i, 0)),
             pl.BlockSpec((tm * per // 2, LANES), lambda i: (i, 0)),
             pl.BlockSpec((tm, LANES), lambda i: (i, 0)),
             pl.BlockSpec((SUBLANES, tm), lambda i: (0, i)),
             pl.BlockSpec((nr, 1), lambda i: (0, 0)))
    return shapes, specs


def _conv_mixer(x, w_in, conv_w, w_out, ln_g, ln_b, router, *, seq, alpha, epg):
    t, d = x.shape
    tm = MIXER_TILE
    wt, bt = router
    shapes, specs = _route_outs(t, d, tm, _router_rows(N_GROUPS * epg))
    return pl.pallas_call(
        functools.partial(_conv_mixer_kernel, tiles_per_seq=seq // tm, alpha=alpha, epg=epg),
        grid=(t // tm,),
        in_specs=[pl.BlockSpec((tm, d), lambda i: (i, 0)),
                  _full(w_in.shape), _full(conv_w.shape), _full(w_out.shape),
                  _full((1, d)), _full((1, d)), _full(wt.shape), _full(bt.shape)],
        out_specs=specs, out_shape=shapes,
        scratch_shapes=[pltpu.VMEM((tm + SUBLANES, d), _F32), pltpu.VMEM((tm, tm), _BF16)],
        compiler_params=_params("arbitrary"),
        name="conv_mixer_ln_router",
    )(x, w_in, conv_w, w_out, ln_g[None, :], ln_b[None, :], wt, bt)


def _attn_out(x, o, w_o, ln_g, ln_b, router, *, alpha, epg):
    t, d = x.shape
    tm = MIXER_TILE
    wt, bt = router
    shapes, specs = _route_outs(t, d, tm, _router_rows(N_GROUPS * epg))
    return pl.pallas_call(
        functools.partial(_attn_out_kernel, alpha=alpha, epg=epg),
        grid=(t // tm,),
        in_specs=[pl.BlockSpec((tm, d), lambda i: (i, 0)),
                  pl.BlockSpec((tm, d), lambda i: (i, 0)),
                  _full(w_o.shape), _full((1, d)), _full((1, d)), _full(wt.shape), _full(bt.shape)],
        out_specs=specs, out_shape=shapes,
        scratch_shapes=[pltpu.VMEM((tm, tm), _BF16)],
        compiler_params=_params("arbitrary"),
        name="attn_out_ln_router",
    )(x, o, w_o, ln_g[None, :], ln_b[None, :], wt, bt)


def _dispatch_plan(route_t, counts, n_experts):
    t = route_t.shape[1]
    blk = EXPERT_BLOCK
    cnt = counts[N_GROUPS:N_GROUPS + n_experts, 0].astype(jnp.int32)
    padded = (cnt + blk - 1) // blk * blk
    pad_end = jnp.cumsum(padded)
    pad_start = pad_end - padded
    ids = jnp.arange(n_experts, dtype=jnp.int32)

    def sorted_row(e_slot, r_slot):
        e = route_t[e_slot].astype(jnp.int32)
        start = jnp.sum(jnp.where(ids[:, None] == e[None, :], pad_start[:, None], 0), axis=0)
        return start + route_t[r_slot].astype(jnp.int32)

    dest = jnp.concatenate([sorted_row(_E1, _R1), sorted_row(_E2, _R2)])
    n_blocks = (2 * t + n_experts * blk) // blk
    block_start = jnp.arange(n_blocks, dtype=jnp.int32) * blk
    block_expert = jnp.minimum(jnp.sum(pad_end[None, :] <= block_start[:, None], axis=1),
                               n_experts - 1).astype(jnp.int32)
    n_used = (pad_end[-1:] // blk).astype(jnp.int32)
    seg = jnp.concatenate([pad_start + cnt, pad_end, n_used]).astype(jnp.int32)
    return dest, block_expert, seg


def _sc_mesh():
    return plsc.VectorSubcoreMesh(core_axis_name="c", subcore_axis_name="s",
                                  num_cores=SC_CORES, num_subcores=SC_SUBCORES)


def _sc_worker_chunks(t, chunk):
    chunks = t // (SC_CORES * SC_SUBCORES * chunk)
    wid = lax.axis_index("c") * SC_SUBCORES + lax.axis_index("s")
    return wid * chunks, chunks


def _sc_dispatch(x1t, route, dest, n_rows):
    t, per, lanes = x1t.shape
    c = SC_CHUNK

    def body(x_hbm, r_hbm, d_hbm, xo_hbm, ro_hbm, xbuf, rbuf, idx1, idx2, lsem, ssem):
        first, chunks = _sc_worker_chunks(t, c)
        pltpu.sync_copy(d_hbm.at[pl.ds(first * c, chunks * c)], idx1)
        pltpu.sync_copy(d_hbm.at[pl.ds(t + first * c, chunks * c)], idx2)

        def loads(j, slot):
            rows = pl.ds((first + j) * c, c)
            return (pltpu.make_async_copy(x_hbm.at[rows], xbuf.at[slot], lsem.at[2 * slot]),
                    pltpu.make_async_copy(r_hbm.at[rows], rbuf.at[slot], lsem.at[2 * slot + 1]))

        for cp in loads(0, 0):
            cp.start()

        @pl.loop(0, chunks)
        def _(j):
            slot = j % 2

            @pl.when(j + 1 < chunks)
            def _():
                for cp in loads(j + 1, 1 - slot):
                    cp.start()

            for cp in loads(j, slot):
                cp.wait()
            i1 = idx1.at[pl.ds(j * c, c)]
            i2 = idx2.at[pl.ds(j * c, c)]
            scatters = (pltpu.make_async_copy(xbuf.at[slot], xo_hbm.at[i1], ssem.at[0]),
                        pltpu.make_async_copy(xbuf.at[slot], xo_hbm.at[i2], ssem.at[1]),
                        pltpu.make_async_copy(rbuf.at[slot], ro_hbm.at[i1], ssem.at[2]),
                        pltpu.make_async_copy(rbuf.at[slot], ro_hbm.at[i2], ssem.at[3]))
            for cp in scatters:
                cp.start()
            for cp in scatters:
                cp.wait()

    per_worker = t // (SC_CORES * SC_SUBCORES)
    return pl.kernel(
        body,
        out_type=(jax.ShapeDtypeStruct((n_rows, per, lanes), x1t.dtype),
                  jax.ShapeDtypeStruct((n_rows, lanes), route.dtype)),
        mesh=_sc_mesh(),
        scratch_types=[pltpu.VMEM((2, c, per, lanes), x1t.dtype),
                       pltpu.VMEM((2, c, lanes), route.dtype),
                       pltpu.VMEM((per_worker,), jnp.int32),
                       pltpu.VMEM((per_worker,), jnp.int32),
                       pltpu.SemaphoreType.DMA((4,)),
                       pltpu.SemaphoreType.DMA((4,))],
        name="sc_dispatch",
    )(x1t, route, dest)


def _sc_combine(y_rows, dest, t):
    _, per, lanes = y_rows.shape
    c = SC_CHUNK

    def body(y_hbm, d_hbm, oa_hbm, ob_hbm, buf, idx1, idx2, gsem, osem):
        first, chunks = _sc_worker_chunks(t, c)
        pltpu.sync_copy(d_hbm.at[pl.ds(first * c, chunks * c)], idx1)
        pltpu.sync_copy(d_hbm.at[pl.ds(t + first * c, chunks * c)], idx2)

        def stores(j, slot):
            rows = pl.ds((first + j) * c, c)
            return (pltpu.make_async_copy(buf.at[slot, 0], oa_hbm.at[rows], osem.at[2 * slot]),
                    pltpu.make_async_copy(buf.at[slot, 1], ob_hbm.at[rows], osem.at[2 * slot + 1]))

        @pl.loop(0, chunks)
        def _(j):
            slot = j % 2

            @pl.when(j >= 2)
            def _():
                for cp in stores(j - 2, slot):
                    cp.wait()

            gathers = (pltpu.make_async_copy(y_hbm.at[idx1.at[pl.ds(j * c, c)]], buf.at[slot, 0], gsem.at[0]),
                       pltpu.make_async_copy(y_hbm.at[idx2.at[pl.ds(j * c, c)]], buf.at[slot, 1], gsem.at[1]))
            for cp in gathers:
                cp.start()
            for cp in gathers:
                cp.wait()
            for cp in stores(j, slot):
                cp.start()

        for j in range(max(chunks - 2, 0), chunks):
            for cp in stores(j, j % 2):
                cp.wait()

    per_worker = t // (SC_CORES * SC_SUBCORES)
    out = jax.ShapeDtypeStruct((t, per, lanes), y_rows.dtype)
    return pl.kernel(
        body,
        out_type=(out, out),
        mesh=_sc_mesh(),
        scratch_types=[pltpu.VMEM((2, 2, c, per, lanes), y_rows.dtype),
                       pltpu.VMEM((per_worker,), jnp.int32),
                       pltpu.VMEM((per_worker,), jnp.int32),
                       pltpu.SemaphoreType.DMA((2,)),
                       pltpu.SemaphoreType.DMA((4,))],
        name="sc_combine",
    )(y_rows, dest)


def _expert_ffn(x_sorted, r_sorted, w_in, w_down, layer, block_expert, seg):
    n_rows, per, _ = x_sorted.shape
    _, n_exp, d, de2 = w_in.shape
    de = w_down.shape[2]
    blk = EXPERT_BLOCK
    n_blocks = block_expert.shape[0]

    def used(i, be, sg):
        return jnp.minimum(i, sg[2 * n_exp] - 1), 0

    y = pl.pallas_call(
        functools.partial(_expert_ffn_kernel, layer=layer),
        grid_spec=pltpu.PrefetchScalarGridSpec(
            num_scalar_prefetch=2,
            grid=(n_blocks,),
            in_specs=[pl.BlockSpec((blk * per, LANES), used),
                      pl.BlockSpec((blk, LANES), used),
                      pl.BlockSpec(memory_space=pl.ANY),
                      pl.BlockSpec(memory_space=pl.ANY)],
            out_specs=pl.BlockSpec((blk * per, LANES), used),
            scratch_shapes=[pltpu.VMEM((d, de2), _F32),
                            pltpu.VMEM((de, d), _F32),
                            pltpu.SemaphoreType.DMA((2,)),
                            pltpu.VMEM((d, de2), _BF16),
                            pltpu.VMEM((de, d), _BF16)]),
        out_shape=jax.ShapeDtypeStruct((n_rows * per, LANES), jnp.uint32),
        compiler_params=_params("arbitrary"),
        name="expert_ffn",
    )(block_expert, seg, x_sorted.reshape(n_rows * per, LANES), r_sorted, w_in, w_down)
    return y.reshape(n_rows, per, LANES)


def _combine(x1, ya, yb, ln_g, ln_b, *, alpha):
    t, d = x1.shape
    tm = ROW_TILE
    per = ya.shape[1]
    row = pl.BlockSpec((tm, d), lambda i: (i, 0))
    words = pl.BlockSpec((tm * per, LANES), lambda i: (i, 0))
    return pl.pallas_call(
        functools.partial(_combine_kernel, alpha=alpha),
        grid=(t // tm,),
        in_specs=[row, words, words, _full((1, d)), _full((1, d))],
        out_specs=row,
        out_shape=jax.ShapeDtypeStruct((t, d), _F32),
        compiler_params=_params("arbitrary"),
        name="combine_ln",
    )(x1, ya.reshape(t * per, LANES), yb.reshape(t * per, LANES), ln_g[None, :], ln_b[None, :])


def _combine_qkv(x1, ya, yb, ln_g, ln_b, w_k, w_v, w_q, *, alpha, q_scale):
    t, d = x1.shape
    tm = ROW_TILE
    per = ya.shape[1]
    row = pl.BlockSpec((tm, d), lambda i: (i, 0))
    words = pl.BlockSpec((tm * per, LANES), lambda i: (i, 0))
    return pl.pallas_call(
        functools.partial(_combine_qkv_kernel, alpha=alpha, q_scale=q_scale),
        grid=(t // tm,),
        in_specs=[row, words, words, _full((1, d)), _full((1, d)),
                  _full((d, d)), _full((d, d)), _full((d, d))],
        out_specs=(row, row, row, row),
        out_shape=(jax.ShapeDtypeStruct((t, d), _F32),
                   jax.ShapeDtypeStruct((t, d), _BF16),
                   jax.ShapeDtypeStruct((t, d), _BF16),
                   jax.ShapeDtypeStruct((t, d), _BF16)),
        compiler_params=_params("arbitrary"),
        name="combine_ln_qkv",
    )(x1, ya.reshape(t * per, LANES), yb.reshape(t * per, LANES), ln_g[None, :], ln_b[None, :],
      w_k, w_v, w_q)


def _diff_attention(q, k, v, lam, subln_g, *, batch, seq, lam_init):
    t, d = q.shape
    tq = ATTN_BLOCK
    hw = d // N_HEADS
    head = pl.BlockSpec((seq, ATTN_HEADS * hw), lambda b, h: (b, h))
    return pl.pallas_call(
        functools.partial(_diff_attn_kernel, lam_init=lam_init),
        grid=(batch, N_HEADS // ATTN_HEADS),
        in_specs=[_full(lam.shape), _full((1, hw)), head, head, head],
        out_specs=head,
        out_shape=jax.ShapeDtypeStruct((t, d), _BF16),
        scratch_shapes=[pltpu.VMEM((1, LANES), _F32),
                        pltpu.VMEM((ATTN_BUFFERS, 2 * tq, seq), _F32),
                        pltpu.VMEM((ATTN_BUFFERS, 2 * tq, seq), _BF16)],
        compiler_params=_params("arbitrary", "arbitrary"),
        name="diff_attention",
    )(lam, subln_g[None, :], q, k, v)


def _moe(x1t, route, route_t, counts, w_in, w_down, layer):
    n_exp, d = w_in.shape[1:3]
    dest, block_expert, seg = _dispatch_plan(route_t, counts, n_exp)
    x1t = x1t.reshape(-1, d // (2 * LANES), LANES)
    x_sorted, r_sorted = _sc_dispatch(x1t, route, dest, block_expert.shape[0] * EXPERT_BLOCK)
    y_rows = _expert_ffn(x_sorted, r_sorted, w_in, w_down, layer, block_expert, seg)
    return _sc_combine(y_rows, dest, x1t.shape[0])


def kernel(x, a_w_in, a_conv_w, a_w_out, kv_w, b_w_q, b_lambda, b_subln_g, b_w_o,
           ln1_g, ln1_b, ln2_g, ln2_b, rg_w, rg_b, re_w, re_b, e_w_in, e_w_down):
    batch, seq, d = x.shape
    depth = ln1_g.shape[0]
    assert depth == 2 and a_w_in.shape[0] == 1 and b_w_q.shape[0] == 1
    assert seq % MIXER_TILE == 0 and seq % ROW_TILE == 0 and seq % ATTN_BLOCK == 0
    t = batch * seq
    assert t % (SC_CORES * SC_SUBCORES * SC_CHUNK) == 0
    alpha = (2.0 * depth) ** 0.25
    epg = re_w.shape[2] // N_GROUPS
    head_dim = d // (2 * N_HEADS)
    xf = x.reshape(t, d)

    router0 = _router_weights(rg_w[0], rg_b[0], re_w[0], re_b[0])
    x1, x1t, route, route_t, counts = _conv_mixer(xf, a_w_in[0].astype(_BF16), a_conv_w[0],
                                                  a_w_out[0].astype(_BF16), ln1_g[0], ln1_b[0], router0,
                                                  seq=seq, alpha=alpha, epg=epg)
    ya, yb = _moe(x1t, route, route_t, counts, e_w_in, e_w_down, 0)
    q_scale = head_dim ** -0.5 * math.log2(math.e)
    x2, k, v, q = _combine_qkv(x1, ya, yb, ln2_g[0], ln2_b[0],
                               kv_w[:, :d].astype(_BF16), kv_w[:, d:].astype(_BF16),
                               b_w_q[0].astype(_BF16), alpha=alpha, q_scale=q_scale)

    lam_init = 0.8 - 0.6 * math.exp(-0.3 * 1)
    o = _diff_attention(q, k, v, b_lambda[0], b_subln_g[0], batch=batch, seq=seq, lam_init=lam_init)
    router1 = _router_weights(rg_w[1], rg_b[1], re_w[1], re_b[1])
    x1, x1t, route, route_t, counts = _attn_out(x2, o, b_w_o[0].astype(_BF16), ln1_g[1], ln1_b[1],
                                                router1, alpha=alpha, epg=epg)
    ya, yb = _moe(x1t, route, route_t, counts, e_w_in, e_w_down, 1)
    out = _combine(x1, ya, yb, ln2_g[1], ln2_b[1], alpha=alpha)
    return out.reshape(batch, seq, d)
```

```python
import functools
import math

import jax
import jax.numpy as jnp
from jax import lax
from jax.experimental import pallas as pl
from jax.experimental.pallas import tpu as pltpu
from jax.experimental.pallas import tpu_sc as plsc

N_HEADS = 8
N_GROUPS = 4
LN_EPS = 1e-5
RMS_EPS = 1e-5

LANES = 128
SUBLANES = 8
VMEM_LIMIT_BYTES = 56 * 1024 * 1024

SC_CORES = 2
SC_SUBCORES = 16

SC_CHUNK = 32
MIXER_TILE = 1024
ROW_TILE = 1024
COMBINE_PARTS = 2
EXPERT_BLOCK = 512
ATTN_BLOCK = 256
ATTN_HEADS = 2
ATTN_BUFFERS = 4

_E1, _E2, _R1, _R2, _G1, _G2 = range(6)

_F32 = jnp.float32
_BF16 = jnp.bfloat16
_NT = (((1,), (1,)), ((), ()))


def _dot(a, b):
    return jnp.dot(a, b, preferred_element_type=_F32)


def _layer_norm(z, g, b):
    mu = jnp.mean(z, axis=-1, keepdims=True)
    d = z - mu
    var = jnp.mean(d * d, axis=-1, keepdims=True)
    return d * lax.rsqrt(var + LN_EPS) * g + b


def _route_tail(x1, wt_ref, bt_ref, route_ref, routet_ref, counts_ref, umat_ref, *, epg):
    tm = x1.shape[0]
    nr = counts_ref.shape[0]
    half = wt_ref.shape[0] // 2

    @pl.when(pl.program_id(0) == 0)
    def _():
        counts_ref[...] = jnp.zeros_like(counts_ref)
        before = (lax.broadcasted_iota(jnp.int32, (tm, tm), 0) < lax.broadcasted_iota(jnp.int32, (tm, tm), 1))
        umat_ref[...] = jnp.where(before, 1.0, 0.0).astype(_BF16)

    xh = x1.astype(_BF16)
    xl = (x1 - xh.astype(_F32)).astype(_BF16)
    a = lax.dot_general(wt_ref[...], xh, _NT, preferred_element_type=_F32)
    b = lax.dot_general(wt_ref[0:half, :], xl, _NT, preferred_element_type=_F32)
    logits = a[0:nr, :] + a[half:half + nr, :] + b[0:nr, :] + bt_ref[0:nr, :]

    rowf = lax.broadcasted_iota(jnp.int32, (nr, tm), 0).astype(_F32)
    neg = -jnp.inf
    big = float(nr)
    gl = jnp.where(rowf < float(N_GROUPS), logits, neg)
    gmax = jnp.max(gl, axis=0, keepdims=True)
    gidx = jnp.min(jnp.where(gl == gmax, rowf, big), axis=0, keepdims=True)
    gtop = 1.0 / jnp.sum(jnp.exp(gl - gmax), axis=0, keepdims=True)

    lo = float(N_GROUPS) + gidx * float(epg)
    el = jnp.where((rowf >= lo) & (rowf < lo + float(epg)), logits, neg)
    m1 = jnp.max(el, axis=0, keepdims=True)
    i1 = jnp.min(jnp.where(el == m1, rowf, big), axis=0, keepdims=True)
    el2 = jnp.where(rowf == i1, neg, el)
    m2 = jnp.max(el2, axis=0, keepdims=True)
    i2 = jnp.min(jnp.where(el2 == m2, rowf, big), axis=0, keepdims=True)
    w2 = jnp.exp(m2 - m1)
    inv = 1.0 / (1.0 + w2)
    g1 = gtop * inv
    g2 = gtop * w2 * inv

    onehot = jnp.where((rowf == i1) | (rowf == i2), 1.0, 0.0)
    total = _dot(onehot.astype(_BF16), umat_ref[...]) + counts_ref[...]
    r1 = jnp.sum(jnp.where(rowf == i1, total, 0.0), axis=0, keepdims=True)
    r2 = jnp.sum(jnp.where(rowf == i2, total, 0.0), axis=0, keepdims=True)
    counts_ref[...] += jnp.sum(onehot, axis=1, keepdims=True)

    slot = lax.broadcasted_iota(jnp.int32, (SUBLANES, tm), 0)
    rec = jnp.zeros((SUBLANES, tm), _F32)
    fields = ((_E1, i1 - float(N_GROUPS)), (_E2, i2 - float(N_GROUPS)), (_R1, r1), (_R2, r2), (_G1, g1), (_G2, g2))
    for s, val in fields:
        rec = jnp.where(slot == s, val, rec)
    routet_ref[...] = rec
    rec = jnp.concatenate([rec, jnp.zeros((LANES - SUBLANES, tm), _F32)], axis=0)
    route_ref[...] = rec.T


def _store_token_words(ref, x):
    n, d = x.shape
    rows = d // (2 * LANES)
    bits = lax.bitcast_convert_type(x.astype(_BF16).astype(_F32), jnp.uint32)
    for s in range(rows):
        lo = bits[:, s * LANES:(s + 1) * LANES] >> 16
        hi = bits[:, d // 2 + s * LANES:d // 2 + (s + 1) * LANES] & jnp.uint32(0xFFFF0000)
        ref[pl.ds(s, n, stride=rows), :] = lo | hi


def _load_token_words(ref, n, d):
    rows = d // (2 * LANES)
    words = [ref[pl.ds(s, n, stride=rows), :] for s in range(rows)]
    lo = [lax.bitcast_convert_type(w << 16, _F32) for w in words]
    hi = [lax.bitcast_convert_type(w & jnp.uint32(0xFFFF0000), _F32) for w in words]
    return jnp.concatenate(lo + hi, axis=1)


def _conv_mixer_kernel(x_ref, win_ref, cw_ref, wout_ref, g_ref, b_ref, wt_ref, bt_ref,
                       x1_ref, x1t_ref, route_ref, routet_ref, counts_ref, ubuf_ref, umat_ref,
                       *, tiles_per_seq, alpha, epg):
    i = pl.program_id(0)
    tm, d = x_ref.shape

    @pl.when(i % tiles_per_seq == 0)
    def _():
        ubuf_ref[0:SUBLANES, :] = jnp.zeros((SUBLANES, d), _F32)

    x = x_ref[...]
    h = _dot(x.astype(_BF16), win_ref[...])
    u = h[:, d:2 * d] * h[:, 2 * d:]
    ubuf_ref[SUBLANES:SUBLANES + tm, :] = u
    cw = cw_ref[...]
    uc = (cw[0:1, :] * ubuf_ref[SUBLANES - 2:SUBLANES - 2 + tm, :]
          + cw[1:2, :] * ubuf_ref[SUBLANES - 1:SUBLANES - 1 + tm, :]
          + cw[2:3, :] * u)
    ubuf_ref[0:SUBLANES, :] = ubuf_ref[tm:tm + SUBLANES, :]
    y = _dot((h[:, :d] * uc).astype(_BF16), wout_ref[...])
    x1 = _layer_norm(alpha * x + y, g_ref[...], b_ref[...])
    x1_ref[...] = x1
    _store_token_words(x1t_ref, x1)
    _route_tail(x1, wt_ref, bt_ref, route_ref, routet_ref, counts_ref, umat_ref, epg=epg)


def _attn_out_kernel(x_ref, o_ref, wo_ref, g_ref, b_ref, wt_ref, bt_ref,
                     x1_ref, x1t_ref, route_ref, routet_ref, counts_ref, umat_ref, *, alpha, epg):
    y = _dot(o_ref[...], wo_ref[...])
    x1 = _layer_norm(alpha * x_ref[...] + y, g_ref[...], b_ref[...])
    x1_ref[...] = x1
    _store_token_words(x1t_ref, x1)
    _route_tail(x1, wt_ref, bt_ref, route_ref, routet_ref, counts_ref, umat_ref, epg=epg)


def _expert_ffn_kernel(bexp_ref, seg_ref, x_ref, rec_ref, win_hbm, wdn_hbm, y_ref,
                       wstage_in, wstage_dn, wsem, winb, wdnb, *, layer):
    i = pl.program_id(0)
    n_exp = (seg_ref.shape[0] - 1) // 2
    n_used = seg_ref[2 * n_exp]
    d, de2 = wstage_in.shape
    blk = rec_ref.shape[0]
    de = de2 // 2

    def weight_copies(e):
        return (pltpu.make_async_copy(win_hbm.at[layer, e], wstage_in, wsem.at[0]),
                pltpu.make_async_copy(wdn_hbm.at[layer, e], wstage_dn, wsem.at[1]))

    @pl.when(i == 0)
    def _():
        for c in weight_copies(bexp_ref[0]):
            c.start()

    @pl.when(i < n_used)
    def _():
        e = bexp_ref[i]

        @pl.when((i == 0) | (e != bexp_ref[jnp.maximum(i - 1, 0)]))
        def _():
            for c in weight_copies(e):
                c.wait()
            winb[...] = wstage_in[...].astype(_BF16)
            wdnb[...] = wstage_dn[...].astype(_BF16)
            nxt = seg_ref[n_exp + e] // blk

            @pl.when(nxt < n_used)
            def _():
                for c in weight_copies(bexp_ref[nxt]):
                    c.start()

        n_valid = seg_ref[e] - i * blk

        def ffn(rows):
            valid = lax.broadcasted_iota(jnp.int32, (rows, 1), 0) < n_valid
            x = jnp.where(valid, _load_token_words(x_ref, rows, d), 0.0)
            h = _dot(x.astype(_BF16), winb[...])
            g = h[:, :de]
            a = g * jax.nn.sigmoid(g) * h[:, de:]
            rec = rec_ref[0:rows, :]
            gate = jnp.where(rec[:, _E1:_E1 + 1] == e.astype(_F32), rec[:, _G1:_G1 + 1], rec[:, _G2:_G2 + 1])
            gate = jnp.where(valid, gate, 0.0)
            _store_token_words(y_ref, _dot(a.astype(_BF16), wdnb[...]) * gate)

        half = blk // 2

        @pl.when(n_valid > half)
        def _():
            ffn(blk)

        @pl.when(n_valid <= half)
        def _():
            ffn(half)
            y_ref[half * (y_ref.shape[0] // blk):, :] = jnp.zeros(
                (half * (y_ref.shape[0] // blk), y_ref.shape[1]), y_ref.dtype)


def _combine_body(x1_ref, ya_ref, yb_ref, g_ref, b_ref, *, alpha):
    tm, d = x1_ref.shape
    ffn = _load_token_words(ya_ref, tm, d) + _load_token_words(yb_ref, tm, d)
    return _layer_norm(alpha * x1_ref[...] + ffn, g_ref[...], b_ref[...])


def _combine_kernel(x1_ref, ya_ref, yb_ref, g_ref, b_ref, *refs, alpha):
    refs[-1][...] = _combine_body(x1_ref, ya_ref, yb_ref, g_ref, b_ref, alpha=alpha)


def _combine_qkv_kernel(x1_ref, ya_ref, yb_ref, g_ref, b_ref, wk_ref, wv_ref, wq_ref, *refs, alpha, q_scale):
    x2_ref, k_ref, v_ref, q_ref = refs[-4:]
    x2 = _combine_body(x1_ref, ya_ref, yb_ref, g_ref, b_ref, alpha=alpha)
    x2_ref[...] = x2
    xb = x2.astype(_BF16)
    k_ref[...] = _dot(xb, wk_ref[...]).astype(_BF16)
    q_ref[...] = (_dot(xb, wq_ref[...]) * q_scale).astype(_BF16)
    v_ref[...] = _dot(xb, wv_ref[...]).astype(_BF16)


def _diff_attn_kernel(lam_ref, g_ref, q_ref, k_ref, v_ref, o_ref, lamfull_ref, s_ref, p_ref, *, lam_init):
    bi, hi = pl.program_id(0), pl.program_id(1)
    seq = k_ref.shape[0]
    hw = g_ref.shape[1]
    heads = k_ref.shape[1] // hw
    tq = tk = ATTN_BLOCK
    nk = seq // tk
    hd = lam_ref.shape[1]

    @pl.when((bi == 0) & (hi == 0))
    def _():
        lam = lam_ref[...]
        a = jnp.sum(lam[0:1, :] * lam[1:2, :], axis=-1, keepdims=True)
        b = jnp.sum(lam[2:3, :] * lam[3:4, :], axis=-1, keepdims=True)
        lamfull_ref[...] = jnp.broadcast_to(jnp.exp(a) - jnp.exp(b) + lam_init, lamfull_ref.shape)

    lam_full = lamfull_ref[0:1, 0:1]
    lane = lax.broadcasted_iota(jnp.int32, (tq, hw), 1)
    row = lax.broadcasted_iota(jnp.int32, (2 * tq, tk), 0)
    col = lax.broadcasted_iota(jnp.int32, (2 * tq, tk), 1)
    causal = col <= jnp.where(row < tq, row, row - tq)
    ones = jnp.ones((seq, hw), _BF16)

    step = 0
    for i in reversed(range(nk)):
        for h in range(heads):
            cols = slice(h * hw, (h + 1) * hw)
            par = step % s_ref.shape[0]
            step += 1
            kv = (i + 1) * tk
            q = q_ref[i * tq:(i + 1) * tq, cols]
            zero = jnp.zeros_like(q)
            qcat = jnp.concatenate([jnp.where(lane < hd, q, zero), jnp.where(lane >= hd, q, zero)], axis=0)
            s_ref[par, :, 0:kv] = lax.dot_general(qcat, k_ref[0:kv, cols], _NT, preferred_element_type=_F32)
            s_ref[par, :, i * tk:kv] = jnp.where(causal, s_ref[par, :, i * tk:kv], -jnp.inf)
            rowmax = jnp.max(s_ref[par, :, 0:kv], axis=1, keepdims=True)
            p_ref[par, :, 0:kv] = jnp.exp2(s_ref[par, :, 0:kv] - rowmax).astype(_BF16)
            v_aug = jnp.concatenate([v_ref[0:kv, cols], ones[0:kv, :]], axis=1)
            acc = _dot(p_ref[par, :, 0:kv], v_aug)
            o = acc[:, 0:hw] * (1.0 / acc[:, hw:hw + 1])
            od = o[0:tq, :] - lam_full * o[tq:, :]
            od = od * lax.rsqrt(jnp.mean(od * od, axis=1, keepdims=True) + RMS_EPS)
            o_ref[i * tq:(i + 1) * tq, cols] = (od * g_ref[...] * (1.0 - lam_init)).astype(o_ref.dtype)


def _params(*sem):
    return pltpu.CompilerParams(dimension_semantics=sem, vmem_limit_bytes=VMEM_LIMIT_BYTES)


def _full(shape):
    return pl.BlockSpec(shape, lambda *_: (0,) * len(shape))


def _router_weights(rg_w, rg_b, re_w, re_b):
    d = rg_w.shape[0]
    n = rg_w.shape[1] + re_w.shape[1]
    w = jnp.concatenate([rg_w, re_w, jnp.zeros((d, LANES - n), _F32)], axis=1).T
    b = jnp.concatenate([rg_b, re_b, jnp.zeros((LANES - n,), _F32)])[:, None]
    wh = w.astype(_BF16)
    wl = (w - wh.astype(_F32)).astype(_BF16)
    return jnp.concatenate([wh, wl], axis=0), b


def _router_rows(n_experts):
    return -(-(N_GROUPS + n_experts) // (2 * SUBLANES)) * (2 * SUBLANES)


def _route_outs(t, d, tm, nr):
    per = d // LANES
    shapes = (jax.ShapeDtypeStruct((t, d), _F32),
              jax.ShapeDtypeStruct((t * per // 2, LANES), jnp.uint32),
              jax.ShapeDtypeStruct((t, LANES), _F32),
              jax.ShapeDtypeStruct((SUBLANES, t), _F32),
              jax.ShapeDtypeStruct((nr, 1), _F32))
    specs = (pl.BlockSpec((tm, d), lambda i: (i, 0)),
             pl.BlockSpec((tm * per // 2, LANES), lambda i: (i, 0)),
             pl.BlockSpec((tm, LANES), lambda i: (i, 0)),
             pl.BlockSpec((SUBLANES, tm), lambda i: (0, i)),
             pl.BlockSpec((nr, 1), lambda i: (0, 0)))
    return shapes, specs


def _conv_mixer(x, w_in, conv_w, w_out, ln_g, ln_b, router, *, seq, alpha, epg):
    t, d = x.shape
    tm = MIXER_TILE
    wt, bt = router
    shapes, specs = _route_outs(t, d, tm, _router_rows(N_GROUPS * epg))
    return pl.pallas_call(
        functools.partial(_conv_mixer_kernel, tiles_per_seq=seq // tm, alpha=alpha, epg=epg),
        grid=(t // tm,),
        in_specs=[pl.BlockSpec((tm, d), lambda i: (i, 0)),
                  _full(w_in.shape), _full(conv_w.shape), _full(w_out.shape),
                  _full((1, d)), _full((1, d)), _full(wt.shape), _full(bt.shape)],
        out_specs=specs, out_shape=shapes,
        scratch_shapes=[pltpu.VMEM((tm + SUBLANES, d), _F32), pltpu.VMEM((tm, tm), _BF16)],
        compiler_params=_params("arbitrary"),
        name="conv_mixer_ln_router",
    )(x, w_in, conv_w, w_out, ln_g[None, :], ln_b[None, :], wt, bt)


def _attn_out(x, o, w_o, ln_g, ln_b, router, *, alpha, epg):
    t, d = x.shape
    tm = MIXER_TILE
    wt, bt = router
    shapes, specs = _route_outs(t, d, tm, _router_rows(N_GROUPS * epg))
    return pl.pallas_call(
        functools.partial(_attn_out_kernel, alpha=alpha, epg=epg),
        grid=(t // tm,),
        in_specs=[pl.BlockSpec((tm, d), lambda i: (i, 0)),
                  pl.BlockSpec((tm, d), lambda i: (i, 0)),
                  _full(w_o.shape), _full((1, d)), _full((1, d)), _full(wt.shape), _full(bt.shape)],
        out_specs=specs, out_shape=shapes,
        scratch_shapes=[pltpu.VMEM((tm, tm), _BF16)],
        compiler_params=_params("arbitrary"),
        name="attn_out_ln_router",
    )(x, o, w_o, ln_g[None, :], ln_b[None, :], wt, bt)


def _dispatch_plan(route_t, counts, n_experts):
    t = route_t.shape[1]
    blk = EXPERT_BLOCK
    cnt = counts[N_GROUPS:N_GROUPS + n_experts, 0].astype(jnp.int32)
    padded = (cnt + blk - 1) // blk * blk
    pad_end = jnp.cumsum(padded)
    pad_start = pad_end - padded
    ids = jnp.arange(n_experts, dtype=jnp.int32)

    def sorted_row(e_slot, r_slot):
        e = route_t[e_slot].astype(jnp.int32)
        start = jnp.sum(jnp.where(ids[:, None] == e[None, :], pad_start[:, None], 0), axis=0)
        return start + route_t[r_slot].astype(jnp.int32)

    dest = jnp.concatenate([sorted_row(_E1, _R1), sorted_row(_E2, _R2)])
    n_blocks = (2 * t + n_experts * blk) // blk
    block_start = jnp.arange(n_blocks, dtype=jnp.int32) * blk
    block_expert = jnp.minimum(jnp.sum(pad_end[None, :] <= block_start[:, None], axis=1),
                               n_experts - 1).astype(jnp.int32)
    n_used = (pad_end[-1:] // blk).astype(jnp.int32)
    seg = jnp.concatenate([pad_start + cnt, pad_end, n_used]).astype(jnp.int32)
    return dest, block_expert, seg


def _sc_mesh():
    return plsc.VectorSubcoreMesh(core_axis_name="c", subcore_axis_name="s",
                                  num_cores=SC_CORES, num_subcores=SC_SUBCORES)


def _sc_worker_chunks(t, chunk):
    chunks = t // (SC_CORES * SC_SUBCORES * chunk)
    wid = lax.axis_index("c") * SC_SUBCORES + lax.axis_index("s")
    return wid * chunks, chunks


def _sc_dispatch(x1t, route, dest, n_rows):
    t, per, lanes = x1t.shape
    c = SC_CHUNK

    def body(x_hbm, r_hbm, d_hbm, xo_hbm, ro_hbm, xbuf, rbuf, idx1, idx2, lsem, ssem):
        first, chunks = _sc_worker_chunks(t, c)
        pltpu.sync_copy(d_hbm.at[pl.ds(first * c, chunks * c)], idx1)
        pltpu.sync_copy(d_hbm.at[pl.ds(t + first * c, chunks * c)], idx2)

        def loads(j, slot):
            rows = pl.ds((first + j) * c, c)
            return (pltpu.make_async_copy(x_hbm.at[rows], xbuf.at[slot], lsem.at[2 * slot]),
                    pltpu.make_async_copy(r_hbm.at[rows], rbuf.at[slot], lsem.at[2 * slot + 1]))

        for cp in loads(0, 0):
            cp.start()

        @pl.loop(0, chunks)
        def _(j):
            slot = j % 2

            @pl.when(j + 1 < chunks)
            def _():
                for cp in loads(j + 1, 1 - slot):
                    cp.start()

            for cp in loads(j, slot):
                cp.wait()
            i1 = idx1.at[pl.ds(j * c, c)]
            i2 = idx2.at[pl.ds(j * c, c)]
            scatters = (pltpu.make_async_copy(xbuf.at[slot], xo_hbm.at[i1], ssem.at[0]),
                        pltpu.make_async_copy(xbuf.at[slot], xo_hbm.at[i2], ssem.at[1]),
                        pltpu.make_async_copy(rbuf.at[slot], ro_hbm.at[i1], ssem.at[2]),
                        pltpu.make_async_copy(rbuf.at[slot], ro_hbm.at[i2], ssem.at[3]))
            for cp in scatters:
                cp.start()
            for cp in scatters:
                cp.wait()

    per_worker = t // (SC_CORES * SC_SUBCORES)
    return pl.kernel(
        body,
        out_type=(jax.ShapeDtypeStruct((n_rows, per, lanes), x1t.dtype),
                  jax.ShapeDtypeStruct((n_rows, lanes), route.dtype)),
        mesh=_sc_mesh(),
        scratch_types=[pltpu.VMEM((2, c, per, lanes), x1t.dtype),
                       pltpu.VMEM((2, c, lanes), route.dtype),
                       pltpu.VMEM((per_worker,), jnp.int32),
                       pltpu.VMEM((per_worker,), jnp.int32),
                       pltpu.SemaphoreType.DMA((4,)),
                       pltpu.SemaphoreType.DMA((4,))],
        name="sc_dispatch",
    )(x1t, route, dest)


def _sc_combine(y_rows, dest_a, dest_b):
    _, per, lanes = y_rows.shape
    t = dest_a.shape[0]
    c = SC_CHUNK

    def body(y_hbm, da_hbm, db_hbm, oa_hbm, ob_hbm, buf, idx1, idx2, gsem, osem):
        first, chunks = _sc_worker_chunks(t, c)
        pltpu.sync_copy(da_hbm.at[pl.ds(first * c, chunks * c)], idx1)
        pltpu.sync_copy(db_hbm.at[pl.ds(first * c, chunks * c)], idx2)

        def stores(j, slot):
            rows = pl.ds((first + j) * c, c)
            return (pltpu.make_async_copy(buf.at[slot, 0], oa_hbm.at[rows], osem.at[2 * slot]),
                    pltpu.make_async_copy(buf.at[slot, 1], ob_hbm.at[rows], osem.at[2 * slot + 1]))

        @pl.loop(0, chunks)
        def _(j):
            slot = j % 2

            @pl.when(j >= 2)
            def _():
                for cp in stores(j - 2, slot):
                    cp.wait()

            gathers = (pltpu.make_async_copy(y_hbm.at[idx1.at[pl.ds(j * c, c)]], buf.at[slot, 0], gsem.at[0]),
                       pltpu.make_async_copy(y_hbm.at[idx2.at[pl.ds(j * c, c)]], buf.at[slot, 1], gsem.at[1]))
            for cp in gathers:
                cp.start()
            for cp in gathers:
                cp.wait()
            for cp in stores(j, slot):
                cp.start()

        for j in range(max(chunks - 2, 0), chunks):
            for cp in stores(j, j % 2):
                cp.wait()

    per_worker = t // (SC_CORES * SC_SUBCORES)
    out = jax.ShapeDtypeStruct((t, per, lanes), y_rows.dtype)
    return pl.kernel(
        body,
        out_type=(out, out),
        mesh=_sc_mesh(),
        scratch_types=[pltpu.VMEM((2, 2, c, per, lanes), y_rows.dtype),
                       pltpu.VMEM((per_worker,), jnp.int32),
                       pltpu.VMEM((per_worker,), jnp.int32),
                       pltpu.SemaphoreType.DMA((2,)),
                       pltpu.SemaphoreType.DMA((4,))],
        name="sc_combine",
    )(y_rows, dest_a, dest_b)


def _expert_ffn(x_sorted, r_sorted, w_in, w_down, layer, block_expert, seg):
    n_rows, per, _ = x_sorted.shape
    _, n_exp, d, de2 = w_in.shape
    de = w_down.shape[2]
    blk = EXPERT_BLOCK
    n_blocks = block_expert.shape[0]

    def used(i, be, sg):
        return jnp.minimum(i, sg[2 * n_exp] - 1), 0

    y = pl.pallas_call(
        functools.partial(_expert_ffn_kernel, layer=layer),
        grid_spec=pltpu.PrefetchScalarGridSpec(
            num_scalar_prefetch=2,
            grid=(n_blocks,),
            in_specs=[pl.BlockSpec((blk * per, LANES), used),
                      pl.BlockSpec((blk, LANES), used),
                      pl.BlockSpec(memory_space=pl.ANY),
                      pl.BlockSpec(memory_space=pl.ANY)],
            out_specs=pl.BlockSpec((blk * per, LANES), used),
            scratch_shapes=[pltpu.VMEM((d, de2), _F32),
                            pltpu.VMEM((de, d), _F32),
                            pltpu.SemaphoreType.DMA((2,)),
                            pltpu.VMEM((d, de2), _BF16),
                            pltpu.VMEM((de, d), _BF16)]),
        out_shape=jax.ShapeDtypeStruct((n_rows * per, LANES), jnp.uint32),
        compiler_params=_params("arbitrary"),
        name="expert_ffn",
    )(block_expert, seg, x_sorted.reshape(n_rows * per, LANES), r_sorted, w_in, w_down)
    return y.reshape(n_rows, per, LANES)


def _combine_parts(kernel_fn, name, x1, parts, params, out_dtypes):
    t, d = x1.shape
    tm = ROW_TILE
    outs = None
    first_tile = 0
    for ya, yb in parts:
        tn, per, _ = ya.shape
        row = pl.BlockSpec((tm, d), lambda i, off=first_tile: (i + off, 0))
        words = pl.BlockSpec((tm * per, LANES), lambda i: (i, 0))
        args = [x1, ya.reshape(tn * per, LANES), yb.reshape(tn * per, LANES)] + list(params)
        specs = [row, words, words] + [_full(p.shape) for p in params]
        aliases = {}
        if outs is not None:
            aliases = {len(args) + k: k for k in range(len(outs))}
            args += list(outs)
            specs += [pl.BlockSpec(memory_space=pl.ANY)] * len(outs)
        outs = pl.pallas_call(
            kernel_fn,
            grid=(tn // tm,),
            in_specs=specs,
            out_specs=tuple(row for _ in out_dtypes),
            out_shape=tuple(jax.ShapeDtypeStruct((t, d), dt) for dt in out_dtypes),
            input_output_aliases=aliases,
            compiler_params=_params("arbitrary"),
            name=name,
        )(*args)
        first_tile += tn // tm
    return outs


def _combine(x1, parts, ln_g, ln_b, *, alpha):
    return _combine_parts(functools.partial(_combine_kernel, alpha=alpha), "combine_ln", x1, parts,
                          (ln_g[None, :], ln_b[None, :]), (_F32,))[0]


def _combine_qkv(x1, parts, ln_g, ln_b, w_k, w_v, w_q, *, alpha, q_scale):
    return _combine_parts(functools.partial(_combine_qkv_kernel, alpha=alpha, q_scale=q_scale),
                          "combine_ln_qkv", x1, parts, (ln_g[None, :], ln_b[None, :], w_k, w_v, w_q),
                          (_F32, _BF16, _BF16, _BF16))


def _diff_attention(q, k, v, lam, subln_g, *, batch, seq, lam_init):
    t, d = q.shape
    tq = ATTN_BLOCK
    hw = d // N_HEADS
    head = pl.BlockSpec((seq, ATTN_HEADS * hw), lambda b, h: (b, h))
    return pl.pallas_call(
        functools.partial(_diff_attn_kernel, lam_init=lam_init),
        grid=(batch, N_HEADS // ATTN_HEADS),
        in_specs=[_full(lam.shape), _full((1, hw)), head, head, head],
        out_specs=head,
        out_shape=jax.ShapeDtypeStruct((t, d), _BF16),
        scratch_shapes=[pltpu.VMEM((1, LANES), _F32),
                        pltpu.VMEM((ATTN_BUFFERS, 2 * tq, seq), _F32),
                        pltpu.VMEM((ATTN_BUFFERS, 2 * tq, seq), _BF16)],
        compiler_params=_params("arbitrary", "arbitrary"),
        name="diff_attention",
    )(lam, subln_g[None, :], q, k, v)


def _moe(x1t, route, route_t, counts, w_in, w_down, layer):
    n_exp, d = w_in.shape[1:3]
    dest, block_expert, seg = _dispatch_plan(route_t, counts, n_exp)
    x1t = x1t.reshape(-1, d // (2 * LANES), LANES)
    x_sorted, r_sorted = _sc_dispatch(x1t, route, dest, block_expert.shape[0] * EXPERT_BLOCK)
    y_rows = _expert_ffn(x_sorted, r_sorted, w_in, w_down, layer, block_expert, seg)
    t = x1t.shape[0]
    part = t // COMBINE_PARTS
    return [_sc_combine(y_rows, dest[p * part:(p + 1) * part], dest[t + p * part:t + (p + 1) * part])
            for p in range(COMBINE_PARTS)]


def kernel(x, a_w_in, a_conv_w, a_w_out, kv_w, b_w_q, b_lambda, b_subln_g, b_w_o,
           ln1_g, ln1_b, ln2_g, ln2_b, rg_w, rg_b, re_w, re_b, e_w_in, e_w_down):
    batch, seq, d = x.shape
    depth = ln1_g.shape[0]
    assert depth == 2 and a_w_in.shape[0] == 1 and b_w_q.shape[0] == 1
    assert seq % MIXER_TILE == 0 and seq % ROW_TILE == 0 and seq % ATTN_BLOCK == 0
    t = batch * seq
    assert t % (COMBINE_PARTS * SC_CORES * SC_SUBCORES * SC_CHUNK) == 0 and t % (COMBINE_PARTS * ROW_TILE) == 0
    alpha = (2.0 * depth) ** 0.25
    epg = re_w.shape[2] // N_GROUPS
    head_dim = d // (2 * N_HEADS)
    xf = x.reshape(t, d)

    router0 = _router_weights(rg_w[0], rg_b[0], re_w[0], re_b[0])
    x1, x1t, route, route_t, counts = _conv_mixer(xf, a_w_in[0].astype(_BF16), a_conv_w[0],
                                                  a_w_out[0].astype(_BF16), ln1_g[0], ln1_b[0], router0,
                                                  seq=seq, alpha=alpha, epg=epg)
    parts = _moe(x1t, route, route_t, counts, e_w_in, e_w_down, 0)
    q_scale = head_dim ** -0.5 * math.log2(math.e)
    x2, k, v, q = _combine_qkv(x1, parts, ln2_g[0], ln2_b[0],
                               kv_w[:, :d].astype(_BF16), kv_w[:, d:].astype(_BF16),
                               b_w_q[0].astype(_BF16), alpha=alpha, q_scale=q_scale)

    lam_init = 0.8 - 0.6 * math.exp(-0.3 * 1)
    o = _diff_attention(q, k, v, b_lambda[0], b_subln_g[0], batch=batch, seq=seq, lam_init=lam_init)
    router1 = _router_weights(rg_w[1], rg_b[1], re_w[1], re_b[1])
    x1, x1t, route, route_t, counts = _attn_out(x2, o, b_w_o[0].astype(_BF16), ln1_g[1], ln1_b[1],
                                                router1, alpha=alpha, epg=epg)
    parts = _moe(x1t, route, route_t, counts, e_w_in, e_w_down, 1)
    out = _combine(x1, parts, ln2_g[1], ln2_b[1], alpha=alpha)
    return out.reshape(batch, seq, d)
```

```python
import functools
import math

import jax
import jax.numpy as jnp
from jax import lax
from jax.experimental import pallas as pl
from jax.experimental.pallas import tpu as pltpu
from jax.experimental.pallas import tpu_sc as plsc

N_HEADS = 8
N_GROUPS = 4
LN_EPS = 1e-5
RMS_EPS = 1e-5

LANES = 128
SUBLANES = 8
VMEM_LIMIT_BYTES = 56 * 1024 * 1024

SC_CORES = 2
SC_SUBCORES = 16

SC_CHUNK = 32
MIXER_TILE = 1024
ROW_TILE = 1024
EXPERT_BLOCK = 512
ATTN_BLOCK = 256
ATTN_HEADS = 2
ATTN_BUFFERS = 4

_E1, _E2, _R1, _R2, _G1, _G2 = range(6)

_F32 = jnp.float32
_BF16 = jnp.bfloat16
_NT = (((1,), (1,)), ((), ()))


def _dot(a, b):
    return jnp.dot(a, b, preferred_element_type=_F32)


def _layer_norm(z, g, b):
    mu = jnp.mean(z, axis=-1, keepdims=True)
    d = z - mu
    var = jnp.mean(d * d, axis=-1, keepdims=True)
    return d * lax.rsqrt(var + LN_EPS) * g + b


def _route_tail(x1, wt_ref, bt_ref, route_ref, routet_ref, counts_ref, umat_ref, *, epg):
    tm = x1.shape[0]
    nr = counts_ref.shape[0]
    half = wt_ref.shape[0] // 2

    @pl.when(pl.program_id(0) == 0)
    def _():
        counts_ref[...] = jnp.zeros_like(counts_ref)
        before = (lax.broadcasted_iota(jnp.int32, (tm, tm), 0) < lax.broadcasted_iota(jnp.int32, (tm, tm), 1))
        umat_ref[...] = jnp.where(before, 1.0, 0.0).astype(_BF16)

    xh = x1.astype(_BF16)
    xl = (x1 - xh.astype(_F32)).astype(_BF16)
    a = lax.dot_general(wt_ref[...], xh, _NT, preferred_element_type=_F32)
    b = lax.dot_general(wt_ref[0:half, :], xl, _NT, preferred_element_type=_F32)
    logits = a[0:nr, :] + a[half:half + nr, :] + b[0:nr, :] + bt_ref[0:nr, :]

    rowf = lax.broadcasted_iota(jnp.int32, (nr, tm), 0).astype(_F32)
    neg = -jnp.inf
    big = float(nr)
    gl = jnp.where(rowf < float(N_GROUPS), logits, neg)
    gmax = jnp.max(gl, axis=0, keepdims=True)
    gidx = jnp.min(jnp.where(gl == gmax, rowf, big), axis=0, keepdims=True)
    gtop = 1.0 / jnp.sum(jnp.exp(gl - gmax), axis=0, keepdims=True)

    lo = float(N_GROUPS) + gidx * float(epg)
    el = jnp.where((rowf >= lo) & (rowf < lo + float(epg)), logits, neg)
    m1 = jnp.max(el, axis=0, keepdims=True)
    i1 = jnp.min(jnp.where(el == m1, rowf, big), axis=0, keepdims=True)
    el2 = jnp.where(rowf == i1, neg, el)
    m2 = jnp.max(el2, axis=0, keepdims=True)
    i2 = jnp.min(jnp.where(el2 == m2, rowf, big), axis=0, keepdims=True)
    w2 = jnp.exp(m2 - m1)
    inv = 1.0 / (1.0 + w2)
    g1 = gtop * inv
    g2 = gtop * w2 * inv

    onehot = jnp.where((rowf == i1) | (rowf == i2), 1.0, 0.0)
    total = _dot(onehot.astype(_BF16), umat_ref[...]) + counts_ref[...]
    r1 = jnp.sum(jnp.where(rowf == i1, total, 0.0), axis=0, keepdims=True)
    r2 = jnp.sum(jnp.where(rowf == i2, total, 0.0), axis=0, keepdims=True)
    counts_ref[...] += jnp.sum(onehot, axis=1, keepdims=True)

    slot = lax.broadcasted_iota(jnp.int32, (SUBLANES, tm), 0)
    rec = jnp.zeros((SUBLANES, tm), _F32)
    fields = ((_E1, i1 - float(N_GROUPS)), (_E2, i2 - float(N_GROUPS)), (_R1, r1), (_R2, r2), (_G1, g1), (_G2, g2))
    for s, val in fields:
        rec = jnp.where(slot == s, val, rec)
    routet_ref[...] = rec
    rec = jnp.concatenate([rec, jnp.zeros((LANES - SUBLANES, tm), _F32)], axis=0)
    route_ref[...] = rec.T


def _store_token_words(ref, x):
    n, d = x.shape
    rows = d // (2 * LANES)
    bits = lax.bitcast_convert_type(x.astype(_BF16).astype(_F32), jnp.uint32)
    for s in range(rows):
        lo = bits[:, s * LANES:(s + 1) * LANES] >> 16
        hi = bits[:, d // 2 + s * LANES:d // 2 + (s + 1) * LANES] & jnp.uint32(0xFFFF0000)
        ref[pl.ds(s, n, stride=rows), :] = lo | hi


def _load_token_words(ref, n, d):
    rows = d // (2 * LANES)
    words = [ref[pl.ds(s, n, stride=rows), :] for s in range(rows)]
    lo = [lax.bitcast_convert_type(w << 16, _F32) for w in words]
    hi = [lax.bitcast_convert_type(w & jnp.uint32(0xFFFF0000), _F32) for w in words]
    return jnp.concatenate(lo + hi, axis=1)


def _conv_mixer_kernel(x_ref, win_ref, cw_ref, wout_ref, g_ref, b_ref, wt_ref, bt_ref,
                       x1_ref, x1t_ref, route_ref, routet_ref, counts_ref, ubuf_ref, umat_ref,
                       *, tiles_per_seq, alpha, epg):
    i = pl.program_id(0)
    tm, d = x_ref.shape

    @pl.when(i % tiles_per_seq == 0)
    def _():
        ubuf_ref[0:SUBLANES, :] = jnp.zeros((SUBLANES, d), _F32)

    x = x_ref[...]
    h = _dot(x.astype(_BF16), win_ref[...])
    u = h[:, d:2 * d] * h[:, 2 * d:]
    ubuf_ref[SUBLANES:SUBLANES + tm, :] = u
    cw = cw_ref[...]
    uc = (cw[0:1, :] * ubuf_ref[SUBLANES - 2:SUBLANES - 2 + tm, :]
          + cw[1:2, :] * ubuf_ref[SUBLANES - 1:SUBLANES - 1 + tm, :]
          + cw[2:3, :] * u)
    ubuf_ref[0:SUBLANES, :] = ubuf_ref[tm:tm + SUBLANES, :]
    y = _dot((h[:, :d] * uc).astype(_BF16), wout_ref[...])
    x1 = _layer_norm(alpha * x + y, g_ref[...], b_ref[...])
    x1_ref[...] = x1
    _store_token_words(x1t_ref, x1)
    _route_tail(x1, wt_ref, bt_ref, route_ref, routet_ref, counts_ref, umat_ref, epg=epg)


def _attn_out_kernel(x_ref, o_ref, wo_ref, g_ref, b_ref, wt_ref, bt_ref,
                     x1_ref, x1t_ref, route_ref, routet_ref, counts_ref, umat_ref, *, alpha, epg):
    y = _dot(o_ref[...], wo_ref[...])
    x1 = _layer_norm(alpha * x_ref[...] + y, g_ref[...], b_ref[...])
    x1_ref[...] = x1
    _store_token_words(x1t_ref, x1)
    _route_tail(x1, wt_ref, bt_ref, route_ref, routet_ref, counts_ref, umat_ref, epg=epg)


def _expert_ffn_kernel(bexp_ref, seg_ref, x_ref, rec_ref, win_hbm, wdn_hbm, y_ref,
                       wstage_in, wstage_dn, wsem, winb, wdnb, *, layer):
    i = pl.program_id(0)
    n_exp = (seg_ref.shape[0] - 1) // 2
    n_used = seg_ref[2 * n_exp]
    d, de2 = wstage_in.shape
    blk = rec_ref.shape[0]
    de = de2 // 2

    def weight_copies(e):
        return (pltpu.make_async_copy(win_hbm.at[layer, e], wstage_in, wsem.at[0]),
                pltpu.make_async_copy(wdn_hbm.at[layer, e], wstage_dn, wsem.at[1]))

    @pl.when(i == 0)
    def _():
        for c in weight_copies(bexp_ref[0]):
            c.start()

    @pl.when(i < n_used)
    def _():
        e = bexp_ref[i]

        @pl.when((i == 0) | (e != bexp_ref[jnp.maximum(i - 1, 0)]))
        def _():
            for c in weight_copies(e):
                c.wait()
            winb[...] = wstage_in[...].astype(_BF16)
            wdnb[...] = wstage_dn[...].astype(_BF16)
            nxt = seg_ref[n_exp + e] // blk

            @pl.when(nxt < n_used)
            def _():
                for c in weight_copies(bexp_ref[nxt]):
                    c.start()

        n_valid = seg_ref[e] - i * blk

        def ffn(rows):
            valid = lax.broadcasted_iota(jnp.int32, (rows, 1), 0) < n_valid
            x = jnp.where(valid, _load_token_words(x_ref, rows, d), 0.0)
            h = _dot(x.astype(_BF16), winb[...])
            g = h[:, :de]
            a = g * jax.nn.sigmoid(g) * h[:, de:]
            rec = rec_ref[0:rows, :]
            gate = jnp.where(rec[:, _E1:_E1 + 1] == e.astype(_F32), rec[:, _G1:_G1 + 1], rec[:, _G2:_G2 + 1])
            gate = jnp.where(valid, gate, 0.0)
            _store_token_words(y_ref, _dot(a.astype(_BF16), wdnb[...]) * gate)

        half = blk // 2

        @pl.when(n_valid > half)
        def _():
            ffn(blk)

        @pl.when(n_valid <= half)
        def _():
            ffn(half)
            y_ref[half * (y_ref.shape[0] // blk):, :] = jnp.zeros(
                (half * (y_ref.shape[0] // blk), y_ref.shape[1]), y_ref.dtype)


def _combine_body(x1_ref, ya_ref, yb_ref, g_ref, b_ref, *, alpha):
    tm, d = x1_ref.shape
    ffn = _load_token_words(ya_ref, tm, d) + _load_token_words(yb_ref, tm, d)
    return _layer_norm(alpha * x1_ref[...] + ffn, g_ref[...], b_ref[...])


def _combine_kernel(x1_ref, ya_ref, yb_ref, g_ref, b_ref, x2_ref, *, alpha):
    x2_ref[...] = _combine_body(x1_ref, ya_ref, yb_ref, g_ref, b_ref, alpha=alpha)


def _combine_qkv_kernel(x1_ref, ya_ref, yb_ref, g_ref, b_ref, wk_ref, wv_ref, wq_ref,
                        x2_ref, k_ref, v_ref, q_ref, *, alpha, q_scale):
    x2 = _combine_body(x1_ref, ya_ref, yb_ref, g_ref, b_ref, alpha=alpha)
    x2_ref[...] = x2
    xb = x2.astype(_BF16)
    k_ref[...] = _dot(xb, wk_ref[...]).astype(_BF16)
    q_ref[...] = (_dot(xb, wq_ref[...]) * q_scale).astype(_BF16)
    v_ref[...] = _dot(xb, wv_ref[...]).astype(_BF16)


def _diff_attn_kernel(lam_ref, g_ref, q_ref, k_ref, v_ref, o_ref, lamfull_ref, s_ref, p_ref, *, lam_init):
    bi, hi = pl.program_id(0), pl.program_id(1)
    seq = k_ref.shape[0]
    hw = g_ref.shape[1]
    heads = k_ref.shape[1] // hw
    tq = tk = ATTN_BLOCK
    nk = seq // tk
    hd = lam_ref.shape[1]

    @pl.when((bi == 0) & (hi == 0))
    def _():
        lam = lam_ref[...]
        a = jnp.sum(lam[0:1, :] * lam[1:2, :], axis=-1, keepdims=True)
        b = jnp.sum(lam[2:3, :] * lam[3:4, :], axis=-1, keepdims=True)
        lamfull_ref[...] = jnp.broadcast_to(jnp.exp(a) - jnp.exp(b) + lam_init, lamfull_ref.shape)

    lam_full = lamfull_ref[0:1, 0:1]
    lane = lax.broadcasted_iota(jnp.int32, (tq, hw), 1)
    row = lax.broadcasted_iota(jnp.int32, (2 * tq, tk), 0)
    col = lax.broadcasted_iota(jnp.int32, (2 * tq, tk), 1)
    causal = col <= jnp.where(row < tq, row, row - tq)
    ones = jnp.ones((seq, hw), _BF16)

    step = 0
    for i in reversed(range(nk)):
        for h in range(heads):
            cols = slice(h * hw, (h + 1) * hw)
            par = step % s_ref.shape[0]
            step += 1
            kv = (i + 1) * tk
            q = q_ref[i * tq:(i + 1) * tq, cols]
            zero = jnp.zeros_like(q)
            qcat = jnp.concatenate([jnp.where(lane < hd, q, zero), jnp.where(lane >= hd, q, zero)], axis=0)
            s_ref[par, :, 0:kv] = lax.dot_general(qcat, k_ref[0:kv, cols], _NT, preferred_element_type=_F32)
            s_ref[par, :, i * tk:kv] = jnp.where(causal, s_ref[par, :, i * tk:kv], -jnp.inf)
            rowmax = jnp.max(s_ref[par, :, 0:kv], axis=1, keepdims=True)
            p_ref[par, :, 0:kv] = jnp.exp2(s_ref[par, :, 0:kv] - rowmax).astype(_BF16)
            v_aug = jnp.concatenate([v_ref[0:kv, cols], ones[0:kv, :]], axis=1)
            acc = _dot(p_ref[par, :, 0:kv], v_aug)
            o = acc[:, 0:hw] * (1.0 / acc[:, hw:hw + 1])
            od = o[0:tq, :] - lam_full * o[tq:, :]
            od = od * lax.rsqrt(jnp.mean(od * od, axis=1, keepdims=True) + RMS_EPS)
            o_ref[i * tq:(i + 1) * tq, cols] = (od * g_ref[...] * (1.0 - lam_init)).astype(o_ref.dtype)


def _params(*sem):
    return pltpu.CompilerParams(dimension_semantics=sem, vmem_limit_bytes=VMEM_LIMIT_BYTES)


def _full(shape):
    return pl.BlockSpec(shape, lambda *_: (0,) * len(shape))


def _router_weights(rg_w, rg_b, re_w, re_b):
    d = rg_w.shape[0]
    n = rg_w.shape[1] + re_w.shape[1]
    w = jnp.concatenate([rg_w, re_w, jnp.zeros((d, LANES - n), _F32)], axis=1).T
    b = jnp.concatenate([rg_b, re_b, jnp.zeros((LANES - n,), _F32)])[:, None]
    wh = w.astype(_BF16)
    wl = (w - wh.astype(_F32)).astype(_BF16)
    return jnp.concatenate([wh, wl], axis=0), b


def _router_rows(n_experts):
    return -(-(N_GROUPS + n_experts) // (2 * SUBLANES)) * (2 * SUBLANES)


def _route_outs(t, d, tm, nr):
    per = d // LANES
    shapes = (jax.ShapeDtypeStruct((t, d), _F32),
              jax.ShapeDtypeStruct((t * per // 2, LANES), jnp.uint32),
              jax.ShapeDtypeStruct((t, LANES), _F32),
              jax.ShapeDtypeStruct((SUBLANES, t), _F32),
              jax.ShapeDtypeStruct((nr, 1), _F32))
    specs = (pl.BlockSpec((tm, d), lambda i: (i, 0)),
             pl.BlockSpec((tm * per // 2, LANES), lambda i: (i, 0)),
             pl.BlockSpec((tm, LANES), lambda i: (i, 0)),
             pl.BlockSpec((SUBLANES, tm), lambda i: (0, i)),
             pl.BlockSpec((nr, 1), lambda i: (0, 0)))
    return shapes, specs


def _conv_mixer(x, w_in, conv_w, w_out, ln_g, ln_b, router, *, seq, alpha, epg):
    t, d = x.shape
    tm = MIXER_TILE
    wt, bt = router
    shapes, specs = _route_outs(t, d, tm, _router_rows(N_GROUPS * epg))
    return pl.pallas_call(
        functools.partial(_conv_mixer_kernel, tiles_per_seq=seq // tm, alpha=alpha, epg=epg),
        grid=(t // tm,),
        in_specs=[pl.BlockSpec((tm, d), lambda i: (i, 0)),
                  _full(w_in.shape), _full(conv_w.shape), _full(w_out.shape),
                  _full((1, d)), _full((1, d)), _full(wt.shape), _full(bt.shape)],
        out_specs=specs, out_shape=shapes,
        scratch_shapes=[pltpu.VMEM((tm + SUBLANES, d), _F32), pltpu.VMEM((tm, tm), _BF16)],
        compiler_params=_params("arbitrary"),
        name="conv_mixer_ln_router",
    )(x, w_in, conv_w, w_out, ln_g[None, :], ln_b[None, :], wt, bt)


def _attn_out(x, o, w_o, ln_g, ln_b, router, *, alpha, epg):
    t, d = x.shape
    tm = MIXER_TILE
    wt, bt = router
    shapes, specs = _route_outs(t, d, tm, _router_rows(N_GROUPS * epg))
    return pl.pallas_call(
        functools.partial(_attn_out_kernel, alpha=alpha, epg=epg),
        grid=(t // tm,),
        in_specs=[pl.BlockSpec((tm, d), lambda i: (i, 0)),
                  pl.BlockSpec((tm, d), lambda i: (i, 0)),
                  _full(w_o.shape), _full((1, d)), _full((1, d)), _full(wt.shape), _full(bt.shape)],
        out_specs=specs, out_shape=shapes,
        scratch_shapes=[pltpu.VMEM((tm, tm), _BF16)],
        compiler_params=_params("arbitrary"),
        name="attn_out_ln_router",
    )(x, o, w_o, ln_g[None, :], ln_b[None, :], wt, bt)


def _dispatch_plan(route_t, counts, n_experts):
    t = route_t.shape[1]
    blk = EXPERT_BLOCK
    cnt = counts[N_GROUPS:N_GROUPS + n_experts, 0].astype(jnp.int32)
    padded = (cnt + blk - 1) // blk * blk
    pad_end = jnp.cumsum(padded)
    pad_start = pad_end - padded
    ids = jnp.arange(n_experts, dtype=jnp.int32)

    def sorted_row(e_slot, r_slot):
        e = route_t[e_slot].astype(jnp.int32)
        start = jnp.sum(jnp.where(ids[:, None] == e[None, :], pad_start[:, None], 0), axis=0)
        return start + route_t[r_slot].astype(jnp.int32)

    dest = jnp.concatenate([sorted_row(_E1, _R1), sorted_row(_E2, _R2)])
    n_blocks = (2 * t + n_experts * blk) // blk
    block_start = jnp.arange(n_blocks, dtype=jnp.int32) * blk
    block_expert = jnp.minimum(jnp.sum(pad_end[None, :] <= block_start[:, None], axis=1),
                               n_experts - 1).astype(jnp.int32)
    n_used = (pad_end[-1:] // blk).astype(jnp.int32)
    seg = jnp.concatenate([pad_start + cnt, pad_end, n_used]).astype(jnp.int32)
    return dest, block_expert, seg


def _sc_mesh():
    return plsc.VectorSubcoreMesh(core_axis_name="c", subcore_axis_name="s",
                                  num_cores=SC_CORES, num_subcores=SC_SUBCORES)


def _sc_worker_chunks(t, chunk):
    chunks = t // (SC_CORES * SC_SUBCORES * chunk)
    wid = lax.axis_index("c") * SC_SUBCORES + lax.axis_index("s")
    return wid * chunks, chunks


def _sc_dispatch(x1t, route, dest, n_rows):
    t, per, lanes = x1t.shape
    c = SC_CHUNK

    def body(x_hbm, r_hbm, d_hbm, xo_hbm, ro_hbm, xbuf, rbuf, idx1, idx2, lsem, ssem):
        first, chunks = _sc_worker_chunks(t, c)
        pltpu.sync_copy(d_hbm.at[pl.ds(first * c, chunks * c)], idx1)
        pltpu.sync_copy(d_hbm.at[pl.ds(t + first * c, chunks * c)], idx2)

        def loads(j, slot):
            rows = pl.ds((first + j) * c, c)
            return (pltpu.make_async_copy(x_hbm.at[rows], xbuf.at[slot], lsem.at[2 * slot]),
                    pltpu.make_async_copy(r_hbm.at[rows], rbuf.at[slot], lsem.at[2 * slot + 1]))

        for cp in loads(0, 0):
            cp.start()

        @pl.loop(0, chunks)
        def _(j):
            slot = j % 2

            @pl.when(j + 1 < chunks)
            def _():
                for cp in loads(j + 1, 1 - slot):
                    cp.start()

            for cp in loads(j, slot):
                cp.wait()
            i1 = idx1.at[pl.ds(j * c, c)]
            i2 = idx2.at[pl.ds(j * c, c)]
            scatters = (pltpu.make_async_copy(xbuf.at[slot], xo_hbm.at[i1], ssem.at[0]),
                        pltpu.make_async_copy(xbuf.at[slot], xo_hbm.at[i2], ssem.at[1]),
                        pltpu.make_async_copy(rbuf.at[slot], ro_hbm.at[i1], ssem.at[2]),
                        pltpu.make_async_copy(rbuf.at[slot], ro_hbm.at[i2], ssem.at[3]))
            for cp in scatters:
                cp.start()
            for cp in scatters:
                cp.wait()

    per_worker = t // (SC_CORES * SC_SUBCORES)
    return pl.kernel(
        body,
        out_type=(jax.ShapeDtypeStruct((n_rows, per, lanes), x1t.dtype),
                  jax.ShapeDtypeStruct((n_rows, lanes), route.dtype)),
        mesh=_sc_mesh(),
        scratch_types=[pltpu.VMEM((2, c, per, lanes), x1t.dtype),
                       pltpu.VMEM((2, c, lanes), route.dtype),
                       pltpu.VMEM((per_worker,), jnp.int32),
                       pltpu.VMEM((per_worker,), jnp.int32),
                       pltpu.SemaphoreType.DMA((4,)),
                       pltpu.SemaphoreType.DMA((4,))],
        name="sc_dispatch",
    )(x1t, route, dest)


def _sc_combine(y_rows, dest, t):
    _, per, lanes = y_rows.shape
    c = SC_CHUNK

    def body(y_hbm, d_hbm, oa_hbm, ob_hbm, buf, idx1, idx2, gsem, osem):
        first, chunks = _sc_worker_chunks(t, c)
        pltpu.sync_copy(d_hbm.at[pl.ds(first * c, chunks * c)], idx1)
        pltpu.sync_copy(d_hbm.at[pl.ds(t + first * c, chunks * c)], idx2)

        def stores(j, slot):
            rows = pl.ds((first + j) * c, c)
            return (pltpu.make_async_copy(buf.at[slot, 0], oa_hbm.at[rows], osem.at[2 * slot]),
                    pltpu.make_async_copy(buf.at[slot, 1], ob_hbm.at[rows], osem.at[2 * slot + 1]))

        @pl.loop(0, chunks)
        def _(j):
            slot = j % 2

            @pl.when(j >= 2)
            def _():
                for cp in stores(j - 2, slot):
                    cp.wait()

            gathers = (pltpu.make_async_copy(y_hbm.at[idx1.at[pl.ds(j * c, c)]], buf.at[slot, 0], gsem.at[0]),
                       pltpu.make_async_copy(y_hbm.at[idx2.at[pl.ds(j * c, c)]], buf.at[slot, 1], gsem.at[1]))
            for cp in gathers:
                cp.start()
            for cp in gathers:
                cp.wait()
            for cp in stores(j, slot):
                cp.start()

        for j in range(max(chunks - 2, 0), chunks):
            for cp in stores(j, j % 2):
                cp.wait()

    per_worker = t // (SC_CORES * SC_SUBCORES)
    out = jax.ShapeDtypeStruct((t, per, lanes), y_rows.dtype)
    return pl.kernel(
        body,
        out_type=(out, out),
        mesh=_sc_mesh(),
        scratch_types=[pltpu.VMEM((2, 2, c, per, lanes), y_rows.dtype),
                       pltpu.VMEM((per_worker,), jnp.int32),
                       pltpu.VMEM((per_worker,), jnp.int32),
                       pltpu.SemaphoreType.DMA((2,)),
                       pltpu.SemaphoreType.DMA((4,))],
        name="sc_combine",
    )(y_rows, dest)


def _expert_ffn(x_sorted, r_sorted, w_in, w_down, layer, block_expert, seg):
    n_rows, per, _ = x_sorted.shape
    _, n_exp, d, de2 = w_in.shape
    de = w_down.shape[2]
    blk = EXPERT_BLOCK
    n_blocks = block_expert.shape[0]

    def used(i, be, sg):
        return jnp.minimum(i, sg[2 * n_exp] - 1), 0

    y = pl.pallas_call(
        functools.partial(_expert_ffn_kernel, layer=layer),
        grid_spec=pltpu.PrefetchScalarGridSpec(
            num_scalar_prefetch=2,
            grid=(n_blocks,),
            in_specs=[pl.BlockSpec((blk * per, LANES), used),
                      pl.BlockSpec((blk, LANES), used),
                      pl.BlockSpec(memory_space=pl.ANY),
                      pl.BlockSpec(memory_space=pl.ANY)],
            out_specs=pl.BlockSpec((blk * per, LANES), used),
            scratch_shapes=[pltpu.VMEM((d, de2), _F32),
                            pltpu.VMEM((de, d), _F32),
                            pltpu.SemaphoreType.DMA((2,)),
                            pltpu.VMEM((d, de2), _BF16),
                            pltpu.VMEM((de, d), _BF16)]),
        out_shape=jax.ShapeDtypeStruct((n_rows * per, LANES), jnp.uint32),
        compiler_params=_params("arbitrary"),
        name="expert_ffn",
    )(block_expert, seg, x_sorted.reshape(n_rows * per, LANES), r_sorted, w_in, w_down)
    return y.reshape(n_rows, per, LANES)


def _combine(x1, ya, yb, ln_g, ln_b, *, alpha):
    t, d = x1.shape
    tm = ROW_TILE
    per = ya.shape[1]
    row = pl.BlockSpec((tm, d), lambda i: (i, 0))
    words = pl.BlockSpec((tm * per, LANES), lambda i: (i, 0))
    return pl.pallas_call(
        functools.partial(_combine_kernel, alpha=alpha),
        grid=(t // tm,),
        in_specs=[row, words, words, _full((1, d)), _full((1, d))],
        out_specs=row,
        out_shape=jax.ShapeDtypeStruct((t, d), _F32),
        compiler_params=_params("arbitrary"),
        name="combine_ln",
    )(x1, ya.reshape(t * per, LANES), yb.reshape(t * per, LANES), ln_g[None, :], ln_b[None, :])


def _combine_qkv(x1, ya, yb, ln_g, ln_b, w_k, w_v, w_q, *, alpha, q_scale):
    t, d = x1.shape
    tm = ROW_TILE
    per = ya.shape[1]
    row = pl.BlockSpec((tm, d), lambda i: (i, 0))
    words = pl.BlockSpec((tm * per, LANES), lambda i: (i, 0))
    return pl.pallas_call(
        functools.partial(_combine_qkv_kernel, alpha=alpha, q_scale=q_scale),
        grid=(t // tm,),
        in_specs=[row, words, words, _full((1, d)), _full((1, d)),
                  _full((d, d)), _full((d, d)), _full((d, d))],
        out_specs=(row, row, row, row),
        out_shape=(jax.ShapeDtypeStruct((t, d), _F32),
                   jax.ShapeDtypeStruct((t, d), _BF16),
                   jax.ShapeDtypeStruct((t, d), _BF16),
                   jax.ShapeDtypeStruct((t, d), _BF16)),
        compiler_params=_params("arbitrary"),
        name="combine_ln_qkv",
    )(x1, ya.reshape(t * per, LANES), yb.reshape(t * per, LANES), ln_g[None, :], ln_b[None, :],
      w_k, w_v, w_q)


def _diff_attention(q, k, v, lam, subln_g, *, batch, seq, lam_init):
    t, d = q.shape
    tq = ATTN_BLOCK
    hw = d // N_HEADS
    head = pl.BlockSpec((seq, ATTN_HEADS * hw), lambda b, h: (b, h))
    return pl.pallas_call(
        functools.partial(_diff_attn_kernel, lam_init=lam_init),
        grid=(batch, N_HEADS // ATTN_HEADS),
        in_specs=[_full(lam.shape), _full((1, hw)), head, head, head],
        out_specs=head,
        out_shape=jax.ShapeDtypeStruct((t, d), _BF16),
        scratch_shapes=[pltpu.VMEM((1, LANES), _F32),
                        pltpu.VMEM((ATTN_BUFFERS, 2 * tq, seq), _F32),
                        pltpu.VMEM((ATTN_BUFFERS, 2 * tq, seq), _BF16)],
        compiler_params=_params("arbitrary", "arbitrary"),
        name="diff_attention",
    )(lam, subln_g[None, :], q, k, v)


def _moe(x1t, route, route_t, counts, w_in, w_down, layer):
    n_exp, d = w_in.shape[1:3]
    dest, block_expert, seg = _dispatch_plan(route_t, counts, n_exp)
    x1t = x1t.reshape(-1, d // (2 * LANES), LANES)
    x_sorted, r_sorted = _sc_dispatch(x1t, route, dest, block_expert.shape[0] * EXPERT_BLOCK)
    y_rows = _expert_ffn(x_sorted, r_sorted, w_in, w_down, layer, block_expert, seg)
    return _sc_combine(y_rows, dest, x1t.shape[0])


def kernel(x, a_w_in, a_conv_w, a_w_out, kv_w, b_w_q, b_lambda, b_subln_g, b_w_o,
           ln1_g, ln1_b, ln2_g, ln2_b, rg_w, rg_b, re_w, re_b, e_w_in, e_w_down):
    batch, seq, d = x.shape
    depth = ln1_g.shape[0]
    assert depth == 2 and a_w_in.shape[0] == 1 and b_w_q.shape[0] == 1
    assert seq % MIXER_TILE == 0 and seq % ROW_TILE == 0 and seq % ATTN_BLOCK == 0
    t = batch * seq
    assert t % (SC_CORES * SC_SUBCORES * SC_CHUNK) == 0
    alpha = (2.0 * depth) ** 0.25
    epg = re_w.shape[2] // N_GROUPS
    head_dim = d // (2 * N_HEADS)
    xf = x.reshape(t, d)

    router0 = _router_weights(rg_w[0], rg_b[0], re_w[0], re_b[0])
    x1, x1t, route, route_t, counts = _conv_mixer(xf, a_w_in[0].astype(_BF16), a_conv_w[0],
                                                  a_w_out[0].astype(_BF16), ln1_g[0], ln1_b[0], router0,
                                                  seq=seq, alpha=alpha, epg=epg)
    ya, yb = _moe(x1t, route, route_t, counts, e_w_in, e_w_down, 0)
    q_scale = head_dim ** -0.5 * math.log2(math.e)
    x2, k, v, q = _combine_qkv(x1, ya, yb, ln2_g[0], ln2_b[0],
                               kv_w[:, :d].astype(_BF16), kv_w[:, d:].astype(_BF16),
                               b_w_q[0].astype(_BF16), alpha=alpha, q_scale=q_scale)

    lam_init = 0.8 - 0.6 * math.exp(-0.3 * 1)
    o = _diff_attention(q, k, v, b_lambda[0], b_subln_g[0], batch=batch, seq=seq, lam_init=lam_init)
    router1 = _router_weights(rg_w[1], rg_b[1], re_w[1], re_b[1])
    x1, x1t, route, route_t, counts = _attn_out(x2, o, b_w_o[0].astype(_BF16), ln1_g[1], ln1_b[1],
                                                router1, alpha=alpha, epg=epg)
    ya, yb = _moe(x1t, route, route_t, counts, e_w_in, e_w_down, 1)
    out = _combine(x1, ya, yb, ln2_g[1], ln2_b[1], alpha=alpha)
    return out.reshape(batch, seq, d)
```

```python
import functools
import math

import jax
import jax.numpy as jnp
from jax import lax
from jax.experimental import pallas as pl
from jax.experimental.pallas import tpu as pltpu
from jax.experimental.pallas import tpu_sc as plsc

N_HEADS = 8
N_GROUPS = 4
LN_EPS = 1e-5
RMS_EPS = 1e-5

LANES = 128
SUBLANES = 8
VMEM_LIMIT_BYTES = 56 * 1024 * 1024

SC_CORES = 2
SC_SUBCORES = 16

SC_CHUNK = 32
MIXER_TILE = 1024
ROW_TILE = 1024
EXPERT_BLOCK = 1024
EXPERT_SUB = 256
ATTN_BLOCK = 256
ATTN_HEADS = 2
ATTN_BUFFERS = 4

_E1, _E2, _R1, _R2, _G1, _G2 = range(6)

_F32 = jnp.float32
_BF16 = jnp.bfloat16
_NT = (((1,), (1,)), ((), ()))


def _dot(a, b):
    return jnp.dot(a, b, preferred_element_type=_F32)


def _layer_norm(z, g, b):
    mu = jnp.mean(z, axis=-1, keepdims=True)
    d = z - mu
    var = jnp.mean(d * d, axis=-1, keepdims=True)
    return d * lax.rsqrt(var + LN_EPS) * g + b


def _route_tail(x1, wt_ref, bt_ref, route_ref, routet_ref, counts_ref, umat_ref, *, epg):
    tm = x1.shape[0]
    nr = counts_ref.shape[0]
    half = wt_ref.shape[0] // 2

    @pl.when(pl.program_id(0) == 0)
    def _():
        counts_ref[...] = jnp.zeros_like(counts_ref)
        before = (lax.broadcasted_iota(jnp.int32, (tm, tm), 0) < lax.broadcasted_iota(jnp.int32, (tm, tm), 1))
        umat_ref[...] = jnp.where(before, 1.0, 0.0).astype(_BF16)

    xh = x1.astype(_BF16)
    xl = (x1 - xh.astype(_F32)).astype(_BF16)
    a = lax.dot_general(wt_ref[...], xh, _NT, preferred_element_type=_F32)
    b = lax.dot_general(wt_ref[0:half, :], xl, _NT, preferred_element_type=_F32)
    logits = a[0:nr, :] + a[half:half + nr, :] + b[0:nr, :] + bt_ref[0:nr, :]

    rowf = lax.broadcasted_iota(jnp.int32, (nr, tm), 0).astype(_F32)
    neg = -jnp.inf
    big = float(nr)
    gl = jnp.where(rowf < float(N_GROUPS), logits, neg)
    gmax = jnp.max(gl, axis=0, keepdims=True)
    gidx = jnp.min(jnp.where(gl == gmax, rowf, big), axis=0, keepdims=True)
    gtop = 1.0 / jnp.sum(jnp.exp(gl - gmax), axis=0, keepdims=True)

    lo = float(N_GROUPS) + gidx * float(epg)
    el = jnp.where((rowf >= lo) & (rowf < lo + float(epg)), logits, neg)
    m1 = jnp.max(el, axis=0, keepdims=True)
    i1 = jnp.min(jnp.where(el == m1, rowf, big), axis=0, keepdims=True)
    el2 = jnp.where(rowf == i1, neg, el)
    m2 = jnp.max(el2, axis=0, keepdims=True)
    i2 = jnp.min(jnp.where(el2 == m2, rowf, big), axis=0, keepdims=True)
    w2 = jnp.exp(m2 - m1)
    inv = 1.0 / (1.0 + w2)
    g1 = gtop * inv
    g2 = gtop * w2 * inv

    onehot = jnp.where((rowf == i1) | (rowf == i2), 1.0, 0.0)
    total = _dot(onehot.astype(_BF16), umat_ref[...]) + counts_ref[...]
    r1 = jnp.sum(jnp.where(rowf == i1, total, 0.0), axis=0, keepdims=True)
    r2 = jnp.sum(jnp.where(rowf == i2, total, 0.0), axis=0, keepdims=True)
    counts_ref[...] += jnp.sum(onehot, axis=1, keepdims=True)

    slot = lax.broadcasted_iota(jnp.int32, (SUBLANES, tm), 0)
    rec = jnp.zeros((SUBLANES, tm), _F32)
    fields = ((_E1, i1 - float(N_GROUPS)), (_E2, i2 - float(N_GROUPS)), (_R1, r1), (_R2, r2), (_G1, g1), (_G2, g2))
    for s, val in fields:
        rec = jnp.where(slot == s, val, rec)
    routet_ref[...] = rec
    rec = jnp.concatenate([rec, jnp.zeros((LANES - SUBLANES, tm), _F32)], axis=0)
    route_ref[...] = rec.T


def _store_token_words(ref, x):
    n, d = x.shape
    rows = d // (2 * LANES)
    bits = lax.bitcast_convert_type(x.astype(_BF16).astype(_F32), jnp.uint32)
    for s in range(rows):
        lo = bits[:, s * LANES:(s + 1) * LANES] >> 16
        hi = bits[:, d // 2 + s * LANES:d // 2 + (s + 1) * LANES] & jnp.uint32(0xFFFF0000)
        ref[pl.ds(s, n, stride=rows), :] = lo | hi


def _load_token_words(ref, n, d):
    rows = d // (2 * LANES)
    words = [ref[pl.ds(s, n, stride=rows), :] for s in range(rows)]
    lo = [lax.bitcast_convert_type(w << 16, _F32) for w in words]
    hi = [lax.bitcast_convert_type(w & jnp.uint32(0xFFFF0000), _F32) for w in words]
    return jnp.concatenate(lo + hi, axis=1)


def _conv_mixer_kernel(x_ref, win_ref, cw_ref, wout_ref, g_ref, b_ref, wt_ref, bt_ref,
                       x1_ref, x1t_ref, route_ref, routet_ref, counts_ref, ubuf_ref, umat_ref,
                       *, tiles_per_seq, alpha, epg):
    i = pl.program_id(0)
    tm, d = x_ref.shape

    @pl.when(i % tiles_per_seq == 0)
    def _():
        ubuf_ref[0:SUBLANES, :] = jnp.zeros((SUBLANES, d), _F32)

    x = x_ref[...]
    h = _dot(x.astype(_BF16), win_ref[...])
    u = h[:, d:2 * d] * h[:, 2 * d:]
    ubuf_ref[SUBLANES:SUBLANES + tm, :] = u
    cw = cw_ref[...]
    uc = (cw[0:1, :] * ubuf_ref[SUBLANES - 2:SUBLANES - 2 + tm, :]
          + cw[1:2, :] * ubuf_ref[SUBLANES - 1:SUBLANES - 1 + tm, :]
          + cw[2:3, :] * u)
    ubuf_ref[0:SUBLANES, :] = ubuf_ref[tm:tm + SUBLANES, :]
    y = _dot((h[:, :d] * uc).astype(_BF16), wout_ref[...])
    x1 = _layer_norm(alpha * x + y, g_ref[...], b_ref[...])
    x1_ref[...] = x1
    _store_token_words(x1t_ref, x1)
    _route_tail(x1, wt_ref, bt_ref, route_ref, routet_ref, counts_ref, umat_ref, epg=epg)


def _attn_out_kernel(x_ref, o_ref, wo_ref, g_ref, b_ref, wt_ref, bt_ref,
                     x1_ref, x1t_ref, route_ref, routet_ref, counts_ref, umat_ref, *, alpha, epg):
    y = _dot(o_ref[...], wo_ref[...])
    x1 = _layer_norm(alpha * x_ref[...] + y, g_ref[...], b_ref[...])
    x1_ref[...] = x1
    _store_token_words(x1t_ref, x1)
    _route_tail(x1, wt_ref, bt_ref, route_ref, routet_ref, counts_ref, umat_ref, epg=epg)


def _expert_ffn_kernel(bexp_ref, seg_ref, x_ref, rec_ref, win_hbm, wdn_hbm, y_ref,
                       wstage_in, wstage_dn, wsem, winb, wdnb, *, layer):
    i = pl.program_id(0)
    n_exp = (seg_ref.shape[0] - 1) // 2
    n_used = seg_ref[2 * n_exp]
    d, de2 = wstage_in.shape
    blk = rec_ref.shape[0]
    de = de2 // 2

    def weight_copies(e):
        return (pltpu.make_async_copy(win_hbm.at[layer, e], wstage_in, wsem.at[0]),
                pltpu.make_async_copy(wdn_hbm.at[layer, e], wstage_dn, wsem.at[1]))

    @pl.when(i == 0)
    def _():
        for c in weight_copies(bexp_ref[0]):
            c.start()

    @pl.when(i < n_used)
    def _():
        e = bexp_ref[i]

        @pl.when((i == 0) | (e != bexp_ref[jnp.maximum(i - 1, 0)]))
        def _():
            for c in weight_copies(e):
                c.wait()
            winb[...] = wstage_in[...].astype(_BF16)
            wdnb[...] = wstage_dn[...].astype(_BF16)
            nxt = seg_ref[n_exp + e] // blk

            @pl.when(nxt < n_used)
            def _():
                for c in weight_copies(bexp_ref[nxt]):
                    c.start()

        n_valid = seg_ref[e] - i * blk

        def ffn(rows):
            valid = lax.broadcasted_iota(jnp.int32, (rows, 1), 0) < n_valid
            x = jnp.where(valid, _load_token_words(x_ref, rows, d), 0.0)
            h = _dot(x.astype(_BF16), winb[...])
            g = h[:, :de]
            a = g * jax.nn.sigmoid(g) * h[:, de:]
            rec = rec_ref[0:rows, :]
            gate = jnp.where(rec[:, _E1:_E1 + 1] == e.astype(_F32), rec[:, _G1:_G1 + 1], rec[:, _G2:_G2 + 1])
            gate = jnp.where(valid, gate, 0.0)
            _store_token_words(y_ref, _dot(a.astype(_BF16), wdnb[...]) * gate)

        per = y_ref.shape[0] // blk
        for rows in range(EXPERT_SUB, blk + 1, EXPERT_SUB):
            @pl.when((n_valid > rows - EXPERT_SUB) & ((n_valid <= rows) | (rows == blk)))
            def _(rows=rows):
                ffn(rows)
                if rows < blk:
                    y_ref[rows * per:, :] = jnp.zeros(((blk - rows) * per, y_ref.shape[1]), y_ref.dtype)


def _combine_body(x1_ref, ya_ref, yb_ref, g_ref, b_ref, *, alpha):
    tm, d = x1_ref.shape
    ffn = _load_token_words(ya_ref, tm, d) + _load_token_words(yb_ref, tm, d)
    return _layer_norm(alpha * x1_ref[...] + ffn, g_ref[...], b_ref[...])


def _combine_kernel(x1_ref, ya_ref, yb_ref, g_ref, b_ref, x2_ref, *, alpha):
    x2_ref[...] = _combine_body(x1_ref, ya_ref, yb_ref, g_ref, b_ref, alpha=alpha)


def _combine_qkv_kernel(x1_ref, ya_ref, yb_ref, g_ref, b_ref, wk_ref, wv_ref, wq_ref,
                        x2_ref, k_ref, v_ref, q_ref, *, alpha, q_scale):
    x2 = _combine_body(x1_ref, ya_ref, yb_ref, g_ref, b_ref, alpha=alpha)
    x2_ref[...] = x2
    xb = x2.astype(_BF16)
    k_ref[...] = _dot(xb, wk_ref[...]).astype(_BF16)
    q_ref[...] = (_dot(xb, wq_ref[...]) * q_scale).astype(_BF16)
    v_ref[...] = _dot(xb, wv_ref[...]).astype(_BF16)


def _diff_attn_kernel(lam_ref, g_ref, q_ref, k_ref, v_ref, o_ref, lamfull_ref, s_ref, p_ref, *, lam_init):
    bi, hi = pl.program_id(0), pl.program_id(1)
    seq = k_ref.shape[0]
    hw = g_ref.shape[1]
    heads = k_ref.shape[1] // hw
    tq = tk = ATTN_BLOCK
    nk = seq // tk
    hd = lam_ref.shape[1]

    @pl.when((bi == 0) & (hi == 0))
    def _():
        lam = lam_ref[...]
        a = jnp.sum(lam[0:1, :] * lam[1:2, :], axis=-1, keepdims=True)
        b = jnp.sum(lam[2:3, :] * lam[3:4, :], axis=-1, keepdims=True)
        lamfull_ref[...] = jnp.broadcast_to(jnp.exp(a) - jnp.exp(b) + lam_init, lamfull_ref.shape)

    lam_full = lamfull_ref[0:1, 0:1]
    lane = lax.broadcasted_iota(jnp.int32, (tq, hw), 1)
    row = lax.broadcasted_iota(jnp.int32, (2 * tq, tk), 0)
    col = lax.broadcasted_iota(jnp.int32, (2 * tq, tk), 1)
    causal = col <= jnp.where(row < tq, row, row - tq)
    ones = jnp.ones((seq, hw), _BF16)

    step = 0
    for i in reversed(range(nk)):
        for h in range(heads):
            cols = slice(h * hw, (h + 1) * hw)
            par = step % s_ref.shape[0]
            step += 1
            kv = (i + 1) * tk
            q = q_ref[i * tq:(i + 1) * tq, cols]
            zero = jnp.zeros_like(q)
            qcat = jnp.concatenate([jnp.where(lane < hd, q, zero), jnp.where(lane >= hd, q, zero)], axis=0)
            s_ref[par, :, 0:kv] = lax.dot_general(qcat, k_ref[0:kv, cols], _NT, preferred_element_type=_F32)
            s_ref[par, :, i * tk:kv] = jnp.where(causal, s_ref[par, :, i * tk:kv], -jnp.inf)
            rowmax = jnp.max(s_ref[par, :, 0:kv], axis=1, keepdims=True)
            p_ref[par, :, 0:kv] = jnp.exp2(s_ref[par, :, 0:kv] - rowmax).astype(_BF16)
            v_aug = jnp.concatenate([v_ref[0:kv, cols], ones[0:kv, :]], axis=1)
            acc = _dot(p_ref[par, :, 0:kv], v_aug)
            o = acc[:, 0:hw] * (1.0 / acc[:, hw:hw + 1])
            od = o[0:tq, :] - lam_full * o[tq:, :]
            od = od * lax.rsqrt(jnp.mean(od * od, axis=1, keepdims=True) + RMS_EPS)
            o_ref[i * tq:(i + 1) * tq, cols] = (od * g_ref[...] * (1.0 - lam_init)).astype(o_ref.dtype)


def _params(*sem):
    return pltpu.CompilerParams(dimension_semantics=sem, vmem_limit_bytes=VMEM_LIMIT_BYTES)


def _full(shape):
    return pl.BlockSpec(shape, lambda *_: (0,) * len(shape))


def _router_weights(rg_w, rg_b, re_w, re_b):
    d = rg_w.shape[0]
    n = rg_w.shape[1] + re_w.shape[1]
    w = jnp.concatenate([rg_w, re_w, jnp.zeros((d, LANES - n), _F32)], axis=1).T
    b = jnp.concatenate([rg_b, re_b, jnp.zeros((LANES - n,), _F32)])[:, None]
    wh = w.astype(_BF16)
    wl = (w - wh.astype(_F32)).astype(_BF16)
    return jnp.concatenate([wh, wl], axis=0), b


def _router_rows(n_experts):
    return -(-(N_GROUPS + n_experts) // (2 * SUBLANES)) * (2 * SUBLANES)


def _route_outs(t, d, tm, nr):
    per = d // LANES
    shapes = (jax.ShapeDtypeStruct((t, d), _F32),
              jax.ShapeDtypeStruct((t * per // 2, LANES), jnp.uint32),
              jax.ShapeDtypeStruct((t, LANES), _F32),
              jax.ShapeDtypeStruct((SUBLANES, t), _F32),
              jax.ShapeDtypeStruct((nr, 1), _F32))
    specs = (pl.BlockSpec((tm, d), lambda i: (i, 0)),
             pl.BlockSpec((tm * per // 2, LANES), lambda i: (i, 0)),
             pl.BlockSpec((tm, LANES), lambda i: (i, 0)),
             pl.BlockSpec((SUBLANES, tm), lambda i: (0, i)),
             pl.BlockSpec((nr, 1), lambda i: (0, 0)))
    return shapes, specs


def _conv_mixer(x, w_in, conv_w, w_out, ln_g, ln_b, router, *, seq, alpha, epg):
    t, d = x.shape
    tm = MIXER_TILE
    wt, bt = router
    shapes, specs = _route_outs(t, d, tm, _router_rows(N_GROUPS * epg))
    return pl.pallas_call(
        functools.partial(_conv_mixer_kernel, tiles_per_seq=seq // tm, alpha=alpha, epg=epg),
        grid=(t // tm,),
        in_specs=[pl.BlockSpec((tm, d), lambda i: (i, 0)),
                  _full(w_in.shape), _full(conv_w.shape), _full(w_out.shape),
                  _full((1, d)), _full((1, d)), _full(wt.shape), _full(bt.shape)],
        out_specs=specs, out_shape=shapes,
        scratch_shapes=[pltpu.VMEM((tm + SUBLANES, d), _F32), pltpu.VMEM((tm, tm), _BF16)],
        compiler_params=_params("arbitrary"),
        name="conv_mixer_ln_router",
    )(x, w_in, conv_w, w_out, ln_g[None, :], ln_b[None, :], wt, bt)


def _attn_out(x, o, w_o, ln_g, ln_b, router, *, alpha, epg):
    t, d = x.shape
    tm = MIXER_TILE
    wt, bt = router
    shapes, specs = _route_outs(t, d, tm, _router_rows(N_GROUPS * epg))
    return pl.pallas_call(
        functools.partial(_attn_out_kernel, alpha=alpha, epg=epg),
        grid=(t // tm,),
        in_specs=[pl.BlockSpec((tm, d), lambda i: (i, 0)),
                  pl.BlockSpec((tm, d), lambda i: (i, 0)),
                  _full(w_o.shape), _full((1, d)), _full((1, d)), _full(wt.shape), _full(bt.shape)],
        out_specs=specs, out_shape=shapes,
        scratch_shapes=[pltpu.VMEM((tm, tm), _BF16)],
        compiler_params=_params("arbitrary"),
        name="attn_out_ln_router",
    )(x, o, w_o, ln_g[None, :], ln_b[None, :], wt, bt)


def _dispatch_plan(route_t, counts, n_experts):
    t = route_t.shape[1]
    blk = EXPERT_BLOCK
    cnt = counts[N_GROUPS:N_GROUPS + n_experts, 0].astype(jnp.int32)
    padded = (cnt + blk - 1) // blk * blk
    pad_end = jnp.cumsum(padded)
    pad_start = pad_end - padded
    ids = jnp.arange(n_experts, dtype=jnp.int32)

    def sorted_row(e_slot, r_slot):
        e = route_t[e_slot].astype(jnp.int32)
        start = jnp.sum(jnp.where(ids[:, None] == e[None, :], pad_start[:, None], 0), axis=0)
        return start + route_t[r_slot].astype(jnp.int32)

    dest = jnp.concatenate([sorted_row(_E1, _R1), sorted_row(_E2, _R2)])
    n_blocks = (2 * t + n_experts * blk) // blk
    block_start = jnp.arange(n_blocks, dtype=jnp.int32) * blk
    block_expert = jnp.minimum(jnp.sum(pad_end[None, :] <= block_start[:, None], axis=1),
                               n_experts - 1).astype(jnp.int32)
    n_used = (pad_end[-1:] // blk).astype(jnp.int32)
    seg = jnp.concatenate([pad_start + cnt, pad_end, n_used]).astype(jnp.int32)
    return dest, block_expert, seg


def _sc_mesh():
    return plsc.VectorSubcoreMesh(core_axis_name="c", subcore_axis_name="s",
                                  num_cores=SC_CORES, num_subcores=SC_SUBCORES)


def _sc_worker_chunks(t, chunk):
    chunks = t // (SC_CORES * SC_SUBCORES * chunk)
    wid = lax.axis_index("c") * SC_SUBCORES + lax.axis_index("s")
    return wid * chunks, chunks


def _sc_dispatch(x1t, route, dest, n_rows):
    t, per, lanes = x1t.shape
    c = SC_CHUNK

    def body(x_hbm, r_hbm, d_hbm, xo_hbm, ro_hbm, xbuf, rbuf, idx1, idx2, lsem, ssem):
        first, chunks = _sc_worker_chunks(t, c)
        pltpu.sync_copy(d_hbm.at[pl.ds(first * c, chunks * c)], idx1)
        pltpu.sync_copy(d_hbm.at[pl.ds(t + first * c, chunks * c)], idx2)

        def loads(j, slot):
            rows = pl.ds((first + j) * c, c)
            return (pltpu.make_async_copy(x_hbm.at[rows], xbuf.at[slot], lsem.at[2 * slot]),
                    pltpu.make_async_copy(r_hbm.at[rows], rbuf.at[slot], lsem.at[2 * slot + 1]))

        for cp in loads(0, 0):
            cp.start()

        @pl.loop(0, chunks)
        def _(j):
            slot = j % 2

            @pl.when(j + 1 < chunks)
            def _():
                for cp in loads(j + 1, 1 - slot):
                    cp.start()

            for cp in loads(j, slot):
                cp.wait()
            i1 = idx1.at[pl.ds(j * c, c)]
            i2 = idx2.at[pl.ds(j * c, c)]
            scatters = (pltpu.make_async_copy(xbuf.at[slot], xo_hbm.at[i1], ssem.at[0]),
                        pltpu.make_async_copy(xbuf.at[slot], xo_hbm.at[i2], ssem.at[1]),
                        pltpu.make_async_copy(rbuf.at[slot], ro_hbm.at[i1], ssem.at[2]),
                        pltpu.make_async_copy(rbuf.at[slot], ro_hbm.at[i2], ssem.at[3]))
            for cp in scatters:
                cp.start()
            for cp in scatters:
                cp.wait()

    per_worker = t // (SC_CORES * SC_SUBCORES)
    return pl.kernel(
        body,
        out_type=(jax.ShapeDtypeStruct((n_rows, per, lanes), x1t.dtype),
                  jax.ShapeDtypeStruct((n_rows, lanes), route.dtype)),
        mesh=_sc_mesh(),
        scratch_types=[pltpu.VMEM((2, c, per, lanes), x1t.dtype),
                       pltpu.VMEM((2, c, lanes), route.dtype),
                       pltpu.VMEM((per_worker,), jnp.int32),
                       pltpu.VMEM((per_worker,), jnp.int32),
                       pltpu.SemaphoreType.DMA((4,)),
                       pltpu.SemaphoreType.DMA((4,))],
        name="sc_dispatch",
    )(x1t, route, dest)


def _sc_combine(y_rows, dest, t):
    _, per, lanes = y_rows.shape
    c = SC_CHUNK

    def body(y_hbm, d_hbm, oa_hbm, ob_hbm, buf, idx1, idx2, gsem, osem):
        first, chunks = _sc_worker_chunks(t, c)
        pltpu.sync_copy(d_hbm.at[pl.ds(first * c, chunks * c)], idx1)
        pltpu.sync_copy(d_hbm.at[pl.ds(t + first * c, chunks * c)], idx2)

        def stores(j, slot):
            rows = pl.ds((first + j) * c, c)
            return (pltpu.make_async_copy(buf.at[slot, 0], oa_hbm.at[rows], osem.at[2 * slot]),
                    pltpu.make_async_copy(buf.at[slot, 1], ob_hbm.at[rows], osem.at[2 * slot + 1]))

        @pl.loop(0, chunks)
        def _(j):
            slot = j % 2

            @pl.when(j >= 2)
            def _():
                for cp in stores(j - 2, slot):
                    cp.wait()

            gathers = (pltpu.make_async_copy(y_hbm.at[idx1.at[pl.ds(j * c, c)]], buf.at[slot, 0], gsem.at[0]),
                       pltpu.make_async_copy(y_hbm.at[idx2.at[pl.ds(j * c, c)]], buf.at[slot, 1], gsem.at[1]))
            for cp in gathers:
                cp.start()
            for cp in gathers:
                cp.wait()
            for cp in stores(j, slot):
                cp.start()

        for j in range(max(chunks - 2, 0), chunks):
            for cp in stores(j, j % 2):
                cp.wait()

    per_worker = t // (SC_CORES * SC_SUBCORES)
    out = jax.ShapeDtypeStruct((t, per, lanes), y_rows.dtype)
    return pl.kernel(
        body,
        out_type=(out, out),
        mesh=_sc_mesh(),
        scratch_types=[pltpu.VMEM((2, 2, c, per, lanes), y_rows.dtype),
                       pltpu.VMEM((per_worker,), jnp.int32),
                       pltpu.VMEM((per_worker,), jnp.int32),
                       pltpu.SemaphoreType.DMA((2,)),
                       pltpu.SemaphoreType.DMA((4,))],
        name="sc_combine",
    )(y_rows, dest)


def _expert_ffn(x_sorted, r_sorted, w_in, w_down, layer, block_expert, seg):
    n_rows, per, _ = x_sorted.shape
    _, n_exp, d, de2 = w_in.shape
    de = w_down.shape[2]
    blk = EXPERT_BLOCK
    n_blocks = block_expert.shape[0]

    def used(i, be, sg):
        return jnp.minimum(i, sg[2 * n_exp] - 1), 0

    y = pl.pallas_call(
        functools.partial(_expert_ffn_kernel, layer=layer),
        grid_spec=pltpu.PrefetchScalarGridSpec(
            num_scalar_prefetch=2,
            grid=(n_blocks,),
            in_specs=[pl.BlockSpec((blk * per, LANES), used),
                      pl.BlockSpec((blk, LANES), used),
                      pl.BlockSpec(memory_space=pl.ANY),
                      pl.BlockSpec(memory_space=pl.ANY)],
            out_specs=pl.BlockSpec((blk * per, LANES), used),
            scratch_shapes=[pltpu.VMEM((d, de2), _F32),
                            pltpu.VMEM((de, d), _F32),
                            pltpu.SemaphoreType.DMA((2,)),
                            pltpu.VMEM((d, de2), _BF16),
                            pltpu.VMEM((de, d), _BF16)]),
        out_shape=jax.ShapeDtypeStruct((n_rows * per, LANES), jnp.uint32),
        compiler_params=_params("arbitrary"),
        name="expert_ffn",
    )(block_expert, seg, x_sorted.reshape(n_rows * per, LANES), r_sorted, w_in, w_down)
    return y.reshape(n_rows, per, LANES)


def _combine(x1, ya, yb, ln_g, ln_b, *, alpha):
    t, d = x1.shape
    tm = ROW_TILE
    per = ya.shape[1]
    row = pl.BlockSpec((tm, d), lambda i: (i, 0))
    words = pl.BlockSpec((tm * per, LANES), lambda i: (i, 0))
    return pl.pallas_call(
        functools.partial(_combine_kernel, alpha=alpha),
        grid=(t // tm,),
        in_specs=[row, words, words, _full((1, d)), _full((1, d))],
        out_specs=row,
        out_shape=jax.ShapeDtypeStruct((t, d), _F32),
        compiler_params=_params("arbitrary"),
        name="combine_ln",
    )(x1, ya.reshape(t * per, LANES), yb.reshape(t * per, LANES), ln_g[None, :], ln_b[None, :])


def _combine_qkv(x1, ya, yb, ln_g, ln_b, w_k, w_v, w_q, *, alpha, q_scale):
    t, d = x1.shape
    tm = ROW_TILE
    per = ya.shape[1]
    row = pl.BlockSpec((tm, d), lambda i: (i, 0))
    words = pl.BlockSpec((tm * per, LANES), lambda i: (i, 0))
    return pl.pallas_call(
        functools.partial(_combine_qkv_kernel, alpha=alpha, q_scale=q_scale),
        grid=(t // tm,),
        in_specs=[row, words, words, _full((1, d)), _full((1, d)),
                  _full((d, d)), _full((d, d)), _full((d, d))],
        out_specs=(row, row, row, row),
        out_shape=(jax.ShapeDtypeStruct((t, d), _F32),
                   jax.ShapeDtypeStruct((t, d), _BF16),
                   jax.ShapeDtypeStruct((t, d), _BF16),
                   jax.ShapeDtypeStruct((t, d), _BF16)),
        compiler_params=_params("arbitrary"),
        name="combine_ln_qkv",
    )(x1, ya.reshape(t * per, LANES), yb.reshape(t * per, LANES), ln_g[None, :], ln_b[None, :],
      w_k, w_v, w_q)


def _diff_attention(q, k, v, lam, subln_g, *, batch, seq, lam_init):
    t, d = q.shape
    tq = ATTN_BLOCK
    hw = d // N_HEADS
    head = pl.BlockSpec((seq, ATTN_HEADS * hw), lambda b, h: (b, h))
    return pl.pallas_call(
        functools.partial(_diff_attn_kernel, lam_init=lam_init),
        grid=(batch, N_HEADS // ATTN_HEADS),
        in_specs=[_full(lam.shape), _full((1, hw)), head, head, head],
        out_specs=head,
        out_shape=jax.ShapeDtypeStruct((t, d), _BF16),
        scratch_shapes=[pltpu.VMEM((1, LANES), _F32),
                        pltpu.VMEM((ATTN_BUFFERS, 2 * tq, seq), _F32),
                        pltpu.VMEM((ATTN_BUFFERS, 2 * tq, seq), _BF16)],
        compiler_params=_params("arbitrary", "arbitrary"),
        name="diff_attention",
    )(lam, subln_g[None, :], q, k, v)


def _moe(x1t, route, route_t, counts, w_in, w_down, layer):
    n_exp, d = w_in.shape[1:3]
    dest, block_expert, seg = _dispatch_plan(route_t, counts, n_exp)
    x1t = x1t.reshape(-1, d // (2 * LANES), LANES)
    x_sorted, r_sorted = _sc_dispatch(x1t, route, dest, block_expert.shape[0] * EXPERT_BLOCK)
    y_rows = _expert_ffn(x_sorted, r_sorted, w_in, w_down, layer, block_expert, seg)
    return _sc_combine(y_rows, dest, x1t.shape[0])


def kernel(x, a_w_in, a_conv_w, a_w_out, kv_w, b_w_q, b_lambda, b_subln_g, b_w_o,
           ln1_g, ln1_b, ln2_g, ln2_b, rg_w, rg_b, re_w, re_b, e_w_in, e_w_down):
    batch, seq, d = x.shape
    depth = ln1_g.shape[0]
    assert depth == 2 and a_w_in.shape[0] == 1 and b_w_q.shape[0] == 1
    assert seq % MIXER_TILE == 0 and seq % ROW_TILE == 0 and seq % ATTN_BLOCK == 0
    t = batch * seq
    assert t % (SC_CORES * SC_SUBCORES * SC_CHUNK) == 0
    alpha = (2.0 * depth) ** 0.25
    epg = re_w.shape[2] // N_GROUPS
    head_dim = d // (2 * N_HEADS)
    xf = x.reshape(t, d)

    router0 = _router_weights(rg_w[0], rg_b[0], re_w[0], re_b[0])
    x1, x1t, route, route_t, counts = _conv_mixer(xf, a_w_in[0].astype(_BF16), a_conv_w[0],
                                                  a_w_out[0].astype(_BF16), ln1_g[0], ln1_b[0], router0,
                                                  seq=seq, alpha=alpha, epg=epg)
    ya, yb = _moe(x1t, route, route_t, counts, e_w_in, e_w_down, 0)
    q_scale = head_dim ** -0.5 * math.log2(math.e)
    x2, k, v, q = _combine_qkv(x1, ya, yb, ln2_g[0], ln2_b[0],
                               kv_w[:, :d].astype(_BF16), kv_w[:, d:].astype(_BF16),
                               b_w_q[0].astype(_BF16), alpha=alpha, q_scale=q_scale)

    lam_init = 0.8 - 0.6 * math.exp(-0.3 * 1)
    o = _diff_attention(q, k, v, b_lambda[0], b_subln_g[0], batch=batch, seq=seq, lam_init=lam_init)
    router1 = _router_weights(rg_w[1], rg_b[1], re_w[1], re_b[1])
    x1, x1t, route, route_t, counts = _attn_out(x2, o, b_w_o[0].astype(_BF16), ln1_g[1], ln1_b[1],
                                                router1, alpha=alpha, epg=epg)
    ya, yb = _moe(x1t, route, route_t, counts, e_w_in, e_w_down, 1)
    out = _combine(x1, ya, yb, ln2_g[1], ln2_b[1], alpha=alpha)
    return out.reshape(batch, seq, d)
```

```python
import functools
import math

import jax
import jax.numpy as jnp
from jax import lax
from jax.experimental import pallas as pl
from jax.experimental.pallas import tpu as pltpu
from jax.experimental.pallas import tpu_sc as plsc

N_HEADS = 8
N_GROUPS = 4
LN_EPS = 1e-5
RMS_EPS = 1e-5

LANES = 128
SUBLANES = 8
VMEM_LIMIT_BYTES = 56 * 1024 * 1024

SC_CORES = 2
SC_SUBCORES = 16

SC_CHUNK = 32
MIXER_TILE = 1024
ROW_TILE = 1024
EXPERT_BLOCK = 512
ATTN_BLOCK = 256
ATTN_HEADS = 2
ATTN_BUFFERS = 4

_E1, _E2, _R1, _R2, _G1, _G2 = range(6)

_F32 = jnp.float32
_BF16 = jnp.bfloat16
_NT = (((1,), (1,)), ((), ()))


def _dot(a, b):
    return jnp.dot(a, b, preferred_element_type=_F32)


def _layer_norm(z, g, b):
    mu = jnp.mean(z, axis=-1, keepdims=True)
    d = z - mu
    var = jnp.mean(d * d, axis=-1, keepdims=True)
    return d * lax.rsqrt(var + LN_EPS) * g + b


def _route_tail(x1, wt_ref, bt_ref, route_ref, routet_ref, counts_ref, umat_ref, *, epg):
    tm = x1.shape[0]
    nr = counts_ref.shape[0]
    half = wt_ref.shape[0] // 2

    @pl.when(pl.program_id(0) == 0)
    def _():
        counts_ref[...] = jnp.zeros_like(counts_ref)
        before = (lax.broadcasted_iota(jnp.int32, (tm, tm), 0) < lax.broadcasted_iota(jnp.int32, (tm, tm), 1))
        umat_ref[...] = jnp.where(before, 1.0, 0.0).astype(_BF16)

    xh = x1.astype(_BF16)
    xl = (x1 - xh.astype(_F32)).astype(_BF16)
    a = lax.dot_general(wt_ref[...], xh, _NT, preferred_element_type=_F32)
    b = lax.dot_general(wt_ref[0:half, :], xl, _NT, preferred_element_type=_F32)
    logits = a[0:nr, :] + a[half:half + nr, :] + b[0:nr, :] + bt_ref[0:nr, :]

    rowf = lax.broadcasted_iota(jnp.int32, (nr, tm), 0).astype(_F32)
    neg = -jnp.inf
    big = float(nr)
    gl = jnp.where(rowf < float(N_GROUPS), logits, neg)
    gmax = jnp.max(gl, axis=0, keepdims=True)
    gidx = jnp.min(jnp.where(gl == gmax, rowf, big), axis=0, keepdims=True)
    gtop = 1.0 / jnp.sum(jnp.exp(gl - gmax), axis=0, keepdims=True)

    lo = float(N_GROUPS) + gidx * float(epg)
    el = jnp.where((rowf >= lo) & (rowf < lo + float(epg)), logits, neg)
    m1 = jnp.max(el, axis=0, keepdims=True)
    i1 = jnp.min(jnp.where(el == m1, rowf, big), axis=0, keepdims=True)
    el2 = jnp.where(rowf == i1, neg, el)
    m2 = jnp.max(el2, axis=0, keepdims=True)
    i2 = jnp.min(jnp.where(el2 == m2, rowf, big), axis=0, keepdims=True)
    w2 = jnp.exp(m2 - m1)
    inv = 1.0 / (1.0 + w2)
    g1 = gtop * inv
    g2 = gtop * w2 * inv

    onehot = jnp.where((rowf == i1) | (rowf == i2), 1.0, 0.0)
    total = _dot(onehot.astype(_BF16), umat_ref[...]) + counts_ref[...]
    r1 = jnp.sum(jnp.where(rowf == i1, total, 0.0), axis=0, keepdims=True)
    r2 = jnp.sum(jnp.where(rowf == i2, total, 0.0), axis=0, keepdims=True)
    counts_ref[...] += jnp.sum(onehot, axis=1, keepdims=True)

    slot = lax.broadcasted_iota(jnp.int32, (SUBLANES, tm), 0)
    rec = jnp.zeros((SUBLANES, tm), _F32)
    fields = ((_E1, i1 - float(N_GROUPS)), (_E2, i2 - float(N_GROUPS)), (_R1, r1), (_R2, r2), (_G1, g1), (_G2, g2))
    for s, val in fields:
        rec = jnp.where(slot == s, val, rec)
    routet_ref[...] = rec
    rec = jnp.concatenate([rec, jnp.zeros((LANES - SUBLANES, tm), _F32)], axis=0)
    route_ref[...] = rec.T


def _store_token_words(ref, x):
    n, d = x.shape
    rows = d // (2 * LANES)
    bits = lax.bitcast_convert_type(x.astype(_BF16).astype(_F32), jnp.uint32)
    for s in range(rows):
        lo = bits[:, s * LANES:(s + 1) * LANES] >> 16
        hi = bits[:, d // 2 + s * LANES:d // 2 + (s + 1) * LANES] & jnp.uint32(0xFFFF0000)
        ref[pl.ds(s, n, stride=rows), :] = lo | hi


def _load_token_words(ref, n, d):
    rows = d // (2 * LANES)
    words = [ref[pl.ds(s, n, stride=rows), :] for s in range(rows)]
    lo = [lax.bitcast_convert_type(w << 16, _F32) for w in words]
    hi = [lax.bitcast_convert_type(w & jnp.uint32(0xFFFF0000), _F32) for w in words]
    return jnp.concatenate(lo + hi, axis=1)


def _conv_mixer_kernel(x_ref, win_ref, cw_ref, wout_ref, g_ref, b_ref, wt_ref, bt_ref,
                       x1_ref, x1t_ref, route_ref, routet_ref, counts_ref, ubuf_ref, umat_ref,
                       *, tiles_per_seq, alpha, epg):
    i = pl.program_id(0)
    tm, d = x_ref.shape

    @pl.when(i % tiles_per_seq == 0)
    def _():
        ubuf_ref[0:SUBLANES, :] = jnp.zeros((SUBLANES, d), _F32)

    x = x_ref[...]
    h = _dot(x.astype(_BF16), win_ref[...])
    u = h[:, d:2 * d] * h[:, 2 * d:]
    ubuf_ref[SUBLANES:SUBLANES + tm, :] = u
    cw = cw_ref[...]
    uc = (cw[0:1, :] * ubuf_ref[SUBLANES - 2:SUBLANES - 2 + tm, :]
          + cw[1:2, :] * ubuf_ref[SUBLANES - 1:SUBLANES - 1 + tm, :]
          + cw[2:3, :] * u)
    ubuf_ref[0:SUBLANES, :] = ubuf_ref[tm:tm + SUBLANES, :]
    y = _dot((h[:, :d] * uc).astype(_BF16), wout_ref[...])
    x1 = _layer_norm(alpha * x + y, g_ref[...], b_ref[...])
    x1_ref[...] = x1
    _store_token_words(x1t_ref, x1)
    _route_tail(x1, wt_ref, bt_ref, route_ref, routet_ref, counts_ref, umat_ref, epg=epg)


def _attn_out_kernel(x_ref, o_ref, wo_ref, g_ref, b_ref, wt_ref, bt_ref,
                     x1_ref, x1t_ref, route_ref, routet_ref, counts_ref, umat_ref, *, alpha, epg):
    y = _dot(o_ref[...], wo_ref[...])
    x1 = _layer_norm(alpha * x_ref[...] + y, g_ref[...], b_ref[...])
    x1_ref[...] = x1
    _store_token_words(x1t_ref, x1)
    _route_tail(x1, wt_ref, bt_ref, route_ref, routet_ref, counts_ref, umat_ref, epg=epg)


def _expert_ffn_kernel(bexp_ref, seg_ref, x_ref, rec_ref, win_hbm, wdn_hbm, y_ref,
                       wstage_in, wstage_dn, wsem, winb, wdnb, *, layer):
    i = pl.program_id(0)
    n_exp = (seg_ref.shape[0] - 1) // 2
    n_used = seg_ref[2 * n_exp]
    d, de2 = wstage_in.shape
    blk = rec_ref.shape[0]
    de = de2 // 2

    def weight_copies(e):
        return (pltpu.make_async_copy(win_hbm.at[layer, e], wstage_in, wsem.at[0]),
                pltpu.make_async_copy(wdn_hbm.at[layer, e], wstage_dn, wsem.at[1]))

    @pl.when(i == 0)
    def _():
        for c in weight_copies(bexp_ref[0]):
            c.start()

    @pl.when(i < n_used)
    def _():
        e = bexp_ref[i]

        @pl.when((i == 0) | (e != bexp_ref[jnp.maximum(i - 1, 0)]))
        def _():
            for c in weight_copies(e):
                c.wait()
            winb[...] = wstage_in[...].astype(_BF16)
            wdnb[...] = wstage_dn[...].astype(_BF16)
            nxt = seg_ref[n_exp + e] // blk

            @pl.when(nxt < n_used)
            def _():
                for c in weight_copies(bexp_ref[nxt]):
                    c.start()

        n_valid = seg_ref[e] - i * blk

        def ffn(rows):
            valid = lax.broadcasted_iota(jnp.int32, (rows, 1), 0) < n_valid
            x = jnp.where(valid, _load_token_words(x_ref, rows, d), 0.0)
            h = _dot(x.astype(_BF16), winb[...])
            g = h[:, :de]
            a = g * jax.nn.sigmoid(g) * h[:, de:]
            rec = rec_ref[0:rows, :]
            gate = jnp.where(rec[:, _E1:_E1 + 1] == e.astype(_F32), rec[:, _G1:_G1 + 1], rec[:, _G2:_G2 + 1])
            gate = jnp.where(valid, gate, 0.0)
            _store_token_words(y_ref, _dot(a.astype(_BF16), wdnb[...]) * gate)

        half = blk // 2

        @pl.when(n_valid > half)
        def _():
            ffn(blk)

        @pl.when(n_valid <= half)
        def _():
            ffn(half)
            y_ref[half * (y_ref.shape[0] // blk):, :] = jnp.zeros(
                (half * (y_ref.shape[0] // blk), y_ref.shape[1]), y_ref.dtype)

    @pl.when(i >= n_used)
    def _():
        y_ref[...] = jnp.zeros_like(y_ref)


def _combine_body(x1_ref, ya_ref, yb_ref, g_ref, b_ref, *, alpha):
    tm, d = x1_ref.shape
    ffn = _load_token_words(ya_ref, tm, d) + _load_token_words(yb_ref, tm, d)
    return _layer_norm(alpha * x1_ref[...] + ffn, g_ref[...], b_ref[...])


def _combine_kernel(x1_ref, ya_ref, yb_ref, g_ref, b_ref, x2_ref, *, alpha):
    x2_ref[...] = _combine_body(x1_ref, ya_ref, yb_ref, g_ref, b_ref, alpha=alpha)


def _combine_qkv_kernel(x1_ref, ya_ref, yb_ref, g_ref, b_ref, wk_ref, wv_ref, wq_ref,
                        x2_ref, k_ref, v_ref, q_ref, *, alpha, q_scale):
    x2 = _combine_body(x1_ref, ya_ref, yb_ref, g_ref, b_ref, alpha=alpha)
    x2_ref[...] = x2
    xb = x2.astype(_BF16)
    k_ref[...] = _dot(xb, wk_ref[...]).astype(_BF16)
    q_ref[...] = (_dot(xb, wq_ref[...]) * q_scale).astype(_BF16)
    v_ref[...] = _dot(xb, wv_ref[...]).astype(_BF16)


def _diff_attn_kernel(lam_ref, g_ref, q_ref, k_ref, v_ref, o_ref, lamfull_ref, s_ref, p_ref, *, lam_init):
    bi, hi = pl.program_id(0), pl.program_id(1)
    seq = k_ref.shape[0]
    hw = g_ref.shape[1]
    heads = k_ref.shape[1] // hw
    tq = tk = ATTN_BLOCK
    nk = seq // tk
    hd = lam_ref.shape[1]

    @pl.when((bi == 0) & (hi == 0))
    def _():
        lam = lam_ref[...]
        a = jnp.sum(lam[0:1, :] * lam[1:2, :], axis=-1, keepdims=True)
        b = jnp.sum(lam[2:3, :] * lam[3:4, :], axis=-1, keepdims=True)
        lamfull_ref[...] = jnp.broadcast_to(jnp.exp(a) - jnp.exp(b) + lam_init, lamfull_ref.shape)

    lam_full = lamfull_ref[0:1, 0:1]
    lane = lax.broadcasted_iota(jnp.int32, (tq, hw), 1)
    row = lax.broadcasted_iota(jnp.int32, (2 * tq, tk), 0)
    col = lax.broadcasted_iota(jnp.int32, (2 * tq, tk), 1)
    causal = col <= jnp.where(row < tq, row, row - tq)
    ones = jnp.ones((seq, hw), _BF16)

    step = 0
    for i in reversed(range(nk)):
        for h in range(heads):
            cols = slice(h * hw, (h + 1) * hw)
            par = step % s_ref.shape[0]
            step += 1
            kv = (i + 1) * tk
            q = q_ref[i * tq:(i + 1) * tq, cols]
            zero = jnp.zeros_like(q)
            qcat = jnp.concatenate([jnp.where(lane < hd, q, zero), jnp.where(lane >= hd, q, zero)], axis=0)
            s_ref[par, :, 0:kv] = lax.dot_general(qcat, k_ref[0:kv, cols], _NT, preferred_element_type=_F32)
            s_ref[par, :, i * tk:kv] = jnp.where(causal, s_ref[par, :, i * tk:kv], -jnp.inf)
            rowmax = jnp.max(s_ref[par, :, 0:kv], axis=1, keepdims=True)
            p_ref[par, :, 0:kv] = jnp.exp2(s_ref[par, :, 0:kv] - rowmax).astype(_BF16)
            v_aug = jnp.concatenate([v_ref[0:kv, cols], ones[0:kv, :]], axis=1)
            acc = _dot(p_ref[par, :, 0:kv], v_aug)
            o = acc[:, 0:hw] * (1.0 / acc[:, hw:hw + 1])
            od = o[0:tq, :] - lam_full * o[tq:, :]
            od = od * lax.rsqrt(jnp.mean(od * od, axis=1, keepdims=True) + RMS_EPS)
            o_ref[i * tq:(i + 1) * tq, cols] = (od * g_ref[...] * (1.0 - lam_init)).astype(o_ref.dtype)


def _params(*sem):
    return pltpu.CompilerParams(dimension_semantics=sem, vmem_limit_bytes=VMEM_LIMIT_BYTES)


def _full(shape):
    return pl.BlockSpec(shape, lambda *_: (0,) * len(shape))


def _router_weights(rg_w, rg_b, re_w, re_b):
    d = rg_w.shape[0]
    n = rg_w.shape[1] + re_w.shape[1]
    w = jnp.concatenate([rg_w, re_w, jnp.zeros((d, LANES - n), _F32)], axis=1).T
    b = jnp.concatenate([rg_b, re_b, jnp.zeros((LANES - n,), _F32)])[:, None]
    wh = w.astype(_BF16)
    wl = (w - wh.astype(_F32)).astype(_BF16)
    return jnp.concatenate([wh, wl], axis=0), b


def _router_rows(n_experts):
    return -(-(N_GROUPS + n_experts) // (2 * SUBLANES)) * (2 * SUBLANES)


def _route_outs(t, d, tm, nr):
    per = d // LANES
    shapes = (jax.ShapeDtypeStruct((t, d), _F32),
              jax.ShapeDtypeStruct((t * per // 2, LANES), jnp.uint32),
              jax.ShapeDtypeStruct((t, LANES), _F32),
              jax.ShapeDtypeStruct((SUBLANES, t), _F32),
              jax.ShapeDtypeStruct((nr, 1), _F32))
    specs = (pl.BlockSpec((tm, d), lambda i: (i, 0)),
             pl.BlockSpec((tm * per // 2, LANES), lambda i: (i, 0)),
             pl.BlockSpec((tm, LANES), lambda i: (i, 0)),
             pl.BlockSpec((SUBLANES, tm), lambda i: (0, i)),
             pl.BlockSpec((nr, 1), lambda i: (0, 0)))
    return shapes, specs


def _conv_mixer(x, w_in, conv_w, w_out, ln_g, ln_b, router, *, seq, alpha, epg):
    t, d = x.shape
    tm = MIXER_TILE
    wt, bt = router
    shapes, specs = _route_outs(t, d, tm, _router_rows(N_GROUPS * epg))
    return pl.pallas_call(
        functools.partial(_conv_mixer_kernel, tiles_per_seq=seq // tm, alpha=alpha, epg=epg),
        grid=(t // tm,),
        in_specs=[pl.BlockSpec((tm, d), lambda i: (i, 0)),
                  _full(w_in.shape), _full(conv_w.shape), _full(w_out.shape),
                  _full((1, d)), _full((1, d)), _full(wt.shape), _full(bt.shape)],
        out_specs=specs, out_shape=shapes,
        scratch_shapes=[pltpu.VMEM((tm + SUBLANES, d), _F32), pltpu.VMEM((tm, tm), _BF16)],
        compiler_params=_params("arbitrary"),
        name="conv_mixer_ln_router",
    )(x, w_in, conv_w, w_out, ln_g[None, :], ln_b[None, :], wt, bt)


def _attn_out(x, o, w_o, ln_g, ln_b, router, *, alpha, epg):
    t, d = x.shape
    tm = MIXER_TILE
    wt, bt = router
    shapes, specs = _route_outs(t, d, tm, _router_rows(N_GROUPS * epg))
    return pl.pallas_call(
        functools.partial(_attn_out_kernel, alpha=alpha, epg=epg),
        grid=(t // tm,),
        in_specs=[pl.BlockSpec((tm, d), lambda i: (i, 0)),
                  pl.BlockSpec((tm, d), lambda i: (i, 0)),
                  _full(w_o.shape), _full((1, d)), _full((1, d)), _full(wt.shape), _full(bt.shape)],
        out_specs=specs, out_shape=shapes,
        scratch_shapes=[pltpu.VMEM((tm, tm), _BF16)],
        compiler_params=_params("arbitrary"),
        name="attn_out_ln_router",
    )(x, o, w_o, ln_g[None, :], ln_b[None, :], wt, bt)


def _dispatch_plan(route_t, counts, n_experts):
    t = route_t.shape[1]
    blk = EXPERT_BLOCK
    cnt = counts[N_GROUPS:N_GROUPS + n_experts, 0].astype(jnp.int32)
    padded = (cnt + blk - 1) // blk * blk
    pad_end = jnp.cumsum(padded)
    pad_start = pad_end - padded
    ids = jnp.arange(n_experts, dtype=jnp.int32)

    def sorted_row(e_slot, r_slot):
        e = route_t[e_slot].astype(jnp.int32)
        start = jnp.sum(jnp.where(ids[:, None] == e[None, :], pad_start[:, None], 0), axis=0)
        return start + route_t[r_slot].astype(jnp.int32)

    dest = jnp.concatenate([sorted_row(_E1, _R1), sorted_row(_E2, _R2)])
    n_blocks = (2 * t + n_experts * blk) // blk
    block_start = jnp.arange(n_blocks, dtype=jnp.int32) * blk
    block_expert = jnp.minimum(jnp.sum(pad_end[None, :] <= block_start[:, None], axis=1),
                               n_experts - 1).astype(jnp.int32)
    n_used = (pad_end[-1:] // blk).astype(jnp.int32)
    seg = jnp.concatenate([pad_start + cnt, pad_end, n_used]).astype(jnp.int32)
    return dest, block_expert, seg


def _sc_mesh():
    return plsc.VectorSubcoreMesh(core_axis_name="c", subcore_axis_name="s",
                                  num_cores=SC_CORES, num_subcores=SC_SUBCORES)


def _sc_worker_chunks(t, chunk):
    chunks = t // (SC_CORES * SC_SUBCORES * chunk)
    wid = lax.axis_index("c") * SC_SUBCORES + lax.axis_index("s")
    return wid * chunks, chunks


def _sc_dispatch(x1t, route, dest, n_rows):
    t, per, lanes = x1t.shape
    c = SC_CHUNK

    def body(x_hbm, r_hbm, d_hbm, xo_hbm, ro_hbm, xbuf, rbuf, idx1, idx2, lsem, ssem):
        first, chunks = _sc_worker_chunks(t, c)
        pltpu.sync_copy(d_hbm.at[pl.ds(first * c, chunks * c)], idx1)
        pltpu.sync_copy(d_hbm.at[pl.ds(t + first * c, chunks * c)], idx2)

        def loads(j, slot):
            rows = pl.ds((first + j) * c, c)
            return (pltpu.make_async_copy(x_hbm.at[rows], xbuf.at[slot], lsem.at[2 * slot]),
                    pltpu.make_async_copy(r_hbm.at[rows], rbuf.at[slot], lsem.at[2 * slot + 1]))

        for cp in loads(0, 0):
            cp.start()

        @pl.loop(0, chunks)
        def _(j):
            slot = j % 2

            @pl.when(j + 1 < chunks)
            def _():
                for cp in loads(j + 1, 1 - slot):
                    cp.start()

            for cp in loads(j, slot):
                cp.wait()
            i1 = idx1.at[pl.ds(j * c, c)]
            i2 = idx2.at[pl.ds(j * c, c)]
            scatters = (pltpu.make_async_copy(xbuf.at[slot], xo_hbm.at[i1], ssem.at[0]),
                        pltpu.make_async_copy(xbuf.at[slot], xo_hbm.at[i2], ssem.at[1]),
                        pltpu.make_async_copy(rbuf.at[slot], ro_hbm.at[i1], ssem.at[2]),
                        pltpu.make_async_copy(rbuf.at[slot], ro_hbm.at[i2], ssem.at[3]))
            for cp in scatters:
                cp.start()
            for cp in scatters:
                cp.wait()

    per_worker = t // (SC_CORES * SC_SUBCORES)
    return pl.kernel(
        body,
        out_type=(jax.ShapeDtypeStruct((n_rows, per, lanes), x1t.dtype),
                  jax.ShapeDtypeStruct((n_rows, lanes), route.dtype)),
        mesh=_sc_mesh(),
        scratch_types=[pltpu.VMEM((2, c, per, lanes), x1t.dtype),
                       pltpu.VMEM((2, c, lanes), route.dtype),
                       pltpu.VMEM((per_worker,), jnp.int32),
                       pltpu.VMEM((per_worker,), jnp.int32),
                       pltpu.SemaphoreType.DMA((4,)),
                       pltpu.SemaphoreType.DMA((4,))],
        name="sc_dispatch",
    )(x1t, route, dest)


def _sc_combine(y_rows, dest, t):
    _, per, lanes = y_rows.shape
    c = SC_CHUNK

    def body(y_hbm, d_hbm, oa_hbm, ob_hbm, buf, idx1, idx2, gsem, osem):
        first, chunks = _sc_worker_chunks(t, c)
        pltpu.sync_copy(d_hbm.at[pl.ds(first * c, chunks * c)], idx1)
        pltpu.sync_copy(d_hbm.at[pl.ds(t + first * c, chunks * c)], idx2)

        def stores(j, slot):
            rows = pl.ds((first + j) * c, c)
            return (pltpu.make_async_copy(buf.at[slot, 0], oa_hbm.at[rows], osem.at[2 * slot]),
                    pltpu.make_async_copy(buf.at[slot, 1], ob_hbm.at[rows], osem.at[2 * slot + 1]))

        @pl.loop(0, chunks)
        def _(j):
            slot = j % 2

            @pl.when(j >= 2)
            def _():
                for cp in stores(j - 2, slot):
                    cp.wait()

            gathers = (pltpu.make_async_copy(y_hbm.at[idx1.at[pl.ds(j * c, c)]], buf.at[slot, 0], gsem.at[0]),
                       pltpu.make_async_copy(y_hbm.at[idx2.at[pl.ds(j * c, c)]], buf.at[slot, 1], gsem.at[1]))
            for cp in gathers:
                cp.start()
            for cp in gathers:
                cp.wait()
            for cp in stores(j, slot):
                cp.start()

        for j in range(max(chunks - 2, 0), chunks):
            for cp in stores(j, j % 2):
                cp.wait()

    per_worker = t // (SC_CORES * SC_SUBCORES)
    out = jax.ShapeDtypeStruct((t, per, lanes), y_rows.dtype)
    return pl.kernel(
        body,
        out_type=(out, out),
        mesh=_sc_mesh(),
        scratch_types=[pltpu.VMEM((2, 2, c, per, lanes), y_rows.dtype),
                       pltpu.VMEM((per_worker,), jnp.int32),
                       pltpu.VMEM((per_worker,), jnp.int32),
                       pltpu.SemaphoreType.DMA((2,)),
                       pltpu.SemaphoreType.DMA((4,))],
        name="sc_combine",
    )(y_rows, dest)


def _expert_ffn(x_sorted, r_sorted, w_in, w_down, layer, block_expert, seg):
    n_rows, per, _ = x_sorted.shape
    _, n_exp, d, de2 = w_in.shape
    de = w_down.shape[2]
    blk = EXPERT_BLOCK
    n_blocks = block_expert.shape[0]

    def used(i, be, sg):
        return jnp.minimum(i, sg[2 * n_exp] - 1), 0

    y = pl.pallas_call(
        functools.partial(_expert_ffn_kernel, layer=layer),
        grid_spec=pltpu.PrefetchScalarGridSpec(
            num_scalar_prefetch=2,
            grid=(n_blocks,),
            in_specs=[pl.BlockSpec((blk * per, LANES), used),
                      pl.BlockSpec((blk, LANES), used),
                      pl.BlockSpec(memory_space=pl.ANY),
                      pl.BlockSpec(memory_space=pl.ANY)],
            out_specs=pl.BlockSpec((blk * per, LANES), lambda i, be, sg: (i, 0)),
            scratch_shapes=[pltpu.VMEM((d, de2), _F32),
                            pltpu.VMEM((de, d), _F32),
                            pltpu.SemaphoreType.DMA((2,)),
                            pltpu.VMEM((d, de2), _BF16),
                            pltpu.VMEM((de, d), _BF16)]),
        out_shape=jax.ShapeDtypeStruct((n_rows * per, LANES), jnp.uint32),
        compiler_params=_params("arbitrary"),
        name="expert_ffn",
    )(block_expert, seg, x_sorted.reshape(n_rows * per, LANES), r_sorted, w_in, w_down)
    return y.reshape(n_rows, per, LANES)


def _combine(x1, ya, yb, ln_g, ln_b, *, alpha):
    t, d = x1.shape
    tm = ROW_TILE
    per = ya.shape[1]
    row = pl.BlockSpec((tm, d), lambda i: (i, 0))
    words = pl.BlockSpec((tm * per, LANES), lambda i: (i, 0))
    return pl.pallas_call(
        functools.partial(_combine_kernel, alpha=alpha),
        grid=(t // tm,),
        in_specs=[row, words, words, _full((1, d)), _full((1, d))],
        out_specs=row,
        out_shape=jax.ShapeDtypeStruct((t, d), _F32),
        compiler_params=_params("arbitrary"),
        name="combine_ln",
    )(x1, ya.reshape(t * per, LANES), yb.reshape(t * per, LANES), ln_g[None, :], ln_b[None, :])


def _combine_qkv(x1, ya, yb, ln_g, ln_b, w_k, w_v, w_q, *, alpha, q_scale):
    t, d = x1.shape
    tm = ROW_TILE
    per = ya.shape[1]
    row = pl.BlockSpec((tm, d), lambda i: (i, 0))
    words = pl.BlockSpec((tm * per, LANES), lambda i: (i, 0))
    return pl.pallas_call(
        functools.partial(_combine_qkv_kernel, alpha=alpha, q_scale=q_scale),
        grid=(t // tm,),
        in_specs=[row, words, words, _full((1, d)), _full((1, d)),
                  _full((d, d)), _full((d, d)), _full((d, d))],
        out_specs=(row, row, row, row),
        out_shape=(jax.ShapeDtypeStruct((t, d), _F32),
                   jax.ShapeDtypeStruct((t, d), _BF16),
                   jax.ShapeDtypeStruct((t, d), _BF16),
                   jax.ShapeDtypeStruct((t, d), _BF16)),
        compiler_params=_params("arbitrary"),
        name="combine_ln_qkv",
    )(x1, ya.reshape(t * per, LANES), yb.reshape(t * per, LANES), ln_g[None, :], ln_b[None, :],
      w_k, w_v, w_q)


def _diff_attention(q, k, v, lam, subln_g, *, batch, seq, lam_init):
    t, d = q.shape
    tq = ATTN_BLOCK
    hw = d // N_HEADS
    head = pl.BlockSpec((seq, ATTN_HEADS * hw), lambda b, h: (b, h))
    return pl.pallas_call(
        functools.partial(_diff_attn_kernel, lam_init=lam_init),
        grid=(batch, N_HEADS // ATTN_HEADS),
        in_specs=[_full(lam.shape), _full((1, hw)), head, head, head],
        out_specs=head,
        out_shape=jax.ShapeDtypeStruct((t, d), _BF16),
        scratch_shapes=[pltpu.VMEM((1, LANES), _F32),
                        pltpu.VMEM((ATTN_BUFFERS, 2 * tq, seq), _F32),
                        pltpu.VMEM((ATTN_BUFFERS, 2 * tq, seq), _BF16)],
        compiler_params=_params("arbitrary", "arbitrary"),
        name="diff_attention",
    )(lam, subln_g[None, :], q, k, v)


def _moe(x1t, route, route_t, counts, w_in, w_down, layer):
    n_exp, d = w_in.shape[1:3]
    dest, block_expert, seg = _dispatch_plan(route_t, counts, n_exp)
    x1t = x1t.reshape(-1, d // (2 * LANES), LANES)
    x_sorted, r_sorted = _sc_dispatch(x1t, route, dest, block_expert.shape[0] * EXPERT_BLOCK)
    y_rows = _expert_ffn(x_sorted, r_sorted, w_in, w_down, layer, block_expert, seg)
    return _sc_combine(y_rows, dest, x1t.shape[0])


def kernel(x, a_w_in, a_conv_w, a_w_out, kv_w, b_w_q, b_lambda, b_subln_g, b_w_o,
           ln1_g, ln1_b, ln2_g, ln2_b, rg_w, rg_b, re_w, re_b, e_w_in, e_w_down):
    batch, seq, d = x.shape
    depth = ln1_g.shape[0]
    assert depth == 2 and a_w_in.shape[0] == 1 and b_w_q.shape[0] == 1
    assert seq % MIXER_TILE == 0 and seq % ROW_TILE == 0 and seq % ATTN_BLOCK == 0
    t = batch * seq
    assert t % (SC_CORES * SC_SUBCORES * SC_CHUNK) == 0
    alpha = (2.0 * depth) ** 0.25
    epg = re_w.shape[2] // N_GROUPS
    head_dim = d // (2 * N_HEADS)
    xf = x.reshape(t, d)

    router0 = _router_weights(rg_w[0], rg_b[0], re_w[0], re_b[0])
    x1, x1t, route, route_t, counts = _conv_mixer(xf, a_w_in[0].astype(_BF16), a_conv_w[0],
                                                  a_w_out[0].astype(_BF16), ln1_g[0], ln1_b[0], router0,
                                                  seq=seq, alpha=alpha, epg=epg)
    ya, yb = _moe(x1t, route, route_t, counts, e_w_in, e_w_down, 0)
    q_scale = head_dim ** -0.5 * math.log2(math.e)
    x2, k, v, q = _combine_qkv(x1, ya, yb, ln2_g[0], ln2_b[0],
                               kv_w[:, :d].astype(_BF16), kv_w[:, d:].astype(_BF16),
                               b_w_q[0].astype(_BF16), alpha=alpha, q_scale=q_scale)

    lam_init = 0.8 - 0.6 * math.exp(-0.3 * 1)
    o = _diff_attention(q, k, v, b_lambda[0], b_subln_g[0], batch=batch, seq=seq, lam_init=lam_init)
    router1 = _router_weights(rg_w[1], rg_b[1], re_w[1], re_b[1])
    x1, x1t, route, route_t, counts = _attn_out(x2, o, b_w_o[0].astype(_BF16), ln1_g[1], ln1_b[1],
                                                router1, alpha=alpha, epg=epg)
    ya, yb = _moe(x1t, route, route_t, counts, e_w_in, e_w_down, 1)
    out = _combine(x1, ya, yb, ln2_g[1], ln2_b[1], alpha=alpha)
    return out.reshape(batch, seq, d)
```

```python
import functools
import math

import jax
import jax.numpy as jnp
from jax import lax
from jax.experimental import pallas as pl
from jax.experimental.pallas import tpu as pltpu
from jax.experimental.pallas import tpu_sc as plsc

N_HEADS = 8
N_GROUPS = 4
LN_EPS = 1e-5
RMS_EPS = 1e-5

LANES = 128
SUBLANES = 8
VMEM_LIMIT_BYTES = 56 * 1024 * 1024

SC_CORES = 2
SC_SUBCORES = 16

SC_CHUNK = 32
MIXER_TILE = 1024
ROW_TILE = 1024
EXPERT_BLOCK = 512
EXPERT_SUB = 128
ATTN_BLOCK = 256
ATTN_HEADS = 2
ATTN_BUFFERS = 4

_E1, _E2, _R1, _R2, _G1, _G2 = range(6)

_F32 = jnp.float32
_BF16 = jnp.bfloat16
_NT = (((1,), (1,)), ((), ()))


def _dot(a, b):
    return jnp.dot(a, b, preferred_element_type=_F32)


def _layer_norm(z, g, b):
    mu = jnp.mean(z, axis=-1, keepdims=True)
    d = z - mu
    var = jnp.mean(d * d, axis=-1, keepdims=True)
    return d * lax.rsqrt(var + LN_EPS) * g + b


def _route_tail(x1, wt_ref, bt_ref, route_ref, routet_ref, counts_ref, umat_ref, *, epg):
    tm = x1.shape[0]
    nr = counts_ref.shape[0]
    half = wt_ref.shape[0] // 2

    @pl.when(pl.program_id(0) == 0)
    def _():
        counts_ref[...] = jnp.zeros_like(counts_ref)
        before = (lax.broadcasted_iota(jnp.int32, (tm, tm), 0) < lax.broadcasted_iota(jnp.int32, (tm, tm), 1))
        umat_ref[...] = jnp.where(before, 1.0, 0.0).astype(_BF16)

    xh = x1.astype(_BF16)
    xl = (x1 - xh.astype(_F32)).astype(_BF16)
    a = lax.dot_general(wt_ref[...], xh, _NT, preferred_element_type=_F32)
    b = lax.dot_general(wt_ref[0:half, :], xl, _NT, preferred_element_type=_F32)
    logits = a[0:nr, :] + a[half:half + nr, :] + b[0:nr, :] + bt_ref[0:nr, :]

    rowf = lax.broadcasted_iota(jnp.int32, (nr, tm), 0).astype(_F32)
    neg = -jnp.inf
    big = float(nr)
    gl = jnp.where(rowf < float(N_GROUPS), logits, neg)
    gmax = jnp.max(gl, axis=0, keepdims=True)
    gidx = jnp.min(jnp.where(gl == gmax, rowf, big), axis=0, keepdims=True)
    gtop = 1.0 / jnp.sum(jnp.exp(gl - gmax), axis=0, keepdims=True)

    lo = float(N_GROUPS) + gidx * float(epg)
    el = jnp.where((rowf >= lo) & (rowf < lo + float(epg)), logits, neg)
    m1 = jnp.max(el, axis=0, keepdims=True)
    i1 = jnp.min(jnp.where(el == m1, rowf, big), axis=0, keepdims=True)
    el2 = jnp.where(rowf == i1, neg, el)
    m2 = jnp.max(el2, axis=0, keepdims=True)
    i2 = jnp.min(jnp.where(el2 == m2, rowf, big), axis=0, keepdims=True)
    w2 = jnp.exp(m2 - m1)
    inv = 1.0 / (1.0 + w2)
    g1 = gtop * inv
    g2 = gtop * w2 * inv

    onehot = jnp.where((rowf == i1) | (rowf == i2), 1.0, 0.0)
    total = _dot(onehot.astype(_BF16), umat_ref[...]) + counts_ref[...]
    r1 = jnp.sum(jnp.where(rowf == i1, total, 0.0), axis=0, keepdims=True)
    r2 = jnp.sum(jnp.where(rowf == i2, total, 0.0), axis=0, keepdims=True)
    counts_ref[...] += jnp.sum(onehot, axis=1, keepdims=True)

    slot = lax.broadcasted_iota(jnp.int32, (SUBLANES, tm), 0)
    rec = jnp.zeros((SUBLANES, tm), _F32)
    fields = ((_E1, i1 - float(N_GROUPS)), (_E2, i2 - float(N_GROUPS)), (_R1, r1), (_R2, r2), (_G1, g1), (_G2, g2))
    for s, val in fields:
        rec = jnp.where(slot == s, val, rec)
    routet_ref[...] = rec
    rec = jnp.concatenate([rec, jnp.zeros((LANES - SUBLANES, tm), _F32)], axis=0)
    route_ref[...] = rec.T


def _store_token_words(ref, x):
    n, d = x.shape
    rows = d // (2 * LANES)
    bits = lax.bitcast_convert_type(x.astype(_BF16).astype(_F32), jnp.uint32)
    for s in range(rows):
        lo = bits[:, s * LANES:(s + 1) * LANES] >> 16
        hi = bits[:, d // 2 + s * LANES:d // 2 + (s + 1) * LANES] & jnp.uint32(0xFFFF0000)
        ref[pl.ds(s, n, stride=rows), :] = lo | hi


def _load_token_words(ref, n, d):
    rows = d // (2 * LANES)
    words = [ref[pl.ds(s, n, stride=rows), :] for s in range(rows)]
    lo = [lax.bitcast_convert_type(w << 16, _F32) for w in words]
    hi = [lax.bitcast_convert_type(w & jnp.uint32(0xFFFF0000), _F32) for w in words]
    return jnp.concatenate(lo + hi, axis=1)


def _conv_mixer_kernel(x_ref, win_ref, cw_ref, wout_ref, g_ref, b_ref, wt_ref, bt_ref,
                       x1_ref, x1t_ref, route_ref, routet_ref, counts_ref, ubuf_ref, umat_ref,
                       *, tiles_per_seq, alpha, epg):
    i = pl.program_id(0)
    tm, d = x_ref.shape

    @pl.when(i % tiles_per_seq == 0)
    def _():
        ubuf_ref[0:SUBLANES, :] = jnp.zeros((SUBLANES, d), _F32)

    x = x_ref[...]
    h = _dot(x.astype(_BF16), win_ref[...])
    u = h[:, d:2 * d] * h[:, 2 * d:]
    ubuf_ref[SUBLANES:SUBLANES + tm, :] = u
    cw = cw_ref[...]
    uc = (cw[0:1, :] * ubuf_ref[SUBLANES - 2:SUBLANES - 2 + tm, :]
          + cw[1:2, :] * ubuf_ref[SUBLANES - 1:SUBLANES - 1 + tm, :]
          + cw[2:3, :] * u)
    ubuf_ref[0:SUBLANES, :] = ubuf_ref[tm:tm + SUBLANES, :]
    y = _dot((h[:, :d] * uc).astype(_BF16), wout_ref[...])
    x1 = _layer_norm(alpha * x + y, g_ref[...], b_ref[...])
    x1_ref[...] = x1
    _store_token_words(x1t_ref, x1)
    _route_tail(x1, wt_ref, bt_ref, route_ref, routet_ref, counts_ref, umat_ref, epg=epg)


def _attn_out_kernel(x_ref, o_ref, wo_ref, g_ref, b_ref, wt_ref, bt_ref,
                     x1_ref, x1t_ref, route_ref, routet_ref, counts_ref, umat_ref, *, alpha, epg):
    y = _dot(o_ref[...], wo_ref[...])
    x1 = _layer_norm(alpha * x_ref[...] + y, g_ref[...], b_ref[...])
    x1_ref[...] = x1
    _store_token_words(x1t_ref, x1)
    _route_tail(x1, wt_ref, bt_ref, route_ref, routet_ref, counts_ref, umat_ref, epg=epg)


def _expert_ffn_kernel(bexp_ref, seg_ref, x_ref, rec_ref, win_hbm, wdn_hbm, y_ref,
                       wstage_in, wstage_dn, wsem, winb, wdnb, *, layer):
    i = pl.program_id(0)
    n_exp = (seg_ref.shape[0] - 1) // 2
    n_used = seg_ref[2 * n_exp]
    d, de2 = wstage_in.shape
    blk = rec_ref.shape[0]
    de = de2 // 2

    def weight_copies(e):
        return (pltpu.make_async_copy(win_hbm.at[layer, e], wstage_in, wsem.at[0]),
                pltpu.make_async_copy(wdn_hbm.at[layer, e], wstage_dn, wsem.at[1]))

    @pl.when(i == 0)
    def _():
        for c in weight_copies(bexp_ref[0]):
            c.start()

    @pl.when(i < n_used)
    def _():
        e = bexp_ref[i]

        @pl.when((i == 0) | (e != bexp_ref[jnp.maximum(i - 1, 0)]))
        def _():
            for c in weight_copies(e):
                c.wait()
            winb[...] = wstage_in[...].astype(_BF16)
            wdnb[...] = wstage_dn[...].astype(_BF16)
            nxt = seg_ref[n_exp + e] // blk

            @pl.when(nxt < n_used)
            def _():
                for c in weight_copies(bexp_ref[nxt]):
                    c.start()

        n_valid = seg_ref[e] - i * blk

        def ffn(rows):
            valid = lax.broadcasted_iota(jnp.int32, (rows, 1), 0) < n_valid
            x = jnp.where(valid, _load_token_words(x_ref, rows, d), 0.0)
            h = _dot(x.astype(_BF16), winb[...])
            g = h[:, :de]
            a = g * jax.nn.sigmoid(g) * h[:, de:]
            rec = rec_ref[0:rows, :]
            gate = jnp.where(rec[:, _E1:_E1 + 1] == e.astype(_F32), rec[:, _G1:_G1 + 1], rec[:, _G2:_G2 + 1])
            gate = jnp.where(valid, gate, 0.0)
            _store_token_words(y_ref, _dot(a.astype(_BF16), wdnb[...]) * gate)

        per = y_ref.shape[0] // blk
        for rows in range(EXPERT_SUB, blk + 1, EXPERT_SUB):
            @pl.when((n_valid > rows - EXPERT_SUB) & ((n_valid <= rows) | (rows == blk)))
            def _(rows=rows):
                ffn(rows)
                if rows < blk:
                    y_ref[rows * per:, :] = jnp.zeros(((blk - rows) * per, y_ref.shape[1]), y_ref.dtype)

    @pl.when(i >= n_used)
    def _():
        y_ref[...] = jnp.zeros_like(y_ref)


def _combine_body(x1_ref, ya_ref, yb_ref, g_ref, b_ref, *, alpha):
    tm, d = x1_ref.shape
    ffn = _load_token_words(ya_ref, tm, d) + _load_token_words(yb_ref, tm, d)
    return _layer_norm(alpha * x1_ref[...] + ffn, g_ref[...], b_ref[...])


def _combine_kernel(x1_ref, ya_ref, yb_ref, g_ref, b_ref, x2_ref, *, alpha):
    x2_ref[...] = _combine_body(x1_ref, ya_ref, yb_ref, g_ref, b_ref, alpha=alpha)


def _combine_qkv_kernel(x1_ref, ya_ref, yb_ref, g_ref, b_ref, wk_ref, wv_ref, wq_ref,
                        x2_ref, k_ref, v_ref, q_ref, *, alpha, q_scale):
    x2 = _combine_body(x1_ref, ya_ref, yb_ref, g_ref, b_ref, alpha=alpha)
    x2_ref[...] = x2
    xb = x2.astype(_BF16)
    k_ref[...] = _dot(xb, wk_ref[...]).astype(_BF16)
    q_ref[...] = (_dot(xb, wq_ref[...]) * q_scale).astype(_BF16)
    v_ref[...] = _dot(xb, wv_ref[...]).astype(_BF16)


def _diff_attn_kernel(lam_ref, g_ref, q_ref, k_ref, v_ref, o_ref, lamfull_ref, s_ref, p_ref, *, lam_init):
    bi, hi = pl.program_id(0), pl.program_id(1)
    seq = k_ref.shape[0]
    hw = g_ref.shape[1]
    heads = k_ref.shape[1] // hw
    tq = tk = ATTN_BLOCK
    nk = seq // tk
    hd = lam_ref.shape[1]

    @pl.when((bi == 0) & (hi == 0))
    def _():
        lam = lam_ref[...]
        a = jnp.sum(lam[0:1, :] * lam[1:2, :], axis=-1, keepdims=True)
        b = jnp.sum(lam[2:3, :] * lam[3:4, :], axis=-1, keepdims=True)
        lamfull_ref[...] = jnp.broadcast_to(jnp.exp(a) - jnp.exp(b) + lam_init, lamfull_ref.shape)

    lam_full = lamfull_ref[0:1, 0:1]
    lane = lax.broadcasted_iota(jnp.int32, (tq, hw), 1)
    row = lax.broadcasted_iota(jnp.int32, (2 * tq, tk), 0)
    col = lax.broadcasted_iota(jnp.int32, (2 * tq, tk), 1)
    causal = col <= jnp.where(row < tq, row, row - tq)
    ones = jnp.ones((seq, hw), _BF16)

    step = 0
    for i in reversed(range(nk)):
        for h in range(heads):
            cols = slice(h * hw, (h + 1) * hw)
            par = step % s_ref.shape[0]
            step += 1
            kv = (i + 1) * tk
            q = q_ref[i * tq:(i + 1) * tq, cols]
            zero = jnp.zeros_like(q)
            qcat = jnp.concatenate([jnp.where(lane < hd, q, zero), jnp.where(lane >= hd, q, zero)], axis=0)
            s_ref[par, :, 0:kv] = lax.dot_general(qcat, k_ref[0:kv, cols], _NT, preferred_element_type=_F32)
            s_ref[par, :, i * tk:kv] = jnp.where(causal, s_ref[par, :, i * tk:kv], -jnp.inf)
            rowmax = jnp.max(s_ref[par, :, 0:kv], axis=1, keepdims=True)
            p_ref[par, :, 0:kv] = jnp.exp2(s_ref[par, :, 0:kv] - rowmax).astype(_BF16)
            v_aug = jnp.concatenate([v_ref[0:kv, cols], ones[0:kv, :]], axis=1)
            acc = _dot(p_ref[par, :, 0:kv], v_aug)
            o = acc[:, 0:hw] * (1.0 / acc[:, hw:hw + 1])
            od = o[0:tq, :] - lam_full * o[tq:, :]
            od = od * lax.rsqrt(jnp.mean(od * od, axis=1, keepdims=True) + RMS_EPS)
            o_ref[i * tq:(i + 1) * tq, cols] = (od * g_ref[...] * (1.0 - lam_init)).astype(o_ref.dtype)


def _params(*sem):
    return pltpu.CompilerParams(dimension_semantics=sem, vmem_limit_bytes=VMEM_LIMIT_BYTES)


def _full(shape):
    return pl.BlockSpec(shape, lambda *_: (0,) * len(shape))


def _router_weights(rg_w, rg_b, re_w, re_b):
    d = rg_w.shape[0]
    n = rg_w.shape[1] + re_w.shape[1]
    w = jnp.concatenate([rg_w, re_w, jnp.zeros((d, LANES - n), _F32)], axis=1).T
    b = jnp.concatenate([rg_b, re_b, jnp.zeros((LANES - n,), _F32)])[:, None]
    wh = w.astype(_BF16)
    wl = (w - wh.astype(_F32)).astype(_BF16)
    return jnp.concatenate([wh, wl], axis=0), b


def _router_rows(n_experts):
    return -(-(N_GROUPS + n_experts) // (2 * SUBLANES)) * (2 * SUBLANES)


def _route_outs(t, d, tm, nr):
    per = d // LANES
    shapes = (jax.ShapeDtypeStruct((t, d), _F32),
              jax.ShapeDtypeStruct((t * per // 2, LANES), jnp.uint32),
              jax.ShapeDtypeStruct((t, LANES), _F32),
              jax.ShapeDtypeStruct((SUBLANES, t), _F32),
              jax.ShapeDtypeStruct((nr, 1), _F32))
    specs = (pl.BlockSpec((tm, d), lambda i: (i, 0)),
             pl.BlockSpec((tm * per // 2, LANES), lambda i: (i, 0)),
             pl.BlockSpec((tm, LANES), lambda i: (i, 0)),
             pl.BlockSpec((SUBLANES, tm), lambda i: (0, i)),
             pl.BlockSpec((nr, 1), lambda i: (0, 0)))
    return shapes, specs


def _conv_mixer(x, w_in, conv_w, w_out, ln_g, ln_b, router, *, seq, alpha, epg):
    t, d = x.shape
    tm = MIXER_TILE
    wt, bt = router
    shapes, specs = _route_outs(t, d, tm, _router_rows(N_GROUPS * epg))
    return pl.pallas_call(
        functools.partial(_conv_mixer_kernel, tiles_per_seq=seq // tm, alpha=alpha, epg=epg),
        grid=(t // tm,),
        in_specs=[pl.BlockSpec((tm, d), lambda i: (i, 0)),
                  _full(w_in.shape), _full(conv_w.shape), _full(w_out.shape),
                  _full((1, d)), _full((1, d)), _full(wt.shape), _full(bt.shape)],
        out_specs=specs, out_shape=shapes,
        scratch_shapes=[pltpu.VMEM((tm + SUBLANES, d), _F32), pltpu.VMEM((tm, tm), _BF16)],
        compiler_params=_params("arbitrary"),
        name="conv_mixer_ln_router",
    )(x, w_in, conv_w, w_out, ln_g[None, :], ln_b[None, :], wt, bt)


def _attn_out(x, o, w_o, ln_g, ln_b, router, *, alpha, epg):
    t, d = x.shape
    tm = MIXER_TILE
    wt, bt = router
    shapes, specs = _route_outs(t, d, tm, _router_rows(N_GROUPS * epg))
    return pl.pallas_call(
        functools.partial(_attn_out_kernel, alpha=alpha, epg=epg),
        grid=(t // tm,),
        in_specs=[pl.BlockSpec((tm, d), lambda i: (i, 0)),
                  pl.BlockSpec((tm, d), lambda i: (i, 0)),
                  _full(w_o.shape), _full((1, d)), _full((1, d)), _full(wt.shape), _full(bt.shape)],
        out_specs=specs, out_shape=shapes,
        scratch_shapes=[pltpu.VMEM((tm, tm), _BF16)],
        compiler_params=_params("arbitrary"),
        name="attn_out_ln_router",
    )(x, o, w_o, ln_g[None, :], ln_b[None, :], wt, bt)


def _dispatch_plan(route_t, counts, n_experts):
    t = route_t.shape[1]
    blk = EXPERT_BLOCK
    cnt = counts[N_GROUPS:N_GROUPS + n_experts, 0].astype(jnp.int32)
    padded = (cnt + blk - 1) // blk * blk
    pad_end = jnp.cumsum(padded)
    pad_start = pad_end - padded
    ids = jnp.arange(n_experts, dtype=jnp.int32)

    def sorted_row(e_slot, r_slot):
        e = route_t[e_slot].astype(jnp.int32)
        start = jnp.sum(jnp.where(ids[:, None] == e[None, :], pad_start[:, None], 0), axis=0)
        return start + route_t[r_slot].astype(jnp.int32)

    dest = jnp.concatenate([sorted_row(_E1, _R1), sorted_row(_E2, _R2)])
    n_blocks = (2 * t + n_experts * blk) // blk
    block_start = jnp.arange(n_blocks, dtype=jnp.int32) * blk
    block_expert = jnp.minimum(jnp.sum(pad_end[None, :] <= block_start[:, None], axis=1),
                               n_experts - 1).astype(jnp.int32)
    n_used = (pad_end[-1:] // blk).astype(jnp.int32)
    seg = jnp.concatenate([pad_start + cnt, pad_end, n_used]).astype(jnp.int32)
    return dest, block_expert, seg


def _sc_mesh():
    return plsc.VectorSubcoreMesh(core_axis_name="c", subcore_axis_name="s",
                                  num_cores=SC_CORES, num_subcores=SC_SUBCORES)


def _sc_worker_chunks(t, chunk):
    chunks = t // (SC_CORES * SC_SUBCORES * chunk)
    wid = lax.axis_index("c") * SC_SUBCORES + lax.axis_index("s")
    return wid * chunks, chunks


def _sc_dispatch(x1t, route, dest, n_rows):
    t, per, lanes = x1t.shape
    c = SC_CHUNK

    def body(x_hbm, r_hbm, d_hbm, xo_hbm, ro_hbm, xbuf, rbuf, idx1, idx2, lsem, ssem):
        first, chunks = _sc_worker_chunks(t, c)
        pltpu.sync_copy(d_hbm.at[pl.ds(first * c, chunks * c)], idx1)
        pltpu.sync_copy(d_hbm.at[pl.ds(t + first * c, chunks * c)], idx2)

        def loads(j, slot):
            rows = pl.ds((first + j) * c, c)
            return (pltpu.make_async_copy(x_hbm.at[rows], xbuf.at[slot], lsem.at[2 * slot]),
                    pltpu.make_async_copy(r_hbm.at[rows], rbuf.at[slot], lsem.at[2 * slot + 1]))

        for cp in loads(0, 0):
            cp.start()

        @pl.loop(0, chunks)
        def _(j):
            slot = j % 2

            @pl.when(j + 1 < chunks)
            def _():
                for cp in loads(j + 1, 1 - slot):
                    cp.start()

            for cp in loads(j, slot):
                cp.wait()
            i1 = idx1.at[pl.ds(j * c, c)]
            i2 = idx2.at[pl.ds(j * c, c)]
            scatters = (pltpu.make_async_copy(xbuf.at[slot], xo_hbm.at[i1], ssem.at[0]),
                        pltpu.make_async_copy(xbuf.at[slot], xo_hbm.at[i2], ssem.at[1]),
                        pltpu.make_async_copy(rbuf.at[slot], ro_hbm.at[i1], ssem.at[2]),
                        pltpu.make_async_copy(rbuf.at[slot], ro_hbm.at[i2], ssem.at[3]))
            for cp in scatters:
                cp.start()
            for cp in scatters:
                cp.wait()

    per_worker = t // (SC_CORES * SC_SUBCORES)
    return pl.kernel(
        body,
        out_type=(jax.ShapeDtypeStruct((n_rows, per, lanes), x1t.dtype),
                  jax.ShapeDtypeStruct((n_rows, lanes), route.dtype)),
        mesh=_sc_mesh(),
        scratch_types=[pltpu.VMEM((2, c, per, lanes), x1t.dtype),
                       pltpu.VMEM((2, c, lanes), route.dtype),
                       pltpu.VMEM((per_worker,), jnp.int32),
                       pltpu.VMEM((per_worker,), jnp.int32),
                       pltpu.SemaphoreType.DMA((4,)),
                       pltpu.SemaphoreType.DMA((4,))],
        name="sc_dispatch",
    )(x1t, route, dest)


def _sc_combine(y_rows, dest, t):
    _, per, lanes = y_rows.shape
    c = SC_CHUNK

    def body(y_hbm, d_hbm, oa_hbm, ob_hbm, buf, idx1, idx2, gsem, osem):
        first, chunks = _sc_worker_chunks(t, c)
        pltpu.sync_copy(d_hbm.at[pl.ds(first * c, chunks * c)], idx1)
        pltpu.sync_copy(d_hbm.at[pl.ds(t + first * c, chunks * c)], idx2)

        def stores(j, slot):
            rows = pl.ds((first + j) * c, c)
            return (pltpu.make_async_copy(buf.at[slot, 0], oa_hbm.at[rows], osem.at[2 * slot]),
                    pltpu.make_async_copy(buf.at[slot, 1], ob_hbm.at[rows], osem.at[2 * slot + 1]))

        @pl.loop(0, chunks)
        def _(j):
            slot = j % 2

            @pl.when(j >= 2)
            def _():
                for cp in stores(j - 2, slot):
                    cp.wait()

            gathers = (pltpu.make_async_copy(y_hbm.at[idx1.at[pl.ds(j * c, c)]], buf.at[slot, 0], gsem.at[0]),
                       pltpu.make_async_copy(y_hbm.at[idx2.at[pl.ds(j * c, c)]], buf.at[slot, 1], gsem.at[1]))
            for cp in gathers:
                cp.start()
            for cp in gathers:
                cp.wait()
            for cp in stores(j, slot):
                cp.start()

        for j in range(max(chunks - 2, 0), chunks):
            for cp in stores(j, j % 2):
                cp.wait()

    per_worker = t // (SC_CORES * SC_SUBCORES)
    out = jax.ShapeDtypeStruct((t, per, lanes), y_rows.dtype)
    return pl.kernel(
        body,
        out_type=(out, out),
        mesh=_sc_mesh(),
        scratch_types=[pltpu.VMEM((2, 2, c, per, lanes), y_rows.dtype),
                       pltpu.VMEM((per_worker,), jnp.int32),
                       pltpu.VMEM((per_worker,), jnp.int32),
                       pltpu.SemaphoreType.DMA((2,)),
                       pltpu.SemaphoreType.DMA((4,))],
        name="sc_combine",
    )(y_rows, dest)


def _expert_ffn(x_sorted, r_sorted, w_in, w_down, layer, block_expert, seg):
    n_rows, per, _ = x_sorted.shape
    _, n_exp, d, de2 = w_in.shape
    de = w_down.shape[2]
    blk = EXPERT_BLOCK
    n_blocks = block_expert.shape[0]

    def used(i, be, sg):
        return jnp.minimum(i, sg[2 * n_exp] - 1), 0

    y = pl.pallas_call(
        functools.partial(_expert_ffn_kernel, layer=layer),
        grid_spec=pltpu.PrefetchScalarGridSpec(
            num_scalar_prefetch=2,
            grid=(n_blocks,),
            in_specs=[pl.BlockSpec((blk * per, LANES), used),
                      pl.BlockSpec((blk, LANES), used),
                      pl.BlockSpec(memory_space=pl.ANY),
                      pl.BlockSpec(memory_space=pl.ANY)],
            out_specs=pl.BlockSpec((blk * per, LANES), lambda i, be, sg: (i, 0)),
            scratch_shapes=[pltpu.VMEM((d, de2), _F32),
                            pltpu.VMEM((de, d), _F32),
                            pltpu.SemaphoreType.DMA((2,)),
                            pltpu.VMEM((d, de2), _BF16),
                            pltpu.VMEM((de, d), _BF16)]),
        out_shape=jax.ShapeDtypeStruct((n_rows * per, LANES), jnp.uint32),
        compiler_params=_params("arbitrary"),
        name="expert_ffn",
    )(block_expert, seg, x_sorted.reshape(n_rows * per, LANES), r_sorted, w_in, w_down)
    return y.reshape(n_rows, per, LANES)


def _combine(x1, ya, yb, ln_g, ln_b, *, alpha):
    t, d = x1.shape
    tm = ROW_TILE
    per = ya.shape[1]
    row = pl.BlockSpec((tm, d), lambda i: (i, 0))
    words = pl.BlockSpec((tm * per, LANES), lambda i: (i, 0))
    return pl.pallas_call(
        functools.partial(_combine_kernel, alpha=alpha),
        grid=(t // tm,),
        in_specs=[row, words, words, _full((1, d)), _full((1, d))],
        out_specs=row,
        out_shape=jax.ShapeDtypeStruct((t, d), _F32),
        compiler_params=_params("arbitrary"),
        name="combine_ln",
    )(x1, ya.reshape(t * per, LANES), yb.reshape(t * per, LANES), ln_g[None, :], ln_b[None, :])


def _combine_qkv(x1, ya, yb, ln_g, ln_b, w_k, w_v, w_q, *, alpha, q_scale):
    t, d = x1.shape
    tm = ROW_TILE
    per = ya.shape[1]
    row = pl.BlockSpec((tm, d), lambda i: (i, 0))
    words = pl.BlockSpec((tm * per, LANES), lambda i: (i, 0))
    return pl.pallas_call(
        functools.partial(_combine_qkv_kernel, alpha=alpha, q_scale=q_scale),
        grid=(t // tm,),
        in_specs=[row, words, words, _full((1, d)), _full((1, d)),
                  _full((d, d)), _full((d, d)), _full((d, d))],
        out_specs=(row, row, row, row),
        out_shape=(jax.ShapeDtypeStruct((t, d), _F32),
                   jax.ShapeDtypeStruct((t, d), _BF16),
                   jax.ShapeDtypeStruct((t, d), _BF16),
                   jax.ShapeDtypeStruct((t, d), _BF16)),
        compiler_params=_params("arbitrary"),
        name="combine_ln_qkv",
    )(x1, ya.reshape(t * per, LANES), yb.reshape(t * per, LANES), ln_g[None, :], ln_b[None, :],
      w_k, w_v, w_q)


def _diff_attention(q, k, v, lam, subln_g, *, batch, seq, lam_init):
    t, d = q.shape
    tq = ATTN_BLOCK
    hw = d // N_HEADS
    head = pl.BlockSpec((seq, ATTN_HEADS * hw), lambda b, h: (b, h))
    return pl.pallas_call(
        functools.partial(_diff_attn_kernel, lam_init=lam_init),
        grid=(batch, N_HEADS // ATTN_HEADS),
        in_specs=[_full(lam.shape), _full((1, hw)), head, head, head],
        out_specs=head,
        out_shape=jax.ShapeDtypeStruct((t, d), _BF16),
        scratch_shapes=[pltpu.VMEM((1, LANES), _F32),
                        pltpu.VMEM((ATTN_BUFFERS, 2 * tq, seq), _F32),
                        pltpu.VMEM((ATTN_BUFFERS, 2 * tq, seq), _BF16)],
        compiler_params=_params("arbitrary", "arbitrary"),
        name="diff_attention",
    )(lam, subln_g[None, :], q, k, v)


def _moe(x1t, route, route_t, counts, w_in, w_down, layer):
    n_exp, d = w_in.shape[1:3]
    dest, block_expert, seg = _dispatch_plan(route_t, counts, n_exp)
    x1t = x1t.reshape(-1, d // (2 * LANES), LANES)
    x_sorted, r_sorted = _sc_dispatch(x1t, route, dest, block_expert.shape[0] * EXPERT_BLOCK)
    y_rows = _expert_ffn(x_sorted, r_sorted, w_in, w_down, layer, block_expert, seg)
    return _sc_combine(y_rows, dest, x1t.shape[0])


def kernel(x, a_w_in, a_conv_w, a_w_out, kv_w, b_w_q, b_lambda, b_subln_g, b_w_o,
           ln1_g, ln1_b, ln2_g, ln2_b, rg_w, rg_b, re_w, re_b, e_w_in, e_w_down):
    batch, seq, d = x.shape
    depth = ln1_g.shape[0]
    assert depth == 2 and a_w_in.shape[0] == 1 and b_w_q.shape[0] == 1
    assert seq % MIXER_TILE == 0 and seq % ROW_TILE == 0 and seq % ATTN_BLOCK == 0
    t = batch * seq
    assert t % (SC_CORES * SC_SUBCORES * SC_CHUNK) == 0
    alpha = (2.0 * depth) ** 0.25
    epg = re_w.shape[2] // N_GROUPS
    head_dim = d // (2 * N_HEADS)
    xf = x.reshape(t, d)

    router0 = _router_weights(rg_w[0], rg_b[0], re_w[0], re_b[0])
    x1, x1t, route, route_t, counts = _conv_mixer(xf, a_w_in[0].astype(_BF16), a_conv_w[0],
                                                  a_w_out[0].astype(_BF16), ln1_g[0], ln1_b[0], router0,
                                                  seq=seq, alpha=alpha, epg=epg)
    ya, yb = _moe(x1t, route, route_t, counts, e_w_in, e_w_down, 0)
    q_scale = head_dim ** -0.5 * math.log2(math.e)
    x2, k, v, q = _combine_qkv(x1, ya, yb, ln2_g[0], ln2_b[0],
                               kv_w[:, :d].astype(_BF16), kv_w[:, d:].astype(_BF16),
                               b_w_q[0].astype(_BF16), alpha=alpha, q_scale=q_scale)

    lam_init = 0.8 - 0.6 * math.exp(-0.3 * 1)
    o = _diff_attention(q, k, v, b_lambda[0], b_subln_g[0], batch=batch, seq=seq, lam_init=lam_init)
    router1 = _router_weights(rg_w[1], rg_b[1], re_w[1], re_b[1])
    x1, x1t, route, route_t, counts = _attn_out(x2, o, b_w_o[0].astype(_BF16), ln1_g[1], ln1_b[1],
                                                router1, alpha=alpha, epg=epg)
    ya, yb = _moe(x1t, route, route_t, counts, e_w_in, e_w_down, 1)
    out = _combine(x1, ya, yb, ln2_g[1], ln2_b[1], alpha=alpha)
    return out.reshape(batch, seq, d)
```

```python
import functools
import math

import jax
import jax.numpy as jnp
from jax import lax
from jax.experimental import pallas as pl
from jax.experimental.pallas import tpu as pltpu
from jax.experimental.pallas import tpu_sc as plsc

N_HEADS = 8
N_GROUPS = 4
LN_EPS = 1e-5
RMS_EPS = 1e-5

LANES = 128
SUBLANES = 8
VMEM_LIMIT_BYTES = 56 * 1024 * 1024

SC_CORES = 2
SC_SUBCORES = 16

SC_CHUNK = 32
MIXER_TILE = 1024
ROW_TILE = 1024
EXPERT_BLOCK = 512
ATTN_BLOCK = 256
ATTN_HEADS = 2
ATTN_BUFFERS = 4

_E1, _E2, _R1, _R2, _G1, _G2 = range(6)

_F32 = jnp.float32
_BF16 = jnp.bfloat16
_NT = (((1,), (1,)), ((), ()))


def _dot(a, b):
    return jnp.dot(a, b, preferred_element_type=_F32)


def _layer_norm(z, g, b):
    mu = jnp.mean(z, axis=-1, keepdims=True)
    d = z - mu
    var = jnp.mean(d * d, axis=-1, keepdims=True)
    return d * lax.rsqrt(var + LN_EPS) * g + b


def _route_tail(x1, wt_ref, bt_ref, route_ref, routet_ref, counts_ref, umat_ref, *, epg):
    tm = x1.shape[0]
    nr = counts_ref.shape[0]
    half = wt_ref.shape[0] // 2

    @pl.when(pl.program_id(0) == 0)
    def _():
        counts_ref[...] = jnp.zeros_like(counts_ref)
        before = (lax.broadcasted_iota(jnp.int32, (tm, tm), 0) < lax.broadcasted_iota(jnp.int32, (tm, tm), 1))
        umat_ref[...] = jnp.where(before, 1.0, 0.0).astype(_BF16)

    xh = x1.astype(_BF16)
    xl = (x1 - xh.astype(_F32)).astype(_BF16)
    a = lax.dot_general(wt_ref[...], xh, _NT, preferred_element_type=_F32)
    b = lax.dot_general(wt_ref[0:half, :], xl, _NT, preferred_element_type=_F32)
    logits = a[0:nr, :] + a[half:half + nr, :] + b[0:nr, :] + bt_ref[0:nr, :]

    rowf = lax.broadcasted_iota(jnp.int32, (nr, tm), 0).astype(_F32)
    neg = -jnp.inf
    big = float(nr)
    gl = jnp.where(rowf < float(N_GROUPS), logits, neg)
    gmax = jnp.max(gl, axis=0, keepdims=True)
    gidx = jnp.min(jnp.where(gl == gmax, rowf, big), axis=0, keepdims=True)
    gtop = 1.0 / jnp.sum(jnp.exp(gl - gmax), axis=0, keepdims=True)

    lo = float(N_GROUPS) + gidx * float(epg)
    el = jnp.where((rowf >= lo) & (rowf < lo + float(epg)), logits, neg)
    m1 = jnp.max(el, axis=0, keepdims=True)
    i1 = jnp.min(jnp.where(el == m1, rowf, big), axis=0, keepdims=True)
    el2 = jnp.where(rowf == i1, neg, el)
    m2 = jnp.max(el2, axis=0, keepdims=True)
    i2 = jnp.min(jnp.where(el2 == m2, rowf, big), axis=0, keepdims=True)
    w2 = jnp.exp(m2 - m1)
    inv = 1.0 / (1.0 + w2)
    g1 = gtop * inv
    g2 = gtop * w2 * inv

    onehot = jnp.where((rowf == i1) | (rowf == i2), 1.0, 0.0)
    total = _dot(onehot.astype(_BF16), umat_ref[...]) + counts_ref[...]
    r1 = jnp.sum(jnp.where(rowf == i1, total, 0.0), axis=0, keepdims=True)
    r2 = jnp.sum(jnp.where(rowf == i2, total, 0.0), axis=0, keepdims=True)
    counts_ref[...] += jnp.sum(onehot, axis=1, keepdims=True)

    slot = lax.broadcasted_iota(jnp.int32, (SUBLANES, tm), 0)
    rec = jnp.zeros((SUBLANES, tm), _F32)
    fields = ((_E1, i1 - float(N_GROUPS)), (_E2, i2 - float(N_GROUPS)), (_R1, r1), (_R2, r2), (_G1, g1), (_G2, g2))
    for s, val in fields:
        rec = jnp.where(slot == s, val, rec)
    routet_ref[...] = rec
    rec = jnp.concatenate([rec, jnp.zeros((LANES - SUBLANES, tm), _F32)], axis=0)
    route_ref[...] = rec.T


def _store_token_words(ref, x):
    n, d = x.shape
    rows = d // (2 * LANES)
    bits = lax.bitcast_convert_type(x.astype(_BF16).astype(_F32), jnp.uint32)
    for s in range(rows):
        lo = bits[:, s * LANES:(s + 1) * LANES] >> 16
        hi = bits[:, d // 2 + s * LANES:d // 2 + (s + 1) * LANES] & jnp.uint32(0xFFFF0000)
        ref[pl.ds(s, n, stride=rows), :] = lo | hi


def _load_token_words(ref, n, d):
    rows = d // (2 * LANES)
    words = [ref[pl.ds(s, n, stride=rows), :] for s in range(rows)]
    lo = [lax.bitcast_convert_type(w << 16, _F32) for w in words]
    hi = [lax.bitcast_convert_type(w & jnp.uint32(0xFFFF0000), _F32) for w in words]
    return jnp.concatenate(lo + hi, axis=1)


def _conv_mixer_kernel(x_ref, win_ref, cw_ref, wout_ref, g_ref, b_ref, wt_ref, bt_ref,
                       x1_ref, x1t_ref, route_ref, routet_ref, counts_ref, ubuf_ref, umat_ref,
                       *, tiles_per_seq, alpha, epg):
    i = pl.program_id(0)
    tm, d = x_ref.shape

    @pl.when(i % tiles_per_seq == 0)
    def _():
        ubuf_ref[0:SUBLANES, :] = jnp.zeros((SUBLANES, d), _F32)

    x = x_ref[...]
    h = _dot(x.astype(_BF16), win_ref[...])
    u = h[:, d:2 * d] * h[:, 2 * d:]
    ubuf_ref[SUBLANES:SUBLANES + tm, :] = u
    cw = cw_ref[...]
    uc = (cw[0:1, :] * ubuf_ref[SUBLANES - 2:SUBLANES - 2 + tm, :]
          + cw[1:2, :] * ubuf_ref[SUBLANES - 1:SUBLANES - 1 + tm, :]
          + cw[2:3, :] * u)
    ubuf_ref[0:SUBLANES, :] = ubuf_ref[tm:tm + SUBLANES, :]
    y = _dot((h[:, :d] * uc).astype(_BF16), wout_ref[...])
    x1 = _layer_norm(alpha * x + y, g_ref[...], b_ref[...])
    x1_ref[...] = x1
    _store_token_words(x1t_ref, x1)
    _route_tail(x1, wt_ref, bt_ref, route_ref, routet_ref, counts_ref, umat_ref, epg=epg)


def _attn_out_kernel(x_ref, o_ref, wo_ref, g_ref, b_ref, wt_ref, bt_ref,
                     x1_ref, x1t_ref, route_ref, routet_ref, counts_ref, umat_ref, *, alpha, epg):
    y = _dot(o_ref[...], wo_ref[...])
    x1 = _layer_norm(alpha * x_ref[...] + y, g_ref[...], b_ref[...])
    x1_ref[...] = x1
    _store_token_words(x1t_ref, x1)
    _route_tail(x1, wt_ref, bt_ref, route_ref, routet_ref, counts_ref, umat_ref, epg=epg)


def _expert_ffn_kernel(bexp_ref, seg_ref, x_ref, rec_ref, win_hbm, wdn_hbm, y_ref,
                       wstage_in, wstage_dn, wsem, winb, wdnb, *, layer):
    i = pl.program_id(0)
    n_exp = (seg_ref.shape[0] - 1) // 2
    n_used = seg_ref[2 * n_exp]
    d, de2 = wstage_in.shape
    blk = rec_ref.shape[0]
    de = de2 // 2

    def weight_copies(e):
        return (pltpu.make_async_copy(win_hbm.at[layer, e], wstage_in, wsem.at[0]),
                pltpu.make_async_copy(wdn_hbm.at[layer, e], wstage_dn, wsem.at[1]))

    @pl.when(i == 0)
    def _():
        for c in weight_copies(bexp_ref[0]):
            c.start()

    @pl.when(i < n_used)
    def _():
        e = bexp_ref[i]

        @pl.when((i == 0) | (e != bexp_ref[jnp.maximum(i - 1, 0)]))
        def _():
            for c in weight_copies(e):
                c.wait()
            winb[...] = wstage_in[...].astype(_BF16)
            wdnb[...] = wstage_dn[...].astype(_BF16)
            nxt = seg_ref[n_exp + e] // blk

            @pl.when(nxt < n_used)
            def _():
                for c in weight_copies(bexp_ref[nxt]):
                    c.start()

        n_valid = seg_ref[e] - i * blk

        def ffn(rows):
            valid = lax.broadcasted_iota(jnp.int32, (rows, 1), 0) < n_valid
            x = jnp.where(valid, _load_token_words(x_ref, rows, d), 0.0)
            h = _dot(x.astype(_BF16), winb[...])
            g = h[:, :de]
            a = g * jax.nn.sigmoid(g) * h[:, de:]
            rec = rec_ref[0:rows, :]
            gate = jnp.where(rec[:, _E1:_E1 + 1] == e.astype(_F32), rec[:, _G1:_G1 + 1], rec[:, _G2:_G2 + 1])
            gate = jnp.where(valid, gate, 0.0)
            _store_token_words(y_ref, _dot(a.astype(_BF16), wdnb[...]) * gate)

        half = blk // 2

        @pl.when(n_valid > half)
        def _():
            ffn(blk)

        @pl.when(n_valid <= half)
        def _():
            ffn(half)
            y_ref[half * (y_ref.shape[0] // blk):, :] = jnp.zeros(
                (half * (y_ref.shape[0] // blk), y_ref.shape[1]), y_ref.dtype)

    @pl.when(i >= n_used)
    def _():
        y_ref[...] = jnp.zeros_like(y_ref)


def _combine_body(x1_ref, ya_ref, yb_ref, g_ref, b_ref, *, alpha):
    tm, d = x1_ref.shape
    ffn = _load_token_words(ya_ref, tm, d) + _load_token_words(yb_ref, tm, d)
    return _layer_norm(alpha * x1_ref[...] + ffn, g_ref[...], b_ref[...])


def _combine_kernel(x1_ref, ya_ref, yb_ref, g_ref, b_ref, x2_ref, *, alpha):
    x2_ref[...] = _combine_body(x1_ref, ya_ref, yb_ref, g_ref, b_ref, alpha=alpha)


def _combine_qkv_kernel(x1_ref, ya_ref, yb_ref, g_ref, b_ref, wk_ref, wv_ref, wq_ref,
                        x2_ref, k_ref, v_ref, q_ref, *, alpha, q_scale):
    x2 = _combine_body(x1_ref, ya_ref, yb_ref, g_ref, b_ref, alpha=alpha)
    x2_ref[...] = x2
    xb = x2.astype(_BF16)
    k_ref[...] = _dot(xb, wk_ref[...]).astype(_BF16)
    q_ref[...] = (_dot(xb, wq_ref[...]) * q_scale).astype(_BF16)
    v_ref[...] = _dot(xb, wv_ref[...]).astype(_BF16)


def _diff_attn_kernel(lam_ref, g_ref, q_ref, k_ref, v_ref, o_ref, lamfull_ref, s_ref, p_ref, *, lam_init):
    bi, hi = pl.program_id(0), pl.program_id(1)
    seq = k_ref.shape[0]
    hw = g_ref.shape[1]
    heads = k_ref.shape[1] // hw
    tq = tk = ATTN_BLOCK
    nk = seq // tk
    hd = lam_ref.shape[1]

    @pl.when((bi == 0) & (hi == 0))
    def _():
        lam = lam_ref[...]
        a = jnp.sum(lam[0:1, :] * lam[1:2, :], axis=-1, keepdims=True)
        b = jnp.sum(lam[2:3, :] * lam[3:4, :], axis=-1, keepdims=True)
        lamfull_ref[...] = jnp.broadcast_to(jnp.exp(a) - jnp.exp(b) + lam_init, lamfull_ref.shape)

    lam_full = lamfull_ref[0:1, 0:1]
    lane = lax.broadcasted_iota(jnp.int32, (tq, hw), 1)
    row = lax.broadcasted_iota(jnp.int32, (2 * tq, tk), 0)
    col = lax.broadcasted_iota(jnp.int32, (2 * tq, tk), 1)
    causal = col <= jnp.where(row < tq, row, row - tq)
    ones = jnp.ones((seq, hw), _BF16)

    step = 0
    for i in reversed(range(nk)):
        for h in range(heads):
            cols = slice(h * hw, (h + 1) * hw)
            par = step % s_ref.shape[0]
            step += 1
            kv = (i + 1) * tk
            q = q_ref[i * tq:(i + 1) * tq, cols]
            zero = jnp.zeros_like(q)
            qcat = jnp.concatenate([jnp.where(lane < hd, q, zero), jnp.where(lane >= hd, q, zero)], axis=0)
            s_ref[par, :, 0:kv] = lax.dot_general(qcat, k_ref[0:kv, cols], _NT, preferred_element_type=_F32)
            s_ref[par, :, i * tk:kv] = jnp.where(causal, s_ref[par, :, i * tk:kv], -jnp.inf)
            rowmax = jnp.max(s_ref[par, :, 0:kv], axis=1, keepdims=True)
            p_ref[par, :, 0:kv] = jnp.exp2(s_ref[par, :, 0:kv] - rowmax).astype(_BF16)
            v_aug = jnp.concatenate([v_ref[0:kv, cols], ones[0:kv, :]], axis=1)
            acc = _dot(p_ref[par, :, 0:kv], v_aug)
            o = acc[:, 0:hw] * (1.0 / acc[:, hw:hw + 1])
            od = o[0:tq, :] - lam_full * o[tq:, :]
            od = od * lax.rsqrt(jnp.mean(od * od, axis=1, keepdims=True) + RMS_EPS)
            o_ref[i * tq:(i + 1) * tq, cols] = (od * g_ref[...] * (1.0 - lam_init)).astype(o_ref.dtype)


def _params(*sem):
    return pltpu.CompilerParams(dimension_semantics=sem, vmem_limit_bytes=VMEM_LIMIT_BYTES)


def _full(shape):
    return pl.BlockSpec(shape, lambda *_: (0,) * len(shape))


def _router_weights(rg_w, rg_b, re_w, re_b):
    d = rg_w.shape[0]
    n = rg_w.shape[1] + re_w.shape[1]
    w = jnp.concatenate([rg_w, re_w, jnp.zeros((d, LANES - n), _F32)], axis=1).T
    b = jnp.concatenate([rg_b, re_b, jnp.zeros((LANES - n,), _F32)])[:, None]
    wh = w.astype(_BF16)
    wl = (w - wh.astype(_F32)).astype(_BF16)
    return jnp.concatenate([wh, wl], axis=0), b


def _router_rows(n_experts):
    return -(-(N_GROUPS + n_experts) // (2 * SUBLANES)) * (2 * SUBLANES)


def _route_outs(t, d, tm, nr):
    per = d // LANES
    shapes = (jax.ShapeDtypeStruct((t, d), _F32),
              jax.ShapeDtypeStruct((t * per // 2, LANES), jnp.uint32),
              jax.ShapeDtypeStruct((t, LANES), _F32),
              jax.ShapeDtypeStruct((SUBLANES, t), _F32),
              jax.ShapeDtypeStruct((nr, 1), _F32))
    specs = (pl.BlockSpec((tm, d), lambda i: (i, 0)),
             pl.BlockSpec((tm * per // 2, LANES), lambda i: (i, 0)),
             pl.BlockSpec((tm, LANES), lambda i: (i, 0)),
             pl.BlockSpec((SUBLANES, tm), lambda i: (0, i)),
             pl.BlockSpec((nr, 1), lambda i: (0, 0)))
    return shapes, specs


def _conv_mixer(x, w_in, conv_w, w_out, ln_g, ln_b, router, *, seq, alpha, epg):
    t, d = x.shape
    tm = MIXER_TILE
    wt, bt = router
    shapes, specs = _route_outs(t, d, tm, _router_rows(N_GROUPS * epg))
    return pl.pallas_call(
        functools.partial(_conv_mixer_kernel, tiles_per_seq=seq // tm, alpha=alpha, epg=epg),
        grid=(t // tm,),
        in_specs=[pl.BlockSpec((tm, d), lambda i: (i, 0)),
                  _full(w_in.shape), _full(conv_w.shape), _full(w_out.shape),
                  _full((1, d)), _full((1, d)), _full(wt.shape), _full(bt.shape)],
        out_specs=specs, out_shape=shapes,
        scratch_shapes=[pltpu.VMEM((tm + SUBLANES, d), _F32), pltpu.VMEM((tm, tm), _BF16)],
        compiler_params=_params("arbitrary"),
        name="conv_mixer_ln_router",
    )(x, w_in, conv_w, w_out, ln_g[None, :], ln_b[None, :], wt, bt)


def _attn_out(x, o, w_o, ln_g, ln_b, router, *, alpha, epg):
    t, d = x.shape
    tm = MIXER_TILE
    wt, bt = router
    shapes, specs = _route_outs(t, d, tm, _router_rows(N_GROUPS * epg))
    return pl.pallas_call(
        functools.partial(_attn_out_kernel, alpha=alpha, epg=epg),
        grid=(t // tm,),
        in_specs=[pl.BlockSpec((tm, d), lambda i: (i, 0)),
                  pl.BlockSpec((tm, d), lambda i: (i, 0)),
                  _full(w_o.shape), _full((1, d)), _full((1, d)), _full(wt.shape), _full(bt.shape)],
        out_specs=specs, out_shape=shapes,
        scratch_shapes=[pltpu.VMEM((tm, tm), _BF16)],
        compiler_params=_params("arbitrary"),
        name="attn_out_ln_router",
    )(x, o, w_o, ln_g[None, :], ln_b[None, :], wt, bt)


def _dispatch_plan(route_t, counts, n_experts):
    t = route_t.shape[1]
    blk = EXPERT_BLOCK
    cnt = counts[N_GROUPS:N_GROUPS + n_experts, 0].astype(jnp.int32)
    padded = (cnt + blk - 1) // blk * blk
    pad_end = jnp.cumsum(padded)
    pad_start = pad_end - padded
    ids = jnp.arange(n_experts, dtype=jnp.int32)

    def sorted_row(e_slot, r_slot):
        e = route_t[e_slot].astype(jnp.int32)
        start = jnp.sum(jnp.where(ids[:, None] == e[None, :], pad_start[:, None], 0), axis=0)
        return start + route_t[r_slot].astype(jnp.int32)

    dest = jnp.concatenate([sorted_row(_E1, _R1), sorted_row(_E2, _R2)])
    n_blocks = (2 * t + n_experts * blk) // blk
    block_start = jnp.arange(n_blocks, dtype=jnp.int32) * blk
    block_expert = jnp.minimum(jnp.sum(pad_end[None, :] <= block_start[:, None], axis=1),
                               n_experts - 1).astype(jnp.int32)
    n_used = (pad_end[-1:] // blk).astype(jnp.int32)
    seg = jnp.concatenate([pad_start + cnt, pad_end, n_used]).astype(jnp.int32)
    return dest, block_expert, seg


def _sc_mesh():
    return plsc.VectorSubcoreMesh(core_axis_name="c", subcore_axis_name="s",
                                  num_cores=SC_CORES, num_subcores=SC_SUBCORES)


def _sc_worker_chunks(t, chunk):
    chunks = t // (SC_CORES * SC_SUBCORES * chunk)
    wid = lax.axis_index("c") * SC_SUBCORES + lax.axis_index("s")
    return wid * chunks, chunks


def _sc_dispatch(x1t, route, dest, n_rows):
    t, per, lanes = x1t.shape
    c = 2 * SC_CHUNK

    def body(x_hbm, r_hbm, d_hbm, xo_hbm, ro_hbm, xbuf, rbuf, idx1, idx2, lsem, ssem):
        first, chunks = _sc_worker_chunks(t, c)
        pltpu.sync_copy(d_hbm.at[pl.ds(first * c, chunks * c)], idx1)
        pltpu.sync_copy(d_hbm.at[pl.ds(t + first * c, chunks * c)], idx2)

        def loads(j, slot):
            rows = pl.ds((first + j) * c, c)
            return (pltpu.make_async_copy(x_hbm.at[rows], xbuf.at[slot], lsem.at[2 * slot]),
                    pltpu.make_async_copy(r_hbm.at[rows], rbuf.at[slot], lsem.at[2 * slot + 1]))

        for cp in loads(0, 0):
            cp.start()

        @pl.loop(0, chunks)
        def _(j):
            slot = j % 2

            @pl.when(j + 1 < chunks)
            def _():
                for cp in loads(j + 1, 1 - slot):
                    cp.start()

            for cp in loads(j, slot):
                cp.wait()
            i1 = idx1.at[pl.ds(j * c, c)]
            i2 = idx2.at[pl.ds(j * c, c)]
            scatters = (pltpu.make_async_copy(xbuf.at[slot], xo_hbm.at[i1], ssem.at[0]),
                        pltpu.make_async_copy(xbuf.at[slot], xo_hbm.at[i2], ssem.at[1]),
                        pltpu.make_async_copy(rbuf.at[slot], ro_hbm.at[i1], ssem.at[2]),
                        pltpu.make_async_copy(rbuf.at[slot], ro_hbm.at[i2], ssem.at[3]))
            for cp in scatters:
                cp.start()
            for cp in scatters:
                cp.wait()

    per_worker = t // (SC_CORES * SC_SUBCORES)
    return pl.kernel(
        body,
        out_type=(jax.ShapeDtypeStruct((n_rows, per, lanes), x1t.dtype),
                  jax.ShapeDtypeStruct((n_rows, lanes), route.dtype)),
        mesh=_sc_mesh(),
        scratch_types=[pltpu.VMEM((2, c, per, lanes), x1t.dtype),
                       pltpu.VMEM((2, c, lanes), route.dtype),
                       pltpu.VMEM((per_worker,), jnp.int32),
                       pltpu.VMEM((per_worker,), jnp.int32),
                       pltpu.SemaphoreType.DMA((4,)),
                       pltpu.SemaphoreType.DMA((4,))],
        name="sc_dispatch",
    )(x1t, route, dest)


def _sc_combine(y_rows, dest, t):
    _, per, lanes = y_rows.shape
    c = SC_CHUNK

    def body(y_hbm, d_hbm, oa_hbm, ob_hbm, buf, idx1, idx2, gsem, osem):
        first, chunks = _sc_worker_chunks(t, c)
        pltpu.sync_copy(d_hbm.at[pl.ds(first * c, chunks * c)], idx1)
        pltpu.sync_copy(d_hbm.at[pl.ds(t + first * c, chunks * c)], idx2)

        def stores(j, slot):
            rows = pl.ds((first + j) * c, c)
            return (pltpu.make_async_copy(buf.at[slot, 0], oa_hbm.at[rows], osem.at[2 * slot]),
                    pltpu.make_async_copy(buf.at[slot, 1], ob_hbm.at[rows], osem.at[2 * slot + 1]))

        @pl.loop(0, chunks)
        def _(j):
            slot = j % 2

            @pl.when(j >= 2)
            def _():
                for cp in stores(j - 2, slot):
                    cp.wait()

            gathers = (pltpu.make_async_copy(y_hbm.at[idx1.at[pl.ds(j * c, c)]], buf.at[slot, 0], gsem.at[0]),
                       pltpu.make_async_copy(y_hbm.at[idx2.at[pl.ds(j * c, c)]], buf.at[slot, 1], gsem.at[1]))
            for cp in gathers:
                cp.start()
            for cp in gathers:
                cp.wait()
            for cp in stores(j, slot):
                cp.start()

        for j in range(max(chunks - 2, 0), chunks):
            for cp in stores(j, j % 2):
                cp.wait()

    per_worker = t // (SC_CORES * SC_SUBCORES)
    out = jax.ShapeDtypeStruct((t, per, lanes), y_rows.dtype)
    return pl.kernel(
        body,
        out_type=(out, out),
        mesh=_sc_mesh(),
        scratch_types=[pltpu.VMEM((2, 2, c, per, lanes), y_rows.dtype),
                       pltpu.VMEM((per_worker,), jnp.int32),
                       pltpu.VMEM((per_worker,), jnp.int32),
                       pltpu.SemaphoreType.DMA((2,)),
                       pltpu.SemaphoreType.DMA((4,))],
        name="sc_combine",
    )(y_rows, dest)


def _expert_ffn(x_sorted, r_sorted, w_in, w_down, layer, block_expert, seg):
    n_rows, per, _ = x_sorted.shape
    _, n_exp, d, de2 = w_in.shape
    de = w_down.shape[2]
    blk = EXPERT_BLOCK
    n_blocks = block_expert.shape[0]

    def used(i, be, sg):
        return jnp.minimum(i, sg[2 * n_exp] - 1), 0

    y = pl.pallas_call(
        functools.partial(_expert_ffn_kernel, layer=layer),
        grid_spec=pltpu.PrefetchScalarGridSpec(
            num_scalar_prefetch=2,
            grid=(n_blocks,),
            in_specs=[pl.BlockSpec((blk * per, LANES), used),
                      pl.BlockSpec((blk, LANES), used),
                      pl.BlockSpec(memory_space=pl.ANY),
                      pl.BlockSpec(memory_space=pl.ANY)],
            out_specs=pl.BlockSpec((blk * per, LANES), lambda i, be, sg: (i, 0)),
            scratch_shapes=[pltpu.VMEM((d, de2), _F32),
                            pltpu.VMEM((de, d), _F32),
                            pltpu.SemaphoreType.DMA((2,)),
                            pltpu.VMEM((d, de2), _BF16),
                            pltpu.VMEM((de, d), _BF16)]),
        out_shape=jax.ShapeDtypeStruct((n_rows * per, LANES), jnp.uint32),
        compiler_params=_params("arbitrary"),
        name="expert_ffn",
    )(block_expert, seg, x_sorted.reshape(n_rows * per, LANES), r_sorted, w_in, w_down)
    return y.reshape(n_rows, per, LANES)


def _combine(x1, ya, yb, ln_g, ln_b, *, alpha):
    t, d = x1.shape
    tm = ROW_TILE
    per = ya.shape[1]
    row = pl.BlockSpec((tm, d), lambda i: (i, 0))
    words = pl.BlockSpec((tm * per, LANES), lambda i: (i, 0))
    return pl.pallas_call(
        functools.partial(_combine_kernel, alpha=alpha),
        grid=(t // tm,),
        in_specs=[row, words, words, _full((1, d)), _full((1, d))],
        out_specs=row,
        out_shape=jax.ShapeDtypeStruct((t, d), _F32),
        compiler_params=_params("arbitrary"),
        name="combine_ln",
    )(x1, ya.reshape(t * per, LANES), yb.reshape(t * per, LANES), ln_g[None, :], ln_b[None, :])


def _combine_qkv(x1, ya, yb, ln_g, ln_b, w_k, w_v, w_q, *, alpha, q_scale):
    t, d = x1.shape
    tm = ROW_TILE
    per = ya.shape[1]
    row = pl.BlockSpec((tm, d), lambda i: (i, 0))
    words = pl.BlockSpec((tm * per, LANES), lambda i: (i, 0))
    return pl.pallas_call(
        functools.partial(_combine_qkv_kernel, alpha=alpha, q_scale=q_scale),
        grid=(t // tm,),
        in_specs=[row, words, words, _full((1, d)), _full((1, d)),
                  _full((d, d)), _full((d, d)), _full((d, d))],
        out_specs=(row, row, row, row),
        out_shape=(jax.ShapeDtypeStruct((t, d), _F32),
                   jax.ShapeDtypeStruct((t, d), _BF16),
                   jax.ShapeDtypeStruct((t, d), _BF16),
                   jax.ShapeDtypeStruct((t, d), _BF16)),
        compiler_params=_params("arbitrary"),
        name="combine_ln_qkv",
    )(x1, ya.reshape(t * per, LANES), yb.reshape(t * per, LANES), ln_g[None, :], ln_b[None, :],
      w_k, w_v, w_q)


def _diff_attention(q, k, v, lam, subln_g, *, batch, seq, lam_init):
    t, d = q.shape
    tq = ATTN_BLOCK
    hw = d // N_HEADS
    head = pl.BlockSpec((seq, ATTN_HEADS * hw), lambda b, h: (b, h))
    return pl.pallas_call(
        functools.partial(_diff_attn_kernel, lam_init=lam_init),
        grid=(batch, N_HEADS // ATTN_HEADS),
        in_specs=[_full(lam.shape), _full((1, hw)), head, head, head],
        out_specs=head,
        out_shape=jax.ShapeDtypeStruct((t, d), _BF16),
        scratch_shapes=[pltpu.VMEM((1, LANES), _F32),
                        pltpu.VMEM((ATTN_BUFFERS, 2 * tq, seq), _F32),
                        pltpu.VMEM((ATTN_BUFFERS, 2 * tq, seq), _BF16)],
        compiler_params=_params("arbitrary", "arbitrary"),
        name="diff_attention",
    )(lam, subln_g[None, :], q, k, v)


def _moe(x1t, route, route_t, counts, w_in, w_down, layer):
    n_exp, d = w_in.shape[1:3]
    dest, block_expert, seg = _dispatch_plan(route_t, counts, n_exp)
    x1t = x1t.reshape(-1, d // (2 * LANES), LANES)
    x_sorted, r_sorted = _sc_dispatch(x1t, route, dest, block_expert.shape[0] * EXPERT_BLOCK)
    y_rows = _expert_ffn(x_sorted, r_sorted, w_in, w_down, layer, block_expert, seg)
    return _sc_combine(y_rows, dest, x1t.shape[0])


def kernel(x, a_w_in, a_conv_w, a_w_out, kv_w, b_w_q, b_lambda, b_subln_g, b_w_o,
           ln1_g, ln1_b, ln2_g, ln2_b, rg_w, rg_b, re_w, re_b, e_w_in, e_w_down):
    batch, seq, d = x.shape
    depth = ln1_g.shape[0]
    assert depth == 2 and a_w_in.shape[0] == 1 and b_w_q.shape[0] == 1
    assert seq % MIXER_TILE == 0 and seq % ROW_TILE == 0 and seq % ATTN_BLOCK == 0
    t = batch * seq
    assert t % (SC_CORES * SC_SUBCORES * 2 * SC_CHUNK) == 0
    alpha = (2.0 * depth) ** 0.25
    epg = re_w.shape[2] // N_GROUPS
    head_dim = d // (2 * N_HEADS)
    xf = x.reshape(t, d)

    router0 = _router_weights(rg_w[0], rg_b[0], re_w[0], re_b[0])
    x1, x1t, route, route_t, counts = _conv_mixer(xf, a_w_in[0].astype(_BF16), a_conv_w[0],
                                                  a_w_out[0].astype(_BF16), ln1_g[0], ln1_b[0], router0,
                                                  seq=seq, alpha=alpha, epg=epg)
    ya, yb = _moe(x1t, route, route_t, counts, e_w_in, e_w_down, 0)
    q_scale = head_dim ** -0.5 * math.log2(math.e)
    x2, k, v, q = _combine_qkv(x1, ya, yb, ln2_g[0], ln2_b[0],
                               kv_w[:, :d].astype(_BF16), kv_w[:, d:].astype(_BF16),
                               b_w_q[0].astype(_BF16), alpha=alpha, q_scale=q_scale)

    lam_init = 0.8 - 0.6 * math.exp(-0.3 * 1)
    o = _diff_attention(q, k, v, b_lambda[0], b_subln_g[0], batch=batch, seq=seq, lam_init=lam_init)
    router1 = _router_weights(rg_w[1], rg_b[1], re_w[1], re_b[1])
    x1, x1t, route, route_t, counts = _attn_out(x2, o, b_w_o[0].astype(_BF16), ln1_g[1], ln1_b[1],
                                                router1, alpha=alpha, epg=epg)
    ya, yb = _moe(x1t, route, route_t, counts, e_w_in, e_w_down, 1)
    out = _combine(x1, ya, yb, ln2_g[1], ln2_b[1], alpha=alpha)
    return out.reshape(batch, seq, d)
```

```python
import functools
import math

import jax
import jax.numpy as jnp
from jax import lax
from jax.experimental import pallas as pl
from jax.experimental.pallas import tpu as pltpu
from jax.experimental.pallas import tpu_sc as plsc

N_HEADS = 8
N_GROUPS = 4
LN_EPS = 1e-5
RMS_EPS = 1e-5

LANES = 128
SUBLANES = 8
VMEM_LIMIT_BYTES = 56 * 1024 * 1024

SC_CORES = 2
SC_SUBCORES = 16

SC_CHUNK = 32
MIXER_TILE = 1024
ROW_TILE = 1024
EXPERT_BLOCK = 512
ATTN_BLOCK = 256
ATTN_HEADS = 2
ATTN_BUFFERS = 4

_E1, _E2, _R1, _R2, _G1, _G2 = range(6)

_F32 = jnp.float32
_BF16 = jnp.bfloat16
_NT = (((1,), (1,)), ((), ()))


def _dot(a, b):
    return jnp.dot(a, b, preferred_element_type=_F32)


def _layer_norm(z, g, b):
    mu = jnp.mean(z, axis=-1, keepdims=True)
    d = z - mu
    var = jnp.mean(d * d, axis=-1, keepdims=True)
    return d * lax.rsqrt(var + LN_EPS) * g + b


def _route_tail(x1, wt_ref, bt_ref, route_ref, routet_ref, counts_ref, umat_ref, *, epg):
    tm = x1.shape[0]
    nr = counts_ref.shape[0]
    half = wt_ref.shape[0] // 2

    @pl.when(pl.program_id(0) == 0)
    def _():
        counts_ref[...] = jnp.zeros_like(counts_ref)
        before = (lax.broadcasted_iota(jnp.int32, (tm, tm), 0) < lax.broadcasted_iota(jnp.int32, (tm, tm), 1))
        umat_ref[...] = jnp.where(before, 1.0, 0.0).astype(_BF16)

    xh = x1.astype(_BF16)
    xl = (x1 - xh.astype(_F32)).astype(_BF16)
    a = lax.dot_general(wt_ref[...], xh, _NT, preferred_element_type=_F32)
    b = lax.dot_general(wt_ref[0:half, :], xl, _NT, preferred_element_type=_F32)
    logits = a[0:nr, :] + a[half:half + nr, :] + b[0:nr, :] + bt_ref[0:nr, :]

    rowf = lax.broadcasted_iota(jnp.int32, (nr, tm), 0).astype(_F32)
    neg = -jnp.inf
    big = float(nr)
    gl = jnp.where(rowf < float(N_GROUPS), logits, neg)
    gmax = jnp.max(gl, axis=0, keepdims=True)
    gidx = jnp.min(jnp.where(gl == gmax, rowf, big), axis=0, keepdims=True)
    gtop = 1.0 / jnp.sum(jnp.exp(gl - gmax), axis=0, keepdims=True)

    lo = float(N_GROUPS) + gidx * float(epg)
    el = jnp.where((rowf >= lo) & (rowf < lo + float(epg)), logits, neg)
    m1 = jnp.max(el, axis=0, keepdims=True)
    i1 = jnp.min(jnp.where(el == m1, rowf, big), axis=0, keepdims=True)
    el2 = jnp.where(rowf == i1, neg, el)
    m2 = jnp.max(el2, axis=0, keepdims=True)
    i2 = jnp.min(jnp.where(el2 == m2, rowf, big), axis=0, keepdims=True)
    w2 = jnp.exp(m2 - m1)
    inv = 1.0 / (1.0 + w2)
    g1 = gtop * inv
    g2 = gtop * w2 * inv

    onehot = jnp.where((rowf == i1) | (rowf == i2), 1.0, 0.0)
    total = _dot(onehot.astype(_BF16), umat_ref[...]) + counts_ref[...]
    r1 = jnp.sum(jnp.where(rowf == i1, total, 0.0), axis=0, keepdims=True)
    r2 = jnp.sum(jnp.where(rowf == i2, total, 0.0), axis=0, keepdims=True)
    counts_ref[...] += jnp.sum(onehot, axis=1, keepdims=True)

    slot = lax.broadcasted_iota(jnp.int32, (SUBLANES, tm), 0)
    rec = jnp.zeros((SUBLANES, tm), _F32)
    fields = ((_E1, i1 - float(N_GROUPS)), (_E2, i2 - float(N_GROUPS)), (_R1, r1), (_R2, r2), (_G1, g1), (_G2, g2))
    for s, val in fields:
        rec = jnp.where(slot == s, val, rec)
    routet_ref[...] = rec
    rec = jnp.concatenate([rec, jnp.zeros((LANES - SUBLANES, tm), _F32)], axis=0)
    route_ref[...] = rec.T


def _store_token_words(ref, x):
    n, d = x.shape
    rows = d // (2 * LANES)
    bits = lax.bitcast_convert_type(x.astype(_BF16).astype(_F32), jnp.uint32)
    for s in range(rows):
        lo = bits[:, s * LANES:(s + 1) * LANES] >> 16
        hi = bits[:, d // 2 + s * LANES:d // 2 + (s + 1) * LANES] & jnp.uint32(0xFFFF0000)
        ref[pl.ds(s, n, stride=rows), :] = lo | hi


def _load_token_words(ref, n, d):
    rows = d // (2 * LANES)
    words = [ref[pl.ds(s, n, stride=rows), :] for s in range(rows)]
    lo = [lax.bitcast_convert_type(w << 16, _F32) for w in words]
    hi = [lax.bitcast_convert_type(w & jnp.uint32(0xFFFF0000), _F32) for w in words]
    return jnp.concatenate(lo + hi, axis=1)


def _conv_mixer_kernel(x_ref, win_hbm, cw_ref, wout_hbm, g_ref, b_ref, wt_ref, bt_ref,
                       x1_ref, x1t_ref, route_ref, routet_ref, counts_ref, ubuf_ref, umat_ref,
                       win_ref, wout_ref, wstage_ref, wsem, *, layer, tiles_per_seq, alpha, epg):
    i = pl.program_id(0)
    tm, d = x_ref.shape

    @pl.when(i == 0)
    def _():
        _, rows, _ = wstage_ref.shape
        panels = d // rows

        def panel(j):
            slot, k = j % 2, j % panels
            if j < panels:
                return pltpu.make_async_copy(win_hbm.at[layer, pl.ds(k * rows, rows), :],
                                             wstage_ref.at[slot], wsem.at[slot])
            return pltpu.make_async_copy(wout_hbm.at[layer, pl.ds(k * rows, rows), :],
                                         wstage_ref.at[slot, :, pl.ds(0, d)], wsem.at[slot])

        panel(0).start()
        for j in range(2 * panels):
            slot, k = j % 2, j % panels
            if j + 1 < 2 * panels:
                panel(j + 1).start()
            panel(j).wait()
            if j < panels:
                win_ref[k * rows:(k + 1) * rows, :] = wstage_ref[slot].astype(_BF16)
            else:
                wout_ref[k * rows:(k + 1) * rows, :] = wstage_ref[slot, :, 0:d].astype(_BF16)

    @pl.when(i % tiles_per_seq == 0)
    def _():
        ubuf_ref[0:SUBLANES, :] = jnp.zeros((SUBLANES, d), _F32)

    x = x_ref[...]
    h = _dot(x.astype(_BF16), win_ref[...])
    u = h[:, d:2 * d] * h[:, 2 * d:]
    ubuf_ref[SUBLANES:SUBLANES + tm, :] = u
    cw = cw_ref[...]
    uc = (cw[0:1, :] * ubuf_ref[SUBLANES - 2:SUBLANES - 2 + tm, :]
          + cw[1:2, :] * ubuf_ref[SUBLANES - 1:SUBLANES - 1 + tm, :]
          + cw[2:3, :] * u)
    ubuf_ref[0:SUBLANES, :] = ubuf_ref[tm:tm + SUBLANES, :]
    y = _dot((h[:, :d] * uc).astype(_BF16), wout_ref[...])
    x1 = _layer_norm(alpha * x + y, g_ref[...], b_ref[...])
    x1_ref[...] = x1
    _store_token_words(x1t_ref, x1)
    _route_tail(x1, wt_ref, bt_ref, route_ref, routet_ref, counts_ref, umat_ref, epg=epg)


def _attn_out_kernel(x_ref, o_ref, wo_ref, g_ref, b_ref, wt_ref, bt_ref,
                     x1_ref, x1t_ref, route_ref, routet_ref, counts_ref, umat_ref, *, alpha, epg):
    y = _dot(o_ref[...], wo_ref[...])
    x1 = _layer_norm(alpha * x_ref[...] + y, g_ref[...], b_ref[...])
    x1_ref[...] = x1
    _store_token_words(x1t_ref, x1)
    _route_tail(x1, wt_ref, bt_ref, route_ref, routet_ref, counts_ref, umat_ref, epg=epg)


def _expert_ffn_kernel(bexp_ref, seg_ref, x_ref, rec_ref, win_hbm, wdn_hbm, y_ref,
                       wstage_in, wstage_dn, wsem, winb, wdnb, *, layer):
    i = pl.program_id(0)
    n_exp = (seg_ref.shape[0] - 1) // 2
    n_used = seg_ref[2 * n_exp]
    d, de2 = wstage_in.shape
    blk = rec_ref.shape[0]
    de = de2 // 2

    def weight_copies(e):
        return (pltpu.make_async_copy(win_hbm.at[layer, e], wstage_in, wsem.at[0]),
                pltpu.make_async_copy(wdn_hbm.at[layer, e], wstage_dn, wsem.at[1]))

    @pl.when(i == 0)
    def _():
        for c in weight_copies(bexp_ref[0]):
            c.start()

    @pl.when(i < n_used)
    def _():
        e = bexp_ref[i]

        @pl.when((i == 0) | (e != bexp_ref[jnp.maximum(i - 1, 0)]))
        def _():
            for c in weight_copies(e):
                c.wait()
            winb[...] = wstage_in[...].astype(_BF16)
            wdnb[...] = wstage_dn[...].astype(_BF16)
            nxt = seg_ref[n_exp + e] // blk

            @pl.when(nxt < n_used)
            def _():
                for c in weight_copies(bexp_ref[nxt]):
                    c.start()

        n_valid = seg_ref[e] - i * blk

        def ffn(rows):
            valid = lax.broadcasted_iota(jnp.int32, (rows, 1), 0) < n_valid
            x = jnp.where(valid, _load_token_words(x_ref, rows, d), 0.0)
            h = _dot(x.astype(_BF16), winb[...])
            g = h[:, :de]
            a = g * jax.nn.sigmoid(g) * h[:, de:]
            rec = rec_ref[0:rows, :]
            gate = jnp.where(rec[:, _E1:_E1 + 1] == e.astype(_F32), rec[:, _G1:_G1 + 1], rec[:, _G2:_G2 + 1])
            gate = jnp.where(valid, gate, 0.0)
            _store_token_words(y_ref, _dot(a.astype(_BF16), wdnb[...]) * gate)

        half = blk // 2

        @pl.when(n_valid > half)
        def _():
            ffn(blk)

        @pl.when(n_valid <= half)
        def _():
            ffn(half)
            y_ref[half * (y_ref.shape[0] // blk):, :] = jnp.zeros(
                (half * (y_ref.shape[0] // blk), y_ref.shape[1]), y_ref.dtype)

    @pl.when(i >= n_used)
    def _():
        y_ref[...] = jnp.zeros_like(y_ref)


def _combine_body(x1_ref, ya_ref, yb_ref, g_ref, b_ref, *, alpha):
    tm, d = x1_ref.shape
    ffn = _load_token_words(ya_ref, tm, d) + _load_token_words(yb_ref, tm, d)
    return _layer_norm(alpha * x1_ref[...] + ffn, g_ref[...], b_ref[...])


def _combine_kernel(x1_ref, ya_ref, yb_ref, g_ref, b_ref, x2_ref, *, alpha):
    x2_ref[...] = _combine_body(x1_ref, ya_ref, yb_ref, g_ref, b_ref, alpha=alpha)


def _combine_qkv_kernel(x1_ref, ya_ref, yb_ref, g_ref, b_ref, wk_ref, wv_ref, wq_ref,
                        x2_ref, k_ref, v_ref, q_ref, *, alpha, q_scale):
    x2 = _combine_body(x1_ref, ya_ref, yb_ref, g_ref, b_ref, alpha=alpha)
    x2_ref[...] = x2
    xb = x2.astype(_BF16)
    k_ref[...] = _dot(xb, wk_ref[...]).astype(_BF16)
    q_ref[...] = (_dot(xb, wq_ref[...]) * q_scale).astype(_BF16)
    v_ref[...] = _dot(xb, wv_ref[...]).astype(_BF16)


def _diff_attn_kernel(lam_ref, g_ref, q_ref, k_ref, v_ref, o_ref, lamfull_ref, s_ref, p_ref, *, lam_init):
    bi, hi = pl.program_id(0), pl.program_id(1)
    seq = k_ref.shape[0]
    hw = g_ref.shape[1]
    heads = k_ref.shape[1] // hw
    tq = tk = ATTN_BLOCK
    nk = seq // tk
    hd = lam_ref.shape[1]

    @pl.when((bi == 0) & (hi == 0))
    def _():
        lam = lam_ref[...]
        a = jnp.sum(lam[0:1, :] * lam[1:2, :], axis=-1, keepdims=True)
        b = jnp.sum(lam[2:3, :] * lam[3:4, :], axis=-1, keepdims=True)
        lamfull_ref[...] = jnp.broadcast_to(jnp.exp(a) - jnp.exp(b) + lam_init, lamfull_ref.shape)

    lam_full = lamfull_ref[0:1, 0:1]
    lane = lax.broadcasted_iota(jnp.int32, (tq, hw), 1)
    row = lax.broadcasted_iota(jnp.int32, (2 * tq, tk), 0)
    col = lax.broadcasted_iota(jnp.int32, (2 * tq, tk), 1)
    causal = col <= jnp.where(row < tq, row, row - tq)
    ones = jnp.ones((seq, hw), _BF16)

    step = 0
    for i in reversed(range(nk)):
        for h in range(heads):
            cols = slice(h * hw, (h + 1) * hw)
            par = step % s_ref.shape[0]
            step += 1
            kv = (i + 1) * tk
            q = q_ref[i * tq:(i + 1) * tq, cols]
            zero = jnp.zeros_like(q)
            qcat = jnp.concatenate([jnp.where(lane < hd, q, zero), jnp.where(lane >= hd, q, zero)], axis=0)
            s_ref[par, :, 0:kv] = lax.dot_general(qcat, k_ref[0:kv, cols], _NT, preferred_element_type=_F32)
            s_ref[par, :, i * tk:kv] = jnp.where(causal, s_ref[par, :, i * tk:kv], -jnp.inf)
            rowmax = jnp.max(s_ref[par, :, 0:kv], axis=1, keepdims=True)
            p_ref[par, :, 0:kv] = jnp.exp2(s_ref[par, :, 0:kv] - rowmax).astype(_BF16)
            v_aug = jnp.concatenate([v_ref[0:kv, cols], ones[0:kv, :]], axis=1)
            acc = _dot(p_ref[par, :, 0:kv], v_aug)
            o = acc[:, 0:hw] * (1.0 / acc[:, hw:hw + 1])
            od = o[0:tq, :] - lam_full * o[tq:, :]
            od = od * lax.rsqrt(jnp.mean(od * od, axis=1, keepdims=True) + RMS_EPS)
            o_ref[i * tq:(i + 1) * tq, cols] = (od * g_ref[...] * (1.0 - lam_init)).astype(o_ref.dtype)


def _params(*sem):
    return pltpu.CompilerParams(dimension_semantics=sem, vmem_limit_bytes=VMEM_LIMIT_BYTES)


def _full(shape):
    return pl.BlockSpec(shape, lambda *_: (0,) * len(shape))


def _router_weights(rg_w, rg_b, re_w, re_b):
    d = rg_w.shape[0]
    n = rg_w.shape[1] + re_w.shape[1]
    w = jnp.concatenate([rg_w, re_w, jnp.zeros((d, LANES - n), _F32)], axis=1).T
    b = jnp.concatenate([rg_b, re_b, jnp.zeros((LANES - n,), _F32)])[:, None]
    wh = w.astype(_BF16)
    wl = (w - wh.astype(_F32)).astype(_BF16)
    return jnp.concatenate([wh, wl], axis=0), b


def _router_rows(n_experts):
    return -(-(N_GROUPS + n_experts) // (2 * SUBLANES)) * (2 * SUBLANES)


def _route_outs(t, d, tm, nr):
    per = d // LANES
    shapes = (jax.ShapeDtypeStruct((t, d), _F32),
              jax.ShapeDtypeStruct((t * per // 2, LANES), jnp.uint32),
              jax.ShapeDtypeStruct((t, LANES), _F32),
              jax.ShapeDtypeStruct((SUBLANES, t), _F32),
              jax.ShapeDtypeStruct((nr, 1), _F32))
    specs = (pl.BlockSpec((tm, d), lambda i: (i, 0)),
             pl.BlockSpec((tm * per // 2, LANES), lambda i: (i, 0)),
             pl.BlockSpec((tm, LANES), lambda i: (i, 0)),
             pl.BlockSpec((SUBLANES, tm), lambda i: (0, i)),
             pl.BlockSpec((nr, 1), lambda i: (0, 0)))
    return shapes, specs


def _conv_mixer(x, w_in_all, w_out_all, layer, conv_w, ln_g, ln_b, router, *, seq, alpha, epg):
    t, d = x.shape
    tm = MIXER_TILE
    wt, bt = router
    shapes, specs = _route_outs(t, d, tm, _router_rows(N_GROUPS * epg))
    d3 = w_in_all.shape[2]
    return pl.pallas_call(
        functools.partial(_conv_mixer_kernel, layer=layer, tiles_per_seq=seq // tm, alpha=alpha, epg=epg),
        grid=(t // tm,),
        in_specs=[pl.BlockSpec((tm, d), lambda i: (i, 0)),
                  pl.BlockSpec(memory_space=pl.ANY), _full(conv_w.shape), pl.BlockSpec(memory_space=pl.ANY),
                  _full((1, d)), _full((1, d)), _full(wt.shape), _full(bt.shape)],
        out_specs=specs, out_shape=shapes,
        scratch_shapes=[pltpu.VMEM((tm + SUBLANES, d), _F32), pltpu.VMEM((tm, tm), _BF16),
                        pltpu.VMEM((d, d3), _BF16), pltpu.VMEM((d, d), _BF16),
                        pltpu.VMEM((2, d // 4, d3), _F32), pltpu.SemaphoreType.DMA((2,))],
        compiler_params=_params("arbitrary"),
        name="conv_mixer_ln_router",
    )(x, w_in_all, conv_w, w_out_all, ln_g[None, :], ln_b[None, :], wt, bt)


def _attn_out(x, o, w_o, ln_g, ln_b, router, *, alpha, epg):
    t, d = x.shape
    tm = MIXER_TILE
    wt, bt = router
    shapes, specs = _route_outs(t, d, tm, _router_rows(N_GROUPS * epg))
    return pl.pallas_call(
        functools.partial(_attn_out_kernel, alpha=alpha, epg=epg),
        grid=(t // tm,),
        in_specs=[pl.BlockSpec((tm, d), lambda i: (i, 0)),
                  pl.BlockSpec((tm, d), lambda i: (i, 0)),
                  _full(w_o.shape), _full((1, d)), _full((1, d)), _full(wt.shape), _full(bt.shape)],
        out_specs=specs, out_shape=shapes,
        scratch_shapes=[pltpu.VMEM((tm, tm), _BF16)],
        compiler_params=_params("arbitrary"),
        name="attn_out_ln_router",
    )(x, o, w_o, ln_g[None, :], ln_b[None, :], wt, bt)


def _dispatch_plan(route_t, counts, n_experts):
    t = route_t.shape[1]
    blk = EXPERT_BLOCK
    cnt = counts[N_GROUPS:N_GROUPS + n_experts, 0].astype(jnp.int32)
    padded = (cnt + blk - 1) // blk * blk
    pad_end = jnp.cumsum(padded)
    pad_start = pad_end - padded
    ids = jnp.arange(n_experts, dtype=jnp.int32)

    def sorted_row(e_slot, r_slot):
        e = route_t[e_slot].astype(jnp.int32)
        start = jnp.sum(jnp.where(ids[:, None] == e[None, :], pad_start[:, None], 0), axis=0)
        return start + route_t[r_slot].astype(jnp.int32)

    dest = jnp.concatenate([sorted_row(_E1, _R1), sorted_row(_E2, _R2)])
    n_blocks = (2 * t + n_experts * blk) // blk
    block_start = jnp.arange(n_blocks, dtype=jnp.int32) * blk
    block_expert = jnp.minimum(jnp.sum(pad_end[None, :] <= block_start[:, None], axis=1),
                               n_experts - 1).astype(jnp.int32)
    n_used = (pad_end[-1:] // blk).astype(jnp.int32)
    seg = jnp.concatenate([pad_start + cnt, pad_end, n_used]).astype(jnp.int32)
    return dest, block_expert, seg


def _sc_mesh():
    return plsc.VectorSubcoreMesh(core_axis_name="c", subcore_axis_name="s",
                                  num_cores=SC_CORES, num_subcores=SC_SUBCORES)


def _sc_worker_chunks(t, chunk):
    chunks = t // (SC_CORES * SC_SUBCORES * chunk)
    wid = lax.axis_index("c") * SC_SUBCORES + lax.axis_index("s")
    return wid * chunks, chunks


def _sc_dispatch(x1t, route, dest, n_rows):
    t, per, lanes = x1t.shape
    c = 2 * SC_CHUNK

    def body(x_hbm, r_hbm, d_hbm, xo_hbm, ro_hbm, xbuf, rbuf, idx1, idx2, lsem, ssem):
        first, chunks = _sc_worker_chunks(t, c)
        pltpu.sync_copy(d_hbm.at[pl.ds(first * c, chunks * c)], idx1)
        pltpu.sync_copy(d_hbm.at[pl.ds(t + first * c, chunks * c)], idx2)

        def loads(j, slot):
            rows = pl.ds((first + j) * c, c)
            return (pltpu.make_async_copy(x_hbm.at[rows], xbuf.at[slot], lsem.at[2 * slot]),
                    pltpu.make_async_copy(r_hbm.at[rows], rbuf.at[slot], lsem.at[2 * slot + 1]))

        for cp in loads(0, 0):
            cp.start()

        @pl.loop(0, chunks)
        def _(j):
            slot = j % 2

            @pl.when(j + 1 < chunks)
            def _():
                for cp in loads(j + 1, 1 - slot):
                    cp.start()

            for cp in loads(j, slot):
                cp.wait()
            i1 = idx1.at[pl.ds(j * c, c)]
            i2 = idx2.at[pl.ds(j * c, c)]
            scatters = (pltpu.make_async_copy(xbuf.at[slot], xo_hbm.at[i1], ssem.at[0]),
                        pltpu.make_async_copy(xbuf.at[slot], xo_hbm.at[i2], ssem.at[1]),
                        pltpu.make_async_copy(rbuf.at[slot], ro_hbm.at[i1], ssem.at[2]),
                        pltpu.make_async_copy(rbuf.at[slot], ro_hbm.at[i2], ssem.at[3]))
            for cp in scatters:
                cp.start()
            for cp in scatters:
                cp.wait()

    per_worker = t // (SC_CORES * SC_SUBCORES)
    return pl.kernel(
        body,
        out_type=(jax.ShapeDtypeStruct((n_rows, per, lanes), x1t.dtype),
                  jax.ShapeDtypeStruct((n_rows, lanes), route.dtype)),
        mesh=_sc_mesh(),
        scratch_types=[pltpu.VMEM((2, c, per, lanes), x1t.dtype),
                       pltpu.VMEM((2, c, lanes), route.dtype),
                       pltpu.VMEM((per_worker,), jnp.int32),
                       pltpu.VMEM((per_worker,), jnp.int32),
                       pltpu.SemaphoreType.DMA((4,)),
                       pltpu.SemaphoreType.DMA((4,))],
        name="sc_dispatch",
    )(x1t, route, dest)


def _sc_combine(y_rows, dest, t):
    _, per, lanes = y_rows.shape
    c = SC_CHUNK

    def body(y_hbm, d_hbm, oa_hbm, ob_hbm, buf, idx1, idx2, gsem, osem):
        first, chunks = _sc_worker_chunks(t, c)
        pltpu.sync_copy(d_hbm.at[pl.ds(first * c, chunks * c)], idx1)
        pltpu.sync_copy(d_hbm.at[pl.ds(t + first * c, chunks * c)], idx2)

        def stores(j, slot):
            rows = pl.ds((first + j) * c, c)
            return (pltpu.make_async_copy(buf.at[slot, 0], oa_hbm.at[rows], osem.at[2 * slot]),
                    pltpu.make_async_copy(buf.at[slot, 1], ob_hbm.at[rows], osem.at[2 * slot + 1]))

        @pl.loop(0, chunks)
        def _(j):
            slot = j % 2

            @pl.when(j >= 2)
            def _():
                for cp in stores(j - 2, slot):
                    cp.wait()

            gathers = (pltpu.make_async_copy(y_hbm.at[idx1.at[pl.ds(j * c, c)]], buf.at[slot, 0], gsem.at[0]),
                       pltpu.make_async_copy(y_hbm.at[idx2.at[pl.ds(j * c, c)]], buf.at[slot, 1], gsem.at[1]))
            for cp in gathers:
                cp.start()
            for cp in gathers:
                cp.wait()
            for cp in stores(j, slot):
                cp.start()

        for j in range(max(chunks - 2, 0), chunks):
            for cp in stores(j, j % 2):
                cp.wait()

    per_worker = t // (SC_CORES * SC_SUBCORES)
    out = jax.ShapeDtypeStruct((t, per, lanes), y_rows.dtype)
    return pl.kernel(
        body,
        out_type=(out, out),
        mesh=_sc_mesh(),
        scratch_types=[pltpu.VMEM((2, 2, c, per, lanes), y_rows.dtype),
                       pltpu.VMEM((per_worker,), jnp.int32),
                       pltpu.VMEM((per_worker,), jnp.int32),
                       pltpu.SemaphoreType.DMA((2,)),
                       pltpu.SemaphoreType.DMA((4,))],
        name="sc_combine",
    )(y_rows, dest)


def _expert_ffn(x_sorted, r_sorted, w_in, w_down, layer, block_expert, seg):
    n_rows, per, _ = x_sorted.shape
    _, n_exp, d, de2 = w_in.shape
    de = w_down.shape[2]
    blk = EXPERT_BLOCK
    n_blocks = block_expert.shape[0]

    def used(i, be, sg):
        return jnp.minimum(i, sg[2 * n_exp] - 1), 0

    y = pl.pallas_call(
        functools.partial(_expert_ffn_kernel, layer=layer),
        grid_spec=pltpu.PrefetchScalarGridSpec(
            num_scalar_prefetch=2,
            grid=(n_blocks,),
            in_specs=[pl.BlockSpec((blk * per, LANES), used),
                      pl.BlockSpec((blk, LANES), used),
                      pl.BlockSpec(memory_space=pl.ANY),
                      pl.BlockSpec(memory_space=pl.ANY)],
            out_specs=pl.BlockSpec((blk * per, LANES), lambda i, be, sg: (i, 0)),
            scratch_shapes=[pltpu.VMEM((d, de2), _F32),
                            pltpu.VMEM((de, d), _F32),
                            pltpu.SemaphoreType.DMA((2,)),
                            pltpu.VMEM((d, de2), _BF16),
                            pltpu.VMEM((de, d), _BF16)]),
        out_shape=jax.ShapeDtypeStruct((n_rows * per, LANES), jnp.uint32),
        compiler_params=_params("arbitrary"),
        name="expert_ffn",
    )(block_expert, seg, x_sorted.reshape(n_rows * per, LANES), r_sorted, w_in, w_down)
    return y.reshape(n_rows, per, LANES)


def _combine(x1, ya, yb, ln_g, ln_b, *, alpha):
    t, d = x1.shape
    tm = ROW_TILE
    per = ya.shape[1]
    row = pl.BlockSpec((tm, d), lambda i: (i, 0))
    words = pl.BlockSpec((tm * per, LANES), lambda i: (i, 0))
    return pl.pallas_call(
        functools.partial(_combine_kernel, alpha=alpha),
        grid=(t // tm,),
        in_specs=[row, words, words, _full((1, d)), _full((1, d))],
        out_specs=row,
        out_shape=jax.ShapeDtypeStruct((t, d), _F32),
        compiler_params=_params("arbitrary"),
        name="combine_ln",
    )(x1, ya.reshape(t * per, LANES), yb.reshape(t * per, LANES), ln_g[None, :], ln_b[None, :])


def _combine_qkv(x1, ya, yb, ln_g, ln_b, w_k, w_v, w_q, *, alpha, q_scale):
    t, d = x1.shape
    tm = ROW_TILE
    per = ya.shape[1]
    row = pl.BlockSpec((tm, d), lambda i: (i, 0))
    words = pl.BlockSpec((tm * per, LANES), lambda i: (i, 0))
    return pl.pallas_call(
        functools.partial(_combine_qkv_kernel, alpha=alpha, q_scale=q_scale),
        grid=(t // tm,),
        in_specs=[row, words, words, _full((1, d)), _full((1, d)),
                  _full((d, d)), _full((d, d)), _full((d, d))],
        out_specs=(row, row, row, row),
        out_shape=(jax.ShapeDtypeStruct((t, d), _F32),
                   jax.ShapeDtypeStruct((t, d), _BF16),
                   jax.ShapeDtypeStruct((t, d), _BF16),
                   jax.ShapeDtypeStruct((t, d), _BF16)),
        compiler_params=_params("arbitrary"),
        name="combine_ln_qkv",
    )(x1, ya.reshape(t * per, LANES), yb.reshape(t * per, LANES), ln_g[None, :], ln_b[None, :],
      w_k, w_v, w_q)


def _diff_attention(q, k, v, lam, subln_g, *, batch, seq, lam_init):
    t, d = q.shape
    tq = ATTN_BLOCK
    hw = d // N_HEADS
    head = pl.BlockSpec((seq, ATTN_HEADS * hw), lambda b, h: (b, h))
    return pl.pallas_call(
        functools.partial(_diff_attn_kernel, lam_init=lam_init),
        grid=(batch, N_HEADS // ATTN_HEADS),
        in_specs=[_full(lam.shape), _full((1, hw)), head, head, head],
        out_specs=head,
        out_shape=jax.ShapeDtypeStruct((t, d), _BF16),
        scratch_shapes=[pltpu.VMEM((1, LANES), _F32),
                        pltpu.VMEM((ATTN_BUFFERS, 2 * tq, seq), _F32),
                        pltpu.VMEM((ATTN_BUFFERS, 2 * tq, seq), _BF16)],
        compiler_params=_params("arbitrary", "arbitrary"),
        name="diff_attention",
    )(lam, subln_g[None, :], q, k, v)


def _moe(x1t, route, route_t, counts, w_in, w_down, layer):
    n_exp, d = w_in.shape[1:3]
    dest, block_expert, seg = _dispatch_plan(route_t, counts, n_exp)
    x1t = x1t.reshape(-1, d // (2 * LANES), LANES)
    x_sorted, r_sorted = _sc_dispatch(x1t, route, dest, block_expert.shape[0] * EXPERT_BLOCK)
    y_rows = _expert_ffn(x_sorted, r_sorted, w_in, w_down, layer, block_expert, seg)
    return _sc_combine(y_rows, dest, x1t.shape[0])


def kernel(x, a_w_in, a_conv_w, a_w_out, kv_w, b_w_q, b_lambda, b_subln_g, b_w_o,
           ln1_g, ln1_b, ln2_g, ln2_b, rg_w, rg_b, re_w, re_b, e_w_in, e_w_down):
    batch, seq, d = x.shape
    depth = ln1_g.shape[0]
    assert depth == 2 and a_w_in.shape[0] == 1 and b_w_q.shape[0] == 1
    assert seq % MIXER_TILE == 0 and seq % ROW_TILE == 0 and seq % ATTN_BLOCK == 0
    t = batch * seq
    assert t % (SC_CORES * SC_SUBCORES * 2 * SC_CHUNK) == 0
    alpha = (2.0 * depth) ** 0.25
    epg = re_w.shape[2] // N_GROUPS
    head_dim = d // (2 * N_HEADS)
    xf = x.reshape(t, d)

    router0 = _router_weights(rg_w[0], rg_b[0], re_w[0], re_b[0])
    x1, x1t, route, route_t, counts = _conv_mixer(xf, a_w_in, a_w_out, 0, a_conv_w[0],
                                                  ln1_g[0], ln1_b[0], router0,
                                                  seq=seq, alpha=alpha, epg=epg)
    ya, yb = _moe(x1t, route, route_t, counts, e_w_in, e_w_down, 0)
    q_scale = head_dim ** -0.5 * math.log2(math.e)
    x2, k, v, q = _combine_qkv(x1, ya, yb, ln2_g[0], ln2_b[0],
                               kv_w[:, :d].astype(_BF16), kv_w[:, d:].astype(_BF16),
                               b_w_q[0].astype(_BF16), alpha=alpha, q_scale=q_scale)

    lam_init = 0.8 - 0.6 * math.exp(-0.3 * 1)
    o = _diff_attention(q, k, v, b_lambda[0], b_subln_g[0], batch=batch, seq=seq, lam_init=lam_init)
    router1 = _router_weights(rg_w[1], rg_b[1], re_w[1], re_b[1])
    x1, x1t, route, route_t, counts = _attn_out(x2, o, b_w_o[0].astype(_BF16), ln1_g[1], ln1_b[1],
                                                router1, alpha=alpha, epg=epg)
    ya, yb = _moe(x1t, route, route_t, counts, e_w_in, e_w_down, 1)
    out = _combine(x1, ya, yb, ln2_g[1], ln2_b[1], alpha=alpha)
    return out.reshape(batch, seq, d)
```

```python
import functools
import math

import jax
import jax.numpy as jnp
from jax import lax
from jax.experimental import pallas as pl
from jax.experimental.pallas import tpu as pltpu
from jax.experimental.pallas import tpu_sc as plsc

N_HEADS = 8
N_GROUPS = 4
LN_EPS = 1e-5
RMS_EPS = 1e-5

LANES = 128
SUBLANES = 8
VMEM_LIMIT_BYTES = 56 * 1024 * 1024

SC_CORES = 2
SC_SUBCORES = 16

SC_CHUNK = 32
MIXER_TILE = 1024
ROW_TILE = 1024
EXPERT_BLOCK = 512
ATTN_BLOCK = 256
ATTN_HEADS = 2
ATTN_BUFFERS = 4

_E1, _E2, _R1, _R2, _G1, _G2 = range(6)

_F32 = jnp.float32
_BF16 = jnp.bfloat16
_NT = (((1,), (1,)), ((), ()))


def _dot(a, b):
    return jnp.dot(a, b, preferred_element_type=_F32)


def _layer_norm(z, g, b):
    mu = jnp.mean(z, axis=-1, keepdims=True)
    d = z - mu
    var = jnp.mean(d * d, axis=-1, keepdims=True)
    return d * lax.rsqrt(var + LN_EPS) * g + b


def _route_tail(x1, wt_ref, bt_ref, route_ref, routet_ref, counts_ref, umat_ref, *, epg):
    tm = x1.shape[0]
    nr = counts_ref.shape[0]
    half = wt_ref.shape[0] // 2

    @pl.when(pl.program_id(0) == 0)
    def _():
        counts_ref[...] = jnp.zeros_like(counts_ref)
        before = (lax.broadcasted_iota(jnp.int32, (tm, tm), 0) < lax.broadcasted_iota(jnp.int32, (tm, tm), 1))
        umat_ref[...] = jnp.where(before, 1.0, 0.0).astype(_BF16)

    xh = x1.astype(_BF16)
    xl = (x1 - xh.astype(_F32)).astype(_BF16)
    a = lax.dot_general(wt_ref[...], xh, _NT, preferred_element_type=_F32)
    b = lax.dot_general(wt_ref[0:half, :], xl, _NT, preferred_element_type=_F32)
    logits = a[0:nr, :] + a[half:half + nr, :] + b[0:nr, :] + bt_ref[0:nr, :]

    rowf = lax.broadcasted_iota(jnp.int32, (nr, tm), 0).astype(_F32)
    neg = -jnp.inf
    big = float(nr)
    gl = jnp.where(rowf < float(N_GROUPS), logits, neg)
    gmax = jnp.max(gl, axis=0, keepdims=True)
    gidx = jnp.min(jnp.where(gl == gmax, rowf, big), axis=0, keepdims=True)
    gtop = 1.0 / jnp.sum(jnp.exp(gl - gmax), axis=0, keepdims=True)

    lo = float(N_GROUPS) + gidx * float(epg)
    el = jnp.where((rowf >= lo) & (rowf < lo + float(epg)), logits, neg)
    m1 = jnp.max(el, axis=0, keepdims=True)
    i1 = jnp.min(jnp.where(el == m1, rowf, big), axis=0, keepdims=True)
    el2 = jnp.where(rowf == i1, neg, el)
    m2 = jnp.max(el2, axis=0, keepdims=True)
    i2 = jnp.min(jnp.where(el2 == m2, rowf, big), axis=0, keepdims=True)
    w2 = jnp.exp(m2 - m1)
    inv = 1.0 / (1.0 + w2)
    g1 = gtop * inv
    g2 = gtop * w2 * inv

    onehot = jnp.where((rowf == i1) | (rowf == i2), 1.0, 0.0)
    total = _dot(onehot.astype(_BF16), umat_ref[...]) + counts_ref[...]
    r1 = jnp.sum(jnp.where(rowf == i1, total, 0.0), axis=0, keepdims=True)
    r2 = jnp.sum(jnp.where(rowf == i2, total, 0.0), axis=0, keepdims=True)
    counts_ref[...] += jnp.sum(onehot, axis=1, keepdims=True)

    slot = lax.broadcasted_iota(jnp.int32, (SUBLANES, tm), 0)
    rec = jnp.zeros((SUBLANES, tm), _F32)
    fields = ((_E1, i1 - float(N_GROUPS)), (_E2, i2 - float(N_GROUPS)), (_R1, r1), (_R2, r2), (_G1, g1), (_G2, g2))
    for s, val in fields:
        rec = jnp.where(slot == s, val, rec)
    routet_ref[...] = rec
    rec = jnp.concatenate([rec, jnp.zeros((LANES - SUBLANES, tm), _F32)], axis=0)
    route_ref[...] = rec.T


def _store_token_words(ref, x):
    n, d = x.shape
    rows = d // (2 * LANES)
    bits = lax.bitcast_convert_type(x.astype(_BF16).astype(_F32), jnp.uint32)
    for s in range(rows):
        lo = bits[:, s * LANES:(s + 1) * LANES] >> 16
        hi = bits[:, d // 2 + s * LANES:d // 2 + (s + 1) * LANES] & jnp.uint32(0xFFFF0000)
        ref[pl.ds(s, n, stride=rows), :] = lo | hi


def _load_token_words(ref, n, d):
    rows = d // (2 * LANES)
    words = [ref[pl.ds(s, n, stride=rows), :] for s in range(rows)]
    lo = [lax.bitcast_convert_type(w << 16, _F32) for w in words]
    hi = [lax.bitcast_convert_type(w & jnp.uint32(0xFFFF0000), _F32) for w in words]
    return jnp.concatenate(lo + hi, axis=1)


def _conv_mixer_kernel(x_ref, win_hbm, cw_ref, wout_hbm, g_ref, b_ref, wt_ref, bt_ref,
                       x1_ref, x1t_ref, route_ref, routet_ref, counts_ref, ubuf_ref, umat_ref,
                       win_ref, wout_ref, wstage_ref, wsem, *, layer, tiles_per_seq, alpha, epg):
    i = pl.program_id(0)
    tm, d = x_ref.shape

    @pl.when(i == 0)
    def _():
        _, rows, _ = wstage_ref.shape
        panels = d // rows

        def panel(j):
            slot, k = j % 2, j % panels
            if j < panels:
                return pltpu.make_async_copy(win_hbm.at[layer, pl.ds(k * rows, rows), :],
                                             wstage_ref.at[slot], wsem.at[slot])
            return pltpu.make_async_copy(wout_hbm.at[layer, pl.ds(k * rows, rows), :],
                                         wstage_ref.at[slot, :, pl.ds(0, d)], wsem.at[slot])

        panel(0).start()
        for j in range(2 * panels):
            slot, k = j % 2, j % panels
            if j + 1 < 2 * panels:
                panel(j + 1).start()
            panel(j).wait()
            if j < panels:
                win_ref[k * rows:(k + 1) * rows, :] = wstage_ref[slot].astype(_BF16)
            else:
                wout_ref[k * rows:(k + 1) * rows, :] = wstage_ref[slot, :, 0:d].astype(_BF16)

    @pl.when(i % tiles_per_seq == 0)
    def _():
        ubuf_ref[0:SUBLANES, :] = jnp.zeros((SUBLANES, d), _F32)

    x = x_ref[...]
    h = _dot(x.astype(_BF16), win_ref[...])
    u = h[:, d:2 * d] * h[:, 2 * d:]
    ubuf_ref[SUBLANES:SUBLANES + tm, :] = u
    cw = cw_ref[...]
    uc = (cw[0:1, :] * ubuf_ref[SUBLANES - 2:SUBLANES - 2 + tm, :]
          + cw[1:2, :] * ubuf_ref[SUBLANES - 1:SUBLANES - 1 + tm, :]
          + cw[2:3, :] * u)
    ubuf_ref[0:SUBLANES, :] = ubuf_ref[tm:tm + SUBLANES, :]
    y = _dot((h[:, :d] * uc).astype(_BF16), wout_ref[...])
    x1 = _layer_norm(alpha * x + y, g_ref[...], b_ref[...])
    x1_ref[...] = x1
    _store_token_words(x1t_ref, x1)
    _route_tail(x1, wt_ref, bt_ref, route_ref, routet_ref, counts_ref, umat_ref, epg=epg)


def _attn_out_kernel(x_ref, o_ref, wo_ref, g_ref, b_ref, wt_ref, bt_ref,
                     x1_ref, x1t_ref, route_ref, routet_ref, counts_ref, umat_ref, *, alpha, epg):
    y = _dot(o_ref[...], wo_ref[...])
    x1 = _layer_norm(alpha * x_ref[...] + y, g_ref[...], b_ref[...])
    x1_ref[...] = x1
    _store_token_words(x1t_ref, x1)
    _route_tail(x1, wt_ref, bt_ref, route_ref, routet_ref, counts_ref, umat_ref, epg=epg)


def _expert_ffn_kernel(bexp_ref, seg_ref, x_ref, rec_ref, win_hbm, wdn_hbm, y_ref,
                       wstage_in, wstage_dn, wsem, winb, wdnb, *, layer):
    i = pl.program_id(0)
    n_exp = (seg_ref.shape[0] - 1) // 2
    n_used = seg_ref[2 * n_exp]
    d, de2 = wstage_in.shape
    blk = rec_ref.shape[0]
    de = de2 // 2

    def weight_copies(e):
        return (pltpu.make_async_copy(win_hbm.at[layer, e], wstage_in, wsem.at[0]),
                pltpu.make_async_copy(wdn_hbm.at[layer, e], wstage_dn, wsem.at[1]))

    @pl.when(i == 0)
    def _():
        for c in weight_copies(bexp_ref[0]):
            c.start()

    @pl.when(i < n_used)
    def _():
        e = bexp_ref[i]

        @pl.when((i == 0) | (e != bexp_ref[jnp.maximum(i - 1, 0)]))
        def _():
            for c in weight_copies(e):
                c.wait()
            winb[...] = wstage_in[...].astype(_BF16)
            wdnb[...] = wstage_dn[...].astype(_BF16)
            nxt = seg_ref[n_exp + e] // blk

            @pl.when(nxt < n_used)
            def _():
                for c in weight_copies(bexp_ref[nxt]):
                    c.start()

        n_valid = seg_ref[e] - i * blk

        def ffn(rows):
            valid = lax.broadcasted_iota(jnp.int32, (rows, 1), 0) < n_valid
            x = jnp.where(valid, _load_token_words(x_ref, rows, d), 0.0)
            h = _dot(x.astype(_BF16), winb[...])
            g = h[:, :de]
            a = g * jax.nn.sigmoid(g) * h[:, de:]
            rec = rec_ref[0:rows, :]
            gate = jnp.where(rec[:, _E1:_E1 + 1] == e.astype(_F32), rec[:, _G1:_G1 + 1], rec[:, _G2:_G2 + 1])
            gate = jnp.where(valid, gate, 0.0)
            _store_token_words(y_ref, _dot(a.astype(_BF16), wdnb[...]) * gate)

        half = blk // 2

        @pl.when(n_valid > half)
        def _():
            ffn(blk)

        @pl.when(n_valid <= half)
        def _():
            ffn(half)
            y_ref[half * (y_ref.shape[0] // blk):, :] = jnp.zeros(
                (half * (y_ref.shape[0] // blk), y_ref.shape[1]), y_ref.dtype)

    @pl.when(i >= n_used)
    def _():
        y_ref[...] = jnp.zeros_like(y_ref)


def _combine_body(x1_ref, ya_ref, yb_ref, g_ref, b_ref, *, alpha):
    tm, d = x1_ref.shape
    ffn = _load_token_words(ya_ref, tm, d) + _load_token_words(yb_ref, tm, d)
    return _layer_norm(alpha * x1_ref[...] + ffn, g_ref[...], b_ref[...])


def _combine_kernel(x1_ref, ya_ref, yb_ref, g_ref, b_ref, x2_ref, *, alpha):
    x2_ref[...] = _combine_body(x1_ref, ya_ref, yb_ref, g_ref, b_ref, alpha=alpha)


def _combine_qkv_kernel(x1_ref, ya_ref, yb_ref, g_ref, b_ref, wk_ref, wv_ref, wq_ref,
                        x2_ref, k_ref, v_ref, q_ref, *, alpha, q_scale):
    x2 = _combine_body(x1_ref, ya_ref, yb_ref, g_ref, b_ref, alpha=alpha)
    x2_ref[...] = x2
    xb = x2.astype(_BF16)
    k_ref[...] = _dot(xb, wk_ref[...]).astype(_BF16)
    q_ref[...] = (_dot(xb, wq_ref[...]) * q_scale).astype(_BF16)
    v_ref[...] = _dot(xb, wv_ref[...]).astype(_BF16)


def _diff_attn_kernel(lam_ref, g_ref, q_ref, k_ref, v_ref, o_ref, lamfull_ref, s_ref, p_ref, *, lam_init):
    bi, hi = pl.program_id(0), pl.program_id(1)
    seq = k_ref.shape[0]
    hw = g_ref.shape[1]
    heads = k_ref.shape[1] // hw
    tq = tk = ATTN_BLOCK
    nk = seq // tk
    hd = lam_ref.shape[1]

    @pl.when((bi == 0) & (hi == 0))
    def _():
        lam = lam_ref[...]
        a = jnp.sum(lam[0:1, :] * lam[1:2, :], axis=-1, keepdims=True)
        b = jnp.sum(lam[2:3, :] * lam[3:4, :], axis=-1, keepdims=True)
        lamfull_ref[...] = jnp.broadcast_to(jnp.exp(a) - jnp.exp(b) + lam_init, lamfull_ref.shape)

    lam_full = lamfull_ref[0:1, 0:1]
    lane = lax.broadcasted_iota(jnp.int32, (tq, hw), 1)
    row = lax.broadcasted_iota(jnp.int32, (2 * tq, tk), 0)
    col = lax.broadcasted_iota(jnp.int32, (2 * tq, tk), 1)
    causal = col <= jnp.where(row < tq, row, row - tq)
    ones = jnp.ones((seq, hw), _BF16)

    step = 0
    for i in reversed(range(nk)):
        for h in range(heads):
            cols = slice(h * hw, (h + 1) * hw)
            par = step % s_ref.shape[0]
            step += 1
            kv = (i + 1) * tk
            q = q_ref[i * tq:(i + 1) * tq, cols]
            zero = jnp.zeros_like(q)
            qcat = jnp.concatenate([jnp.where(lane < hd, q, zero), jnp.where(lane >= hd, q, zero)], axis=0)
            s_ref[par, :, 0:kv] = lax.dot_general(qcat, k_ref[0:kv, cols], _NT, preferred_element_type=_F32)
            s_ref[par, :, i * tk:kv] = jnp.where(causal, s_ref[par, :, i * tk:kv], -jnp.inf)
            rowmax = jnp.max(s_ref[par, :, 0:kv], axis=1, keepdims=True)
            p_ref[par, :, 0:kv] = jnp.exp2(s_ref[par, :, 0:kv] - rowmax).astype(_BF16)
            v_aug = jnp.concatenate([v_ref[0:kv, cols], ones[0:kv, :]], axis=1)
            acc = _dot(p_ref[par, :, 0:kv], v_aug)
            o = acc[:, 0:hw] * (1.0 / acc[:, hw:hw + 1])
            od = o[0:tq, :] - lam_full * o[tq:, :]
            od = od * lax.rsqrt(jnp.mean(od * od, axis=1, keepdims=True) + RMS_EPS)
            o_ref[i * tq:(i + 1) * tq, cols] = (od * g_ref[...] * (1.0 - lam_init)).astype(o_ref.dtype)


def _params(*sem):
    return pltpu.CompilerParams(dimension_semantics=sem, vmem_limit_bytes=VMEM_LIMIT_BYTES)


def _full(shape):
    return pl.BlockSpec(shape, lambda *_: (0,) * len(shape))


def _router_weights(rg_w, rg_b, re_w, re_b):
    d = rg_w.shape[0]
    n = rg_w.shape[1] + re_w.shape[1]
    w = jnp.concatenate([rg_w, re_w, jnp.zeros((d, LANES - n), _F32)], axis=1).T
    b = jnp.concatenate([rg_b, re_b, jnp.zeros((LANES - n,), _F32)])[:, None]
    wh = w.astype(_BF16)
    wl = (w - wh.astype(_F32)).astype(_BF16)
    return jnp.concatenate([wh, wl], axis=0), b


def _router_rows(n_experts):
    return -(-(N_GROUPS + n_experts) // (2 * SUBLANES)) * (2 * SUBLANES)


def _route_outs(t, d, tm, nr):
    per = d // LANES
    shapes = (jax.ShapeDtypeStruct((t, d), _F32),
              jax.ShapeDtypeStruct((t * per // 2, LANES), jnp.uint32),
              jax.ShapeDtypeStruct((t, LANES), _F32),
              jax.ShapeDtypeStruct((SUBLANES, t), _F32),
              jax.ShapeDtypeStruct((nr, 1), _F32))
    specs = (pl.BlockSpec((tm, d), lambda i: (i, 0)),
             pl.BlockSpec((tm * per // 2, LANES), lambda i: (i, 0)),
             pl.BlockSpec((tm, LANES), lambda i: (i, 0)),
             pl.BlockSpec((SUBLANES, tm), lambda i: (0, i)),
             pl.BlockSpec((nr, 1), lambda i: (0, 0)))
    return shapes, specs


def _conv_mixer(x, w_in_all, w_out_all, layer, conv_w, ln_g, ln_b, router, *, seq, alpha, epg):
    t, d = x.shape
    tm = MIXER_TILE
    wt, bt = router
    shapes, specs = _route_outs(t, d, tm, _router_rows(N_GROUPS * epg))
    d3 = w_in_all.shape[2]
    return pl.pallas_call(
        functools.partial(_conv_mixer_kernel, layer=layer, tiles_per_seq=seq // tm, alpha=alpha, epg=epg),
        grid=(t // tm,),
        in_specs=[pl.BlockSpec((tm, d), lambda i: (i, 0)),
                  pl.BlockSpec(memory_space=pl.ANY), _full(conv_w.shape), pl.BlockSpec(memory_space=pl.ANY),
                  _full((1, d)), _full((1, d)), _full(wt.shape), _full(bt.shape)],
        out_specs=specs, out_shape=shapes,
        scratch_shapes=[pltpu.VMEM((tm + SUBLANES, d), _F32), pltpu.VMEM((tm, tm), _BF16),
                        pltpu.VMEM((d, d3), _BF16), pltpu.VMEM((d, d), _BF16),
                        pltpu.VMEM((2, d // 4, d3), _F32), pltpu.SemaphoreType.DMA((2,))],
        compiler_params=_params("arbitrary"),
        name="conv_mixer_ln_router",
    )(x, w_in_all, conv_w, w_out_all, ln_g[None, :], ln_b[None, :], wt, bt)


def _attn_out(x, o, w_o, ln_g, ln_b, router, *, alpha, epg):
    t, d = x.shape
    tm = MIXER_TILE
    wt, bt = router
    shapes, specs = _route_outs(t, d, tm, _router_rows(N_GROUPS * epg))
    return pl.pallas_call(
        functools.partial(_attn_out_kernel, alpha=alpha, epg=epg),
        grid=(t // tm,),
        in_specs=[pl.BlockSpec((tm, d), lambda i: (i, 0)),
                  pl.BlockSpec((tm, d), lambda i: (i, 0)),
                  _full(w_o.shape), _full((1, d)), _full((1, d)), _full(wt.shape), _full(bt.shape)],
        out_specs=specs, out_shape=shapes,
        scratch_shapes=[pltpu.VMEM((tm, tm), _BF16)],
        compiler_params=_params("arbitrary"),
        name="attn_out_ln_router",
    )(x, o, w_o, ln_g[None, :], ln_b[None, :], wt, bt)


def _dispatch_plan(route_t, counts, n_experts):
    t = route_t.shape[1]
    blk = EXPERT_BLOCK
    cnt = counts[N_GROUPS:N_GROUPS + n_experts, 0].astype(jnp.int32)
    padded = (cnt + blk - 1) // blk * blk
    pad_end = jnp.cumsum(padded)
    pad_start = pad_end - padded
    ids = jnp.arange(n_experts, dtype=jnp.int32)

    def sorted_row(e_slot, r_slot):
        e = route_t[e_slot].astype(jnp.int32)
        start = jnp.sum(jnp.where(ids[:, None] == e[None, :], pad_start[:, None], 0), axis=0)
        return start + route_t[r_slot].astype(jnp.int32)

    dest = jnp.concatenate([sorted_row(_E1, _R1), sorted_row(_E2, _R2)])
    n_blocks = (2 * t + n_experts * blk) // blk
    block_start = jnp.arange(n_blocks, dtype=jnp.int32) * blk
    block_expert = jnp.minimum(jnp.sum(pad_end[None, :] <= block_start[:, None], axis=1),
                               n_experts - 1).astype(jnp.int32)
    n_used = (pad_end[-1:] // blk).astype(jnp.int32)
    seg = jnp.concatenate([pad_start + cnt, pad_end, n_used]).astype(jnp.int32)
    return dest, block_expert, seg


def _sc_mesh():
    return plsc.VectorSubcoreMesh(core_axis_name="c", subcore_axis_name="s",
                                  num_cores=SC_CORES, num_subcores=SC_SUBCORES)


def _sc_worker_chunks(t, chunk):
    chunks = t // (SC_CORES * SC_SUBCORES * chunk)
    wid = lax.axis_index("c") * SC_SUBCORES + lax.axis_index("s")
    return wid * chunks, chunks


def _sc_dispatch(x1t, route, dest, n_rows):
    t, per, lanes = x1t.shape
    c = SC_CHUNK
    slots = 3

    def body(x_hbm, r_hbm, d_hbm, xo_hbm, ro_hbm, xbuf, rbuf, idx1, idx2, lsem, ssem):
        first, chunks = _sc_worker_chunks(t, c)
        pltpu.sync_copy(d_hbm.at[pl.ds(first * c, chunks * c)], idx1)
        pltpu.sync_copy(d_hbm.at[pl.ds(t + first * c, chunks * c)], idx2)

        def loads(j, slot):
            rows = pl.ds((first + j) * c, c)
            return (pltpu.make_async_copy(x_hbm.at[rows], xbuf.at[slot], lsem.at[2 * slot]),
                    pltpu.make_async_copy(r_hbm.at[rows], rbuf.at[slot], lsem.at[2 * slot + 1]))

        def scatters(j, slot):
            i1 = idx1.at[pl.ds(j * c, c)]
            i2 = idx2.at[pl.ds(j * c, c)]
            return (pltpu.make_async_copy(xbuf.at[slot], xo_hbm.at[i1], ssem.at[4 * slot]),
                    pltpu.make_async_copy(xbuf.at[slot], xo_hbm.at[i2], ssem.at[4 * slot + 1]),
                    pltpu.make_async_copy(rbuf.at[slot], ro_hbm.at[i1], ssem.at[4 * slot + 2]),
                    pltpu.make_async_copy(rbuf.at[slot], ro_hbm.at[i2], ssem.at[4 * slot + 3]))

        for cp in loads(0, 0):
            cp.start()

        @pl.loop(0, chunks)
        def _(j):
            slot = j % slots
            nxt = (j + 1) % slots

            @pl.when(j >= 2)
            def _():
                for cp in scatters(j - 2, nxt):
                    cp.wait()

            @pl.when(j + 1 < chunks)
            def _():
                for cp in loads(j + 1, nxt):
                    cp.start()

            for cp in loads(j, slot):
                cp.wait()
            for cp in scatters(j, slot):
                cp.start()

        for j in range(max(chunks - 2, 0), chunks):
            for cp in scatters(j, j % slots):
                cp.wait()

    per_worker = t // (SC_CORES * SC_SUBCORES)
    return pl.kernel(
        body,
        out_type=(jax.ShapeDtypeStruct((n_rows, per, lanes), x1t.dtype),
                  jax.ShapeDtypeStruct((n_rows, lanes), route.dtype)),
        mesh=_sc_mesh(),
        scratch_types=[pltpu.VMEM((slots, c, per, lanes), x1t.dtype),
                       pltpu.VMEM((slots, c, lanes), route.dtype),
                       pltpu.VMEM((per_worker,), jnp.int32),
                       pltpu.VMEM((per_worker,), jnp.int32),
                       pltpu.SemaphoreType.DMA((2 * slots,)),
                       pltpu.SemaphoreType.DMA((4 * slots,))],
        name="sc_dispatch",
    )(x1t, route, dest)


def _sc_combine(y_rows, dest, t):
    _, per, lanes = y_rows.shape
    c = SC_CHUNK

    def body(y_hbm, d_hbm, oa_hbm, ob_hbm, buf, idx1, idx2, gsem, osem):
        first, chunks = _sc_worker_chunks(t, c)
        pltpu.sync_copy(d_hbm.at[pl.ds(first * c, chunks * c)], idx1)
        pltpu.sync_copy(d_hbm.at[pl.ds(t + first * c, chunks * c)], idx2)

        def stores(j, slot):
            rows = pl.ds((first + j) * c, c)
            return (pltpu.make_async_copy(buf.at[slot, 0], oa_hbm.at[rows], osem.at[2 * slot]),
                    pltpu.make_async_copy(buf.at[slot, 1], ob_hbm.at[rows], osem.at[2 * slot + 1]))

        @pl.loop(0, chunks)
        def _(j):
            slot = j % 2

            @pl.when(j >= 2)
            def _():
                for cp in stores(j - 2, slot):
                    cp.wait()

            gathers = (pltpu.make_async_copy(y_hbm.at[idx1.at[pl.ds(j * c, c)]], buf.at[slot, 0], gsem.at[0]),
                       pltpu.make_async_copy(y_hbm.at[idx2.at[pl.ds(j * c, c)]], buf.at[slot, 1], gsem.at[1]))
            for cp in gathers:
                cp.start()
            for cp in gathers:
                cp.wait()
            for cp in stores(j, slot):
                cp.start()

        for j in range(max(chunks - 2, 0), chunks):
            for cp in stores(j, j % 2):
                cp.wait()

    per_worker = t // (SC_CORES * SC_SUBCORES)
    out = jax.ShapeDtypeStruct((t, per, lanes), y_rows.dtype)
    return pl.kernel(
        body,
        out_type=(out, out),
        mesh=_sc_mesh(),
        scratch_types=[pltpu.VMEM((2, 2, c, per, lanes), y_rows.dtype),
                       pltpu.VMEM((per_worker,), jnp.int32),
                       pltpu.VMEM((per_worker,), jnp.int32),
                       pltpu.SemaphoreType.DMA((2,)),
                       pltpu.SemaphoreType.DMA((4,))],
        name="sc_combine",
    )(y_rows, dest)


def _expert_ffn(x_sorted, r_sorted, w_in, w_down, layer, block_expert, seg):
    n_rows, per, _ = x_sorted.shape
    _, n_exp, d, de2 = w_in.shape
    de = w_down.shape[2]
    blk = EXPERT_BLOCK
    n_blocks = block_expert.shape[0]

    def used(i, be, sg):
        return jnp.minimum(i, sg[2 * n_exp] - 1), 0

    y = pl.pallas_call(
        functools.partial(_expert_ffn_kernel, layer=layer),
        grid_spec=pltpu.PrefetchScalarGridSpec(
            num_scalar_prefetch=2,
            grid=(n_blocks,),
            in_specs=[pl.BlockSpec((blk * per, LANES), used),
                      pl.BlockSpec((blk, LANES), used),
                      pl.BlockSpec(memory_space=pl.ANY),
                      pl.BlockSpec(memory_space=pl.ANY)],
            out_specs=pl.BlockSpec((blk * per, LANES), lambda i, be, sg: (i, 0)),
            scratch_shapes=[pltpu.VMEM((d, de2), _F32),
                            pltpu.VMEM((de, d), _F32),
                            pltpu.SemaphoreType.DMA((2,)),
                            pltpu.VMEM((d, de2), _BF16),
                            pltpu.VMEM((de, d), _BF16)]),
        out_shape=jax.ShapeDtypeStruct((n_rows * per, LANES), jnp.uint32),
        compiler_params=_params("arbitrary"),
        name="expert_ffn",
    )(block_expert, seg, x_sorted.reshape(n_rows * per, LANES), r_sorted, w_in, w_down)
    return y.reshape(n_rows, per, LANES)


def _combine(x1, ya, yb, ln_g, ln_b, *, alpha):
    t, d = x1.shape
    tm = ROW_TILE
    per = ya.shape[1]
    row = pl.BlockSpec((tm, d), lambda i: (i, 0))
    words = pl.BlockSpec((tm * per, LANES), lambda i: (i, 0))
    return pl.pallas_call(
        functools.partial(_combine_kernel, alpha=alpha),
        grid=(t // tm,),
        in_specs=[row, words, words, _full((1, d)), _full((1, d))],
        out_specs=row,
        out_shape=jax.ShapeDtypeStruct((t, d), _F32),
        compiler_params=_params("arbitrary"),
        name="combine_ln",
    )(x1, ya.reshape(t * per, LANES), yb.reshape(t * per, LANES), ln_g[None, :], ln_b[None, :])


def _combine_qkv(x1, ya, yb, ln_g, ln_b, w_k, w_v, w_q, *, alpha, q_scale):
    t, d = x1.shape
    tm = ROW_TILE
    per = ya.shape[1]
    row = pl.BlockSpec((tm, d), lambda i: (i, 0))
    words = pl.BlockSpec((tm * per, LANES), lambda i: (i, 0))
    return pl.pallas_call(
        functools.partial(_combine_qkv_kernel, alpha=alpha, q_scale=q_scale),
        grid=(t // tm,),
        in_specs=[row, words, words, _full((1, d)), _full((1, d)),
                  _full((d, d)), _full((d, d)), _full((d, d))],
        out_specs=(row, row, row, row),
        out_shape=(jax.ShapeDtypeStruct((t, d), _F32),
                   jax.ShapeDtypeStruct((t, d), _BF16),
                   jax.ShapeDtypeStruct((t, d), _BF16),
                   jax.ShapeDtypeStruct((t, d), _BF16)),
        compiler_params=_params("arbitrary"),
        name="combine_ln_qkv",
    )(x1, ya.reshape(t * per, LANES), yb.reshape(t * per, LANES), ln_g[None, :], ln_b[None, :],
      w_k, w_v, w_q)


def _diff_attention(q, k, v, lam, subln_g, *, batch, seq, lam_init):
    t, d = q.shape
    tq = ATTN_BLOCK
    hw = d // N_HEADS
    head = pl.BlockSpec((seq, ATTN_HEADS * hw), lambda b, h: (b, h))
    return pl.pallas_call(
        functools.partial(_diff_attn_kernel, lam_init=lam_init),
        grid=(batch, N_HEADS // ATTN_HEADS),
        in_specs=[_full(lam.shape), _full((1, hw)), head, head, head],
        out_specs=head,
        out_shape=jax.ShapeDtypeStruct((t, d), _BF16),
        scratch_shapes=[pltpu.VMEM((1, LANES), _F32),
                        pltpu.VMEM((ATTN_BUFFERS, 2 * tq, seq), _F32),
                        pltpu.VMEM((ATTN_BUFFERS, 2 * tq, seq), _BF16)],
        compiler_params=_params("arbitrary", "arbitrary"),
        name="diff_attention",
    )(lam, subln_g[None, :], q, k, v)


def _moe(x1t, route, route_t, counts, w_in, w_down, layer):
    n_exp, d = w_in.shape[1:3]
    dest, block_expert, seg = _dispatch_plan(route_t, counts, n_exp)
    x1t = x1t.reshape(-1, d // (2 * LANES), LANES)
    x_sorted, r_sorted = _sc_dispatch(x1t, route, dest, block_expert.shape[0] * EXPERT_BLOCK)
    y_rows = _expert_ffn(x_sorted, r_sorted, w_in, w_down, layer, block_expert, seg)
    return _sc_combine(y_rows, dest, x1t.shape[0])


def kernel(x, a_w_in, a_conv_w, a_w_out, kv_w, b_w_q, b_lambda, b_subln_g, b_w_o,
           ln1_g, ln1_b, ln2_g, ln2_b, rg_w, rg_b, re_w, re_b, e_w_in, e_w_down):
    batch, seq, d = x.shape
    depth = ln1_g.shape[0]
    assert depth == 2 and a_w_in.shape[0] == 1 and b_w_q.shape[0] == 1
    assert seq % MIXER_TILE == 0 and seq % ROW_TILE == 0 and seq % ATTN_BLOCK == 0
    t = batch * seq
    assert t % (SC_CORES * SC_SUBCORES * SC_CHUNK) == 0
    alpha = (2.0 * depth) ** 0.25
    epg = re_w.shape[2] // N_GROUPS
    head_dim = d // (2 * N_HEADS)
    xf = x.reshape(t, d)

    router0 = _router_weights(rg_w[0], rg_b[0], re_w[0], re_b[0])
    x1, x1t, route, route_t, counts = _conv_mixer(xf, a_w_in, a_w_out, 0, a_conv_w[0],
                                                  ln1_g[0], ln1_b[0], router0,
                                                  seq=seq, alpha=alpha, epg=epg)
    ya, yb = _moe(x1t, route, route_t, counts, e_w_in, e_w_down, 0)
    q_scale = head_dim ** -0.5 * math.log2(math.e)
    x2, k, v, q = _combine_qkv(x1, ya, yb, ln2_g[0], ln2_b[0],
                               kv_w[:, :d].astype(_BF16), kv_w[:, d:].astype(_BF16),
                               b_w_q[0].astype(_BF16), alpha=alpha, q_scale=q_scale)

    lam_init = 0.8 - 0.6 * math.exp(-0.3 * 1)
    o = _diff_attention(q, k, v, b_lambda[0], b_subln_g[0], batch=batch, seq=seq, lam_init=lam_init)
    router1 = _router_weights(rg_w[1], rg_b[1], re_w[1], re_b[1])
    x1, x1t, route, route_t, counts = _attn_out(x2, o, b_w_o[0].astype(_BF16), ln1_g[1], ln1_b[1],
                                                router1, alpha=alpha, epg=epg)
    ya, yb = _moe(x1t, route, route_t, counts, e_w_in, e_w_down, 1)
    out = _combine(x1, ya, yb, ln2_g[1], ln2_b[1], alpha=alpha)
    return out.reshape(batch, seq, d)
```
